```python
import jax, jax.numpy as jnp
from jax import lax
import numpy as np

D_MODEL = 1024
BATCH = 32
SEQ = 2048
DEPTH = 1

CTX_LEN = 256
GRID_W = 64
MIX_WIDTH = D_MODEL
RWKV_WIDTH = MIX_WIDTH // 2
CONV_WIDTH = MIX_WIDTH - RWKV_WIDTH
HEAD_SIZE = 64
RWKV_HEADS = RWKV_WIDTH // HEAD_SIZE
DECAY_RANK = max(32, int(round(1.8 * D_MODEL ** 0.5 / 32)) * 32)
ICLR_RANK = max(32, int(round(1.8 * D_MODEL ** 0.5 / 32)) * 32)
GATE_RANK = max(32, int(round(0.6 * D_MODEL ** 0.8 / 32)) * 32)
CONV_KERNEL = 31
D_FF = 4 * D_MODEL
N_MOD = 6
EPS_RMS = 1e-6
EPS_LN = 1e-5
EPS_GN = 64e-5

IN_SPLITS = (RWKV_WIDTH, RWKV_WIDTH, RWKV_WIDTH, DECAY_RANK, DECAY_RANK,
             ICLR_RANK, ICLR_RANK, GATE_RANK, 2 * CONV_WIDTH)
IN_COLS = sum(IN_SPLITS)
SHIFT_COLS = sum(IN_SPLITS[:-1])
RWKV_CUTS = tuple(int(v) for v in np.cumsum(IN_SPLITS[:-1])[:-1])

kernel_name = "hymba_rwkv7_conformer_dit_block"


def rms_norm(x, g):
    xf = x.astype(jnp.float32)
    y = xf * lax.rsqrt(jnp.mean(xf * xf, axis=-1, keepdims=True) + EPS_RMS)
    return (y * g.astype(jnp.float32)).astype(x.dtype)


def modulate(h, shift, scale):
    return h * (1 + scale) + shift


def split_heads(t):
    return t.reshape(t.shape[:-1] + (RWKV_HEADS, HEAD_SIZE))


def token_shift(z, mu_prev, mu_next):
    zp = jnp.pad(z, ((0, 0), (1, 1), (0, 0)))
    mp = mu_prev.astype(jnp.float32)
    mn = mu_next.astype(jnp.float32)
    return z + mp * (zp[:, :-2] - z) + mn * (zp[:, 2:] - z)


def project_stream(h, w_in, mu_prev, mu_next):
    p = h @ w_in
    rw = token_shift(p[..., :SHIFT_COLS].astype(jnp.float32), mu_prev, mu_next)
    pieces = tuple(jnp.split(rw, RWKV_CUTS, axis=-1))
    return pieces, p[..., SHIFT_COLS:]


def wkv_scan(s0, r, decay, k, v, a_vec, b_vec, reverse):
    xs = tuple(jnp.moveaxis(t, 1, 0) for t in (r, decay, k, v, a_vec, b_vec))

    def step(s, inp):
        r_t, w_t, k_t, v_t, a_t, b_t = inp
        sa = jnp.einsum('bhvk,bhk->bhv', s, a_t)
        s = (s * w_t[:, :, None, :] + sa[..., None] * b_t[:, :, None, :]
             + v_t[..., None] * k_t[:, :, None, :])
        return s, jnp.einsum('bhvk,bhk->bhv', s, r_t)

    s_fin, ys = lax.scan(step, s0, xs, reverse=reverse)
    return s_fin, jnp.moveaxis(ys, 0, 1)


def rwkv_direction(s0, r, k, v, wd, ad, w0, w2, a0, a2, k_k, k_a, reverse):
    w_log = -jax.nn.softplus(-(w0 + jnp.tanh(wd) @ w2)) - 0.5
    decay = jnp.exp(-jnp.exp(w_log))
    iclr = jax.nn.sigmoid(a0 + ad @ a2)
    kk = split_heads(k * k_k)
    kk = kk / jnp.maximum(jnp.sqrt(jnp.sum(kk * kk, axis=-1, keepdims=True)), 1e-12)
    k_dir = split_heads(k * (1 + (iclr - 1) * k_a))
    s_fin, y = wkv_scan(s0, split_heads(r), split_heads(decay), k_dir, split_heads(v),
                        -kk, kk * split_heads(iclr), reverse)
    return s_fin, y, k_dir


def rwkv_bidir(s0_f, s0_b, pieces, decay_w0, decay_w2, iclr_a0, iclr_a2, k_k, k_a):
    r, k, v, wd_f, wd_b, ad_f, ad_b, _ = pieces
    s_f, y_f, kd_f = rwkv_direction(s0_f, r, k, v, wd_f, ad_f, decay_w0[0], decay_w2[0],
                                    iclr_a0[0], iclr_a2[0], k_k, k_a, False)
    s_b, y_b, kd_b = rwkv_direction(s0_b, r, k, v, wd_b, ad_b, decay_w0[1], decay_w2[1],
                                    iclr_a0[1], iclr_a2[1], k_k, k_a, True)
    return s_f, s_b, y_f + y_b, 0.5 * (kd_f + kd_b)


def rwkv_readout(y, k_bar, pieces, r_k, gate_w2, lnx_w, lnx_b):
    r, _, v, _, _, _, _, gd = pieces
    mu = jnp.mean(y, axis=-1, keepdims=True)
    var = jnp.mean(jnp.square(y - mu), axis=-1, keepdims=True)
    yn = ((y - mu) * lax.rsqrt(var + EPS_GN)).reshape(r.shape) * lnx_w + lnx_b
    rh = split_heads(r)
    bonus = jnp.sum(rh * k_bar * r_k, axis=-1, keepdims=True) * split_heads(v)
    g = jax.nn.sigmoid(gd) @ gate_w2
    return (yn + bonus.reshape(r.shape)) * g


def conformer_conv(pcv, n_lines, line_len, conv_w, conv_b, ln_w, ln_b):
    u = pcv[..., :CONV_WIDTH] * jax.nn.sigmoid(pcv[..., CONV_WIDTH:])
    bsz, n_tok, ch = u.shape
    lines = u.reshape(bsz * n_lines, line_len, ch)
    pad = CONV_KERNEL // 2
    y = lax.conv_general_dilated(lines, conv_w[:, None, :].astype(lines.dtype), (1,),
                                 [(pad, pad)], dimension_numbers=('NWC', 'WIO', 'NWC'),
                                 feature_group_count=ch)
    yf = (y.reshape(bsz, n_tok, ch) + conv_b).astype(jnp.float32)
    mu = jnp.mean(yf, axis=-1, keepdims=True)
    var = jnp.mean(jnp.square(yf - mu), axis=-1, keepdims=True)
    yn = (yf - mu) * lax.rsqrt(var + EPS_LN) * ln_w + ln_b
    return jax.nn.silu(yn).astype(pcv.dtype)


def sqrelu_mlp(h, w1, w2):
    return jnp.square(jax.nn.relu(h @ w1)) @ w2


def setup_inputs(seed: int = 0) -> dict:
    key = jax.random.key(seed)
    ks = jax.random.split(key, 32)
    L, D, W, CW = DEPTH, D_MODEL, RWKV_WIDTH, CONV_WIDTH
    nrm = lambda k, shape, s: jax.random.normal(k, shape, jnp.float32) * s
    gain = lambda k, shape: 1.0 + nrm(k, shape, 0.02)
    return {
        "x": nrm(ks[0], (BATCH, SEQ, D), 1.0),
        "c": nrm(ks[1], (BATCH, D), 1.0),
        "ctx": nrm(ks[2], (BATCH, CTX_LEN, D), 1.0),
        "c_ctx": nrm(ks[3], (D,), 1.0),
        "ada_w": nrm(ks[4], (L, D, N_MOD * D), 0.5 * D ** -0.5),
        "ada_b": nrm(ks[5], (L, N_MOD * D), 0.02),
        "mix_pre_g": gain(ks[6], (L, D)),
        "mix_post_g": gain(ks[7], (L, D)),
        "mlp_pre_g": gain(ks[8], (L, D)),
        "mlp_post_g": gain(ks[9], (L, D)),
        "w_in": nrm(ks[10], (L, D, IN_COLS), D ** -0.5),
        "mu_prev": jax.random.uniform(ks[11], (L, SHIFT_COLS), jnp.float32, 0.0, 0.5),
        "mu_next": jax.random.uniform(ks[12], (L, SHIFT_COLS), jnp.float32, 0.0, 0.5),
        "decay_w0": jax.random.uniform(ks[13], (L, 2, W), jnp.float32, -6.0, 1.0),
        "decay_w2": nrm(ks[14], (L, 2, DECAY_RANK, W), 0.5 * DECAY_RANK ** -0.5),
        "iclr_a0": nrm(ks[15], (L, 2, W), 0.5),
        "iclr_a2": nrm(ks[16], (L, 2, ICLR_RANK, W), 0.5 * ICLR_RANK ** -0.5),
        "k_k": 0.85 + nrm(ks[17], (L, W), 0.02),
        "k_a": 1.0 + nrm(ks[18], (L, W), 0.02),
        "r_k": nrm(ks[19], (L, RWKV_HEADS, HEAD_SIZE), 0.1),
        "gate_w2": nrm(ks[20], (L, GATE_RANK, W), GATE_RANK ** -0.5),
        "lnx_w": gain(ks[21], (L, W)),
        "lnx_b": nrm(ks[22], (L, W), 0.02),
        "conv_w": nrm(ks[23], (L, CONV_KERNEL, CW), CONV_KERNEL ** -0.5),
        "conv_b": nrm(ks[24], (L, CW), 0.02),
        "conv_ln_w": gain(ks[25], (L, CW)),
        "conv_ln_b": nrm(ks[26], (L, CW), 0.02),
        "w_out": nrm(ks[27], (L, MIX_WIDTH, D), MIX_WIDTH ** -0.5),
        "mlp_w1": nrm(ks[28], (L, D, D_FF), D ** -0.5),
        "mlp_w2": nrm(ks[29], (L, D_FF, D), D_FF ** -0.5),
    }


def reference(x, c, ctx, c_ctx, ada_w, ada_b, mix_pre_g, mix_post_g, mlp_pre_g, mlp_post_g,
              w_in, mu_prev, mu_next, decay_w0, decay_w2, iclr_a0, iclr_a2, k_k, k_a, r_k,
              gate_w2, lnx_w, lnx_b, conv_w, conv_b, conv_ln_w, conv_ln_b, w_out,
              mlp_w1, mlp_w2):
    n_rows = x.shape[1] // GRID_W
    s_zero = jnp.zeros((x.shape[0], RWKV_HEADS, HEAD_SIZE, HEAD_SIZE), jnp.float32)
    for l in range(DEPTH):
        update_ctx = l + 1 < DEPTH
        mod_x = jnp.split((jax.nn.silu(c) @ ada_w[l] + ada_b[l])[:, None, :], N_MOD, axis=-1)
        mod_c = jnp.split(jax.nn.silu(c_ctx) @ ada_w[l] + ada_b[l], N_MOD, axis=-1)
        rw_params = (decay_w0[l], decay_w2[l], iclr_a0[l], iclr_a2[l], k_k[l], k_a[l])
        ro_params = (r_k[l], gate_w2[l], lnx_w[l], lnx_b[l])
        cv_params = (conv_w[l], conv_b[l], conv_ln_w[l], conv_ln_b[l])

        hx = modulate(rms_norm(x, mix_pre_g[l]), mod_x[0], mod_x[1])
        hc = modulate(rms_norm(ctx, mix_pre_g[l]), mod_c[0], mod_c[1])
        px, cvx = project_stream(hx, w_in[l], mu_prev[l], mu_next[l])
        pc, cvc = project_stream(hc, w_in[l], mu_prev[l], mu_next[l])
        s_f_c, s_b_c, y_c, kbar_c = rwkv_bidir(s_zero, s_zero, pc, *rw_params)
        _, _, y_x, kbar_x = rwkv_bidir(s_f_c, s_b_c, px, *rw_params)
        mix_x = jnp.concatenate(
            [rwkv_readout(y_x, kbar_x, px, *ro_params).astype(x.dtype),
             conformer_conv(cvx, n_rows, GRID_W, *cv_params)], axis=-1) @ w_out[l]
        x = x + mod_x[2] * rms_norm(mix_x, mix_post_g[l])
        if update_ctx:
            mix_c = jnp.concatenate(
                [rwkv_readout(y_c, kbar_c, pc, *ro_params).astype(ctx.dtype),
                 conformer_conv(cvc, 1, ctx.shape[1], *cv_params)], axis=-1) @ w_out[l]
            ctx = ctx + mod_c[2] * rms_norm(mix_c, mix_post_g[l])

        hx = modulate(rms_norm(x, mlp_pre_g[l]), mod_x[3], mod_x[4])
        x = x + mod_x[5] * rms_norm(sqrelu_mlp(hx, mlp_w1[l], mlp_w2[l]), mlp_post_g[l])
        if update_ctx:
            hc = modulate(rms_norm(ctx, mlp_pre_g[l]), mod_c[3], mod_c[4])
            ctx = ctx + mod_c[5] * rms_norm(sqrelu_mlp(hc, mlp_w1[l], mlp_w2[l]), mlp_post_g[l])
    return x
```

```python
import functools

import jax
import jax.numpy as jnp
from jax import lax
from jax.experimental import pallas as pl
from jax.experimental.pallas import tpu as pltpu

F32 = jnp.float32
BF16 = jnp.bfloat16

D_MODEL = 1024
RWKV_WIDTH = 512
CONV_WIDTH = 512
HEAD_SIZE = 64
RWKV_HEADS = 8
DECAY_RANK = 64
ICLR_RANK = 64
GATE_RANK = 160
CONV_KERNEL = 31
CONV_PAD = CONV_KERNEL // 2
GRID_W = 64
D_FF = 4 * D_MODEL
N_MOD = 6
EPS_RMS = 1e-6
EPS_LN = 1e-5
EPS_GN = 64e-5

LORA_OFF = 3 * RWKV_WIDTH
LORA_COLS = 256
GATE_OFF = LORA_OFF + LORA_COLS
GATE_PAD = 256
CONV_OFF = GATE_OFF + GATE_PAD
P_COLS = CONV_OFF + 2 * CONV_WIDTH
SCAN_COLS = GATE_OFF
PROJ_BLOCK = 512
N_SHIFT_BLOCKS = CONV_OFF // PROJ_BLOCK

CHUNK = 64
GROUP_HEADS = 4
GROUP_W = GROUP_HEADS * HEAD_SIZE
N_GROUPS = RWKV_HEADS // GROUP_HEADS

VMEM_LIMIT = 56 * 1024 * 1024


def _silu(x):
    return x * jax.nn.sigmoid(x)


def _rms(x, g):
    return x * lax.rsqrt(jnp.mean(x * x, axis=-1, keepdims=True) + EPS_RMS) * g


def _dot(a, b):
    return jnp.dot(a.astype(BF16), b.astype(BF16), preferred_element_type=F32)


def _dot_nt(a, b):
    return lax.dot_general(a.astype(BF16), b.astype(BF16), (((1,), (1,)), ((), ())),
                           preferred_element_type=F32)


def _dot_tn(a, b):
    return lax.dot_general(a.astype(BF16), b.astype(BF16), (((0,), (0,)), ((), ())),
                           preferred_element_type=F32)


def _split3(x):
    hi = x.astype(BF16)
    r1 = x - hi.astype(F32)
    mid = r1.astype(BF16)
    lo = (r1 - mid.astype(F32)).astype(BF16)
    return hi, mid, lo


def _dot_exact_rhs(a_bf16, x):
    hi, mid, lo = _split3(x)
    d = lambda p: jnp.dot(a_bf16, p, preferred_element_type=F32)
    return d(hi) + d(mid) + d(lo)


def _head_sum(z, head_ones):
    hi = z.astype(BF16)
    lo = (z - hi.astype(F32)).astype(BF16)
    d = lambda p: jnp.dot(p, head_ones, preferred_element_type=F32)
    return d(hi) + d(lo)


def _ada_kernel(c_ref, w_ref, b_ref, o_ref):
    o_ref[...] = _dot(_silu(c_ref[...]), w_ref[...]) + b_ref[...]


def _ada_call(cc, ada_w, ada_b):
    rows = cc.shape[0]
    n_out = ada_w.shape[1]
    blk = 1536
    return pl.pallas_call(
        _ada_kernel,
        grid=(n_out // blk,),
        in_specs=[pl.BlockSpec((rows, D_MODEL), lambda n: (0, 0)),
                  pl.BlockSpec((D_MODEL, blk), lambda n: (0, n)),
                  pl.BlockSpec((1, blk), lambda n: (0, n))],
        out_specs=pl.BlockSpec((rows, blk), lambda n: (0, n)),
        out_shape=jax.ShapeDtypeStruct((rows, n_out), F32),
        compiler_params=pltpu.CompilerParams(vmem_limit_bytes=VMEM_LIMIT),
        name="ada_mod",
    )(cc, ada_w, ada_b)


def _proj_kernel(x_ref, mod_ref, g_ref, w_ref, mp_ref, mn_ref, o_ref, h_scr, z_scr, *, seq):
    n = pl.program_id(1)
    row_chunk = min(seq, 512)

    @pl.when(n == 0)
    def _():
        shift = mod_ref[0, 0:1, :]
        scale = mod_ref[0, 1:2, :]
        g = g_ref[...]
        for s in range(0, seq, row_chunk):
            xs = x_ref[0, s:s + row_chunk, :]
            h_scr[s:s + row_chunk, :] = (_rms(xs, g) * (1.0 + scale) + shift).astype(BF16)

    @pl.when(n < N_SHIFT_BLOCKS)
    def _():
        z_scr[0:8, :] = jnp.zeros((8, PROJ_BLOCK), F32)
        z_scr[8 + seq:16 + seq, :] = jnp.zeros((8, PROJ_BLOCK), F32)
        for s in range(0, seq, row_chunk):
            z_scr[8 + s:8 + s + row_chunk, :] = jnp.dot(
                h_scr[s:s + row_chunk, :], w_ref[...], preferred_element_type=F32)
        mp = mp_ref[...]
        mn = mn_ref[...]
        for s in range(0, seq, row_chunk):
            cur = z_scr[8 + s:8 + s + row_chunk, :]
            prv = z_scr[7 + s:7 + s + row_chunk, :]
            nxt = z_scr[9 + s:9 + s + row_chunk, :]
            o_ref[0, s:s + row_chunk, :] = cur + mp * (prv - cur) + mn * (nxt - cur)

    @pl.when(n >= N_SHIFT_BLOCKS)
    def _():
        for s in range(0, seq, row_chunk):
            o_ref[0, s:s + row_chunk, :] = jnp.dot(
                h_scr[s:s + row_chunk, :], w_ref[...], preferred_element_type=F32)


def _proj_call(x, mods, g, w_pad, mp_pad, mn_pad, n_blocks):
    bsz, seq, _ = x.shape
    n_cols = n_blocks * PROJ_BLOCK
    return pl.pallas_call(
        functools.partial(_proj_kernel, seq=seq),
        grid=(bsz, n_blocks),
        in_specs=[pl.BlockSpec((1, seq, D_MODEL), lambda b, n: (b, 0, 0)),
                  pl.BlockSpec((1, N_MOD, D_MODEL), lambda b, n: (b, 0, 0)),
                  pl.BlockSpec((1, D_MODEL), lambda b, n: (0, 0)),
                  pl.BlockSpec((D_MODEL, PROJ_BLOCK), lambda b, n: (0, n)),
                  pl.BlockSpec((1, PROJ_BLOCK), lambda b, n: (0, n)),
                  pl.BlockSpec((1, PROJ_BLOCK), lambda b, n: (0, n))],
        out_specs=pl.BlockSpec((1, seq, PROJ_BLOCK), lambda b, n: (b, 0, n)),
        out_shape=jax.ShapeDtypeStruct((bsz, seq, n_cols), F32),
        scratch_shapes=[pltpu.VMEM((seq, D_MODEL), BF16),
                        pltpu.VMEM((seq + 16, PROJ_BLOCK), F32)],
        compiler_params=pltpu.CompilerParams(
            dimension_semantics=("parallel", "arbitrary"), vmem_limit_bytes=VMEM_LIMIT),
        name="in_proj",
    )(x, mods, g, w_pad, mp_pad, mn_pad)


def _expand(x, bd_mask):
    return jnp.where(bd_mask, jnp.concatenate([x] * GROUP_HEADS, axis=0), 0.0)


def _scan_direction(pb, s_ref, d, lora_w, vecs, head_ones, reverse):
    L = CHUNK
    r = pb[:, 0:RWKV_WIDTH]
    k = pb[:, RWKV_WIDTH:2 * RWKV_WIDTH]
    v = pb[:, 2 * RWKV_WIDTH:3 * RWKV_WIDTH]
    lora = pb[:, LORA_OFF:LORA_OFF + LORA_COLS]
    lane = lax.broadcasted_iota(jnp.int32, lora.shape, 1)
    lora = jnp.where(lane < 2 * DECAY_RANK, jnp.tanh(lora), lora)
    lin = _dot(lora, lora_w)
    w0 = vecs[0:1, :]
    a0 = vecs[1:2, :]
    k_k = vecs[2:3, :]
    k_a = vecs[3:4, :]
    zw = -(lin[:, 0:RWKV_WIDTH] + w0)
    softplus = jnp.maximum(zw, 0.0) + jnp.log1p(jnp.exp(-jnp.abs(zw)))
    lw = -jnp.exp(-softplus - 0.5)
    iclr = jax.nn.sigmoid(lin[:, RWKV_WIDTH:] + a0)
    kk0 = k * k_k
    kk = kk0 * lax.rsqrt(jnp.maximum(_head_sum(kk0 * kk0, head_ones), 1e-24))
    kd = k * (1.0 + (iclr - 1.0) * k_a)
    a = -kk
    b = kk * iclr

    row = lax.broadcasted_iota(jnp.int32, (L, L), 0)
    col = lax.broadcasted_iota(jnp.int32, (L, L), 1)
    tri = jnp.where((col >= row) if reverse else (col <= row), 1.0, 0.0).astype(BF16)
    cum = _dot_exact_rhs(tri, lw)
    tot = cum[0:1, :] if reverse else cum[L - 1:L, :]
    e_incl = jnp.exp(cum)
    e_excl = jnp.exp(cum - lw)
    e_neg = jnp.exp(-cum)
    e_rem = jnp.exp(tot - cum)
    w_tot = jnp.exp(tot)
    r_t = r * e_incl
    a_t = a * e_excl
    b_t = b * e_neg
    k_t = kd * e_neg
    b_h = b * e_rem
    k_h = kd * e_rem

    rr = lax.broadcasted_iota(jnp.int32, (GROUP_W, GROUP_W), 0)
    cc = lax.broadcasted_iota(jnp.int32, (GROUP_W, GROUP_W), 1)
    bd_mask = (rr // HEAD_SIZE) == (cc // HEAD_SIZE)
    tr = lax.broadcasted_iota(jnp.int32, (L, GROUP_W), 0)
    tc = lax.broadcasted_iota(jnp.int32, (L, GROUP_W), 1) % L
    strict = (tc > tr) if reverse else (tc < tr)
    incl = (tc >= tr) if reverse else (tc <= tr)
    eye = jnp.where(tc == tr, 1.0, 0.0)

    ys = []
    for g in range(N_GROUPS):
        sl = slice(g * GROUP_W, (g + 1) * GROUP_W)
        lhs = jnp.concatenate([a_t[:, sl], r_t[:, sl]], axis=0)
        e_bk = jnp.concatenate([_expand(b_t[:, sl], bd_mask),
                                _expand(k_t[:, sl], bd_mask)], axis=0)
        a_all = _dot_nt(lhs, e_bk)
        a_ab = jnp.where(strict, a_all[0:L, 0:GROUP_W], 0.0)
        a_ak = jnp.where(strict, a_all[0:L, GROUP_W:], 0.0)
        a_rb = jnp.where(incl, a_all[L:, 0:GROUP_W], 0.0)
        a_rk = jnp.where(incl, a_all[L:, GROUP_W:], 0.0)

        npow = a_ab
        pinv = eye + npow
        for _ in range(5):
            npow = _dot(npow, _expand(npow, bd_mask))
            pinv = pinv + _dot(pinv, _expand(npow, bd_mask))

        s_old = s_ref[d, g]
        sa = _dot_nt(lhs, s_old)
        v_g = v[:, sl]
        e_v = _expand(v_g, bd_mask)
        x_g = sa[0:L] + _dot(a_ak, e_v)
        u_g = _dot(pinv, _expand(x_g, bd_mask))
        y_g = sa[L:] + _dot(jnp.concatenate([a_rb, a_rk], axis=1),
                            jnp.concatenate([_expand(u_g, bd_mask), e_v], axis=0))
        upd = _dot_tn(jnp.concatenate([u_g, v_g], axis=0),
                      jnp.concatenate([b_h[:, sl], k_h[:, sl]], axis=0))
        s_ref[d, g] = s_old * w_tot[:, sl] + jnp.where(bd_mask, upd, 0.0)
        ys.append(y_g)
    return jnp.concatenate(ys, axis=1)


def _scan_kernel(pf_ref, pb_ref, s0_ref, lw_ref, vec_ref, ho_ref, yf_ref, yb_ref, sfin_ref, s_scr):
    c = pl.program_id(1)

    @pl.when(c == 0)
    def _():
        s_scr[...] = s0_ref[0]

    head_ones = ho_ref[...]
    yf_ref[0] = _scan_direction(pf_ref[0], s_scr, 0, lw_ref[0], vec_ref[0], head_ones, False)
    yb_ref[0] = _scan_direction(pb_ref[0], s_scr, 1, lw_ref[1], vec_ref[1], head_ones, True)

    @pl.when(c == pl.num_programs(1) - 1)
    def _():
        sfin_ref[0] = s_scr[...]


def _scan_call(p, s0, lora_w, vecs, head_ones):
    bsz, seq, _ = p.shape
    nc = seq // CHUNK
    state_block = (1, 2, N_GROUPS, GROUP_W, GROUP_W)
    state_map = lambda b, c: (b, 0, 0, 0, 0)
    return pl.pallas_call(
        _scan_kernel,
        grid=(bsz, nc),
        in_specs=[pl.BlockSpec((1, CHUNK, SCAN_COLS), lambda b, c: (b, c, 0)),
                  pl.BlockSpec((1, CHUNK, SCAN_COLS), lambda b, c: (b, nc - 1 - c, 0)),
                  pl.BlockSpec(state_block, state_map),
                  pl.BlockSpec((2, LORA_COLS, 2 * RWKV_WIDTH), lambda b, c: (0, 0, 0)),
                  pl.BlockSpec((2, 8, RWKV_WIDTH), lambda b, c: (0, 0, 0)),
                  pl.BlockSpec((RWKV_WIDTH, RWKV_WIDTH), lambda b, c: (0, 0))],
        out_specs=[pl.BlockSpec((1, CHUNK, RWKV_WIDTH), lambda b, c: (b, c, 0)),
                   pl.BlockSpec((1, CHUNK, RWKV_WIDTH), lambda b, c: (b, nc - 1 - c, 0)),
                   pl.BlockSpec(state_block, state_map)],
        out_shape=[jax.ShapeDtypeStruct((bsz, seq, RWKV_WIDTH), F32),
                   jax.ShapeDtypeStruct((bsz, seq, RWKV_WIDTH), F32),
                   jax.ShapeDtypeStruct((bsz, 2, N_GROUPS, GROUP_W, GROUP_W), F32)],
        scratch_shapes=[pltpu.VMEM((2, N_GROUPS, GROUP_W, GROUP_W), F32)],
        compiler_params=pltpu.CompilerParams(
            dimension_semantics=("parallel", "arbitrary"), vmem_limit_bytes=VMEM_LIMIT),
        name="wkv_scan",
    )(p, p, s0, lora_w, vecs, head_ones)


def _mix_kernel(x_ref, yf_ref, yb_ref, rkv_ref, sm_ref, cv_ref, mod_ref, rw_ref, vec_ref, ho_ref,
                cw_ref, cvec_ref, wout_ref, g_ref, o_ref, upad, *, tm):
    n_lines = tm // GRID_W
    head_ones = ho_ref[...]
    y = yf_ref[0] + yb_ref[0]
    inv_n = 1.0 / HEAD_SIZE
    mu = _head_sum(y, head_ones) * inv_n
    dy = y - mu
    var = _head_sum(dy * dy, head_ones) * inv_n
    lnx_w = vec_ref[0:1, :]
    lnx_b = vec_ref[1:2, :]
    a0_f = vec_ref[2:3, :]
    a0_b = vec_ref[3:4, :]
    k_a = vec_ref[4:5, :]
    r_k = vec_ref[5:6, :]
    yn = dy * lax.rsqrt(var + EPS_GN) * lnx_w + lnx_b
    sm = sm_ref[0]
    lane = lax.broadcasted_iota(jnp.int32, sm.shape, 1)
    sm = jnp.where(lane >= LORA_COLS, jax.nn.sigmoid(sm), sm)
    lin = _dot(sm, rw_ref[...])
    iclr_f = jax.nn.sigmoid(lin[:, 0:RWKV_WIDTH] + a0_f)
    iclr_b = jax.nn.sigmoid(lin[:, RWKV_WIDTH:2 * RWKV_WIDTH] + a0_b)
    gate = lin[:, 2 * RWKV_WIDTH:]
    r = rkv_ref[0, :, 0:RWKV_WIDTH]
    k = rkv_ref[0, :, RWKV_WIDTH:2 * RWKV_WIDTH]
    v = rkv_ref[0, :, 2 * RWKV_WIDTH:3 * RWKV_WIDTH]
    kd_f = k * (1.0 + (iclr_f - 1.0) * k_a)
    kd_b = k * (1.0 + (iclr_b - 1.0) * k_a)
    k_bar = 0.5 * (kd_f + kd_b)
    bonus = _head_sum(r * k_bar * r_k, head_ones) * v
    rw_out = (yn + bonus) * gate

    cv = cv_ref[0]
    u = cv[:, 0:CONV_WIDTH] * jax.nn.sigmoid(cv[:, CONV_WIDTH:])
    zeros = jnp.zeros((16, CONV_WIDTH), F32)
    for ln in range(n_lines):
        upad[ln, 0:16, :] = zeros
        upad[ln, 16:16 + GRID_W, :] = u[ln * GRID_W:(ln + 1) * GRID_W, :]
        upad[ln, 16 + GRID_W:32 + GRID_W, :] = zeros
    conv_b = cvec_ref[0:1, :]
    ln_w = cvec_ref[1:2, :]
    ln_b = cvec_ref[2:3, :]
    conv_lines = []
    for ln in range(n_lines):
        acc = jnp.zeros((GRID_W, CONV_WIDTH), F32)
        for j in range(CONV_KERNEL):
            off = 16 - CONV_PAD + j
            acc = acc + cw_ref[j:j + 1, :] * upad[ln, off:off + GRID_W, :]
        conv_lines.append(acc)
    yc = jnp.concatenate(conv_lines, axis=0) + conv_b
    mu_c = jnp.mean(yc, axis=-1, keepdims=True)
    dc = yc - mu_c
    var_c = jnp.mean(dc * dc, axis=-1, keepdims=True)
    conv_out = _silu(dc * lax.rsqrt(var_c + EPS_LN) * ln_w + ln_b)

    mix_in = jnp.concatenate([rw_out.astype(BF16), conv_out.astype(BF16)], axis=1)
    mix = jnp.dot(mix_in, wout_ref[...], preferred_element_type=F32)
    gate_mix = mod_ref[0, 2:3, :]
    o_ref[0] = x_ref[0] + gate_mix * _rms(mix, g_ref[...])


def _mix_call(x, y_f, y_b, p, mods, rw_w, rvecs, head_ones, conv_w, cvecs, w_out, post_g):
    bsz, seq, _ = x.shape
    tm = 512
    tok = lambda width, blk: pl.BlockSpec((1, tm, width), lambda b, t: (b, t, blk))
    const2 = lambda shape: pl.BlockSpec(shape, lambda b, t: (0, 0))
    return pl.pallas_call(
        functools.partial(_mix_kernel, tm=tm),
        grid=(bsz, seq // tm),
        in_specs=[tok(D_MODEL, 0), tok(RWKV_WIDTH, 0), tok(RWKV_WIDTH, 0),
                  tok(3 * RWKV_WIDTH, 0),
                  tok(512, LORA_OFF // 512),
                  tok(2 * CONV_WIDTH, CONV_OFF // (2 * CONV_WIDTH)),
                  pl.BlockSpec((1, N_MOD, D_MODEL), lambda b, t: (b, 0, 0)),
                  const2((512, 3 * RWKV_WIDTH)), const2((8, RWKV_WIDTH)),
                  const2((RWKV_WIDTH, RWKV_WIDTH)), const2((32, CONV_WIDTH)),
                  const2((8, CONV_WIDTH)), const2((D_MODEL, D_MODEL)), const2((1, D_MODEL))],
        out_specs=tok(D_MODEL, 0),
        out_shape=jax.ShapeDtypeStruct((bsz, seq, D_MODEL), F32),
        scratch_shapes=[pltpu.VMEM((tm // GRID_W, GRID_W + 32, CONV_WIDTH), F32)],
        compiler_params=pltpu.CompilerParams(
            dimension_semantics=("parallel", "parallel"), vmem_limit_bytes=VMEM_LIMIT),
        name="mix_out",
    )(x, y_f, y_b, p, p, p, mods, rw_w, rvecs, head_ones, conv_w, cvecs, w_out, post_g)


def _mlp_kernel(x_ref, mod_ref, gpre_ref, gpost_ref, w1_ref, w2_ref, o_ref, acc_ref):
    x = x_ref[0]
    shift = mod_ref[0, 3:4, :]
    scale = mod_ref[0, 4:5, :]
    gate = mod_ref[0, 5:6, :]
    h = (_rms(x, gpre_ref[...]) * (1.0 + scale) + shift).astype(BF16)
    ff_blk = 1024
    for c in range(D_FF // ff_blk):
        hid = jnp.dot(h, w1_ref[:, c * ff_blk:(c + 1) * ff_blk], preferred_element_type=F32)
        hid = jnp.square(jnp.maximum(hid, 0.0)).astype(BF16)
        part = jnp.dot(hid, w2_ref[c * ff_blk:(c + 1) * ff_blk, :], preferred_element_type=F32)
        if c == 0:
            acc_ref[...] = part
        else:
            acc_ref[...] += part
    o_ref[0] = x + gate * _rms(acc_ref[...], gpost_ref[...])


def _mlp_call(x, mods, pre_g, post_g, w1, w2):
    bsz, seq, _ = x.shape
    tm = 512
    return pl.pallas_call(
        _mlp_kernel,
        grid=(bsz, seq // tm),
        in_specs=[pl.BlockSpec((1, tm, D_MODEL), lambda b, t: (b, t, 0)),
                  pl.BlockSpec((1, N_MOD, D_MODEL), lambda b, t: (b, 0, 0)),
                  pl.BlockSpec((1, D_MODEL), lambda b, t: (0, 0)),
                  pl.BlockSpec((1, D_MODEL), lambda b, t: (0, 0)),
                  pl.BlockSpec((D_MODEL, D_FF), lambda b, t: (0, 0)),
                  pl.BlockSpec((D_FF, D_MODEL), lambda b, t: (0, 0))],
        out_specs=pl.BlockSpec((1, tm, D_MODEL), lambda b, t: (b, t, 0)),
        out_shape=jax.ShapeDtypeStruct((bsz, seq, D_MODEL), F32),
        scratch_shapes=[pltpu.VMEM((tm, D_MODEL), F32)],
        compiler_params=pltpu.CompilerParams(
            dimension_semantics=("parallel", "parallel"), vmem_limit_bytes=VMEM_LIMIT),
        name="sqrelu_mlp",
    )(x, mods, pre_g, post_g, w1, w2)


def _pad_cols(a, lo, hi):
    pad = jnp.zeros(a.shape[:-1] + (GATE_PAD - GATE_RANK,), a.dtype)
    return jnp.concatenate([a[..., :lo], pad, a[..., lo:hi]], axis=-1)


def _layer(x, ctx, mods_x, mods_c, prm):
    (mix_pre_g, mix_post_g, mlp_pre_g, mlp_post_g, w_in, mu_prev, mu_next, decay_w0, decay_w2,
     iclr_a0, iclr_a2, k_k, k_a, r_k, gate_w2, lnx_w, lnx_b, conv_w, conv_b, conv_ln_w,
     conv_ln_b, w_out, mlp_w1, mlp_w2) = prm
    bsz = x.shape[0]
    shift_cols = GATE_OFF + GATE_RANK
    in_cols = w_in.shape[1]

    w_pad = _pad_cols(w_in, shift_cols, in_cols).astype(BF16)
    zc = jnp.zeros((2 * CONV_WIDTH,), F32)
    mp_pad = _pad_cols(jnp.concatenate([mu_prev, zc]), shift_cols, in_cols)[None, :]
    mn_pad = _pad_cols(jnp.concatenate([mu_next, zc]), shift_cols, in_cols)[None, :]
    zr = jnp.zeros((DECAY_RANK, RWKV_WIDTH), F32)
    lora_dir = []
    for d in range(2):
        dec_rows = [zr, zr]
        dec_rows[d] = decay_w2[d]
        icl_rows = [zr, zr]
        icl_rows[d] = iclr_a2[d]
        left = jnp.concatenate(dec_rows + [zr, zr], axis=0)
        right = jnp.concatenate([zr, zr] + icl_rows, axis=0)
        lora_dir.append(jnp.concatenate([left, right], axis=1))
    lora_w = jnp.stack(lora_dir).astype(BF16)
    z4 = jnp.zeros((4, RWKV_WIDTH), F32)
    scan_vecs = jnp.stack([jnp.concatenate([decay_w0[d][None], iclr_a0[d][None], k_k[None],
                                            k_a[None], z4], axis=0) for d in range(2)])
    hid = jnp.arange(RWKV_WIDTH) // HEAD_SIZE
    head_ones = (hid[:, None] == hid[None, :]).astype(BF16)
    zl = jnp.zeros((2 * DECAY_RANK, 3 * RWKV_WIDTH), F32)
    za = jnp.zeros((ICLR_RANK, RWKV_WIDTH), F32)
    zg = jnp.zeros((GATE_PAD, RWKV_WIDTH), F32)
    gate_pad = jnp.concatenate([gate_w2, jnp.zeros((GATE_PAD - GATE_RANK, RWKV_WIDTH), F32)], axis=0)
    rw_w = jnp.concatenate([
        zl,
        jnp.concatenate([iclr_a2[0], za, za], axis=1),
        jnp.concatenate([za, iclr_a2[1], za], axis=1),
        jnp.concatenate([zg, zg, gate_pad], axis=1)], axis=0).astype(BF16)
    rvecs = jnp.concatenate([lnx_w[None], lnx_b[None], iclr_a0[0][None], iclr_a0[1][None],
                             k_a[None], r_k.reshape(1, RWKV_WIDTH), jnp.zeros((2, RWKV_WIDTH), F32)],
                            axis=0)
    conv_w_pad = jnp.concatenate([conv_w, jnp.zeros((1, CONV_WIDTH), F32)], axis=0)
    cvecs = jnp.concatenate([conv_b[None], conv_ln_w[None], conv_ln_b[None],
                             jnp.zeros((5, CONV_WIDTH), F32)], axis=0)

    p_x = _proj_call(x, mods_x, mix_pre_g[None], w_pad, mp_pad, mn_pad, P_COLS // PROJ_BLOCK)
    p_c = _proj_call(ctx, mods_c, mix_pre_g[None], w_pad, mp_pad, mn_pad, N_SHIFT_BLOCKS)
    s_zero = jnp.zeros((bsz, 2, N_GROUPS, GROUP_W, GROUP_W), F32)
    _, _, s_ctx = _scan_call(p_c, s_zero, lora_w, scan_vecs, head_ones)
    y_f, y_b, _ = _scan_call(p_x, s_ctx, lora_w, scan_vecs, head_ones)
    x = _mix_call(x, y_f, y_b, p_x, mods_x, rw_w, rvecs, head_ones, conv_w_pad, cvecs,
                  w_out.astype(BF16), mix_post_g[None])
    x = _mlp_call(x, mods_x, mlp_pre_g[None], mlp_post_g[None], mlp_w1.astype(BF16),
                  mlp_w2.astype(BF16))
    return x


def kernel(x, c, ctx, c_ctx, ada_w, ada_b, mix_pre_g, mix_post_g, mlp_pre_g, mlp_post_g, w_in, mu_prev, mu_next, decay_w0, decay_w2, iclr_a0, iclr_a2, k_k, k_a, r_k, gate_w2, lnx_w, lnx_b, conv_w, conv_b, conv_ln_w, conv_ln_b, w_out, mlp_w1, mlp_w2):
    depth = ada_w.shape[0]
    assert depth == 1, "context-stream update between layers is not implemented"
    bsz = x.shape[0]
    rows = 8 * ((bsz + 1 + 7) // 8)
    cc = jnp.concatenate([c, c_ctx[None, :], jnp.zeros((rows - bsz - 1, D_MODEL), F32)], axis=0)
    per_layer = (mix_pre_g, mix_post_g, mlp_pre_g, mlp_post_g, w_in, mu_prev, mu_next, decay_w0,
                 decay_w2, iclr_a0, iclr_a2, k_k, k_a, r_k, gate_w2, lnx_w, lnx_b, conv_w, conv_b,
                 conv_ln_w, conv_ln_b, w_out, mlp_w1, mlp_w2)
    for l in range(depth):
        mods = _ada_call(cc, ada_w[l], ada_b[l][None, :]).reshape(rows, N_MOD, D_MODEL)
        mods_x = mods[:bsz]
        mods_c = jnp.broadcast_to(mods[bsz:bsz + 1], (bsz, N_MOD, D_MODEL))
        x = _layer(x, ctx, mods_x, mods_c, tuple(a[l] for a in per_layer))
    return x
```

```python
import functools

import jax
import jax.numpy as jnp
from jax import lax
from jax.experimental import pallas as pl
from jax.experimental.pallas import tpu as pltpu

F32 = jnp.float32
BF16 = jnp.bfloat16

D_MODEL = 1024
RWKV_WIDTH = 512
CONV_WIDTH = 512
HEAD_SIZE = 64
RWKV_HEADS = 8
DECAY_RANK = 64
ICLR_RANK = 64
GATE_RANK = 160
CONV_KERNEL = 31
CONV_PAD = CONV_KERNEL // 2
GRID_W = 64
D_FF = 4 * D_MODEL
N_MOD = 6
EPS_RMS = 1e-6
EPS_LN = 1e-5
EPS_GN = 64e-5

LORA_OFF = 3 * RWKV_WIDTH
LORA_COLS = 256
GATE_OFF = LORA_OFF + LORA_COLS
GATE_PAD = 256
CONV_OFF = GATE_OFF + GATE_PAD
P_COLS = CONV_OFF + 2 * CONV_WIDTH
SCAN_COLS = GATE_OFF
PROJ_BLOCK = 512
N_SHIFT_BLOCKS = CONV_OFF // PROJ_BLOCK

CHUNK = 64
GROUP_HEADS = 4
GROUP_W = GROUP_HEADS * HEAD_SIZE
N_GROUPS = RWKV_HEADS // GROUP_HEADS

VMEM_LIMIT = 56 * 1024 * 1024


def _silu(x):
    return x * jax.nn.sigmoid(x)


def _rms(x, g):
    return x * lax.rsqrt(jnp.mean(x * x, axis=-1, keepdims=True) + EPS_RMS) * g


def _dot(a, b):
    return jnp.dot(a.astype(BF16), b.astype(BF16), preferred_element_type=F32)


def _dot_nt(a, b):
    return lax.dot_general(a.astype(BF16), b.astype(BF16), (((1,), (1,)), ((), ())),
                           preferred_element_type=F32)


def _dot_tn(a, b):
    return lax.dot_general(a.astype(BF16), b.astype(BF16), (((0,), (0,)), ((), ())),
                           preferred_element_type=F32)


def _split3(x):
    hi = x.astype(BF16)
    r1 = x - hi.astype(F32)
    mid = r1.astype(BF16)
    lo = (r1 - mid.astype(F32)).astype(BF16)
    return hi, mid, lo


def _dot_exact_rhs(a_bf16, x):
    hi, mid, lo = _split3(x)
    d = lambda p: jnp.dot(a_bf16, p, preferred_element_type=F32)
    return d(hi) + d(mid) + d(lo)


def _head_sum(z, head_ones):
    hi = z.astype(BF16)
    lo = (z - hi.astype(F32)).astype(BF16)
    d = lambda p: jnp.dot(p, head_ones, preferred_element_type=F32)
    return d(hi) + d(lo)


def _ada_kernel(c_ref, w_ref, b_ref, o_ref):
    o_ref[...] = _dot(_silu(c_ref[...]), w_ref[...]) + b_ref[...]


def _ada_call(cc, ada_w, ada_b):
    rows = cc.shape[0]
    n_out = ada_w.shape[1]
    blk = 1536
    return pl.pallas_call(
        _ada_kernel,
        grid=(n_out // blk,),
        in_specs=[pl.BlockSpec((rows, D_MODEL), lambda n: (0, 0)),
                  pl.BlockSpec((D_MODEL, blk), lambda n: (0, n)),
                  pl.BlockSpec((1, blk), lambda n: (0, n))],
        out_specs=pl.BlockSpec((rows, blk), lambda n: (0, n)),
        out_shape=jax.ShapeDtypeStruct((rows, n_out), F32),
        compiler_params=pltpu.CompilerParams(vmem_limit_bytes=VMEM_LIMIT),
        name="ada_mod",
    )(cc, ada_w, ada_b)


def _proj_kernel(x_ref, mod_ref, g_ref, w_ref, mp_ref, mn_ref, o_ref, h_scr, z_scr, *, seq):
    n = pl.program_id(1)
    row_chunk = min(seq, 512)

    @pl.when(n == 0)
    def _():
        shift = mod_ref[0, 0:1, :]
        scale = mod_ref[0, 1:2, :]
        g = g_ref[...]
        for s in range(0, seq, row_chunk):
            xs = x_ref[0, s:s + row_chunk, :]
            h_scr[s:s + row_chunk, :] = (_rms(xs, g) * (1.0 + scale) + shift).astype(BF16)

    @pl.when(n < N_SHIFT_BLOCKS)
    def _():
        z_scr[0:8, :] = jnp.zeros((8, PROJ_BLOCK), F32)
        z_scr[8 + seq:16 + seq, :] = jnp.zeros((8, PROJ_BLOCK), F32)
        for s in range(0, seq, row_chunk):
            z_scr[8 + s:8 + s + row_chunk, :] = jnp.dot(
                h_scr[s:s + row_chunk, :], w_ref[...], preferred_element_type=F32)
        mp = mp_ref[...]
        mn = mn_ref[...]
        for s in range(0, seq, row_chunk):
            cur = z_scr[8 + s:8 + s + row_chunk, :]
            prv = z_scr[7 + s:7 + s + row_chunk, :]
            nxt = z_scr[9 + s:9 + s + row_chunk, :]
            o_ref[0, s:s + row_chunk, :] = cur + mp * (prv - cur) + mn * (nxt - cur)

    @pl.when(n >= N_SHIFT_BLOCKS)
    def _():
        for s in range(0, seq, row_chunk):
            o_ref[0, s:s + row_chunk, :] = jnp.dot(
                h_scr[s:s + row_chunk, :], w_ref[...], preferred_element_type=F32)


def _proj_call(x, mods, g, w_pad, mp_pad, mn_pad, n_blocks):
    bsz, seq, _ = x.shape
    n_cols = n_blocks * PROJ_BLOCK
    return pl.pallas_call(
        functools.partial(_proj_kernel, seq=seq),
        grid=(bsz, n_blocks),
        in_specs=[pl.BlockSpec((1, seq, D_MODEL), lambda b, n: (b, 0, 0)),
                  pl.BlockSpec((1, N_MOD, D_MODEL), lambda b, n: (b, 0, 0)),
                  pl.BlockSpec((1, D_MODEL), lambda b, n: (0, 0)),
                  pl.BlockSpec((D_MODEL, PROJ_BLOCK), lambda b, n: (0, n)),
                  pl.BlockSpec((1, PROJ_BLOCK), lambda b, n: (0, n)),
                  pl.BlockSpec((1, PROJ_BLOCK), lambda b, n: (0, n))],
        out_specs=pl.BlockSpec((1, seq, PROJ_BLOCK), lambda b, n: (b, 0, n)),
        out_shape=jax.ShapeDtypeStruct((bsz, seq, n_cols), F32),
        scratch_shapes=[pltpu.VMEM((seq, D_MODEL), BF16),
                        pltpu.VMEM((seq + 16, PROJ_BLOCK), F32)],
        compiler_params=pltpu.CompilerParams(
            dimension_semantics=("parallel", "arbitrary"), vmem_limit_bytes=VMEM_LIMIT),
        name="in_proj",
    )(x, mods, g, w_pad, mp_pad, mn_pad)


def _expand(x, bd_mask):
    return jnp.where(bd_mask, jnp.concatenate([x] * GROUP_HEADS, axis=0), 0.0)


def _scan_prep(pb, lora_w, vecs, head_ones, reverse):
    L = CHUNK
    r = pb[:, 0:RWKV_WIDTH]
    k = pb[:, RWKV_WIDTH:2 * RWKV_WIDTH]
    v = pb[:, 2 * RWKV_WIDTH:3 * RWKV_WIDTH]
    lora = pb[:, LORA_OFF:LORA_OFF + LORA_COLS]
    lane = lax.broadcasted_iota(jnp.int32, lora.shape, 1)
    lora = jnp.where(lane < 2 * DECAY_RANK, jnp.tanh(lora), lora)
    lin = _dot(lora, lora_w)
    w0 = vecs[0:1, :]
    a0 = vecs[1:2, :]
    k_k = vecs[2:3, :]
    k_a = vecs[3:4, :]
    zw = -(lin[:, 0:RWKV_WIDTH] + w0)
    softplus = jnp.maximum(zw, 0.0) + jnp.log1p(jnp.exp(-jnp.abs(zw)))
    lw = -jnp.exp(-softplus - 0.5)
    iclr = jax.nn.sigmoid(lin[:, RWKV_WIDTH:] + a0)
    kk0 = k * k_k
    kk = kk0 * lax.rsqrt(jnp.maximum(_head_sum(kk0 * kk0, head_ones), 1e-24))
    kd = k * (1.0 + (iclr - 1.0) * k_a)
    a = -kk
    b = kk * iclr

    row = lax.broadcasted_iota(jnp.int32, (L, L), 0)
    col = lax.broadcasted_iota(jnp.int32, (L, L), 1)
    tri = jnp.where((col >= row) if reverse else (col <= row), 1.0, 0.0).astype(BF16)
    cum = _dot_exact_rhs(tri, lw)
    tot = cum[0:1, :] if reverse else cum[L - 1:L, :]
    e_neg = jnp.exp(-cum)
    e_rem = jnp.exp(tot - cum)
    tr = lax.broadcasted_iota(jnp.int32, (L, GROUP_W), 0)
    tc = jnp.bitwise_and(lax.broadcasted_iota(jnp.int32, (L, GROUP_W), 1), L - 1)
    return dict(
        r_t=r * jnp.exp(cum), a_t=a * jnp.exp(cum - lw), b_t=b * e_neg, k_t=kd * e_neg,
        b_h=b * e_rem, k_h=kd * e_rem, v=v, w_tot=jnp.exp(tot),
        strict=(tc > tr) if reverse else (tc < tr),
        incl=(tc >= tr) if reverse else (tc <= tr))


def _scan_units(preps, s_ref):
    L = CHUNK
    rr = lax.broadcasted_iota(jnp.int32, (GROUP_W, GROUP_W), 0)
    cc = lax.broadcasted_iota(jnp.int32, (GROUP_W, GROUP_W), 1)
    bd_mask = jnp.right_shift(rr, 6) == jnp.right_shift(cc, 6)
    tr = lax.broadcasted_iota(jnp.int32, (L, GROUP_W), 0)
    tc = jnp.bitwise_and(lax.broadcasted_iota(jnp.int32, (L, GROUP_W), 1), L - 1)
    eye = jnp.where(tc == tr, 1.0, 0.0)
    units = [(d, g) for d in range(len(preps)) for g in range(N_GROUPS)]
    sl = lambda g: slice(g * GROUP_W, (g + 1) * GROUP_W)
    ex = lambda x: _expand(x, bd_mask)

    lhs, a_ab, a_ak, a_rb, a_rk = {}, {}, {}, {}, {}
    for u in units:
        d, g = u
        p = preps[d]
        lhs[u] = jnp.concatenate([p["a_t"][:, sl(g)], p["r_t"][:, sl(g)]], axis=0)
        e_bk = jnp.concatenate([ex(p["b_t"][:, sl(g)]), ex(p["k_t"][:, sl(g)])], axis=0)
        a_all = _dot_nt(lhs[u], e_bk)
        a_ab[u] = jnp.where(p["strict"], a_all[0:L, 0:GROUP_W], 0.0)
        a_ak[u] = jnp.where(p["strict"], a_all[0:L, GROUP_W:], 0.0)
        a_rb[u] = jnp.where(p["incl"], a_all[L:, 0:GROUP_W], 0.0)
        a_rk[u] = jnp.where(p["incl"], a_all[L:, GROUP_W:], 0.0)

    npow = dict(a_ab)
    pinv = {u: eye + a_ab[u] for u in units}
    for _ in range(5):
        for u in units:
            npow[u] = _dot(npow[u], ex(npow[u]))
        for u in units:
            pinv[u] = pinv[u] + _dot(pinv[u], ex(npow[u]))

    s_old, sa, e_v, x_u, u_u, y_u = {}, {}, {}, {}, {}, {}
    for u in units:
        d, g = u
        s_old[u] = s_ref[d, g]
        sa[u] = _dot_nt(lhs[u], s_old[u])
        e_v[u] = ex(preps[d]["v"][:, sl(g)])
    for u in units:
        x_u[u] = sa[u][0:L] + _dot(a_ak[u], e_v[u])
    for u in units:
        u_u[u] = _dot(pinv[u], ex(x_u[u]))
    for u in units:
        y_u[u] = sa[u][L:] + _dot(jnp.concatenate([a_rb[u], a_rk[u]], axis=1),
                                  jnp.concatenate([ex(u_u[u]), e_v[u]], axis=0))
    for u in units:
        d, g = u
        p = preps[d]
        upd = _dot_tn(jnp.concatenate([u_u[u], p["v"][:, sl(g)]], axis=0),
                      jnp.concatenate([p["b_h"][:, sl(g)], p["k_h"][:, sl(g)]], axis=0))
        s_ref[d, g] = s_old[u] * p["w_tot"][:, sl(g)] + jnp.where(bd_mask, upd, 0.0)
    return [jnp.concatenate([y_u[(d, g)] for g in range(N_GROUPS)], axis=1)
            for d in range(len(preps))]


def _scan_kernel(pf_ref, pb_ref, s0_ref, lw_ref, vec_ref, ho_ref, yf_ref, yb_ref, sfin_ref, s_scr):
    c = pl.program_id(1)

    @pl.when(c == 0)
    def _():
        s_scr[...] = s0_ref[0]

    head_ones = ho_ref[...]
    preps = [_scan_prep(pf_ref[0], lw_ref[0], vec_ref[0], head_ones, False),
             _scan_prep(pb_ref[0], lw_ref[1], vec_ref[1], head_ones, True)]
    y_f, y_b = _scan_units(preps, s_scr)
    yf_ref[0] = y_f
    yb_ref[0] = y_b

    @pl.when(c == pl.num_programs(1) - 1)
    def _():
        sfin_ref[0] = s_scr[...]


def _scan_call(p, s0, lora_w, vecs, head_ones):
    bsz, seq, _ = p.shape
    nc = seq // CHUNK
    state_block = (1, 2, N_GROUPS, GROUP_W, GROUP_W)
    state_map = lambda b, c: (b, 0, 0, 0, 0)
    return pl.pallas_call(
        _scan_kernel,
        grid=(bsz, nc),
        in_specs=[pl.BlockSpec((1, CHUNK, SCAN_COLS), lambda b, c: (b, c, 0)),
                  pl.BlockSpec((1, CHUNK, SCAN_COLS), lambda b, c: (b, nc - 1 - c, 0)),
                  pl.BlockSpec(state_block, state_map),
                  pl.BlockSpec((2, LORA_COLS, 2 * RWKV_WIDTH), lambda b, c: (0, 0, 0)),
                  pl.BlockSpec((2, 8, RWKV_WIDTH), lambda b, c: (0, 0, 0)),
                  pl.BlockSpec((RWKV_WIDTH, RWKV_WIDTH), lambda b, c: (0, 0))],
        out_specs=[pl.BlockSpec((1, CHUNK, RWKV_WIDTH), lambda b, c: (b, c, 0)),
                   pl.BlockSpec((1, CHUNK, RWKV_WIDTH), lambda b, c: (b, nc - 1 - c, 0)),
                   pl.BlockSpec(state_block, state_map)],
        out_shape=[jax.ShapeDtypeStruct((bsz, seq, RWKV_WIDTH), F32),
                   jax.ShapeDtypeStruct((bsz, seq, RWKV_WIDTH), F32),
                   jax.ShapeDtypeStruct((bsz, 2, N_GROUPS, GROUP_W, GROUP_W), F32)],
        scratch_shapes=[pltpu.VMEM((2, N_GROUPS, GROUP_W, GROUP_W), F32)],
        compiler_params=pltpu.CompilerParams(
            dimension_semantics=("parallel", "arbitrary"), vmem_limit_bytes=VMEM_LIMIT),
        name="wkv_scan",
    )(p, p, s0, lora_w, vecs, head_ones)


def _mix_kernel(x_ref, yf_ref, yb_ref, rkv_ref, sm_ref, cv_ref, mod_ref, rw_ref, vec_ref, ho_ref,
                cw_ref, cvec_ref, wout_ref, g_ref, o_ref, upad, *, tm):
    n_lines = tm // GRID_W
    head_ones = ho_ref[...]
    y = yf_ref[0] + yb_ref[0]
    inv_n = 1.0 / HEAD_SIZE
    mu = _head_sum(y, head_ones) * inv_n
    dy = y - mu
    var = _head_sum(dy * dy, head_ones) * inv_n
    lnx_w = vec_ref[0:1, :]
    lnx_b = vec_ref[1:2, :]
    a0_f = vec_ref[2:3, :]
    a0_b = vec_ref[3:4, :]
    k_a = vec_ref[4:5, :]
    r_k = vec_ref[5:6, :]
    yn = dy * lax.rsqrt(var + EPS_GN) * lnx_w + lnx_b
    sm = sm_ref[0]
    lane = lax.broadcasted_iota(jnp.int32, sm.shape, 1)
    sm = jnp.where(lane >= LORA_COLS, jax.nn.sigmoid(sm), sm)
    lin = _dot(sm, rw_ref[...])
    iclr_f = jax.nn.sigmoid(lin[:, 0:RWKV_WIDTH] + a0_f)
    iclr_b = jax.nn.sigmoid(lin[:, RWKV_WIDTH:2 * RWKV_WIDTH] + a0_b)
    gate = lin[:, 2 * RWKV_WIDTH:]
    r = rkv_ref[0, :, 0:RWKV_WIDTH]
    k = rkv_ref[0, :, RWKV_WIDTH:2 * RWKV_WIDTH]
    v = rkv_ref[0, :, 2 * RWKV_WIDTH:3 * RWKV_WIDTH]
    kd_f = k * (1.0 + (iclr_f - 1.0) * k_a)
    kd_b = k * (1.0 + (iclr_b - 1.0) * k_a)
    k_bar = 0.5 * (kd_f + kd_b)
    bonus = _head_sum(r * k_bar * r_k, head_ones) * v
    rw_out = (yn + bonus) * gate

    cv = cv_ref[0]
    u = cv[:, 0:CONV_WIDTH] * jax.nn.sigmoid(cv[:, CONV_WIDTH:])
    zeros = jnp.zeros((16, CONV_WIDTH), F32)
    for ln in range(n_lines):
        upad[ln, 0:16, :] = zeros
        upad[ln, 16:16 + GRID_W, :] = u[ln * GRID_W:(ln + 1) * GRID_W, :]
        upad[ln, 16 + GRID_W:32 + GRID_W, :] = zeros
    conv_b = cvec_ref[0:1, :]
    ln_w = cvec_ref[1:2, :]
    ln_b = cvec_ref[2:3, :]
    conv_lines = []
    for ln in range(n_lines):
        acc = jnp.zeros((GRID_W, CONV_WIDTH), F32)
        for j in range(CONV_KERNEL):
            off = 16 - CONV_PAD + j
            acc = acc + cw_ref[j:j + 1, :] * upad[ln, off:off + GRID_W, :]
        conv_lines.append(acc)
    yc = jnp.concatenate(conv_lines, axis=0) + conv_b
    mu_c = jnp.mean(yc, axis=-1, keepdims=True)
    dc = yc - mu_c
    var_c = jnp.mean(dc * dc, axis=-1, keepdims=True)
    conv_out = _silu(dc * lax.rsqrt(var_c + EPS_LN) * ln_w + ln_b)

    mix_in = jnp.concatenate([rw_out.astype(BF16), conv_out.astype(BF16)], axis=1)
    mix = jnp.dot(mix_in, wout_ref[...], preferred_element_type=F32)
    gate_mix = mod_ref[0, 2:3, :]
    o_ref[0] = x_ref[0] + gate_mix * _rms(mix, g_ref[...])


def _mix_call(x, y_f, y_b, p, mods, rw_w, rvecs, head_ones, conv_w, cvecs, w_out, post_g):
    bsz, seq, _ = x.shape
    tm = 512
    tok = lambda width, blk: pl.BlockSpec((1, tm, width), lambda b, t: (b, t, blk))
    const2 = lambda shape: pl.BlockSpec(shape, lambda b, t: (0, 0))
    return pl.pallas_call(
        functools.partial(_mix_kernel, tm=tm),
        grid=(bsz, seq // tm),
        in_specs=[tok(D_MODEL, 0), tok(RWKV_WIDTH, 0), tok(RWKV_WIDTH, 0),
                  tok(3 * RWKV_WIDTH, 0),
                  tok(512, LORA_OFF // 512),
                  tok(2 * CONV_WIDTH, CONV_OFF // (2 * CONV_WIDTH)),
                  pl.BlockSpec((1, N_MOD, D_MODEL), lambda b, t: (b, 0, 0)),
                  const2((512, 3 * RWKV_WIDTH)), const2((8, RWKV_WIDTH)),
                  const2((RWKV_WIDTH, RWKV_WIDTH)), const2((32, CONV_WIDTH)),
                  const2((8, CONV_WIDTH)), const2((D_MODEL, D_MODEL)), const2((1, D_MODEL))],
        out_specs=tok(D_MODEL, 0),
        out_shape=jax.ShapeDtypeStruct((bsz, seq, D_MODEL), F32),
        scratch_shapes=[pltpu.VMEM((tm // GRID_W, GRID_W + 32, CONV_WIDTH), F32)],
        compiler_params=pltpu.CompilerParams(
            dimension_semantics=("parallel", "parallel"), vmem_limit_bytes=VMEM_LIMIT),
        name="mix_out",
    )(x, y_f, y_b, p, p, p, mods, rw_w, rvecs, head_ones, conv_w, cvecs, w_out, post_g)


def _mlp_kernel(x_ref, mod_ref, gpre_ref, gpost_ref, w1_ref, w2_ref, o_ref, acc_ref):
    x = x_ref[0]
    shift = mod_ref[0, 3:4, :]
    scale = mod_ref[0, 4:5, :]
    gate = mod_ref[0, 5:6, :]
    h = (_rms(x, gpre_ref[...]) * (1.0 + scale) + shift).astype(BF16)
    ff_blk = 1024
    for c in range(D_FF // ff_blk):
        hid = jnp.dot(h, w1_ref[:, c * ff_blk:(c + 1) * ff_blk], preferred_element_type=F32)
        hid = jnp.square(jnp.maximum(hid, 0.0)).astype(BF16)
        part = jnp.dot(hid, w2_ref[c * ff_blk:(c + 1) * ff_blk, :], preferred_element_type=F32)
        if c == 0:
            acc_ref[...] = part
        else:
            acc_ref[...] += part
    o_ref[0] = x + gate * _rms(acc_ref[...], gpost_ref[...])


def _mlp_call(x, mods, pre_g, post_g, w1, w2):
    bsz, seq, _ = x.shape
    tm = 512
    return pl.pallas_call(
        _mlp_kernel,
        grid=(bsz, seq // tm),
        in_specs=[pl.BlockSpec((1, tm, D_MODEL), lambda b, t: (b, t, 0)),
                  pl.BlockSpec((1, N_MOD, D_MODEL), lambda b, t: (b, 0, 0)),
                  pl.BlockSpec((1, D_MODEL), lambda b, t: (0, 0)),
                  pl.BlockSpec((1, D_MODEL), lambda b, t: (0, 0)),
                  pl.BlockSpec((D_MODEL, D_FF), lambda b, t: (0, 0)),
                  pl.BlockSpec((D_FF, D_MODEL), lambda b, t: (0, 0))],
        out_specs=pl.BlockSpec((1, tm, D_MODEL), lambda b, t: (b, t, 0)),
        out_shape=jax.ShapeDtypeStruct((bsz, seq, D_MODEL), F32),
        scratch_shapes=[pltpu.VMEM((tm, D_MODEL), F32)],
        compiler_params=pltpu.CompilerParams(
            dimension_semantics=("parallel", "parallel"), vmem_limit_bytes=VMEM_LIMIT),
        name="sqrelu_mlp",
    )(x, mods, pre_g, post_g, w1, w2)


def _pad_cols(a, lo, hi):
    pad = jnp.zeros(a.shape[:-1] + (GATE_PAD - GATE_RANK,), a.dtype)
    return jnp.concatenate([a[..., :lo], pad, a[..., lo:hi]], axis=-1)


def _layer(x, ctx, mods_x, mods_c, prm):
    (mix_pre_g, mix_post_g, mlp_pre_g, mlp_post_g, w_in, mu_prev, mu_next, decay_w0, decay_w2,
     iclr_a0, iclr_a2, k_k, k_a, r_k, gate_w2, lnx_w, lnx_b, conv_w, conv_b, conv_ln_w,
     conv_ln_b, w_out, mlp_w1, mlp_w2) = prm
    bsz = x.shape[0]
    shift_cols = GATE_OFF + GATE_RANK
    in_cols = w_in.shape[1]

    w_pad = _pad_cols(w_in, shift_cols, in_cols).astype(BF16)
    zc = jnp.zeros((2 * CONV_WIDTH,), F32)
    mp_pad = _pad_cols(jnp.concatenate([mu_prev, zc]), shift_cols, in_cols)[None, :]
    mn_pad = _pad_cols(jnp.concatenate([mu_next, zc]), shift_cols, in_cols)[None, :]
    zr = jnp.zeros((DECAY_RANK, RWKV_WIDTH), F32)
    lora_dir = []
    for d in range(2):
        dec_rows = [zr, zr]
        dec_rows[d] = decay_w2[d]
        icl_rows = [zr, zr]
        icl_rows[d] = iclr_a2[d]
        left = jnp.concatenate(dec_rows + [zr, zr], axis=0)
        right = jnp.concatenate([zr, zr] + icl_rows, axis=0)
        lora_dir.append(jnp.concatenate([left, right], axis=1))
    lora_w = jnp.stack(lora_dir).astype(BF16)
    z4 = jnp.zeros((4, RWKV_WIDTH), F32)
    scan_vecs = jnp.stack([jnp.concatenate([decay_w0[d][None], iclr_a0[d][None], k_k[None],
                                            k_a[None], z4], axis=0) for d in range(2)])
    hid = jnp.arange(RWKV_WIDTH) // HEAD_SIZE
    head_ones = (hid[:, None] == hid[None, :]).astype(BF16)
    zl = jnp.zeros((2 * DECAY_RANK, 3 * RWKV_WIDTH), F32)
    za = jnp.zeros((ICLR_RANK, RWKV_WIDTH), F32)
    zg = jnp.zeros((GATE_PAD, RWKV_WIDTH), F32)
    gate_pad = jnp.concatenate([gate_w2, jnp.zeros((GATE_PAD - GATE_RANK, RWKV_WIDTH), F32)], axis=0)
    rw_w = jnp.concatenate([
        zl,
        jnp.concatenate([iclr_a2[0], za, za], axis=1),
        jnp.concatenate([za, iclr_a2[1], za], axis=1),
        jnp.concatenate([zg, zg, gate_pad], axis=1)], axis=0).astype(BF16)
    rvecs = jnp.concatenate([lnx_w[None], lnx_b[None], iclr_a0[0][None], iclr_a0[1][None],
                             k_a[None], r_k.reshape(1, RWKV_WIDTH), jnp.zeros((2, RWKV_WIDTH), F32)],
                            axis=0)
    conv_w_pad = jnp.concatenate([conv_w, jnp.zeros((1, CONV_WIDTH), F32)], axis=0)
    cvecs = jnp.concatenate([conv_b[None], conv_ln_w[None], conv_ln_b[None],
                             jnp.zeros((5, CONV_WIDTH), F32)], axis=0)

    p_x = _proj_call(x, mods_x, mix_pre_g[None], w_pad, mp_pad, mn_pad, P_COLS // PROJ_BLOCK)
    p_c = _proj_call(ctx, mods_c, mix_pre_g[None], w_pad, mp_pad, mn_pad, N_SHIFT_BLOCKS)
    s_zero = jnp.zeros((bsz, 2, N_GROUPS, GROUP_W, GROUP_W), F32)
    _, _, s_ctx = _scan_call(p_c, s_zero, lora_w, scan_vecs, head_ones)
    y_f, y_b, _ = _scan_call(p_x, s_ctx, lora_w, scan_vecs, head_ones)
    x = _mix_call(x, y_f, y_b, p_x, mods_x, rw_w, rvecs, head_ones, conv_w_pad, cvecs,
                  w_out.astype(BF16), mix_post_g[None])
    x = _mlp_call(x, mods_x, mlp_pre_g[None], mlp_post_g[None], mlp_w1.astype(BF16),
                  mlp_w2.astype(BF16))
    return x


def kernel(x, c, ctx, c_ctx, ada_w, ada_b, mix_pre_g, mix_post_g, mlp_pre_g, mlp_post_g, w_in, mu_prev, mu_next, decay_w0, decay_w2, iclr_a0, iclr_a2, k_k, k_a, r_k, gate_w2, lnx_w, lnx_b, conv_w, conv_b, conv_ln_w, conv_ln_b, w_out, mlp_w1, mlp_w2):
    depth = ada_w.shape[0]
    assert depth == 1, "context-stream update between layers is not implemented"
    bsz = x.shape[0]
    rows = 8 * ((bsz + 1 + 7) // 8)
    cc = jnp.concatenate([c, c_ctx[None, :], jnp.zeros((rows - bsz - 1, D_MODEL), F32)], axis=0)
    per_layer = (mix_pre_g, mix_post_g, mlp_pre_g, mlp_post_g, w_in, mu_prev, mu_next, decay_w0,
                 decay_w2, iclr_a0, iclr_a2, k_k, k_a, r_k, gate_w2, lnx_w, lnx_b, conv_w, conv_b,
                 conv_ln_w, conv_ln_b, w_out, mlp_w1, mlp_w2)
    for l in range(depth):
        mods = _ada_call(cc, ada_w[l], ada_b[l][None, :]).reshape(rows, N_MOD, D_MODEL)
        mods_x = mods[:bsz]
        mods_c = jnp.broadcast_to(mods[bsz:bsz + 1], (bsz, N_MOD, D_MODEL))
        x = _layer(x, ctx, mods_x, mods_c, tuple(a[l] for a in per_layer))
    return x
```

```python
import functools

import jax
import jax.numpy as jnp
from jax import lax
from jax.experimental import pallas as pl
from jax.experimental.pallas import tpu as pltpu

F32 = jnp.float32
BF16 = jnp.bfloat16

D_MODEL = 1024
RWKV_WIDTH = 512
CONV_WIDTH = 512
HEAD_SIZE = 64
RWKV_HEADS = 8
DECAY_RANK = 64
ICLR_RANK = 64
GATE_RANK = 160
CONV_KERNEL = 31
CONV_PAD = CONV_KERNEL // 2
GRID_W = 64
CONV_ROWS = GRID_W + 24
D_FF = 4 * D_MODEL
N_MOD = 6
EPS_RMS = 1e-6
EPS_LN = 1e-5
EPS_GN = 64e-5

LORA_OFF = 3 * RWKV_WIDTH
LORA_COLS = 256
GATE_OFF = LORA_OFF + LORA_COLS
GATE_PAD = 256
CONV_OFF = GATE_OFF + GATE_PAD
P_COLS = CONV_OFF + 2 * CONV_WIDTH
SCAN_COLS = GATE_OFF
PROJ_BLOCK = 512
N_SHIFT_BLOCKS = CONV_OFF // PROJ_BLOCK

CHUNK = 64
GROUP_HEADS = 4
GROUP_W = GROUP_HEADS * HEAD_SIZE
N_GROUPS = RWKV_HEADS // GROUP_HEADS
SCAN_BATCH = 2

VMEM_LIMIT = 56 * 1024 * 1024


def _silu(x):
    return x * jax.nn.sigmoid(x)


def _rms(x, g):
    return x * lax.rsqrt(jnp.mean(x * x, axis=-1, keepdims=True) + EPS_RMS) * g


def _dot(a, b):
    return jnp.dot(a.astype(BF16), b.astype(BF16), preferred_element_type=F32)


def _dot_nt(a, b):
    return lax.dot_general(a.astype(BF16), b.astype(BF16), (((1,), (1,)), ((), ())),
                           preferred_element_type=F32)


def _dot_tn(a, b):
    return lax.dot_general(a.astype(BF16), b.astype(BF16), (((0,), (0,)), ((), ())),
                           preferred_element_type=F32)


def _split3(x):
    hi = x.astype(BF16)
    r1 = x - hi.astype(F32)
    mid = r1.astype(BF16)
    lo = (r1 - mid.astype(F32)).astype(BF16)
    return hi, mid, lo


def _dot_exact_rhs(a_bf16, x):
    hi, mid, lo = _split3(x)
    d = lambda p: jnp.dot(a_bf16, p, preferred_element_type=F32)
    return d(hi) + d(mid) + d(lo)


def _head_sum(z, head_ones):
    hi = z.astype(BF16)
    lo = (z - hi.astype(F32)).astype(BF16)
    d = lambda p: jnp.dot(p, head_ones, preferred_element_type=F32)
    return d(hi) + d(lo)


def _ada_kernel(c_ref, w_ref, b_ref, o_ref):
    o_ref[...] = _dot(_silu(c_ref[...]), w_ref[...]) + b_ref[...]


def _ada_call(cc, ada_w, ada_b):
    rows = cc.shape[0]
    n_out = ada_w.shape[1]
    blk = 1536
    return pl.pallas_call(
        _ada_kernel,
        grid=(n_out // blk,),
        in_specs=[pl.BlockSpec((rows, D_MODEL), lambda n: (0, 0)),
                  pl.BlockSpec((D_MODEL, blk), lambda n: (0, n)),
                  pl.BlockSpec((1, blk), lambda n: (0, n))],
        out_specs=pl.BlockSpec((rows, blk), lambda n: (0, n)),
        out_shape=jax.ShapeDtypeStruct((rows, n_out), F32),
        compiler_params=pltpu.CompilerParams(vmem_limit_bytes=VMEM_LIMIT),
        name="ada_mod",
    )(cc, ada_w, ada_b)


def _proj_kernel(x_ref, mod_ref, g_ref, w_ref, mp_ref, mn_ref, o_ref, h_scr, z_scr, *, seq):
    n = pl.program_id(1)
    row_chunk = min(seq, 512)

    @pl.when(n == 0)
    def _():
        shift = mod_ref[0, 0:1, :]
        scale = mod_ref[0, 1:2, :]
        g = g_ref[...]
        for s in range(0, seq, row_chunk):
            xs = x_ref[0, s:s + row_chunk, :]
            h_scr[s:s + row_chunk, :] = (_rms(xs, g) * (1.0 + scale) + shift).astype(BF16)

    @pl.when(n < N_SHIFT_BLOCKS)
    def _():
        z_scr[0:8, :] = jnp.zeros((8, PROJ_BLOCK), F32)
        z_scr[8 + seq:16 + seq, :] = jnp.zeros((8, PROJ_BLOCK), F32)
        mp = mp_ref[...]
        mn = mn_ref[...]

        def shift_rows(s):
            cur = z_scr[8 + s:8 + s + row_chunk, :]
            prv = z_scr[7 + s:7 + s + row_chunk, :]
            nxt = z_scr[9 + s:9 + s + row_chunk, :]
            o_ref[0, s:s + row_chunk, :] = (cur + mp * (prv - cur) + mn * (nxt - cur)).astype(BF16)

        for s in range(0, seq, row_chunk):
            z_scr[8 + s:8 + s + row_chunk, :] = jnp.dot(
                h_scr[s:s + row_chunk, :], w_ref[...], preferred_element_type=F32)
            if s > 0:
                shift_rows(s - row_chunk)
        shift_rows(seq - row_chunk)

    @pl.when(n >= N_SHIFT_BLOCKS)
    def _():
        for s in range(0, seq, row_chunk):
            o_ref[0, s:s + row_chunk, :] = jnp.dot(
                h_scr[s:s + row_chunk, :], w_ref[...], preferred_element_type=F32).astype(BF16)


def _proj_call(x, mods, g, w_pad, mp_pad, mn_pad, n_blocks):
    bsz, seq, _ = x.shape
    n_cols = n_blocks * PROJ_BLOCK
    return pl.pallas_call(
        functools.partial(_proj_kernel, seq=seq),
        grid=(bsz, n_blocks),
        in_specs=[pl.BlockSpec((1, seq, D_MODEL), lambda b, n: (b, 0, 0)),
                  pl.BlockSpec((1, N_MOD, D_MODEL), lambda b, n: (b, 0, 0)),
                  pl.BlockSpec((1, D_MODEL), lambda b, n: (0, 0)),
                  pl.BlockSpec((D_MODEL, PROJ_BLOCK), lambda b, n: (0, n)),
                  pl.BlockSpec((1, PROJ_BLOCK), lambda b, n: (0, n)),
                  pl.BlockSpec((1, PROJ_BLOCK), lambda b, n: (0, n))],
        out_specs=pl.BlockSpec((1, seq, PROJ_BLOCK), lambda b, n: (b, 0, n)),
        out_shape=jax.ShapeDtypeStruct((bsz, seq, n_cols), BF16),
        scratch_shapes=[pltpu.VMEM((seq, D_MODEL), BF16),
                        pltpu.VMEM((seq + 16, PROJ_BLOCK), F32)],
        compiler_params=pltpu.CompilerParams(
            dimension_semantics=("parallel", "arbitrary"), vmem_limit_bytes=VMEM_LIMIT),
        name="in_proj",
    )(x, mods, g, w_pad, mp_pad, mn_pad)


def _expand(x, bd_mask):
    return jnp.where(bd_mask, jnp.concatenate([x] * GROUP_HEADS, axis=0), 0.0)


def _scan_prep(pb, lora_w, vecs, head_ones, reverse):
    L = CHUNK
    r = pb[:, 0:RWKV_WIDTH]
    k = pb[:, RWKV_WIDTH:2 * RWKV_WIDTH]
    v = pb[:, 2 * RWKV_WIDTH:3 * RWKV_WIDTH]
    lora = pb[:, LORA_OFF:LORA_OFF + LORA_COLS]
    lane = lax.broadcasted_iota(jnp.int32, lora.shape, 1)
    lora = jnp.where(lane < 2 * DECAY_RANK, jnp.tanh(lora), lora)
    lin = _dot(lora, lora_w)
    w0 = vecs[0:1, :]
    a0 = vecs[1:2, :]
    k_k = vecs[2:3, :]
    k_a = vecs[3:4, :]
    zw = -(lin[:, 0:RWKV_WIDTH] + w0)
    softplus = jnp.maximum(zw, 0.0) + jnp.log1p(jnp.exp(-jnp.abs(zw)))
    lw = -jnp.exp(-softplus - 0.5)
    iclr = jax.nn.sigmoid(lin[:, RWKV_WIDTH:] + a0)
    kk0 = k * k_k
    kk = kk0 * lax.rsqrt(jnp.maximum(_head_sum(kk0 * kk0, head_ones), 1e-24))
    kd = k * (1.0 + (iclr - 1.0) * k_a)
    a = -kk
    b = kk * iclr

    row = lax.broadcasted_iota(jnp.int32, (L, L), 0)
    col = lax.broadcasted_iota(jnp.int32, (L, L), 1)
    tri = jnp.where((col >= row) if reverse else (col <= row), 1.0, 0.0).astype(BF16)
    cum = _dot_exact_rhs(tri, lw)
    tot = cum[0:1, :] if reverse else cum[L - 1:L, :]
    e_neg = jnp.exp(-cum)
    e_rem = jnp.exp(tot - cum)
    tr = lax.broadcasted_iota(jnp.int32, (L, GROUP_W), 0)
    tc = jnp.bitwise_and(lax.broadcasted_iota(jnp.int32, (L, GROUP_W), 1), L - 1)
    return dict(
        r_t=r * jnp.exp(cum), a_t=a * jnp.exp(cum - lw), b_t=b * e_neg, k_t=kd * e_neg,
        b_h=b * e_rem, k_h=kd * e_rem, v=v, w_tot=jnp.exp(tot),
        strict=(tc > tr) if reverse else (tc < tr),
        incl=(tc >= tr) if reverse else (tc <= tr))


def _scan_units(preps, s_idx, s_ref):
    L = CHUNK
    rr = lax.broadcasted_iota(jnp.int32, (GROUP_W, GROUP_W), 0)
    cc = lax.broadcasted_iota(jnp.int32, (GROUP_W, GROUP_W), 1)
    bd_mask = jnp.right_shift(rr, 6) == jnp.right_shift(cc, 6)
    tr = lax.broadcasted_iota(jnp.int32, (L, GROUP_W), 0)
    tc = jnp.bitwise_and(lax.broadcasted_iota(jnp.int32, (L, GROUP_W), 1), L - 1)
    eye = jnp.where(tc == tr, 1.0, 0.0)
    units = [(d, g) for d in range(len(preps)) for g in range(N_GROUPS)]
    sl = lambda g: slice(g * GROUP_W, (g + 1) * GROUP_W)
    ex = lambda x: _expand(x, bd_mask)

    lhs, a_ab, a_ak, a_rb, a_rk = {}, {}, {}, {}, {}
    for u in units:
        d, g = u
        p = preps[d]
        lhs[u] = jnp.concatenate([p["a_t"][:, sl(g)], p["r_t"][:, sl(g)]], axis=0)
        e_bk = jnp.concatenate([ex(p["b_t"][:, sl(g)]), ex(p["k_t"][:, sl(g)])], axis=0)
        a_all = _dot_nt(lhs[u], e_bk)
        a_ab[u] = jnp.where(p["strict"], a_all[0:L, 0:GROUP_W], 0.0)
        a_ak[u] = jnp.where(p["strict"], a_all[0:L, GROUP_W:], 0.0)
        a_rb[u] = jnp.where(p["incl"], a_all[L:, 0:GROUP_W], 0.0)
        a_rk[u] = jnp.where(p["incl"], a_all[L:, GROUP_W:], 0.0)

    n_levels = 6
    npow = {u: _dot(a_ab[u], ex(a_ab[u])) for u in units}
    pinv = {u: eye + a_ab[u] for u in units}
    for lvl in range(1, n_levels):
        for u in units:
            rhs = ex(npow[u])
            if lvl < n_levels - 1:
                both = _dot(jnp.concatenate([npow[u], pinv[u]], axis=0), rhs)
                npow[u] = both[0:L]
                pinv[u] = pinv[u] + both[L:]
            else:
                pinv[u] = pinv[u] + _dot(pinv[u], rhs)

    s_old, sa, e_v, x_u, u_u, y_u = {}, {}, {}, {}, {}, {}
    for u in units:
        d, g = u
        s_old[u] = s_ref[s_idx[d] + (g,)]
        sa[u] = _dot_nt(lhs[u], s_old[u])
        e_v[u] = ex(preps[d]["v"][:, sl(g)])
    for u in units:
        x_u[u] = sa[u][0:L] + _dot(a_ak[u], e_v[u])
    for u in units:
        u_u[u] = _dot(pinv[u], ex(x_u[u]))
    for u in units:
        y_u[u] = sa[u][L:] + _dot(jnp.concatenate([a_rb[u], a_rk[u]], axis=1),
                                  jnp.concatenate([ex(u_u[u]), e_v[u]], axis=0))
    for u in units:
        d, g = u
        p = preps[d]
        upd = _dot_tn(jnp.concatenate([u_u[u], p["v"][:, sl(g)]], axis=0),
                      jnp.concatenate([p["b_h"][:, sl(g)], p["k_h"][:, sl(g)]], axis=0))
        s_ref[s_idx[d] + (g,)] = s_old[u] * p["w_tot"][:, sl(g)] + jnp.where(bd_mask, upd, 0.0)
    return [jnp.concatenate([y_u[(d, g)] for g in range(N_GROUPS)], axis=1)
            for d in range(len(preps))]


def _scan_kernel(pf_ref, pb_ref, s0_ref, lw_ref, vec_ref, ho_ref, yf_ref, yb_ref, sfin_ref, s_scr):
    c = pl.program_id(1)

    @pl.when(c == 0)
    def _():
        s_scr[...] = s0_ref[...]

    head_ones = ho_ref[...]
    preps, s_idx = [], []
    for bb in range(SCAN_BATCH):
        preps.append(_scan_prep(pf_ref[bb].astype(F32), lw_ref[0], vec_ref[0], head_ones, False))
        preps.append(_scan_prep(pb_ref[bb].astype(F32), lw_ref[1], vec_ref[1], head_ones, True))
        s_idx += [(bb, 0), (bb, 1)]
    ys = _scan_units(preps, s_idx, s_scr)
    for bb in range(SCAN_BATCH):
        yf_ref[bb] = ys[2 * bb]
        yb_ref[bb] = ys[2 * bb + 1]

    @pl.when(c == pl.num_programs(1) - 1)
    def _():
        sfin_ref[...] = s_scr[...]


def _scan_call(p, s0, lora_w, vecs, head_ones):
    bsz, seq, _ = p.shape
    nc = seq // CHUNK
    state_block = (SCAN_BATCH, 2, N_GROUPS, GROUP_W, GROUP_W)
    state_map = lambda b, c: (b, 0, 0, 0, 0)
    return pl.pallas_call(
        _scan_kernel,
        grid=(bsz // SCAN_BATCH, nc),
        in_specs=[pl.BlockSpec((SCAN_BATCH, CHUNK, SCAN_COLS), lambda b, c: (b, c, 0)),
                  pl.BlockSpec((SCAN_BATCH, CHUNK, SCAN_COLS), lambda b, c: (b, nc - 1 - c, 0)),
                  pl.BlockSpec(state_block, state_map),
                  pl.BlockSpec((2, LORA_COLS, 2 * RWKV_WIDTH), lambda b, c: (0, 0, 0)),
                  pl.BlockSpec((2, 8, RWKV_WIDTH), lambda b, c: (0, 0, 0)),
                  pl.BlockSpec((RWKV_WIDTH, RWKV_WIDTH), lambda b, c: (0, 0))],
        out_specs=[pl.BlockSpec((SCAN_BATCH, CHUNK, RWKV_WIDTH), lambda b, c: (b, c, 0)),
                   pl.BlockSpec((SCAN_BATCH, CHUNK, RWKV_WIDTH), lambda b, c: (b, nc - 1 - c, 0)),
                   pl.BlockSpec(state_block, state_map)],
        out_shape=[jax.ShapeDtypeStruct((bsz, seq, RWKV_WIDTH), F32),
                   jax.ShapeDtypeStruct((bsz, seq, RWKV_WIDTH), F32),
                   jax.ShapeDtypeStruct((bsz, 2, N_GROUPS, GROUP_W, GROUP_W), F32)],
        scratch_shapes=[pltpu.VMEM(state_block, F32)],
        compiler_params=pltpu.CompilerParams(
            dimension_semantics=("parallel", "arbitrary"), vmem_limit_bytes=VMEM_LIMIT),
        name="wkv_scan",
    )(p, p, s0, lora_w, vecs, head_ones)


def _mix_kernel(x_ref, yf_ref, yb_ref, rkv_ref, sm_ref, cv_ref, mod_ref, rw_ref, vec_ref, ho_ref,
                cw_ref, cvec_ref, wout_ref, g_ref, o_ref, upad, *, tm):
    n_lines = tm // GRID_W
    head_ones = ho_ref[...]
    y = yf_ref[0] + yb_ref[0]
    inv_n = 1.0 / HEAD_SIZE
    mu = _head_sum(y, head_ones) * inv_n
    dy = y - mu
    var = _head_sum(dy * dy, head_ones) * inv_n
    lnx_w = vec_ref[0:1, :]
    lnx_b = vec_ref[1:2, :]
    a0_f = vec_ref[2:3, :]
    a0_b = vec_ref[3:4, :]
    k_a = vec_ref[4:5, :]
    r_k = vec_ref[5:6, :]
    yn = dy * lax.rsqrt(var + EPS_GN) * lnx_w + lnx_b
    sm = sm_ref[0].astype(F32)
    lane = lax.broadcasted_iota(jnp.int32, sm.shape, 1)
    sm = jnp.where(lane >= LORA_COLS, jax.nn.sigmoid(sm), sm)
    lin = _dot(sm, rw_ref[...])
    iclr_f = jax.nn.sigmoid(lin[:, 0:RWKV_WIDTH] + a0_f)
    iclr_b = jax.nn.sigmoid(lin[:, RWKV_WIDTH:2 * RWKV_WIDTH] + a0_b)
    gate = lin[:, 2 * RWKV_WIDTH:]
    r = rkv_ref[0, :, 0:RWKV_WIDTH].astype(F32)
    k = rkv_ref[0, :, RWKV_WIDTH:2 * RWKV_WIDTH].astype(F32)
    v = rkv_ref[0, :, 2 * RWKV_WIDTH:3 * RWKV_WIDTH].astype(F32)
    kd_f = k * (1.0 + (iclr_f - 1.0) * k_a)
    kd_b = k * (1.0 + (iclr_b - 1.0) * k_a)
    k_bar = 0.5 * (kd_f + kd_b)
    bonus = _head_sum(r * k_bar * r_k, head_ones) * v
    rw_out = (yn + bonus) * gate

    cv = cv_ref[0].astype(F32)
    u = cv[:, 0:CONV_WIDTH] * jax.nn.sigmoid(cv[:, CONV_WIDTH:])
    zeros = jnp.zeros((16, CONV_WIDTH), F32)
    for ln in range(n_lines):
        u_ln = u[ln * GRID_W:(ln + 1) * GRID_W, :]
        for s in range(8):
            upad[s, ln, 0:16, :] = zeros
            upad[s, ln, CONV_ROWS - 16:CONV_ROWS, :] = zeros
            upad[s, ln, 16 - s:16 - s + GRID_W, :] = u_ln
    conv_b = cvec_ref[0:1, :]
    ln_w = cvec_ref[1:2, :]
    ln_b = cvec_ref[2:3, :]
    conv_lines = []
    for ln in range(n_lines):
        acc = jnp.zeros((GRID_W, CONV_WIDTH), F32)
        for j in range(CONV_KERNEL):
            off = 16 - CONV_PAD + j
            q8 = (off // 8) * 8
            acc = acc + cw_ref[j:j + 1, :] * upad[off % 8, ln, q8:q8 + GRID_W, :]
        conv_lines.append(acc)
    yc = jnp.concatenate(conv_lines, axis=0) + conv_b
    mu_c = jnp.mean(yc, axis=-1, keepdims=True)
    dc = yc - mu_c
    var_c = jnp.mean(dc * dc, axis=-1, keepdims=True)
    conv_out = _silu(dc * lax.rsqrt(var_c + EPS_LN) * ln_w + ln_b)

    mix_in = jnp.concatenate([rw_out.astype(BF16), conv_out.astype(BF16)], axis=1)
    mix = jnp.dot(mix_in, wout_ref[...], preferred_element_type=F32)
    gate_mix = mod_ref[0, 2:3, :]
    o_ref[0] = x_ref[0] + gate_mix * _rms(mix, g_ref[...])


def _mix_call(x, y_f, y_b, p, mods, rw_w, rvecs, head_ones, conv_w, cvecs, w_out, post_g):
    bsz, seq, _ = x.shape
    tm = 512
    tok = lambda width, blk: pl.BlockSpec((1, tm, width), lambda b, t: (b, t, blk))
    const2 = lambda shape: pl.BlockSpec(shape, lambda b, t: (0, 0))
    return pl.pallas_call(
        functools.partial(_mix_kernel, tm=tm),
        grid=(bsz, seq // tm),
        in_specs=[tok(D_MODEL, 0), tok(RWKV_WIDTH, 0), tok(RWKV_WIDTH, 0),
                  tok(3 * RWKV_WIDTH, 0),
                  tok(512, LORA_OFF // 512),
                  tok(2 * CONV_WIDTH, CONV_OFF // (2 * CONV_WIDTH)),
                  pl.BlockSpec((1, N_MOD, D_MODEL), lambda b, t: (b, 0, 0)),
                  const2((512, 3 * RWKV_WIDTH)), const2((8, RWKV_WIDTH)),
                  const2((RWKV_WIDTH, RWKV_WIDTH)), const2((32, CONV_WIDTH)),
                  const2((8, CONV_WIDTH)), const2((D_MODEL, D_MODEL)), const2((1, D_MODEL))],
        out_specs=tok(D_MODEL, 0),
        out_shape=jax.ShapeDtypeStruct((bsz, seq, D_MODEL), F32),
        scratch_shapes=[pltpu.VMEM((8, tm // GRID_W, CONV_ROWS, CONV_WIDTH), F32)],
        compiler_params=pltpu.CompilerParams(
            dimension_semantics=("parallel", "parallel"), vmem_limit_bytes=VMEM_LIMIT),
        name="mix_out",
    )(x, y_f, y_b, p, p, p, mods, rw_w, rvecs, head_ones, conv_w, cvecs, w_out, post_g)


def _mlp_kernel(x_ref, mod_ref, gpre_ref, gpost_ref, w1_ref, w2_ref, o_ref, acc_ref):
    x = x_ref[0]
    shift = mod_ref[0, 3:4, :]
    scale = mod_ref[0, 4:5, :]
    gate = mod_ref[0, 5:6, :]
    h = (_rms(x, gpre_ref[...]) * (1.0 + scale) + shift).astype(BF16)
    ff_blk = 1024
    for c in range(D_FF // ff_blk):
        hid = jnp.dot(h, w1_ref[:, c * ff_blk:(c + 1) * ff_blk], preferred_element_type=F32)
        hid = jnp.square(jnp.maximum(hid, 0.0)).astype(BF16)
        part = jnp.dot(hid, w2_ref[c * ff_blk:(c + 1) * ff_blk, :], preferred_element_type=F32)
        if c == 0:
            acc_ref[...] = part
        else:
            acc_ref[...] += part
    o_ref[0] = x + gate * _rms(acc_ref[...], gpost_ref[...])


def _mlp_call(x, mods, pre_g, post_g, w1, w2):
    bsz, seq, _ = x.shape
    tm = 512
    return pl.pallas_call(
        _mlp_kernel,
        grid=(bsz, seq // tm),
        in_specs=[pl.BlockSpec((1, tm, D_MODEL), lambda b, t: (b, t, 0)),
                  pl.BlockSpec((1, N_MOD, D_MODEL), lambda b, t: (b, 0, 0)),
                  pl.BlockSpec((1, D_MODEL), lambda b, t: (0, 0)),
                  pl.BlockSpec((1, D_MODEL), lambda b, t: (0, 0)),
                  pl.BlockSpec((D_MODEL, D_FF), lambda b, t: (0, 0)),
                  pl.BlockSpec((D_FF, D_MODEL), lambda b, t: (0, 0))],
        out_specs=pl.BlockSpec((1, tm, D_MODEL), lambda b, t: (b, t, 0)),
        out_shape=jax.ShapeDtypeStruct((bsz, seq, D_MODEL), F32),
        scratch_shapes=[pltpu.VMEM((tm, D_MODEL), F32)],
        compiler_params=pltpu.CompilerParams(
            dimension_semantics=("parallel", "parallel"), vmem_limit_bytes=VMEM_LIMIT),
        name="sqrelu_mlp",
    )(x, mods, pre_g, post_g, w1, w2)


def _pad_cols(a, lo, hi):
    pad = jnp.zeros(a.shape[:-1] + (GATE_PAD - GATE_RANK,), a.dtype)
    return jnp.concatenate([a[..., :lo], pad, a[..., lo:hi]], axis=-1)


def _layer(x, ctx, mods_x, mods_c, prm):
    (mix_pre_g, mix_post_g, mlp_pre_g, mlp_post_g, w_in, mu_prev, mu_next, decay_w0, decay_w2,
     iclr_a0, iclr_a2, k_k, k_a, r_k, gate_w2, lnx_w, lnx_b, conv_w, conv_b, conv_ln_w,
     conv_ln_b, w_out, mlp_w1, mlp_w2) = prm
    bsz = x.shape[0]
    shift_cols = GATE_OFF + GATE_RANK
    in_cols = w_in.shape[1]

    w_pad = _pad_cols(w_in, shift_cols, in_cols).astype(BF16)
    zc = jnp.zeros((2 * CONV_WIDTH,), F32)
    mp_pad = _pad_cols(jnp.concatenate([mu_prev, zc]), shift_cols, in_cols)[None, :]
    mn_pad = _pad_cols(jnp.concatenate([mu_next, zc]), shift_cols, in_cols)[None, :]
    zr = jnp.zeros((DECAY_RANK, RWKV_WIDTH), F32)
    lora_dir = []
    for d in range(2):
        dec_rows = [zr, zr]
        dec_rows[d] = decay_w2[d]
        icl_rows = [zr, zr]
        icl_rows[d] = iclr_a2[d]
        left = jnp.concatenate(dec_rows + [zr, zr], axis=0)
        right = jnp.concatenate([zr, zr] + icl_rows, axis=0)
        lora_dir.append(jnp.concatenate([left, right], axis=1))
    lora_w = jnp.stack(lora_dir).astype(BF16)
    z4 = jnp.zeros((4, RWKV_WIDTH), F32)
    scan_vecs = jnp.stack([jnp.concatenate([decay_w0[d][None], iclr_a0[d][None], k_k[None],
                                            k_a[None], z4], axis=0) for d in range(2)])
    hid = jnp.arange(RWKV_WIDTH) // HEAD_SIZE
    head_ones = (hid[:, None] == hid[None, :]).astype(BF16)
    zl = jnp.zeros((2 * DECAY_RANK, 3 * RWKV_WIDTH), F32)
    za = jnp.zeros((ICLR_RANK, RWKV_WIDTH), F32)
    zg = jnp.zeros((GATE_PAD, RWKV_WIDTH), F32)
    gate_pad = jnp.concatenate([gate_w2, jnp.zeros((GATE_PAD - GATE_RANK, RWKV_WIDTH), F32)], axis=0)
    rw_w = jnp.concatenate([
        zl,
        jnp.concatenate([iclr_a2[0], za, za], axis=1),
        jnp.concatenate([za, iclr_a2[1], za], axis=1),
        jnp.concatenate([zg, zg, gate_pad], axis=1)], axis=0).astype(BF16)
    rvecs = jnp.concatenate([lnx_w[None], lnx_b[None], iclr_a0[0][None], iclr_a0[1][None],
                             k_a[None], r_k.reshape(1, RWKV_WIDTH), jnp.zeros((2, RWKV_WIDTH), F32)],
                            axis=0)
    conv_w_pad = jnp.concatenate([conv_w, jnp.zeros((1, CONV_WIDTH), F32)], axis=0)
    cvecs = jnp.concatenate([conv_b[None], conv_ln_w[None], conv_ln_b[None],
                             jnp.zeros((5, CONV_WIDTH), F32)], axis=0)

    p_x = _proj_call(x, mods_x, mix_pre_g[None], w_pad, mp_pad, mn_pad, P_COLS // PROJ_BLOCK)
    p_c = _proj_call(ctx, mods_c, mix_pre_g[None], w_pad, mp_pad, mn_pad, N_SHIFT_BLOCKS)
    s_zero = jnp.zeros((bsz, 2, N_GROUPS, GROUP_W, GROUP_W), F32)
    _, _, s_ctx = _scan_call(p_c, s_zero, lora_w, scan_vecs, head_ones)
    y_f, y_b, _ = _scan_call(p_x, s_ctx, lora_w, scan_vecs, head_ones)
    x = _mix_call(x, y_f, y_b, p_x, mods_x, rw_w, rvecs, head_ones, conv_w_pad, cvecs,
                  w_out.astype(BF16), mix_post_g[None])
    x = _mlp_call(x, mods_x, mlp_pre_g[None], mlp_post_g[None], mlp_w1.astype(BF16),
                  mlp_w2.astype(BF16))
    return x


def kernel(x, c, ctx, c_ctx, ada_w, ada_b, mix_pre_g, mix_post_g, mlp_pre_g, mlp_post_g, w_in, mu_prev, mu_next, decay_w0, decay_w2, iclr_a0, iclr_a2, k_k, k_a, r_k, gate_w2, lnx_w, lnx_b, conv_w, conv_b, conv_ln_w, conv_ln_b, w_out, mlp_w1, mlp_w2):
    depth = ada_w.shape[0]
    assert depth == 1, "context-stream update between layers is not implemented"
    bsz = x.shape[0]
    rows = 8 * ((bsz + 1 + 7) // 8)
    cc = jnp.concatenate([c, c_ctx[None, :], jnp.zeros((rows - bsz - 1, D_MODEL), F32)], axis=0)
    per_layer = (mix_pre_g, mix_post_g, mlp_pre_g, mlp_post_g, w_in, mu_prev, mu_next, decay_w0,
                 decay_w2, iclr_a0, iclr_a2, k_k, k_a, r_k, gate_w2, lnx_w, lnx_b, conv_w, conv_b,
                 conv_ln_w, conv_ln_b, w_out, mlp_w1, mlp_w2)
    for l in range(depth):
        mods = _ada_call(cc, ada_w[l], ada_b[l][None, :]).reshape(rows, N_MOD, D_MODEL)
        mods_x = mods[:bsz]
        mods_c = jnp.broadcast_to(mods[bsz:bsz + 1], (bsz, N_MOD, D_MODEL))
        x = _layer(x, ctx, mods_x, mods_c, tuple(a[l] for a in per_layer))
    return x
```

```python
import functools

import jax
import jax.numpy as jnp
from jax import lax
from jax.experimental import pallas as pl
from jax.experimental.pallas import tpu as pltpu

F32 = jnp.float32
BF16 = jnp.bfloat16

D_MODEL = 1024
RWKV_WIDTH = 512
CONV_WIDTH = 512
HEAD_SIZE = 64
RWKV_HEADS = 8
DECAY_RANK = 64
ICLR_RANK = 64
GATE_RANK = 160
CONV_KERNEL = 31
CONV_PAD = CONV_KERNEL // 2
GRID_W = 64
CONV_ROWS = GRID_W + 24
D_FF = 4 * D_MODEL
N_MOD = 6
EPS_RMS = 1e-6
EPS_LN = 1e-5
EPS_GN = 64e-5
DECAY_SCALE = 0.6065306597126334

LORA_OFF = 3 * RWKV_WIDTH
LORA_COLS = 256
GATE_OFF = LORA_OFF + LORA_COLS
GATE_PAD = 256
CONV_OFF = GATE_OFF + GATE_PAD
P_COLS = CONV_OFF + 2 * CONV_WIDTH
SCAN_COLS = GATE_OFF
PROJ_BLOCK = 512
N_SHIFT_BLOCKS = CONV_OFF // PROJ_BLOCK

CHUNK = 64
GROUP_HEADS = 4
GROUP_W = GROUP_HEADS * HEAD_SIZE
N_GROUPS = RWKV_HEADS // GROUP_HEADS
SCAN_BATCH = 4

VMEM_LIMIT = 56 * 1024 * 1024


def _silu(x):
    return x * jax.nn.sigmoid(x)


def _rms(x, g):
    return x * lax.rsqrt(jnp.mean(x * x, axis=-1, keepdims=True) + EPS_RMS) * g


def _dot(a, b):
    return jnp.dot(a.astype(BF16), b.astype(BF16), preferred_element_type=F32)


def _dot_nt(a, b):
    return lax.dot_general(a.astype(BF16), b.astype(BF16), (((1,), (1,)), ((), ())),
                           preferred_element_type=F32)


def _dot_tn(a, b):
    return lax.dot_general(a.astype(BF16), b.astype(BF16), (((0,), (0,)), ((), ())),
                           preferred_element_type=F32)


def _split3(x):
    hi = x.astype(BF16)
    r1 = x - hi.astype(F32)
    mid = r1.astype(BF16)
    lo = (r1 - mid.astype(F32)).astype(BF16)
    return hi, mid, lo


def _dot_exact_rhs(a_bf16, x):
    hi, mid, lo = _split3(x)
    d = lambda p: jnp.dot(a_bf16, p, preferred_element_type=F32)
    return d(hi) + d(mid) + d(lo)


def _head_sum(z, head_ones):
    hi = z.astype(BF16)
    lo = (z - hi.astype(F32)).astype(BF16)
    d = lambda p: jnp.dot(p, head_ones, preferred_element_type=F32)
    return d(hi) + d(lo)


def _ada_kernel(c_ref, w_ref, b_ref, o_ref):
    o_ref[...] = _dot(_silu(c_ref[...]), w_ref[...]) + b_ref[...]


def _ada_call(cc, ada_w, ada_b):
    rows = cc.shape[0]
    n_out = ada_w.shape[1]
    blk = 1536
    return pl.pallas_call(
        _ada_kernel,
        grid=(n_out // blk,),
        in_specs=[pl.BlockSpec((rows, D_MODEL), lambda n: (0, 0)),
                  pl.BlockSpec((D_MODEL, blk), lambda n: (0, n)),
                  pl.BlockSpec((1, blk), lambda n: (0, n))],
        out_specs=pl.BlockSpec((rows, blk), lambda n: (0, n)),
        out_shape=jax.ShapeDtypeStruct((rows, n_out), F32),
        compiler_params=pltpu.CompilerParams(vmem_limit_bytes=VMEM_LIMIT),
        name="ada_mod",
    )(cc, ada_w, ada_b)


def _proj_kernel(x_ref, mod_ref, g_ref, w_ref, mp_ref, mn_ref, o_ref, h_scr, z_scr, *, seq):
    n = pl.program_id(1)
    row_chunk = min(seq, 512)

    @pl.when(n == 0)
    def _():
        shift = mod_ref[0, 0:1, :]
        scale = mod_ref[0, 1:2, :]
        g = g_ref[...]
        for s in range(0, seq, row_chunk):
            xs = x_ref[0, s:s + row_chunk, :]
            h_scr[s:s + row_chunk, :] = (_rms(xs, g) * (1.0 + scale) + shift).astype(BF16)

    @pl.when(n < N_SHIFT_BLOCKS)
    def _():
        z_scr[0:8, :] = jnp.zeros((8, PROJ_BLOCK), F32)
        z_scr[8 + seq:16 + seq, :] = jnp.zeros((8, PROJ_BLOCK), F32)
        mp = mp_ref[...]
        mn = mn_ref[...]

        def shift_rows(s):
            cur = z_scr[8 + s:8 + s + row_chunk, :]
            prv = z_scr[7 + s:7 + s + row_chunk, :]
            nxt = z_scr[9 + s:9 + s + row_chunk, :]
            o_ref[0, s:s + row_chunk, :] = (cur + mp * (prv - cur) + mn * (nxt - cur)).astype(BF16)

        for s in range(0, seq, row_chunk):
            z_scr[8 + s:8 + s + row_chunk, :] = jnp.dot(
                h_scr[s:s + row_chunk, :], w_ref[...], preferred_element_type=F32)
            if s > 0:
                shift_rows(s - row_chunk)
        shift_rows(seq - row_chunk)

    @pl.when(n >= N_SHIFT_BLOCKS)
    def _():
        for s in range(0, seq, row_chunk):
            o_ref[0, s:s + row_chunk, :] = jnp.dot(
                h_scr[s:s + row_chunk, :], w_ref[...], preferred_element_type=F32).astype(BF16)


def _proj_call(x, mods, g, w_pad, mp_pad, mn_pad, n_blocks):
    bsz, seq, _ = x.shape
    n_cols = n_blocks * PROJ_BLOCK
    return pl.pallas_call(
        functools.partial(_proj_kernel, seq=seq),
        grid=(bsz, n_blocks),
        in_specs=[pl.BlockSpec((1, seq, D_MODEL), lambda b, n: (b, 0, 0)),
                  pl.BlockSpec((1, N_MOD, D_MODEL), lambda b, n: (b, 0, 0)),
                  pl.BlockSpec((1, D_MODEL), lambda b, n: (0, 0)),
                  pl.BlockSpec((D_MODEL, PROJ_BLOCK), lambda b, n: (0, n)),
                  pl.BlockSpec((1, PROJ_BLOCK), lambda b, n: (0, n)),
                  pl.BlockSpec((1, PROJ_BLOCK), lambda b, n: (0, n))],
        out_specs=pl.BlockSpec((1, seq, PROJ_BLOCK), lambda b, n: (b, 0, n)),
        out_shape=jax.ShapeDtypeStruct((bsz, seq, n_cols), BF16),
        scratch_shapes=[pltpu.VMEM((seq, D_MODEL), BF16),
                        pltpu.VMEM((seq + 16, PROJ_BLOCK), F32)],
        compiler_params=pltpu.CompilerParams(
            dimension_semantics=("parallel", "arbitrary"), vmem_limit_bytes=VMEM_LIMIT),
        name="in_proj",
    )(x, mods, g, w_pad, mp_pad, mn_pad)


def _expand(x, bd_mask):
    return jnp.where(bd_mask, jnp.concatenate([x] * GROUP_HEADS, axis=0), 0.0)


def _scan_prep(blocks, lw_ref, vec_ref, ones_bd):
    L = CHUNK
    n = len(blocks)
    lane = lax.broadcasted_iota(jnp.int32, (L, LORA_COLS), 1)
    loras = []
    for pb, _ in blocks:
        lora = pb[:, LORA_OFF:LORA_OFF + LORA_COLS]
        loras.append(jnp.where(lane < 2 * DECAY_RANK, jnp.tanh(lora), lora))
    lin = [None] * n
    for d in range(2):
        idx = [i for i in range(n) if blocks[i][1] == bool(d)]
        res = _dot(jnp.concatenate([loras[i] for i in idx], axis=0), lw_ref[d])
        for j, i in enumerate(idx):
            lin[i] = res[j * L:(j + 1) * L]

    kk0s, pieces = [], []
    for i, (pb, reverse) in enumerate(blocks):
        kk0 = pb[:, RWKV_WIDTH:2 * RWKV_WIDTH] * vec_ref[int(reverse), 2:3, :]
        kk0s.append(kk0)
        sq = kk0 * kk0
        hi = sq.astype(BF16)
        lo = (sq - hi.astype(F32)).astype(BF16)
        for g in range(N_GROUPS):
            pieces += [hi[:, g * GROUP_W:(g + 1) * GROUP_W], lo[:, g * GROUP_W:(g + 1) * GROUP_W]]
    sums = jnp.dot(jnp.concatenate(pieces, axis=0), ones_bd, preferred_element_type=F32)

    row = lax.broadcasted_iota(jnp.int32, (L, L), 0)
    col = lax.broadcasted_iota(jnp.int32, (L, L), 1)
    tr = lax.broadcasted_iota(jnp.int32, (L, GROUP_W), 0)
    tc = jnp.bitwise_and(lax.broadcasted_iota(jnp.int32, (L, GROUP_W), 1), L - 1)
    preps = []
    for i, (pb, reverse) in enumerate(blocks):
        vecs = vec_ref[int(reverse)]
        w0 = vecs[0:1, :]
        a0 = vecs[1:2, :]
        k_a = vecs[3:4, :]
        r = pb[:, 0:RWKV_WIDTH]
        k = pb[:, RWKV_WIDTH:2 * RWKV_WIDTH]
        v = pb[:, 2 * RWKV_WIDTH:3 * RWKV_WIDTH]
        lw = -DECAY_SCALE * jax.nn.sigmoid(lin[i][:, 0:RWKV_WIDTH] + w0)
        iclr = jax.nn.sigmoid(lin[i][:, RWKV_WIDTH:] + a0)
        base = i * 2 * N_GROUPS * L
        ss = jnp.concatenate(
            [sums[base + (2 * g) * L:base + (2 * g + 1) * L]
             + sums[base + (2 * g + 1) * L:base + (2 * g + 2) * L] for g in range(N_GROUPS)], axis=1)
        kk = kk0s[i] * lax.rsqrt(jnp.maximum(ss, 1e-24))
        kd = k * (1.0 + (iclr - 1.0) * k_a)
        a = -kk
        b = kk * iclr
        tri = jnp.where((col >= row) if reverse else (col <= row), 1.0, 0.0).astype(BF16)
        cum = _dot_exact_rhs(tri, lw)
        tot = cum[0:1, :] if reverse else cum[L - 1:L, :]
        e_neg = jnp.exp(-cum)
        e_rem = jnp.exp(tot - cum)
        preps.append(dict(
            r_t=r * jnp.exp(cum), a_t=a * jnp.exp(cum - lw), b_t=b * e_neg, k_t=kd * e_neg,
            b_h=b * e_rem, k_h=kd * e_rem, v=v, w_tot=jnp.exp(tot),
            strict=(tc > tr) if reverse else (tc < tr),
            incl=(tc >= tr) if reverse else (tc <= tr)))
    return preps


def _scan_units(preps, s_idx, s_ref):
    L = CHUNK
    rr = lax.broadcasted_iota(jnp.int32, (GROUP_W, GROUP_W), 0)
    cc = lax.broadcasted_iota(jnp.int32, (GROUP_W, GROUP_W), 1)
    bd_mask = jnp.right_shift(rr, 6) == jnp.right_shift(cc, 6)
    tr = lax.broadcasted_iota(jnp.int32, (L, GROUP_W), 0)
    tc = jnp.bitwise_and(lax.broadcasted_iota(jnp.int32, (L, GROUP_W), 1), L - 1)
    eye = jnp.where(tc == tr, 1.0, 0.0)
    units = [(d, g) for d in range(len(preps)) for g in range(N_GROUPS)]
    sl = lambda g: slice(g * GROUP_W, (g + 1) * GROUP_W)
    ex = lambda x: _expand(x, bd_mask)

    lhs, a_ab, a_ak, a_rb, a_rk = {}, {}, {}, {}, {}
    for u in units:
        d, g = u
        p = preps[d]
        lhs[u] = jnp.concatenate([p["a_t"][:, sl(g)], p["r_t"][:, sl(g)]], axis=0)
        e_bk = jnp.concatenate([ex(p["b_t"][:, sl(g)]), ex(p["k_t"][:, sl(g)])], axis=0)
        a_all = _dot_nt(lhs[u], e_bk)
        a_ab[u] = jnp.where(p["strict"], a_all[0:L, 0:GROUP_W], 0.0)
        a_ak[u] = jnp.where(p["strict"], a_all[0:L, GROUP_W:], 0.0)
        a_rb[u] = jnp.where(p["incl"], a_all[L:, 0:GROUP_W], 0.0)
        a_rk[u] = jnp.where(p["incl"], a_all[L:, GROUP_W:], 0.0)

    n_levels = 6
    npow = {u: _dot(a_ab[u], ex(a_ab[u])) for u in units}
    pinv = {u: eye + a_ab[u] for u in units}
    for lvl in range(1, n_levels):
        for u in units:
            rhs = ex(npow[u])
            if lvl < n_levels - 1:
                both = _dot(jnp.concatenate([npow[u], pinv[u]], axis=0), rhs)
                npow[u] = both[0:L]
                pinv[u] = pinv[u] + both[L:]
            else:
                pinv[u] = pinv[u] + _dot(pinv[u], rhs)

    s_old, sa, e_v, x_u, u_u, y_u = {}, {}, {}, {}, {}, {}
    for u in units:
        d, g = u
        s_old[u] = s_ref[s_idx[d] + (g,)]
        sa[u] = _dot_nt(lhs[u], s_old[u])
        e_v[u] = ex(preps[d]["v"][:, sl(g)])
    for u in units:
        x_u[u] = sa[u][0:L] + _dot(a_ak[u], e_v[u])
    for u in units:
        u_u[u] = _dot(pinv[u], ex(x_u[u]))
    for u in units:
        y_u[u] = sa[u][L:] + _dot(jnp.concatenate([a_rb[u], a_rk[u]], axis=1),
                                  jnp.concatenate([ex(u_u[u]), e_v[u]], axis=0))
    for u in units:
        d, g = u
        p = preps[d]
        upd = _dot_tn(jnp.concatenate([u_u[u], p["v"][:, sl(g)]], axis=0),
                      jnp.concatenate([p["b_h"][:, sl(g)], p["k_h"][:, sl(g)]], axis=0))
        s_ref[s_idx[d] + (g,)] = s_old[u] * p["w_tot"][:, sl(g)] + jnp.where(bd_mask, upd, 0.0)
    return [jnp.concatenate([y_u[(d, g)] for g in range(N_GROUPS)], axis=1)
            for d in range(len(preps))]


def _scan_kernel(pf_ref, pb_ref, s0_ref, lw_ref, vec_ref, ho_ref, yf_ref, yb_ref, sfin_ref, s_scr):
    c = pl.program_id(1)

    @pl.when(c == 0)
    def _():
        s_scr[...] = s0_ref[...]

    blocks, s_idx = [], []
    for bb in range(SCAN_BATCH):
        blocks += [(pf_ref[bb].astype(F32), False), (pb_ref[bb].astype(F32), True)]
        s_idx += [(bb, 0), (bb, 1)]
    preps = _scan_prep(blocks, lw_ref, vec_ref, ho_ref[...])
    ys = _scan_units(preps, s_idx, s_scr)
    for bb in range(SCAN_BATCH):
        yf_ref[bb] = ys[2 * bb]
        yb_ref[bb] = ys[2 * bb + 1]

    @pl.when(c == pl.num_programs(1) - 1)
    def _():
        sfin_ref[...] = s_scr[...]


def _scan_call(p, s0, lora_w, vecs, head_ones):
    bsz, seq, _ = p.shape
    nc = seq // CHUNK
    state_block = (SCAN_BATCH, 2, N_GROUPS, GROUP_W, GROUP_W)
    state_map = lambda b, c: (b, 0, 0, 0, 0)
    return pl.pallas_call(
        _scan_kernel,
        grid=(bsz // SCAN_BATCH, nc),
        in_specs=[pl.BlockSpec((SCAN_BATCH, CHUNK, SCAN_COLS), lambda b, c: (b, c, 0)),
                  pl.BlockSpec((SCAN_BATCH, CHUNK, SCAN_COLS), lambda b, c: (b, nc - 1 - c, 0)),
                  pl.BlockSpec(state_block, state_map),
                  pl.BlockSpec((2, LORA_COLS, 2 * RWKV_WIDTH), lambda b, c: (0, 0, 0)),
                  pl.BlockSpec((2, 8, RWKV_WIDTH), lambda b, c: (0, 0, 0)),
                  pl.BlockSpec((GROUP_W, GROUP_W), lambda b, c: (0, 0))],
        out_specs=[pl.BlockSpec((SCAN_BATCH, CHUNK, RWKV_WIDTH), lambda b, c: (b, c, 0)),
                   pl.BlockSpec((SCAN_BATCH, CHUNK, RWKV_WIDTH), lambda b, c: (b, nc - 1 - c, 0)),
                   pl.BlockSpec(state_block, state_map)],
        out_shape=[jax.ShapeDtypeStruct((bsz, seq, RWKV_WIDTH), F32),
                   jax.ShapeDtypeStruct((bsz, seq, RWKV_WIDTH), F32),
                   jax.ShapeDtypeStruct((bsz, 2, N_GROUPS, GROUP_W, GROUP_W), F32)],
        scratch_shapes=[pltpu.VMEM(state_block, F32)],
        compiler_params=pltpu.CompilerParams(
            dimension_semantics=("parallel", "arbitrary"), vmem_limit_bytes=VMEM_LIMIT),
        name="wkv_scan",
    )(p, p, s0, lora_w, vecs, head_ones)


def _mix_kernel(x_ref, yf_ref, yb_ref, rkv_ref, sm_ref, cv_ref, mod_ref, rw_ref, vec_ref, ho_ref,
                cw_ref, cvec_ref, wout_ref, g_ref, o_ref, upad, *, tm):
    n_lines = tm // GRID_W
    head_ones = ho_ref[...]
    y = yf_ref[0] + yb_ref[0]
    inv_n = 1.0 / HEAD_SIZE
    mu = _head_sum(y, head_ones) * inv_n
    dy = y - mu
    var = _head_sum(dy * dy, head_ones) * inv_n
    lnx_w = vec_ref[0:1, :]
    lnx_b = vec_ref[1:2, :]
    a0_f = vec_ref[2:3, :]
    a0_b = vec_ref[3:4, :]
    k_a = vec_ref[4:5, :]
    r_k = vec_ref[5:6, :]
    yn = dy * lax.rsqrt(var + EPS_GN) * lnx_w + lnx_b
    sm = sm_ref[0].astype(F32)
    lane = lax.broadcasted_iota(jnp.int32, sm.shape, 1)
    sm = jnp.where(lane >= LORA_COLS, jax.nn.sigmoid(sm), sm)
    lin = _dot(sm, rw_ref[...])
    iclr_f = jax.nn.sigmoid(lin[:, 0:RWKV_WIDTH] + a0_f)
    iclr_b = jax.nn.sigmoid(lin[:, RWKV_WIDTH:2 * RWKV_WIDTH] + a0_b)
    gate = lin[:, 2 * RWKV_WIDTH:]
    r = rkv_ref[0, :, 0:RWKV_WIDTH].astype(F32)
    k = rkv_ref[0, :, RWKV_WIDTH:2 * RWKV_WIDTH].astype(F32)
    v = rkv_ref[0, :, 2 * RWKV_WIDTH:3 * RWKV_WIDTH].astype(F32)
    kd_f = k * (1.0 + (iclr_f - 1.0) * k_a)
    kd_b = k * (1.0 + (iclr_b - 1.0) * k_a)
    k_bar = 0.5 * (kd_f + kd_b)
    bonus = _head_sum(r * k_bar * r_k, head_ones) * v
    rw_out = (yn + bonus) * gate

    cv = cv_ref[0].astype(F32)
    u = cv[:, 0:CONV_WIDTH] * jax.nn.sigmoid(cv[:, CONV_WIDTH:])
    zeros = jnp.zeros((16, CONV_WIDTH), F32)
    for ln in range(n_lines):
        u_ln = u[ln * GRID_W:(ln + 1) * GRID_W, :]
        for s in range(8):
            upad[s, ln, 0:16, :] = zeros
            upad[s, ln, CONV_ROWS - 16:CONV_ROWS, :] = zeros
            upad[s, ln, 16 - s:16 - s + GRID_W, :] = u_ln
    conv_b = cvec_ref[0:1, :]
    ln_w = cvec_ref[1:2, :]
    ln_b = cvec_ref[2:3, :]
    conv_lines = []
    for ln in range(n_lines):
        acc = jnp.zeros((GRID_W, CONV_WIDTH), F32)
        for j in range(CONV_KERNEL):
            off = 16 - CONV_PAD + j
            q8 = (off // 8) * 8
            acc = acc + cw_ref[j:j + 1, :] * upad[off % 8, ln, q8:q8 + GRID_W, :]
        conv_lines.append(acc)
    yc = jnp.concatenate(conv_lines, axis=0) + conv_b
    mu_c = jnp.mean(yc, axis=-1, keepdims=True)
    dc = yc - mu_c
    var_c = jnp.mean(dc * dc, axis=-1, keepdims=True)
    conv_out = _silu(dc * lax.rsqrt(var_c + EPS_LN) * ln_w + ln_b)

    mix_in = jnp.concatenate([rw_out.astype(BF16), conv_out.astype(BF16)], axis=1)
    mix = jnp.dot(mix_in, wout_ref[...], preferred_element_type=F32)
    gate_mix = mod_ref[0, 2:3, :]
    o_ref[0] = x_ref[0] + gate_mix * _rms(mix, g_ref[...])


def _mix_call(x, y_f, y_b, p, mods, rw_w, rvecs, head_ones, conv_w, cvecs, w_out, post_g):
    bsz, seq, _ = x.shape
    tm = 512
    tok = lambda width, blk: pl.BlockSpec((1, tm, width), lambda b, t: (b, t, blk))
    const2 = lambda shape: pl.BlockSpec(shape, lambda b, t: (0, 0))
    return pl.pallas_call(
        functools.partial(_mix_kernel, tm=tm),
        grid=(bsz, seq // tm),
        in_specs=[tok(D_MODEL, 0), tok(RWKV_WIDTH, 0), tok(RWKV_WIDTH, 0),
                  tok(3 * RWKV_WIDTH, 0),
                  tok(512, LORA_OFF // 512),
                  tok(2 * CONV_WIDTH, CONV_OFF // (2 * CONV_WIDTH)),
                  pl.BlockSpec((1, N_MOD, D_MODEL), lambda b, t: (b, 0, 0)),
                  const2((512, 3 * RWKV_WIDTH)), const2((8, RWKV_WIDTH)),
                  const2((RWKV_WIDTH, RWKV_WIDTH)), const2((32, CONV_WIDTH)),
                  const2((8, CONV_WIDTH)), const2((D_MODEL, D_MODEL)), const2((1, D_MODEL))],
        out_specs=tok(D_MODEL, 0),
        out_shape=jax.ShapeDtypeStruct((bsz, seq, D_MODEL), F32),
        scratch_shapes=[pltpu.VMEM((8, tm // GRID_W, CONV_ROWS, CONV_WIDTH), F32)],
        compiler_params=pltpu.CompilerParams(
            dimension_semantics=("parallel", "parallel"), vmem_limit_bytes=VMEM_LIMIT),
        name="mix_out",
    )(x, y_f, y_b, p, p, p, mods, rw_w, rvecs, head_ones, conv_w, cvecs, w_out, post_g)


def _mlp_kernel(x_ref, mod_ref, gpre_ref, gpost_ref, w1_ref, w2_ref, o_ref, acc_ref):
    x = x_ref[0]
    shift = mod_ref[0, 3:4, :]
    scale = mod_ref[0, 4:5, :]
    gate = mod_ref[0, 5:6, :]
    h = (_rms(x, gpre_ref[...]) * (1.0 + scale) + shift).astype(BF16)
    ff_blk = 1024
    for c in range(D_FF // ff_blk):
        hid = jnp.dot(h, w1_ref[:, c * ff_blk:(c + 1) * ff_blk], preferred_element_type=F32)
        hid = jnp.square(jnp.maximum(hid, 0.0)).astype(BF16)
        part = jnp.dot(hid, w2_ref[c * ff_blk:(c + 1) * ff_blk, :], preferred_element_type=F32)
        if c == 0:
            acc_ref[...] = part
        else:
            acc_ref[...] += part
    o_ref[0] = x + gate * _rms(acc_ref[...], gpost_ref[...])


def _mlp_call(x, mods, pre_g, post_g, w1, w2):
    bsz, seq, _ = x.shape
    tm = 512
    return pl.pallas_call(
        _mlp_kernel,
        grid=(bsz, seq // tm),
        in_specs=[pl.BlockSpec((1, tm, D_MODEL), lambda b, t: (b, t, 0)),
                  pl.BlockSpec((1, N_MOD, D_MODEL), lambda b, t: (b, 0, 0)),
                  pl.BlockSpec((1, D_MODEL), lambda b, t: (0, 0)),
                  pl.BlockSpec((1, D_MODEL), lambda b, t: (0, 0)),
                  pl.BlockSpec((D_MODEL, D_FF), lambda b, t: (0, 0)),
                  pl.BlockSpec((D_FF, D_MODEL), lambda b, t: (0, 0))],
        out_specs=pl.BlockSpec((1, tm, D_MODEL), lambda b, t: (b, t, 0)),
        out_shape=jax.ShapeDtypeStruct((bsz, seq, D_MODEL), F32),
        scratch_shapes=[pltpu.VMEM((tm, D_MODEL), F32)],
        compiler_params=pltpu.CompilerParams(
            dimension_semantics=("parallel", "parallel"), vmem_limit_bytes=VMEM_LIMIT),
        name="sqrelu_mlp",
    )(x, mods, pre_g, post_g, w1, w2)


def _pad_cols(a, lo, hi):
    pad = jnp.zeros(a.shape[:-1] + (GATE_PAD - GATE_RANK,), a.dtype)
    return jnp.concatenate([a[..., :lo], pad, a[..., lo:hi]], axis=-1)


def _layer(x, ctx, mods_x, mods_c, prm):
    (mix_pre_g, mix_post_g, mlp_pre_g, mlp_post_g, w_in, mu_prev, mu_next, decay_w0, decay_w2,
     iclr_a0, iclr_a2, k_k, k_a, r_k, gate_w2, lnx_w, lnx_b, conv_w, conv_b, conv_ln_w,
     conv_ln_b, w_out, mlp_w1, mlp_w2) = prm
    bsz = x.shape[0]
    shift_cols = GATE_OFF + GATE_RANK
    in_cols = w_in.shape[1]

    w_pad = _pad_cols(w_in, shift_cols, in_cols).astype(BF16)
    zc = jnp.zeros((2 * CONV_WIDTH,), F32)
    mp_pad = _pad_cols(jnp.concatenate([mu_prev, zc]), shift_cols, in_cols)[None, :]
    mn_pad = _pad_cols(jnp.concatenate([mu_next, zc]), shift_cols, in_cols)[None, :]
    zr = jnp.zeros((DECAY_RANK, RWKV_WIDTH), F32)
    lora_dir = []
    for d in range(2):
        dec_rows = [zr, zr]
        dec_rows[d] = decay_w2[d]
        icl_rows = [zr, zr]
        icl_rows[d] = iclr_a2[d]
        left = jnp.concatenate(dec_rows + [zr, zr], axis=0)
        right = jnp.concatenate([zr, zr] + icl_rows, axis=0)
        lora_dir.append(jnp.concatenate([left, right], axis=1))
    lora_w = jnp.stack(lora_dir).astype(BF16)
    z4 = jnp.zeros((4, RWKV_WIDTH), F32)
    scan_vecs = jnp.stack([jnp.concatenate([decay_w0[d][None], iclr_a0[d][None], k_k[None],
                                            k_a[None], z4], axis=0) for d in range(2)])
    hid = jnp.arange(RWKV_WIDTH) // HEAD_SIZE
    head_ones = (hid[:, None] == hid[None, :]).astype(BF16)
    zl = jnp.zeros((2 * DECAY_RANK, 3 * RWKV_WIDTH), F32)
    za = jnp.zeros((ICLR_RANK, RWKV_WIDTH), F32)
    zg = jnp.zeros((GATE_PAD, RWKV_WIDTH), F32)
    gate_pad = jnp.concatenate([gate_w2, jnp.zeros((GATE_PAD - GATE_RANK, RWKV_WIDTH), F32)], axis=0)
    rw_w = jnp.concatenate([
        zl,
        jnp.concatenate([iclr_a2[0], za, za], axis=1),
        jnp.concatenate([za, iclr_a2[1], za], axis=1),
        jnp.concatenate([zg, zg, gate_pad], axis=1)], axis=0).astype(BF16)
    rvecs = jnp.concatenate([lnx_w[None], lnx_b[None], iclr_a0[0][None], iclr_a0[1][None],
                             k_a[None], r_k.reshape(1, RWKV_WIDTH), jnp.zeros((2, RWKV_WIDTH), F32)],
                            axis=0)
    conv_w_pad = jnp.concatenate([conv_w, jnp.zeros((1, CONV_WIDTH), F32)], axis=0)
    cvecs = jnp.concatenate([conv_b[None], conv_ln_w[None], conv_ln_b[None],
                             jnp.zeros((5, CONV_WIDTH), F32)], axis=0)

    p_x = _proj_call(x, mods_x, mix_pre_g[None], w_pad, mp_pad, mn_pad, P_COLS // PROJ_BLOCK)
    p_c = _proj_call(ctx, mods_c, mix_pre_g[None], w_pad, mp_pad, mn_pad, N_SHIFT_BLOCKS)
    s_zero = jnp.zeros((bsz, 2, N_GROUPS, GROUP_W, GROUP_W), F32)
    _, _, s_ctx = _scan_call(p_c, s_zero, lora_w, scan_vecs, head_ones)
    y_f, y_b, _ = _scan_call(p_x, s_ctx, lora_w, scan_vecs, head_ones)
    x = _mix_call(x, y_f, y_b, p_x, mods_x, rw_w, rvecs, head_ones, conv_w_pad, cvecs,
                  w_out.astype(BF16), mix_post_g[None])
    x = _mlp_call(x, mods_x, mlp_pre_g[None], mlp_post_g[None], mlp_w1.astype(BF16),
                  mlp_w2.astype(BF16))
    return x


def kernel(x, c, ctx, c_ctx, ada_w, ada_b, mix_pre_g, mix_post_g, mlp_pre_g, mlp_post_g, w_in, mu_prev, mu_next, decay_w0, decay_w2, iclr_a0, iclr_a2, k_k, k_a, r_k, gate_w2, lnx_w, lnx_b, conv_w, conv_b, conv_ln_w, conv_ln_b, w_out, mlp_w1, mlp_w2):
    depth = ada_w.shape[0]
    assert depth == 1, "context-stream update between layers is not implemented"
    bsz = x.shape[0]
    rows = 8 * ((bsz + 1 + 7) // 8)
    cc = jnp.concatenate([c, c_ctx[None, :], jnp.zeros((rows - bsz - 1, D_MODEL), F32)], axis=0)
    per_layer = (mix_pre_g, mix_post_g, mlp_pre_g, mlp_post_g, w_in, mu_prev, mu_next, decay_w0,
                 decay_w2, iclr_a0, iclr_a2, k_k, k_a, r_k, gate_w2, lnx_w, lnx_b, conv_w, conv_b,
                 conv_ln_w, conv_ln_b, w_out, mlp_w1, mlp_w2)
    for l in range(depth):
        mods = _ada_call(cc, ada_w[l], ada_b[l][None, :]).reshape(rows, N_MOD, D_MODEL)
        mods_x = mods[:bsz]
        mods_c = jnp.broadcast_to(mods[bsz:bsz + 1], (bsz, N_MOD, D_MODEL))
        x = _layer(x, ctx, mods_x, mods_c, tuple(a[l] for a in per_layer))
    return x
```

```python
import functools

import jax
import jax.numpy as jnp
from jax import lax
from jax.experimental import pallas as pl
from jax.experimental.pallas import tpu as pltpu

F32 = jnp.float32
BF16 = jnp.bfloat16

D_MODEL = 1024
RWKV_WIDTH = 512
CONV_WIDTH = 512
HEAD_SIZE = 64
RWKV_HEADS = 8
DECAY_RANK = 64
ICLR_RANK = 64
GATE_RANK = 160
CONV_KERNEL = 31
CONV_PAD = CONV_KERNEL // 2
GRID_W = 64
CONV_ROWS = GRID_W + 24
D_FF = 4 * D_MODEL
N_MOD = 6
EPS_RMS = 1e-6
EPS_LN = 1e-5
EPS_GN = 64e-5
DECAY_SCALE = 0.6065306597126334

LORA_OFF = 3 * RWKV_WIDTH
LORA_COLS = 256
GATE_OFF = LORA_OFF + LORA_COLS
GATE_PAD = 256
CONV_OFF = GATE_OFF + GATE_PAD
P_COLS = CONV_OFF + 2 * CONV_WIDTH
SCAN_COLS = GATE_OFF
PROJ_BLOCK = 512
N_SHIFT_BLOCKS = CONV_OFF // PROJ_BLOCK

CHUNK = 64
GROUP_HEADS = 4
GROUP_W = GROUP_HEADS * HEAD_SIZE
N_GROUPS = RWKV_HEADS // GROUP_HEADS
SCAN_BATCH = 4
MIX_TILE = 256

VMEM_LIMIT = 56 * 1024 * 1024


def _silu(x):
    return x * jax.nn.sigmoid(x)


def _rms(x, g):
    return x * lax.rsqrt(jnp.mean(x * x, axis=-1, keepdims=True) + EPS_RMS) * g


def _dot(a, b):
    return jnp.dot(a.astype(BF16), b.astype(BF16), preferred_element_type=F32)


def _dot_nt(a, b):
    return lax.dot_general(a.astype(BF16), b.astype(BF16), (((1,), (1,)), ((), ())),
                           preferred_element_type=F32)


def _dot_tn(a, b):
    return lax.dot_general(a.astype(BF16), b.astype(BF16), (((0,), (0,)), ((), ())),
                           preferred_element_type=F32)


def _split3(x):
    hi = x.astype(BF16)
    r1 = x - hi.astype(F32)
    mid = r1.astype(BF16)
    lo = (r1 - mid.astype(F32)).astype(BF16)
    return hi, mid, lo


def _dot_exact_rhs(a_bf16, x):
    hi, mid, lo = _split3(x)
    d = lambda p: jnp.dot(a_bf16, p, preferred_element_type=F32)
    return d(hi) + d(mid) + d(lo)


def _ada_kernel(c_ref, w_ref, b_ref, o_ref):
    o_ref[...] = _dot(_silu(c_ref[...]), w_ref[...]) + b_ref[...]


def _ada_call(cc, ada_w, ada_b):
    rows = cc.shape[0]
    n_out = ada_w.shape[1]
    blk = 1536
    return pl.pallas_call(
        _ada_kernel,
        grid=(n_out // blk,),
        in_specs=[pl.BlockSpec((rows, D_MODEL), lambda n: (0, 0)),
                  pl.BlockSpec((D_MODEL, blk), lambda n: (0, n)),
                  pl.BlockSpec((1, blk), lambda n: (0, n))],
        out_specs=pl.BlockSpec((rows, blk), lambda n: (0, n)),
        out_shape=jax.ShapeDtypeStruct((rows, n_out), F32),
        compiler_params=pltpu.CompilerParams(vmem_limit_bytes=VMEM_LIMIT),
        name="ada_mod",
    )(cc, ada_w, ada_b)


def _proj_kernel(x_ref, mod_ref, g_ref, w_ref, mp_ref, mn_ref, o_ref, h_scr, z_scr, *, seq):
    n = pl.program_id(1)
    row_chunk = min(seq, 256)

    @pl.when(n == 0)
    def _():
        shift = mod_ref[0, 0:1, :]
        scale = mod_ref[0, 1:2, :]
        g = g_ref[...]
        for s in range(0, seq, row_chunk):
            xs = x_ref[0, s:s + row_chunk, :]
            h_scr[s:s + row_chunk, :] = (_rms(xs, g) * (1.0 + scale) + shift).astype(BF16)

    @pl.when(n < N_SHIFT_BLOCKS)
    def _():
        z_scr[0:8, :] = jnp.zeros((8, PROJ_BLOCK), F32)
        z_scr[8 + seq:16 + seq, :] = jnp.zeros((8, PROJ_BLOCK), F32)
        mp = mp_ref[...]
        mn = mn_ref[...]

        def shift_rows(s):
            cur = z_scr[8 + s:8 + s + row_chunk, :]
            prv = z_scr[7 + s:7 + s + row_chunk, :]
            nxt = z_scr[9 + s:9 + s + row_chunk, :]
            o_ref[0, s:s + row_chunk, :] = (cur + mp * (prv - cur) + mn * (nxt - cur)).astype(BF16)

        for s in range(0, seq, row_chunk):
            z_scr[8 + s:8 + s + row_chunk, :] = jnp.dot(
                h_scr[s:s + row_chunk, :], w_ref[...], preferred_element_type=F32)
            if s > 0:
                shift_rows(s - row_chunk)
        shift_rows(seq - row_chunk)

    @pl.when(n >= N_SHIFT_BLOCKS)
    def _():
        for s in range(0, seq, row_chunk):
            o_ref[0, s:s + row_chunk, :] = jnp.dot(
                h_scr[s:s + row_chunk, :], w_ref[...], preferred_element_type=F32).astype(BF16)


def _proj_call(x, mods, g, w_pad, mp_pad, mn_pad, n_blocks):
    bsz, seq, _ = x.shape
    n_cols = n_blocks * PROJ_BLOCK
    return pl.pallas_call(
        functools.partial(_proj_kernel, seq=seq),
        grid=(bsz, n_blocks),
        in_specs=[pl.BlockSpec((1, seq, D_MODEL), lambda b, n: (b, 0, 0)),
                  pl.BlockSpec((1, N_MOD, D_MODEL), lambda b, n: (b, 0, 0)),
                  pl.BlockSpec((1, D_MODEL), lambda b, n: (0, 0)),
                  pl.BlockSpec((D_MODEL, PROJ_BLOCK), lambda b, n: (0, n)),
                  pl.BlockSpec((1, PROJ_BLOCK), lambda b, n: (0, n)),
                  pl.BlockSpec((1, PROJ_BLOCK), lambda b, n: (0, n))],
        out_specs=pl.BlockSpec((1, seq, PROJ_BLOCK), lambda b, n: (b, 0, n)),
        out_shape=jax.ShapeDtypeStruct((bsz, seq, n_cols), BF16),
        scratch_shapes=[pltpu.VMEM((seq, D_MODEL), BF16),
                        pltpu.VMEM((seq + 16, PROJ_BLOCK), F32)],
        compiler_params=pltpu.CompilerParams(
            dimension_semantics=("parallel", "arbitrary"), vmem_limit_bytes=VMEM_LIMIT),
        name="in_proj",
    )(x, mods, g, w_pad, mp_pad, mn_pad)


def _expand(x, bd_mask):
    return jnp.where(bd_mask, jnp.concatenate([x] * GROUP_HEADS, axis=0), 0.0)


def _scan_prep(blocks, lw_ref, vec_ref, ones_bd):
    L = CHUNK
    n = len(blocks)
    lane = lax.broadcasted_iota(jnp.int32, (L, LORA_COLS), 1)
    loras = []
    for pb, _ in blocks:
        lora = pb[:, LORA_OFF:LORA_OFF + LORA_COLS]
        loras.append(jnp.where(lane < 2 * DECAY_RANK, jnp.tanh(lora), lora))
    lin = [None] * n
    for d in range(2):
        idx = [i for i in range(n) if blocks[i][1] == bool(d)]
        res = _dot(jnp.concatenate([loras[i] for i in idx], axis=0), lw_ref[d])
        for j, i in enumerate(idx):
            lin[i] = res[j * L:(j + 1) * L]

    kk0s, pieces = [], []
    for i, (pb, reverse) in enumerate(blocks):
        kk0 = pb[:, RWKV_WIDTH:2 * RWKV_WIDTH] * vec_ref[int(reverse), 2:3, :]
        kk0s.append(kk0)
        sq = kk0 * kk0
        hi = sq.astype(BF16)
        lo = (sq - hi.astype(F32)).astype(BF16)
        for g in range(N_GROUPS):
            pieces += [hi[:, g * GROUP_W:(g + 1) * GROUP_W], lo[:, g * GROUP_W:(g + 1) * GROUP_W]]
    sums = jnp.dot(jnp.concatenate(pieces, axis=0), ones_bd, preferred_element_type=F32)

    row = lax.broadcasted_iota(jnp.int32, (L, L), 0)
    col = lax.broadcasted_iota(jnp.int32, (L, L), 1)
    tr = lax.broadcasted_iota(jnp.int32, (L, GROUP_W), 0)
    tc = jnp.bitwise_and(lax.broadcasted_iota(jnp.int32, (L, GROUP_W), 1), L - 1)
    preps = []
    for i, (pb, reverse) in enumerate(blocks):
        vecs = vec_ref[int(reverse)]
        w0 = vecs[0:1, :]
        a0 = vecs[1:2, :]
        k_a = vecs[3:4, :]
        r = pb[:, 0:RWKV_WIDTH]
        k = pb[:, RWKV_WIDTH:2 * RWKV_WIDTH]
        v = pb[:, 2 * RWKV_WIDTH:3 * RWKV_WIDTH]
        lw = -DECAY_SCALE * jax.nn.sigmoid(lin[i][:, 0:RWKV_WIDTH] + w0)
        iclr = jax.nn.sigmoid(lin[i][:, RWKV_WIDTH:] + a0)
        base = i * 2 * N_GROUPS * L
        ss = jnp.concatenate(
            [sums[base + (2 * g) * L:base + (2 * g + 1) * L]
             + sums[base + (2 * g + 1) * L:base + (2 * g + 2) * L] for g in range(N_GROUPS)], axis=1)
        kk = kk0s[i] * lax.rsqrt(jnp.maximum(ss, 1e-24))
        kd = k * (1.0 + (iclr - 1.0) * k_a)
        a = -kk
        b = kk * iclr
        tri = jnp.where((col >= row) if reverse else (col <= row), 1.0, 0.0).astype(BF16)
        cum = _dot_exact_rhs(tri, lw)
        tot = cum[0:1, :] if reverse else cum[L - 1:L, :]
        e_neg = jnp.exp(-cum)
        e_rem = jnp.exp(tot - cum)
        preps.append(dict(
            r_t=r * jnp.exp(cum), a_t=a * jnp.exp(cum - lw), b_t=b * e_neg, k_t=kd * e_neg,
            b_h=b * e_rem, k_h=kd * e_rem, v=v, w_tot=jnp.exp(tot),
            strict=(tc > tr) if reverse else (tc < tr),
            incl=(tc >= tr) if reverse else (tc <= tr)))
    return preps


def _scan_units(preps, s_idx, s_ref):
    L = CHUNK
    rr = lax.broadcasted_iota(jnp.int32, (GROUP_W, GROUP_W), 0)
    cc = lax.broadcasted_iota(jnp.int32, (GROUP_W, GROUP_W), 1)
    bd_mask = jnp.right_shift(rr, 6) == jnp.right_shift(cc, 6)
    tr = lax.broadcasted_iota(jnp.int32, (L, GROUP_W), 0)
    tc = jnp.bitwise_and(lax.broadcasted_iota(jnp.int32, (L, GROUP_W), 1), L - 1)
    eye = jnp.where(tc == tr, 1.0, 0.0)
    units = [(d, g) for d in range(len(preps)) for g in range(N_GROUPS)]
    sl = lambda g: slice(g * GROUP_W, (g + 1) * GROUP_W)
    ex = lambda x: _expand(x, bd_mask)

    lhs, a_ab, a_ak, a_rb, a_rk = {}, {}, {}, {}, {}
    for u in units:
        d, g = u
        p = preps[d]
        lhs[u] = jnp.concatenate([p["a_t"][:, sl(g)], p["r_t"][:, sl(g)]], axis=0)
        e_bk = jnp.concatenate([ex(p["b_t"][:, sl(g)]), ex(p["k_t"][:, sl(g)])], axis=0)
        a_all = _dot_nt(lhs[u], e_bk)
        a_ab[u] = jnp.where(p["strict"], a_all[0:L, 0:GROUP_W], 0.0)
        a_ak[u] = jnp.where(p["strict"], a_all[0:L, GROUP_W:], 0.0)
        a_rb[u] = jnp.where(p["incl"], a_all[L:, 0:GROUP_W], 0.0)
        a_rk[u] = jnp.where(p["incl"], a_all[L:, GROUP_W:], 0.0)

    n_levels = 6
    npow = {u: _dot(a_ab[u], ex(a_ab[u])) for u in units}
    pinv = {u: eye + a_ab[u] for u in units}
    for lvl in range(1, n_levels):
        for u in units:
            rhs = ex(npow[u])
            if lvl < n_levels - 1:
                both = _dot(jnp.concatenate([npow[u], pinv[u]], axis=0), rhs)
                npow[u] = both[0:L]
                pinv[u] = pinv[u] + both[L:]
            else:
                pinv[u] = pinv[u] + _dot(pinv[u], rhs)

    s_old, sa, e_v, x_u, u_u, y_u = {}, {}, {}, {}, {}, {}
    for u in units:
        d, g = u
        s_old[u] = s_ref[s_idx[d] + (g,)]
        sa[u] = _dot_nt(lhs[u], s_old[u])
        e_v[u] = ex(preps[d]["v"][:, sl(g)])
    for u in units:
        x_u[u] = sa[u][0:L] + _dot(a_ak[u], e_v[u])
    for u in units:
        u_u[u] = _dot(pinv[u], ex(x_u[u]))
    for u in units:
        y_u[u] = sa[u][L:] + _dot(jnp.concatenate([a_rb[u], a_rk[u]], axis=1),
                                  jnp.concatenate([ex(u_u[u]), e_v[u]], axis=0))
    for u in units:
        d, g = u
        p = preps[d]
        upd = _dot_tn(jnp.concatenate([u_u[u], p["v"][:, sl(g)]], axis=0),
                      jnp.concatenate([p["b_h"][:, sl(g)], p["k_h"][:, sl(g)]], axis=0))
        s_ref[s_idx[d] + (g,)] = s_old[u] * p["w_tot"][:, sl(g)] + jnp.where(bd_mask, upd, 0.0)
    return [jnp.concatenate([y_u[(d, g)] for g in range(N_GROUPS)], axis=1)
            for d in range(len(preps))]


def _scan_kernel(pf_ref, pb_ref, s0_ref, lw_ref, vec_ref, ho_ref, yf_ref, yb_ref, sfin_ref, s_scr):
    c = pl.program_id(1)

    @pl.when(c == 0)
    def _():
        s_scr[...] = s0_ref[...]

    blocks, s_idx = [], []
    for bb in range(SCAN_BATCH):
        blocks += [(pf_ref[bb].astype(F32), False), (pb_ref[bb].astype(F32), True)]
        s_idx += [(bb, 0), (bb, 1)]
    preps = _scan_prep(blocks, lw_ref, vec_ref, ho_ref[...])
    ys = _scan_units(preps, s_idx, s_scr)
    for bb in range(SCAN_BATCH):
        yf_ref[bb] = ys[2 * bb]
        yb_ref[bb] = ys[2 * bb + 1]

    @pl.when(c == pl.num_programs(1) - 1)
    def _():
        sfin_ref[...] = s_scr[...]


def _scan_call(p, s0, lora_w, vecs, head_ones):
    bsz, seq, _ = p.shape
    nc = seq // CHUNK
    state_block = (SCAN_BATCH, 2, N_GROUPS, GROUP_W, GROUP_W)
    state_map = lambda b, c: (b, 0, 0, 0, 0)
    return pl.pallas_call(
        _scan_kernel,
        grid=(bsz // SCAN_BATCH, nc),
        in_specs=[pl.BlockSpec((SCAN_BATCH, CHUNK, SCAN_COLS), lambda b, c: (b, c, 0)),
                  pl.BlockSpec((SCAN_BATCH, CHUNK, SCAN_COLS), lambda b, c: (b, nc - 1 - c, 0)),
                  pl.BlockSpec(state_block, state_map),
                  pl.BlockSpec((2, LORA_COLS, 2 * RWKV_WIDTH), lambda b, c: (0, 0, 0)),
                  pl.BlockSpec((2, 8, RWKV_WIDTH), lambda b, c: (0, 0, 0)),
                  pl.BlockSpec((GROUP_W, GROUP_W), lambda b, c: (0, 0))],
        out_specs=[pl.BlockSpec((SCAN_BATCH, CHUNK, RWKV_WIDTH), lambda b, c: (b, c, 0)),
                   pl.BlockSpec((SCAN_BATCH, CHUNK, RWKV_WIDTH), lambda b, c: (b, nc - 1 - c, 0)),
                   pl.BlockSpec(state_block, state_map)],
        out_shape=[jax.ShapeDtypeStruct((bsz, seq, RWKV_WIDTH), F32),
                   jax.ShapeDtypeStruct((bsz, seq, RWKV_WIDTH), F32),
                   jax.ShapeDtypeStruct((bsz, 2, N_GROUPS, GROUP_W, GROUP_W), F32)],
        scratch_shapes=[pltpu.VMEM(state_block, F32)],
        compiler_params=pltpu.CompilerParams(
            dimension_semantics=("parallel", "arbitrary"), vmem_limit_bytes=VMEM_LIMIT),
        name="wkv_scan",
    )(p, p, s0, lora_w, vecs, head_ones)


def _head_sum_bd(z, ones_bd):
    rows = z.shape[0]
    hi = z.astype(BF16)
    lo = (z - hi.astype(F32)).astype(BF16)
    pieces = []
    for g in range(N_GROUPS):
        pieces += [hi[:, g * GROUP_W:(g + 1) * GROUP_W], lo[:, g * GROUP_W:(g + 1) * GROUP_W]]
    res = jnp.dot(jnp.concatenate(pieces, axis=0), ones_bd, preferred_element_type=F32)
    return jnp.concatenate(
        [res[(2 * g) * rows:(2 * g + 1) * rows] + res[(2 * g + 1) * rows:(2 * g + 2) * rows]
         for g in range(N_GROUPS)], axis=1)


def _mixmlp_kernel(x_ref, yf_ref, yb_ref, rkv_ref, sm_ref, cv_ref, moda_ref, modb_ref, rw_ref,
                   vec_ref, ho_ref, cw_ref, cvec_ref, wout_ref, gmix_ref, gpre_ref, gpost_ref,
                   w1_ref, w2_ref, o_ref, x1_scr, upad, mixin_scr, acc_ref, *, tm):
    n_lines = tm // GRID_W
    n_chunks = 4
    ff_blk = D_FF // n_chunks

    @pl.when(pl.program_id(0) == 0)
    def _():
        x1_scr[...] = jnp.zeros_like(x1_scr)

    x1 = x1_scr[...]
    h = (_rms(x1, gpre_ref[...]) * (1.0 + modb_ref[0, 4:5, :]) + modb_ref[0, 3:4, :]).astype(BF16)
    o_ref[0] = x1

    def mlp_chunk(c):
        hid = jnp.dot(h, w1_ref[:, c * ff_blk:(c + 1) * ff_blk], preferred_element_type=F32)
        hid = jnp.square(jnp.maximum(hid, 0.0)).astype(BF16)
        part = jnp.dot(hid, w2_ref[c * ff_blk:(c + 1) * ff_blk, :], preferred_element_type=F32)
        if c == 0:
            acc_ref[...] = part
        else:
            acc_ref[...] += part

    ones_bd = ho_ref[...]
    lnx_w = vec_ref[0:1, :]
    lnx_b = vec_ref[1:2, :]
    a0_f = vec_ref[2:3, :]
    a0_b = vec_ref[3:4, :]
    k_a = vec_ref[4:5, :]
    r_k = vec_ref[5:6, :]

    def readout():
        y = yf_ref[0] + yb_ref[0]
        inv_n = 1.0 / HEAD_SIZE
        dy = y - _head_sum_bd(y, ones_bd) * inv_n
        var = _head_sum_bd(dy * dy, ones_bd) * inv_n
        yn = dy * lax.rsqrt(var + EPS_GN) * lnx_w + lnx_b
        sm = sm_ref[0].astype(F32)
        lane = lax.broadcasted_iota(jnp.int32, sm.shape, 1)
        sm = jnp.where(lane >= LORA_COLS, jax.nn.sigmoid(sm), sm)
        lin = _dot(sm, rw_ref[...])
        iclr_f = jax.nn.sigmoid(lin[:, 0:RWKV_WIDTH] + a0_f)
        iclr_b = jax.nn.sigmoid(lin[:, RWKV_WIDTH:2 * RWKV_WIDTH] + a0_b)
        gate = lin[:, 2 * RWKV_WIDTH:]
        r = rkv_ref[0, :, 0:RWKV_WIDTH].astype(F32)
        k = rkv_ref[0, :, RWKV_WIDTH:2 * RWKV_WIDTH].astype(F32)
        v = rkv_ref[0, :, 2 * RWKV_WIDTH:3 * RWKV_WIDTH].astype(F32)
        k_bar = k * (1.0 + (0.5 * (iclr_f + iclr_b) - 1.0) * k_a)
        bonus = _head_sum_bd(r * k_bar * r_k, ones_bd) * v
        mixin_scr[:, 0:RWKV_WIDTH] = ((yn + bonus) * gate).astype(BF16)

    conv_b = cvec_ref[0:1, :]
    ln_w = cvec_ref[1:2, :]
    ln_b = cvec_ref[2:3, :]
    zeros = jnp.zeros((16, CONV_WIDTH), F32)

    def conv_line(ln):
        rows = slice(ln * GRID_W, (ln + 1) * GRID_W)
        u = (cv_ref[0, rows, 0:CONV_WIDTH].astype(F32)
             * jax.nn.sigmoid(cv_ref[0, rows, CONV_WIDTH:].astype(F32)))
        for s in range(8):
            upad[s, ln, 0:16, :] = zeros
            upad[s, ln, CONV_ROWS - 16:CONV_ROWS, :] = zeros
            upad[s, ln, 16 - s:16 - s + GRID_W, :] = u
        acc = jnp.zeros((GRID_W, CONV_WIDTH), F32)
        for j in range(CONV_KERNEL):
            off = 16 - CONV_PAD + j
            q8 = (off // 8) * 8
            acc = acc + cw_ref[j:j + 1, :] * upad[off % 8, ln, q8:q8 + GRID_W, :]
        yc = acc + conv_b
        dc = yc - jnp.mean(yc, axis=-1, keepdims=True)
        var_c = jnp.mean(dc * dc, axis=-1, keepdims=True)
        mixin_scr[rows, RWKV_WIDTH:] = _silu(dc * lax.rsqrt(var_c + EPS_LN) * ln_w + ln_b).astype(BF16)

    readout()
    for ln in range(n_lines):
        conv_line(ln)
    for c in range(n_chunks):
        mlp_chunk(c)

    mix = jnp.dot(mixin_scr[...], wout_ref[...], preferred_element_type=F32)
    x1_scr[...] = x_ref[0] + moda_ref[0, 2:3, :] * _rms(mix, gmix_ref[...])

    o_ref[0] = o_ref[0] + modb_ref[0, 5:6, :] * _rms(acc_ref[...], gpost_ref[...])


def _mixmlp_call(x, y_f, y_b, p, mods, rw_w, rvecs, head_ones, conv_w, cvecs, w_out, mix_post_g,
                 mlp_pre_g, mlp_post_g, w1, w2):
    bsz, seq, _ = x.shape
    tm = MIX_TILE
    tpb = seq // tm
    nt = bsz * tpb
    cur = lambda s: jnp.minimum(s, nt - 1)
    prv = lambda s: jnp.maximum(s - 1, 0)
    tok = lambda width, blk: pl.BlockSpec(
        (1, tm, width), lambda s: (cur(s) // tpb, cur(s) % tpb, blk))
    const2 = lambda shape: pl.BlockSpec(shape, lambda s: (0, 0), pipeline_mode=pl.Buffered(1))
    return pl.pallas_call(
        functools.partial(_mixmlp_kernel, tm=tm),
        grid=(nt + 1,),
        in_specs=[tok(D_MODEL, 0), tok(RWKV_WIDTH, 0), tok(RWKV_WIDTH, 0),
                  tok(3 * RWKV_WIDTH, 0),
                  tok(512, LORA_OFF // 512),
                  tok(2 * CONV_WIDTH, CONV_OFF // (2 * CONV_WIDTH)),
                  pl.BlockSpec((1, N_MOD, D_MODEL), lambda s: (cur(s) // tpb, 0, 0)),
                  pl.BlockSpec((1, N_MOD, D_MODEL), lambda s: (prv(s) // tpb, 0, 0)),
                  const2((512, 3 * RWKV_WIDTH)), const2((8, RWKV_WIDTH)),
                  const2((GROUP_W, GROUP_W)), const2((32, CONV_WIDTH)),
                  const2((8, CONV_WIDTH)), const2((D_MODEL, D_MODEL)), const2((1, D_MODEL)),
                  const2((1, D_MODEL)), const2((1, D_MODEL)),
                  const2((D_MODEL, D_FF)), const2((D_FF, D_MODEL))],
        out_specs=pl.BlockSpec((1, tm, D_MODEL), lambda s: (prv(s) // tpb, prv(s) % tpb, 0)),
        out_shape=jax.ShapeDtypeStruct((bsz, seq, D_MODEL), F32),
        scratch_shapes=[pltpu.VMEM((tm, D_MODEL), F32),
                        pltpu.VMEM((8, tm // GRID_W, CONV_ROWS, CONV_WIDTH), F32),
                        pltpu.VMEM((tm, D_MODEL), BF16),
                        pltpu.VMEM((tm, D_MODEL), F32)],
        compiler_params=pltpu.CompilerParams(
            dimension_semantics=("arbitrary",), vmem_limit_bytes=VMEM_LIMIT),
        name="mix_mlp",
    )(x, y_f, y_b, p, p, p, mods, mods, rw_w, rvecs, head_ones, conv_w, cvecs, w_out, mix_post_g,
      mlp_pre_g, mlp_post_g, w1, w2)


def _pad_cols(a, lo, hi):
    pad = jnp.zeros(a.shape[:-1] + (GATE_PAD - GATE_RANK,), a.dtype)
    return jnp.concatenate([a[..., :lo], pad, a[..., lo:hi]], axis=-1)


def _layer(x, ctx, mods_x, mods_c, prm):
    (mix_pre_g, mix_post_g, mlp_pre_g, mlp_post_g, w_in, mu_prev, mu_next, decay_w0, decay_w2,
     iclr_a0, iclr_a2, k_k, k_a, r_k, gate_w2, lnx_w, lnx_b, conv_w, conv_b, conv_ln_w,
     conv_ln_b, w_out, mlp_w1, mlp_w2) = prm
    bsz = x.shape[0]
    shift_cols = GATE_OFF + GATE_RANK
    in_cols = w_in.shape[1]

    w_pad = _pad_cols(w_in, shift_cols, in_cols).astype(BF16)
    zc = jnp.zeros((2 * CONV_WIDTH,), F32)
    mp_pad = _pad_cols(jnp.concatenate([mu_prev, zc]), shift_cols, in_cols)[None, :]
    mn_pad = _pad_cols(jnp.concatenate([mu_next, zc]), shift_cols, in_cols)[None, :]
    zr = jnp.zeros((DECAY_RANK, RWKV_WIDTH), F32)
    lora_dir = []
    for d in range(2):
        dec_rows = [zr, zr]
        dec_rows[d] = decay_w2[d]
        icl_rows = [zr, zr]
        icl_rows[d] = iclr_a2[d]
        left = jnp.concatenate(dec_rows + [zr, zr], axis=0)
        right = jnp.concatenate([zr, zr] + icl_rows, axis=0)
        lora_dir.append(jnp.concatenate([left, right], axis=1))
    lora_w = jnp.stack(lora_dir).astype(BF16)
    z4 = jnp.zeros((4, RWKV_WIDTH), F32)
    scan_vecs = jnp.stack([jnp.concatenate([decay_w0[d][None], iclr_a0[d][None], k_k[None],
                                            k_a[None], z4], axis=0) for d in range(2)])
    hid = jnp.arange(RWKV_WIDTH) // HEAD_SIZE
    head_ones = (hid[:, None] == hid[None, :]).astype(BF16)
    zl = jnp.zeros((2 * DECAY_RANK, 3 * RWKV_WIDTH), F32)
    za = jnp.zeros((ICLR_RANK, RWKV_WIDTH), F32)
    zg = jnp.zeros((GATE_PAD, RWKV_WIDTH), F32)
    gate_pad = jnp.concatenate([gate_w2, jnp.zeros((GATE_PAD - GATE_RANK, RWKV_WIDTH), F32)], axis=0)
    rw_w = jnp.concatenate([
        zl,
        jnp.concatenate([iclr_a2[0], za, za], axis=1),
        jnp.concatenate([za, iclr_a2[1], za], axis=1),
        jnp.concatenate([zg, zg, gate_pad], axis=1)], axis=0).astype(BF16)
    rvecs = jnp.concatenate([lnx_w[None], lnx_b[None], iclr_a0[0][None], iclr_a0[1][None],
                             k_a[None], r_k.reshape(1, RWKV_WIDTH), jnp.zeros((2, RWKV_WIDTH), F32)],
                            axis=0)
    conv_w_pad = jnp.concatenate([conv_w, jnp.zeros((1, CONV_WIDTH), F32)], axis=0)
    cvecs = jnp.concatenate([conv_b[None], conv_ln_w[None], conv_ln_b[None],
                             jnp.zeros((5, CONV_WIDTH), F32)], axis=0)

    p_x = _proj_call(x, mods_x, mix_pre_g[None], w_pad, mp_pad, mn_pad, P_COLS // PROJ_BLOCK)
    p_c = _proj_call(ctx, mods_c, mix_pre_g[None], w_pad, mp_pad, mn_pad, N_SHIFT_BLOCKS)
    s_zero = jnp.zeros((bsz, 2, N_GROUPS, GROUP_W, GROUP_W), F32)
    _, _, s_ctx = _scan_call(p_c, s_zero, lora_w, scan_vecs, head_ones)
    y_f, y_b, _ = _scan_call(p_x, s_ctx, lora_w, scan_vecs, head_ones)
    return _mixmlp_call(x, y_f, y_b, p_x, mods_x, rw_w, rvecs, head_ones, conv_w_pad, cvecs,
                        w_out.astype(BF16), mix_post_g[None], mlp_pre_g[None], mlp_post_g[None],
                        mlp_w1.astype(BF16), mlp_w2.astype(BF16))


def kernel(x, c, ctx, c_ctx, ada_w, ada_b, mix_pre_g, mix_post_g, mlp_pre_g, mlp_post_g, w_in, mu_prev, mu_next, decay_w0, decay_w2, iclr_a0, iclr_a2, k_k, k_a, r_k, gate_w2, lnx_w, lnx_b, conv_w, conv_b, conv_ln_w, conv_ln_b, w_out, mlp_w1, mlp_w2):
    depth = ada_w.shape[0]
    assert depth == 1, "context-stream update between layers is not implemented"
    bsz = x.shape[0]
    rows = 8 * ((bsz + 1 + 7) // 8)
    cc = jnp.concatenate([c, c_ctx[None, :], jnp.zeros((rows - bsz - 1, D_MODEL), F32)], axis=0)
    per_layer = (mix_pre_g, mix_post_g, mlp_pre_g, mlp_post_g, w_in, mu_prev, mu_next, decay_w0,
                 decay_w2, iclr_a0, iclr_a2, k_k, k_a, r_k, gate_w2, lnx_w, lnx_b, conv_w, conv_b,
                 conv_ln_w, conv_ln_b, w_out, mlp_w1, mlp_w2)
    for l in range(depth):
        mods = _ada_call(cc, ada_w[l], ada_b[l][None, :]).reshape(rows, N_MOD, D_MODEL)
        mods_x = mods[:bsz]
        mods_c = jnp.broadcast_to(mods[bsz:bsz + 1], (bsz, N_MOD, D_MODEL))
        x = _layer(x, ctx, mods_x, mods_c, tuple(a[l] for a in per_layer))
    return x
```

```python
import functools

import jax
import jax.numpy as jnp
from jax import lax
from jax.experimental import pallas as pl
from jax.experimental.pallas import tpu as pltpu

F32 = jnp.float32
BF16 = jnp.bfloat16

D_MODEL = 1024
RWKV_WIDTH = 512
CONV_WIDTH = 512
HEAD_SIZE = 64
RWKV_HEADS = 8
DECAY_RANK = 64
ICLR_RANK = 64
GATE_RANK = 160
CONV_KERNEL = 31
CONV_PAD = CONV_KERNEL // 2
GRID_W = 64
CONV_ROWS = GRID_W + 24
D_FF = 4 * D_MODEL
N_MOD = 6
EPS_RMS = 1e-6
EPS_LN = 1e-5
EPS_GN = 64e-5
DECAY_SCALE = 0.6065306597126334

LORA_OFF = 3 * RWKV_WIDTH
LORA_COLS = 256
GATE_OFF = LORA_OFF + LORA_COLS
GATE_PAD = 256
CONV_OFF = GATE_OFF + GATE_PAD
P_COLS = CONV_OFF + 2 * CONV_WIDTH
SCAN_COLS = GATE_OFF
PROJ_BLOCK = 512
N_SHIFT_BLOCKS = CONV_OFF // PROJ_BLOCK

CHUNK = 64
GROUP_HEADS = 4
GROUP_W = GROUP_HEADS * HEAD_SIZE
N_GROUPS = RWKV_HEADS // GROUP_HEADS
SCAN_BATCH = 4

VMEM_LIMIT = 56 * 1024 * 1024


def _silu(x):
    return x * jax.nn.sigmoid(x)


def _rms(x, g):
    return x * lax.rsqrt(jnp.mean(x * x, axis=-1, keepdims=True) + EPS_RMS) * g


def _dot(a, b):
    return jnp.dot(a.astype(BF16), b.astype(BF16), preferred_element_type=F32)


def _dot_nt(a, b):
    return lax.dot_general(a.astype(BF16), b.astype(BF16), (((1,), (1,)), ((), ())),
                           preferred_element_type=F32)


def _dot_tn(a, b):
    return lax.dot_general(a.astype(BF16), b.astype(BF16), (((0,), (0,)), ((), ())),
                           preferred_element_type=F32)


def _dot_split_rhs(a_bf16, x):
    hi = x.astype(BF16)
    lo = (x - hi.astype(F32)).astype(BF16)
    d = lambda p: jnp.dot(a_bf16, p, preferred_element_type=F32)
    return d(hi) + d(lo)


def _head_sum(z, ones_bd):
    rows = z.shape[0]
    hi = z.astype(BF16)
    lo = (z - hi.astype(F32)).astype(BF16)
    pieces = []
    for g in range(N_GROUPS):
        pieces += [hi[:, g * GROUP_W:(g + 1) * GROUP_W], lo[:, g * GROUP_W:(g + 1) * GROUP_W]]
    res = jnp.dot(jnp.concatenate(pieces, axis=0), ones_bd, preferred_element_type=F32)
    return jnp.concatenate(
        [res[(2 * g) * rows:(2 * g + 1) * rows] + res[(2 * g + 1) * rows:(2 * g + 2) * rows]
         for g in range(N_GROUPS)], axis=1)


def _ada_kernel(c_ref, w_ref, b_ref, o_ref):
    o_ref[...] = _dot(_silu(c_ref[...]), w_ref[...]) + b_ref[...]


def _ada_call(cc, ada_w, ada_b):
    rows = cc.shape[0]
    n_out = ada_w.shape[1]
    blk = 1536
    return pl.pallas_call(
        _ada_kernel,
        grid=(n_out // blk,),
        in_specs=[pl.BlockSpec((rows, D_MODEL), lambda n: (0, 0)),
                  pl.BlockSpec((D_MODEL, blk), lambda n: (0, n)),
                  pl.BlockSpec((1, blk), lambda n: (0, n))],
        out_specs=pl.BlockSpec((rows, blk), lambda n: (0, n)),
        out_shape=jax.ShapeDtypeStruct((rows, n_out), F32),
        compiler_params=pltpu.CompilerParams(vmem_limit_bytes=VMEM_LIMIT),
        name="ada_mod",
    )(cc, ada_w, ada_b)


def _proj_kernel(x_ref, mod_ref, g_ref, w_ref, mp_ref, mn_ref, o_ref, h_scr, z_scr, *, seq):
    n = pl.program_id(1)
    row_chunk = min(seq, 512)

    @pl.when(n == 0)
    def _():
        shift = mod_ref[0, 0:1, :]
        scale = mod_ref[0, 1:2, :]
        g = g_ref[...]
        for s in range(0, seq, row_chunk):
            xs = x_ref[0, s:s + row_chunk, :]
            h_scr[s:s + row_chunk, :] = (_rms(xs, g) * (1.0 + scale) + shift).astype(BF16)

    @pl.when(n < N_SHIFT_BLOCKS)
    def _():
        z_scr[0:8, :] = jnp.zeros((8, PROJ_BLOCK), F32)
        z_scr[8 + seq:16 + seq, :] = jnp.zeros((8, PROJ_BLOCK), F32)
        mp = mp_ref[...]
        mn = mn_ref[...]

        def shift_rows(s):
            cur = z_scr[8 + s:8 + s + row_chunk, :]
            prv = z_scr[7 + s:7 + s + row_chunk, :]
            nxt = z_scr[9 + s:9 + s + row_chunk, :]
            o_ref[0, s:s + row_chunk, :] = (cur + mp * (prv - cur) + mn * (nxt - cur)).astype(BF16)

        for s in range(0, seq, row_chunk):
            z_scr[8 + s:8 + s + row_chunk, :] = jnp.dot(
                h_scr[s:s + row_chunk, :], w_ref[...], preferred_element_type=F32)
            if s > 0:
                shift_rows(s - row_chunk)
        shift_rows(seq - row_chunk)

    @pl.when(n >= N_SHIFT_BLOCKS)
    def _():
        for s in range(0, seq, row_chunk):
            o_ref[0, s:s + row_chunk, :] = jnp.dot(
                h_scr[s:s + row_chunk, :], w_ref[...], preferred_element_type=F32).astype(BF16)


def _proj_call(x, mods, g, w_pad, mp_pad, mn_pad, n_blocks):
    bsz, seq, _ = x.shape
    n_cols = n_blocks * PROJ_BLOCK
    return pl.pallas_call(
        functools.partial(_proj_kernel, seq=seq),
        grid=(bsz, n_blocks),
        in_specs=[pl.BlockSpec((1, seq, D_MODEL), lambda b, n: (b, 0, 0)),
                  pl.BlockSpec((1, N_MOD, D_MODEL), lambda b, n: (b, 0, 0)),
                  pl.BlockSpec((1, D_MODEL), lambda b, n: (0, 0)),
                  pl.BlockSpec((D_MODEL, PROJ_BLOCK), lambda b, n: (0, n)),
                  pl.BlockSpec((1, PROJ_BLOCK), lambda b, n: (0, n)),
                  pl.BlockSpec((1, PROJ_BLOCK), lambda b, n: (0, n))],
        out_specs=pl.BlockSpec((1, seq, PROJ_BLOCK), lambda b, n: (b, 0, n)),
        out_shape=jax.ShapeDtypeStruct((bsz, seq, n_cols), BF16),
        scratch_shapes=[pltpu.VMEM((seq, D_MODEL), BF16),
                        pltpu.VMEM((seq + 16, PROJ_BLOCK), F32)],
        compiler_params=pltpu.CompilerParams(
            dimension_semantics=("parallel", "arbitrary"), vmem_limit_bytes=VMEM_LIMIT),
        name="in_proj",
    )(x, mods, g, w_pad, mp_pad, mn_pad)


def _expand(x, bd_mask):
    return jnp.where(bd_mask, jnp.concatenate([x] * GROUP_HEADS, axis=0), 0.0)


def _scan_prep(blocks, lw_ref, vec_ref, ones_bd):
    L = CHUNK
    n = len(blocks)
    lane = lax.broadcasted_iota(jnp.int32, (L, LORA_COLS), 1)
    loras = []
    for pb, _ in blocks:
        lora = pb[:, LORA_OFF:LORA_OFF + LORA_COLS]
        loras.append(jnp.where(lane < 2 * DECAY_RANK, jnp.tanh(lora), lora))
    lin = [None] * n
    for d in range(2):
        idx = [i for i in range(n) if blocks[i][1] == bool(d)]
        res = _dot(jnp.concatenate([loras[i] for i in idx], axis=0), lw_ref[d])
        for j, i in enumerate(idx):
            lin[i] = res[j * L:(j + 1) * L]

    kk0s, pieces = [], []
    for i, (pb, reverse) in enumerate(blocks):
        kk0 = pb[:, RWKV_WIDTH:2 * RWKV_WIDTH] * vec_ref[int(reverse), 2:3, :]
        kk0s.append(kk0)
        sq = (kk0 * kk0).astype(BF16)
        pieces += [sq[:, g * GROUP_W:(g + 1) * GROUP_W] for g in range(N_GROUPS)]
    sums = jnp.dot(jnp.concatenate(pieces, axis=0), ones_bd, preferred_element_type=F32)

    row = lax.broadcasted_iota(jnp.int32, (L, L), 0)
    col = lax.broadcasted_iota(jnp.int32, (L, L), 1)
    preps = []
    for i, (pb, reverse) in enumerate(blocks):
        vecs = vec_ref[int(reverse)]
        w0 = vecs[0:1, :]
        a0 = vecs[1:2, :]
        k_a = vecs[3:4, :]
        r = pb[:, 0:RWKV_WIDTH]
        k = pb[:, RWKV_WIDTH:2 * RWKV_WIDTH]
        v = pb[:, 2 * RWKV_WIDTH:3 * RWKV_WIDTH]
        lw = -DECAY_SCALE * jax.nn.sigmoid(lin[i][:, 0:RWKV_WIDTH] + w0)
        iclr = jax.nn.sigmoid(lin[i][:, RWKV_WIDTH:] + a0)
        base = i * N_GROUPS * L
        ss = jnp.concatenate(
            [sums[base + g * L:base + (g + 1) * L] for g in range(N_GROUPS)], axis=1)
        kk = kk0s[i] * lax.rsqrt(jnp.maximum(ss, 1e-24))
        kd = k * (1.0 + (iclr - 1.0) * k_a)
        a = -kk
        b = kk * iclr
        tri = jnp.where((col >= row) if reverse else (col <= row), 1.0, 0.0).astype(BF16)
        cum = _dot_split_rhs(tri, lw)
        tot = cum[0:1, :] if reverse else cum[L - 1:L, :]
        e_incl = jnp.exp(cum)
        e_neg = 1.0 / e_incl
        w_tot = jnp.exp(tot)
        e_rem = w_tot * e_neg
        preps.append(dict(
            r_t=r * e_incl, a_t=a * jnp.exp(cum - lw), b_t=b * e_neg, k_t=kd * e_neg,
            b_h=b * e_rem, k_h=kd * e_rem, v=v, w_tot=w_tot))
    return preps


def _scan_units(preps, reverse, s_idx, s_ref):
    L = CHUNK
    rr = lax.broadcasted_iota(jnp.int32, (GROUP_W, GROUP_W), 0)
    cc = lax.broadcasted_iota(jnp.int32, (GROUP_W, GROUP_W), 1)
    bd_mask = jnp.right_shift(rr, 6) == jnp.right_shift(cc, 6)
    tr = lax.broadcasted_iota(jnp.int32, (L, GROUP_W), 0)
    tc = jnp.bitwise_and(lax.broadcasted_iota(jnp.int32, (L, GROUP_W), 1), L - 1)
    eye = jnp.where(tc == tr, 1.0, 0.0)
    strict = {True: tc > tr, False: tc < tr}
    incl = {True: tc >= tr, False: tc <= tr}
    units = [(d, g) for d in range(len(preps)) for g in range(N_GROUPS)]
    sl = lambda g: slice(g * GROUP_W, (g + 1) * GROUP_W)
    ex = lambda x: _expand(x, bd_mask)

    lhs, a_ab, a_ak, a_rb, a_rk = {}, {}, {}, {}, {}
    for u in units:
        d, g = u
        p = preps[d]
        lhs[u] = jnp.concatenate([p["a_t"][:, sl(g)], p["r_t"][:, sl(g)]], axis=0)
        e_bk = jnp.concatenate([ex(p["b_t"][:, sl(g)]), ex(p["k_t"][:, sl(g)])], axis=0)
        a_all = _dot_nt(lhs[u], e_bk)
        a_ab[u] = jnp.where(strict[reverse[d]], a_all[0:L, 0:GROUP_W], 0.0)
        a_ak[u] = jnp.where(strict[reverse[d]], a_all[0:L, GROUP_W:], 0.0)
        a_rb[u] = jnp.where(incl[reverse[d]], a_all[L:, 0:GROUP_W], 0.0)
        a_rk[u] = jnp.where(incl[reverse[d]], a_all[L:, GROUP_W:], 0.0)

    n_levels = 6
    npow = {u: _dot(a_ab[u], ex(a_ab[u])) for u in units}
    pinv = {u: eye + a_ab[u] for u in units}
    for lvl in range(1, n_levels):
        for u in units:
            rhs = ex(npow[u])
            if lvl < n_levels - 1:
                both = _dot(jnp.concatenate([npow[u], pinv[u]], axis=0), rhs)
                npow[u] = both[0:L]
                pinv[u] = pinv[u] + both[L:]
            else:
                pinv[u] = pinv[u] + _dot(pinv[u], rhs)

    s_old, sa, e_v, x_u, u_u, y_u = {}, {}, {}, {}, {}, {}
    for u in units:
        d, g = u
        s_old[u] = s_ref[s_idx[d] + (g,)]
        sa[u] = _dot_nt(lhs[u], s_old[u])
        e_v[u] = ex(preps[d]["v"][:, sl(g)])
    for u in units:
        x_u[u] = sa[u][0:L] + _dot(a_ak[u], e_v[u])
    for u in units:
        u_u[u] = _dot(pinv[u], ex(x_u[u]))
    for u in units:
        y_u[u] = sa[u][L:] + _dot(jnp.concatenate([a_rb[u], a_rk[u]], axis=1),
                                  jnp.concatenate([ex(u_u[u]), e_v[u]], axis=0))
    for u in units:
        d, g = u
        p = preps[d]
        upd = _dot_tn(jnp.concatenate([u_u[u], p["v"][:, sl(g)]], axis=0),
                      jnp.concatenate([p["b_h"][:, sl(g)], p["k_h"][:, sl(g)]], axis=0))
        s_ref[s_idx[d] + (g,)] = s_old[u] * p["w_tot"][:, sl(g)] + jnp.where(bd_mask, upd, 0.0)
    return [jnp.concatenate([y_u[(d, g)] for g in range(N_GROUPS)], axis=1)
            for d in range(len(preps))]


def _scan_kernel(pf_ref, pb_ref, s0_ref, lw_ref, vec_ref, ho_ref, yf_ref, yb_ref, sfin_ref, s_scr):
    c = pl.program_id(1)

    @pl.when(c == 0)
    def _():
        s_scr[...] = s0_ref[...]

    n_streams = 2 * SCAN_BATCH
    reverse = [bool(i % 2) for i in range(n_streams)]
    s_idx = [(i // 2, i % 2) for i in range(n_streams)]
    blocks = []
    for bb in range(SCAN_BATCH):
        blocks += [(pf_ref[bb].astype(F32), False), (pb_ref[bb].astype(F32), True)]
    preps = _scan_prep(blocks, lw_ref, vec_ref, ho_ref[...])
    ys = _scan_units(preps, reverse, s_idx, s_scr)
    for bb in range(SCAN_BATCH):
        yf_ref[bb] = ys[2 * bb]
        yb_ref[bb] = ys[2 * bb + 1]

    @pl.when(c == pl.num_programs(1) - 1)
    def _():
        sfin_ref[...] = s_scr[...]


def _scan_call(p, s0, lora_w, vecs, head_ones):
    bsz, seq, _ = p.shape
    nc = seq // CHUNK
    state_block = (SCAN_BATCH, 2, N_GROUPS, GROUP_W, GROUP_W)
    state_map = lambda b, c: (b, 0, 0, 0, 0)
    return pl.pallas_call(
        _scan_kernel,
        grid=(bsz // SCAN_BATCH, nc),
        in_specs=[pl.BlockSpec((SCAN_BATCH, CHUNK, SCAN_COLS), lambda b, c: (b, c, 0)),
                  pl.BlockSpec((SCAN_BATCH, CHUNK, SCAN_COLS), lambda b, c: (b, nc - 1 - c, 0)),
                  pl.BlockSpec(state_block, state_map),
                  pl.BlockSpec((2, LORA_COLS, 2 * RWKV_WIDTH), lambda b, c: (0, 0, 0)),
                  pl.BlockSpec((2, 8, RWKV_WIDTH), lambda b, c: (0, 0, 0)),
                  pl.BlockSpec((GROUP_W, GROUP_W), lambda b, c: (0, 0))],
        out_specs=[pl.BlockSpec((SCAN_BATCH, CHUNK, RWKV_WIDTH), lambda b, c: (b, c, 0)),
                   pl.BlockSpec((SCAN_BATCH, CHUNK, RWKV_WIDTH), lambda b, c: (b, nc - 1 - c, 0)),
                   pl.BlockSpec(state_block, state_map)],
        out_shape=[jax.ShapeDtypeStruct((bsz, seq, RWKV_WIDTH), F32),
                   jax.ShapeDtypeStruct((bsz, seq, RWKV_WIDTH), F32),
                   jax.ShapeDtypeStruct((bsz, 2, N_GROUPS, GROUP_W, GROUP_W), F32)],
        scratch_shapes=[pltpu.VMEM(state_block, F32)],
        compiler_params=pltpu.CompilerParams(
            dimension_semantics=("parallel", "arbitrary"), vmem_limit_bytes=VMEM_LIMIT),
        name="wkv_scan",
    )(p, p, s0, lora_w, vecs, head_ones)


def _mix_kernel(x_ref, yf_ref, yb_ref, rkv_ref, sm_ref, cv_ref, mod_ref, rw_ref, vec_ref, ho_ref,
                cw_ref, cvec_ref, wout_ref, g_ref, o_ref, upad, *, tm):
    n_lines = tm // GRID_W
    ones_bd = ho_ref[...]
    y = yf_ref[0] + yb_ref[0]
    inv_n = 1.0 / HEAD_SIZE
    mu = _head_sum(y, ones_bd) * inv_n
    dy = y - mu
    var = _head_sum(dy * dy, ones_bd) * inv_n
    lnx_w = vec_ref[0:1, :]
    lnx_b = vec_ref[1:2, :]
    a0_f = vec_ref[2:3, :]
    a0_b = vec_ref[3:4, :]
    k_a = vec_ref[4:5, :]
    r_k = vec_ref[5:6, :]
    yn = dy * lax.rsqrt(var + EPS_GN) * lnx_w + lnx_b
    sm = sm_ref[0].astype(F32)
    lane = lax.broadcasted_iota(jnp.int32, sm.shape, 1)
    sm = jnp.where(lane >= LORA_COLS, jax.nn.sigmoid(sm), sm)
    lin = _dot(sm, rw_ref[...])
    iclr_f = jax.nn.sigmoid(lin[:, 0:RWKV_WIDTH] + a0_f)
    iclr_b = jax.nn.sigmoid(lin[:, RWKV_WIDTH:2 * RWKV_WIDTH] + a0_b)
    gate = lin[:, 2 * RWKV_WIDTH:]
    r = rkv_ref[0, :, 0:RWKV_WIDTH].astype(F32)
    k = rkv_ref[0, :, RWKV_WIDTH:2 * RWKV_WIDTH].astype(F32)
    v = rkv_ref[0, :, 2 * RWKV_WIDTH:3 * RWKV_WIDTH].astype(F32)
    k_bar = k * (1.0 + (0.5 * (iclr_f + iclr_b) - 1.0) * k_a)
    bonus = _head_sum(r * k_bar * r_k, ones_bd) * v
    rw_out = (yn + bonus) * gate

    cv = cv_ref[0].astype(F32)
    u = cv[:, 0:CONV_WIDTH] * jax.nn.sigmoid(cv[:, CONV_WIDTH:])
    zeros = jnp.zeros((16, CONV_WIDTH), F32)
    for ln in range(n_lines):
        u_ln = u[ln * GRID_W:(ln + 1) * GRID_W, :]
        for s in range(8):
            upad[s, ln, 0:16, :] = zeros
            upad[s, ln, CONV_ROWS - 16:CONV_ROWS, :] = zeros
            upad[s, ln, 16 - s:16 - s + GRID_W, :] = u_ln
    conv_b = cvec_ref[0:1, :]
    ln_w = cvec_ref[1:2, :]
    ln_b = cvec_ref[2:3, :]
    conv_lines = []
    for ln in range(n_lines):
        acc = jnp.zeros((GRID_W, CONV_WIDTH), F32)
        for j in range(CONV_KERNEL):
            off = 16 - CONV_PAD + j
            q8 = (off // 8) * 8
            acc = acc + cw_ref[j:j + 1, :] * upad[off % 8, ln, q8:q8 + GRID_W, :]
        conv_lines.append(acc)
    yc = jnp.concatenate(conv_lines, axis=0) + conv_b
    mu_c = jnp.mean(yc, axis=-1, keepdims=True)
    dc = yc - mu_c
    var_c = jnp.mean(dc * dc, axis=-1, keepdims=True)
    conv_out = _silu(dc * lax.rsqrt(var_c + EPS_LN) * ln_w + ln_b)

    mix_in = jnp.concatenate([rw_out.astype(BF16), conv_out.astype(BF16)], axis=1)
    mix = jnp.dot(mix_in, wout_ref[...], preferred_element_type=F32)
    gate_mix = mod_ref[0, 2:3, :]
    o_ref[0] = x_ref[0] + gate_mix * _rms(mix, g_ref[...])


def _mix_call(x, y_f, y_b, p, mods, rw_w, rvecs, head_ones, conv_w, cvecs, w_out, post_g):
    bsz, seq, _ = x.shape
    tm = 512
    tok = lambda width, blk: pl.BlockSpec((1, tm, width), lambda b, t: (b, t, blk))
    const2 = lambda shape: pl.BlockSpec(shape, lambda b, t: (0, 0))
    return pl.pallas_call(
        functools.partial(_mix_kernel, tm=tm),
        grid=(bsz, seq // tm),
        in_specs=[tok(D_MODEL, 0), tok(RWKV_WIDTH, 0), tok(RWKV_WIDTH, 0),
                  tok(3 * RWKV_WIDTH, 0),
                  tok(512, LORA_OFF // 512),
                  tok(2 * CONV_WIDTH, CONV_OFF // (2 * CONV_WIDTH)),
                  pl.BlockSpec((1, N_MOD, D_MODEL), lambda b, t: (b, 0, 0)),
                  const2((512, 3 * RWKV_WIDTH)), const2((8, RWKV_WIDTH)),
                  const2((GROUP_W, GROUP_W)), const2((32, CONV_WIDTH)),
                  const2((8, CONV_WIDTH)), const2((D_MODEL, D_MODEL)), const2((1, D_MODEL))],
        out_specs=tok(D_MODEL, 0),
        out_shape=jax.ShapeDtypeStruct((bsz, seq, D_MODEL), F32),
        scratch_shapes=[pltpu.VMEM((8, tm // GRID_W, CONV_ROWS, CONV_WIDTH), F32)],
        compiler_params=pltpu.CompilerParams(
            dimension_semantics=("parallel", "parallel"), vmem_limit_bytes=VMEM_LIMIT),
        name="mix_out",
    )(x, y_f, y_b, p, p, p, mods, rw_w, rvecs, head_ones, conv_w, cvecs, w_out, post_g)


def _mlp_kernel(x_ref, mod_ref, gpre_ref, gpost_ref, w1_ref, w2_ref, o_ref, acc_ref):
    x = x_ref[0]
    shift = mod_ref[0, 3:4, :]
    scale = mod_ref[0, 4:5, :]
    gate = mod_ref[0, 5:6, :]
    h = (_rms(x, gpre_ref[...]) * (1.0 + scale) + shift).astype(BF16)
    ff_blk = 1024
    for c in range(D_FF // ff_blk):
        hid = jnp.dot(h, w1_ref[:, c * ff_blk:(c + 1) * ff_blk], preferred_element_type=F32)
        hid = jnp.square(jnp.maximum(hid, 0.0)).astype(BF16)
        part = jnp.dot(hid, w2_ref[c * ff_blk:(c + 1) * ff_blk, :], preferred_element_type=F32)
        if c == 0:
            acc_ref[...] = part
        else:
            acc_ref[...] += part
    o_ref[0] = x + gate * _rms(acc_ref[...], gpost_ref[...])


def _mlp_call(x, mods, pre_g, post_g, w1, w2):
    bsz, seq, _ = x.shape
    tm = 512
    return pl.pallas_call(
        _mlp_kernel,
        grid=(bsz, seq // tm),
        in_specs=[pl.BlockSpec((1, tm, D_MODEL), lambda b, t: (b, t, 0)),
                  pl.BlockSpec((1, N_MOD, D_MODEL), lambda b, t: (b, 0, 0)),
                  pl.BlockSpec((1, D_MODEL), lambda b, t: (0, 0)),
                  pl.BlockSpec((1, D_MODEL), lambda b, t: (0, 0)),
                  pl.BlockSpec((D_MODEL, D_FF), lambda b, t: (0, 0)),
                  pl.BlockSpec((D_FF, D_MODEL), lambda b, t: (0, 0))],
        out_specs=pl.BlockSpec((1, tm, D_MODEL), lambda b, t: (b, t, 0)),
        out_shape=jax.ShapeDtypeStruct((bsz, seq, D_MODEL), F32),
        scratch_shapes=[pltpu.VMEM((tm, D_MODEL), F32)],
        compiler_params=pltpu.CompilerParams(
            dimension_semantics=("parallel", "parallel"), vmem_limit_bytes=VMEM_LIMIT),
        name="sqrelu_mlp",
    )(x, mods, pre_g, post_g, w1, w2)


def _pad_cols(a, lo, hi):
    pad = jnp.zeros(a.shape[:-1] + (GATE_PAD - GATE_RANK,), a.dtype)
    return jnp.concatenate([a[..., :lo], pad, a[..., lo:hi]], axis=-1)


def _layer(x, ctx, mods_x, mods_c, prm):
    (mix_pre_g, mix_post_g, mlp_pre_g, mlp_post_g, w_in, mu_prev, mu_next, decay_w0, decay_w2,
     iclr_a0, iclr_a2, k_k, k_a, r_k, gate_w2, lnx_w, lnx_b, conv_w, conv_b, conv_ln_w,
     conv_ln_b, w_out, mlp_w1, mlp_w2) = prm
    bsz = x.shape[0]
    shift_cols = GATE_OFF + GATE_RANK
    in_cols = w_in.shape[1]

    w_pad = _pad_cols(w_in, shift_cols, in_cols).astype(BF16)
    zc = jnp.zeros((2 * CONV_WIDTH,), F32)
    mp_pad = _pad_cols(jnp.concatenate([mu_prev, zc]), shift_cols, in_cols)[None, :]
    mn_pad = _pad_cols(jnp.concatenate([mu_next, zc]), shift_cols, in_cols)[None, :]
    zr = jnp.zeros((DECAY_RANK, RWKV_WIDTH), F32)
    lora_dir = []
    for d in range(2):
        dec_rows = [zr, zr]
        dec_rows[d] = decay_w2[d]
        icl_rows = [zr, zr]
        icl_rows[d] = iclr_a2[d]
        left = jnp.concatenate(dec_rows + [zr, zr], axis=0)
        right = jnp.concatenate([zr, zr] + icl_rows, axis=0)
        lora_dir.append(jnp.concatenate([left, right], axis=1))
    lora_w = jnp.stack(lora_dir).astype(BF16)
    z4 = jnp.zeros((4, RWKV_WIDTH), F32)
    scan_vecs = jnp.stack([jnp.concatenate([decay_w0[d][None], iclr_a0[d][None], k_k[None],
                                            k_a[None], z4], axis=0) for d in range(2)])
    hid = jnp.arange(RWKV_WIDTH) // HEAD_SIZE
    head_ones = (hid[:, None] == hid[None, :]).astype(BF16)
    zl = jnp.zeros((2 * DECAY_RANK, 3 * RWKV_WIDTH), F32)
    za = jnp.zeros((ICLR_RANK, RWKV_WIDTH), F32)
    zg = jnp.zeros((GATE_PAD, RWKV_WIDTH), F32)
    gate_pad = jnp.concatenate([gate_w2, jnp.zeros((GATE_PAD - GATE_RANK, RWKV_WIDTH), F32)], axis=0)
    rw_w = jnp.concatenate([
        zl,
        jnp.concatenate([iclr_a2[0], za, za], axis=1),
        jnp.concatenate([za, iclr_a2[1], za], axis=1),
        jnp.concatenate([zg, zg, gate_pad], axis=1)], axis=0).astype(BF16)
    rvecs = jnp.concatenate([lnx_w[None], lnx_b[None], iclr_a0[0][None], iclr_a0[1][None],
                             k_a[None], r_k.reshape(1, RWKV_WIDTH), jnp.zeros((2, RWKV_WIDTH), F32)],
                            axis=0)
    conv_w_pad = jnp.concatenate([conv_w, jnp.zeros((1, CONV_WIDTH), F32)], axis=0)
    cvecs = jnp.concatenate([conv_b[None], conv_ln_w[None], conv_ln_b[None],
                             jnp.zeros((5, CONV_WIDTH), F32)], axis=0)

    p_x = _proj_call(x, mods_x, mix_pre_g[None], w_pad, mp_pad, mn_pad, P_COLS // PROJ_BLOCK)
    p_c = _proj_call(ctx, mods_c, mix_pre_g[None], w_pad, mp_pad, mn_pad, N_SHIFT_BLOCKS)
    s_zero = jnp.zeros((bsz, 2, N_GROUPS, GROUP_W, GROUP_W), F32)
    _, _, s_ctx = _scan_call(p_c, s_zero, lora_w, scan_vecs, head_ones)
    y_f, y_b, _ = _scan_call(p_x, s_ctx, lora_w, scan_vecs, head_ones)
    x = _mix_call(x, y_f, y_b, p_x, mods_x, rw_w, rvecs, head_ones, conv_w_pad, cvecs,
                  w_out.astype(BF16), mix_post_g[None])
    x = _mlp_call(x, mods_x, mlp_pre_g[None], mlp_post_g[None], mlp_w1.astype(BF16),
                  mlp_w2.astype(BF16))
    return x


def kernel(x, c, ctx, c_ctx, ada_w, ada_b, mix_pre_g, mix_post_g, mlp_pre_g, mlp_post_g, w_in, mu_prev, mu_next, decay_w0, decay_w2, iclr_a0, iclr_a2, k_k, k_a, r_k, gate_w2, lnx_w, lnx_b, conv_w, conv_b, conv_ln_w, conv_ln_b, w_out, mlp_w1, mlp_w2):
    depth = ada_w.shape[0]
    assert depth == 1, "context-stream update between layers is not implemented"
    bsz = x.shape[0]
    rows = 8 * ((bsz + 1 + 7) // 8)
    cc = jnp.concatenate([c, c_ctx[None, :], jnp.zeros((rows - bsz - 1, D_MODEL), F32)], axis=0)
    per_layer = (mix_pre_g, mix_post_g, mlp_pre_g, mlp_post_g, w_in, mu_prev, mu_next, decay_w0,
                 decay_w2, iclr_a0, iclr_a2, k_k, k_a, r_k, gate_w2, lnx_w, lnx_b, conv_w, conv_b,
                 conv_ln_w, conv_ln_b, w_out, mlp_w1, mlp_w2)
    for l in range(depth):
        mods = _ada_call(cc, ada_w[l], ada_b[l][None, :]).reshape(rows, N_MOD, D_MODEL)
        mods_x = mods[:bsz]
        mods_c = jnp.broadcast_to(mods[bsz:bsz + 1], (bsz, N_MOD, D_MODEL))
        x = _layer(x, ctx, mods_x, mods_c, tuple(a[l] for a in per_layer))
    return x
```

```python
import functools

import jax
import jax.numpy as jnp
from jax import lax
from jax.experimental import pallas as pl
from jax.experimental.pallas import tpu as pltpu

F32 = jnp.float32
BF16 = jnp.bfloat16

D_MODEL = 1024
RWKV_WIDTH = 512
CONV_WIDTH = 512
HEAD_SIZE = 64
RWKV_HEADS = 8
DECAY_RANK = 64
ICLR_RANK = 64
GATE_RANK = 160
CONV_KERNEL = 31
CONV_PAD = CONV_KERNEL // 2
GRID_W = 64
CONV_ROWS = GRID_W + 24
D_FF = 4 * D_MODEL
N_MOD = 6
EPS_RMS = 1e-6
EPS_LN = 1e-5
EPS_GN = 64e-5
DECAY_SCALE = 0.6065306597126334

LORA_OFF = 3 * RWKV_WIDTH
LORA_COLS = 256
GATE_OFF = LORA_OFF + LORA_COLS
GATE_PAD = 256
CONV_OFF = GATE_OFF + GATE_PAD
P_COLS = CONV_OFF + 2 * CONV_WIDTH
SCAN_COLS = GATE_OFF
PROJ_BLOCK = 1024
N_SHIFT_BLOCKS = CONV_OFF // PROJ_BLOCK

CHUNK = 64
GROUP_HEADS = 4
GROUP_W = GROUP_HEADS * HEAD_SIZE
N_GROUPS = RWKV_HEADS // GROUP_HEADS
SCAN_BATCH = 4

VMEM_LIMIT = 56 * 1024 * 1024


def _silu(x):
    return x * jax.nn.sigmoid(x)


def _rms(x, g):
    return x * lax.rsqrt(jnp.mean(x * x, axis=-1, keepdims=True) + EPS_RMS) * g


def _dot(a, b):
    return jnp.dot(a.astype(BF16), b.astype(BF16), preferred_element_type=F32)


def _dot_nt(a, b):
    return lax.dot_general(a.astype(BF16), b.astype(BF16), (((1,), (1,)), ((), ())),
                           preferred_element_type=F32)


def _dot_tn(a, b):
    return lax.dot_general(a.astype(BF16), b.astype(BF16), (((0,), (0,)), ((), ())),
                           preferred_element_type=F32)


def _dot_split_rhs(a_bf16, x):
    hi = x.astype(BF16)
    lo = (x - hi.astype(F32)).astype(BF16)
    d = lambda p: jnp.dot(a_bf16, p, preferred_element_type=F32)
    return d(hi) + d(lo)


def _head_sum(z, ones_bd):
    rows = z.shape[0]
    hi = z.astype(BF16)
    lo = (z - hi.astype(F32)).astype(BF16)
    pieces = []
    for g in range(N_GROUPS):
        pieces += [hi[:, g * GROUP_W:(g + 1) * GROUP_W], lo[:, g * GROUP_W:(g + 1) * GROUP_W]]
    res = jnp.dot(jnp.concatenate(pieces, axis=0), ones_bd, preferred_element_type=F32)
    return jnp.concatenate(
        [res[(2 * g) * rows:(2 * g + 1) * rows] + res[(2 * g + 1) * rows:(2 * g + 2) * rows]
         for g in range(N_GROUPS)], axis=1)


def _ada_kernel(c_ref, w_ref, b_ref, o_ref):
    o_ref[...] = _dot(_silu(c_ref[...]), w_ref[...]) + b_ref[...]


def _ada_call(cc, ada_w, ada_b):
    rows = cc.shape[0]
    n_out = ada_w.shape[1]
    blk = 1536
    return pl.pallas_call(
        _ada_kernel,
        grid=(n_out // blk,),
        in_specs=[pl.BlockSpec((rows, D_MODEL), lambda n: (0, 0)),
                  pl.BlockSpec((D_MODEL, blk), lambda n: (0, n)),
                  pl.BlockSpec((1, blk), lambda n: (0, n))],
        out_specs=pl.BlockSpec((rows, blk), lambda n: (0, n)),
        out_shape=jax.ShapeDtypeStruct((rows, n_out), F32),
        compiler_params=pltpu.CompilerParams(vmem_limit_bytes=VMEM_LIMIT),
        name="ada_mod",
    )(cc, ada_w, ada_b)


def _proj_kernel(x_ref, mod_ref, g_ref, w_ref, mp_ref, mn_ref, o_ref, h_scr, z_scr, *, seq):
    n = pl.program_id(1)
    row_chunk = min(seq, 512)

    @pl.when(n == 0)
    def _():
        shift = mod_ref[0, 0:1, :]
        scale = mod_ref[0, 1:2, :]
        g = g_ref[...]
        for s in range(0, seq, row_chunk):
            xs = x_ref[0, s:s + row_chunk, :]
            h_scr[s:s + row_chunk, :] = (_rms(xs, g) * (1.0 + scale) + shift).astype(BF16)

    @pl.when(n < N_SHIFT_BLOCKS)
    def _():
        z_scr[0:8, :] = jnp.zeros((8, PROJ_BLOCK), F32)
        z_scr[8 + seq:16 + seq, :] = jnp.zeros((8, PROJ_BLOCK), F32)
        mp = mp_ref[...]
        mn = mn_ref[...]

        def shift_rows(s):
            cur = z_scr[8 + s:8 + s + row_chunk, :]
            prv = z_scr[7 + s:7 + s + row_chunk, :]
            nxt = z_scr[9 + s:9 + s + row_chunk, :]
            o_ref[0, s:s + row_chunk, :] = (cur + mp * (prv - cur) + mn * (nxt - cur)).astype(BF16)

        for s in range(0, seq, row_chunk):
            z_scr[8 + s:8 + s + row_chunk, :] = jnp.dot(
                h_scr[s:s + row_chunk, :], w_ref[...], preferred_element_type=F32)
            if s > 0:
                shift_rows(s - row_chunk)
        shift_rows(seq - row_chunk)

    @pl.when(n >= N_SHIFT_BLOCKS)
    def _():
        for s in range(0, seq, row_chunk):
            o_ref[0, s:s + row_chunk, :] = jnp.dot(
                h_scr[s:s + row_chunk, :], w_ref[...], preferred_element_type=F32).astype(BF16)


def _proj_call(x, mods, g, w_pad, mp_pad, mn_pad, n_blocks):
    bsz, seq, _ = x.shape
    n_cols = n_blocks * PROJ_BLOCK
    return pl.pallas_call(
        functools.partial(_proj_kernel, seq=seq),
        grid=(bsz, n_blocks),
        in_specs=[pl.BlockSpec((1, seq, D_MODEL), lambda b, n: (b, 0, 0)),
                  pl.BlockSpec((1, N_MOD, D_MODEL), lambda b, n: (b, 0, 0)),
                  pl.BlockSpec((1, D_MODEL), lambda b, n: (0, 0)),
                  pl.BlockSpec((D_MODEL, PROJ_BLOCK), lambda b, n: (0, n)),
                  pl.BlockSpec((1, PROJ_BLOCK), lambda b, n: (0, n)),
                  pl.BlockSpec((1, PROJ_BLOCK), lambda b, n: (0, n))],
        out_specs=pl.BlockSpec((1, seq, PROJ_BLOCK), lambda b, n: (b, 0, n)),
        out_shape=jax.ShapeDtypeStruct((bsz, seq, n_cols), BF16),
        scratch_shapes=[pltpu.VMEM((seq, D_MODEL), BF16),
                        pltpu.VMEM((seq + 16, PROJ_BLOCK), F32)],
        compiler_params=pltpu.CompilerParams(
            dimension_semantics=("parallel", "arbitrary"), vmem_limit_bytes=VMEM_LIMIT),
        name="in_proj",
    )(x, mods, g, w_pad, mp_pad, mn_pad)


def _expand(x, bd_mask):
    return jnp.where(bd_mask, jnp.concatenate([x] * GROUP_HEADS, axis=0), 0.0)


def _scan_prep(blocks, lw_ref, vec_ref, ones_bd):
    L = CHUNK
    n = len(blocks)
    lane = lax.broadcasted_iota(jnp.int32, (L, LORA_COLS), 1)
    loras = []
    for pb, _ in blocks:
        lora = pb[:, LORA_OFF:LORA_OFF + LORA_COLS]
        loras.append(jnp.where(lane < 2 * DECAY_RANK, jnp.tanh(lora), lora))
    lin = [None] * n
    for d in range(2):
        idx = [i for i in range(n) if blocks[i][1] == bool(d)]
        res = _dot(jnp.concatenate([loras[i] for i in idx], axis=0), lw_ref[d])
        for j, i in enumerate(idx):
            lin[i] = res[j * L:(j + 1) * L]

    kk0s, pieces = [], []
    for i, (pb, reverse) in enumerate(blocks):
        kk0 = pb[:, RWKV_WIDTH:2 * RWKV_WIDTH] * vec_ref[int(reverse), 2:3, :]
        kk0s.append(kk0)
        sq = (kk0 * kk0).astype(BF16)
        pieces += [sq[:, g * GROUP_W:(g + 1) * GROUP_W] for g in range(N_GROUPS)]
    sums = jnp.dot(jnp.concatenate(pieces, axis=0), ones_bd, preferred_element_type=F32)

    row = lax.broadcasted_iota(jnp.int32, (L, L), 0)
    col = lax.broadcasted_iota(jnp.int32, (L, L), 1)
    preps = []
    for i, (pb, reverse) in enumerate(blocks):
        vecs = vec_ref[int(reverse)]
        w0 = vecs[0:1, :]
        a0 = vecs[1:2, :]
        k_a = vecs[3:4, :]
        r = pb[:, 0:RWKV_WIDTH]
        k = pb[:, RWKV_WIDTH:2 * RWKV_WIDTH]
        v = pb[:, 2 * RWKV_WIDTH:3 * RWKV_WIDTH]
        lw = -DECAY_SCALE * jax.nn.sigmoid(lin[i][:, 0:RWKV_WIDTH] + w0)
        iclr = jax.nn.sigmoid(lin[i][:, RWKV_WIDTH:] + a0)
        base = i * N_GROUPS * L
        ss = jnp.concatenate(
            [sums[base + g * L:base + (g + 1) * L] for g in range(N_GROUPS)], axis=1)
        kk = kk0s[i] * lax.rsqrt(jnp.maximum(ss, 1e-24))
        kd = k * (1.0 + (iclr - 1.0) * k_a)
        a = -kk
        b = kk * iclr
        tri = jnp.where((col >= row) if reverse else (col <= row), 1.0, 0.0).astype(BF16)
        cum = _dot_split_rhs(tri, lw)
        tot = cum[0:1, :] if reverse else cum[L - 1:L, :]
        e_incl = jnp.exp(cum)
        e_neg = 1.0 / e_incl
        w_tot = jnp.exp(tot)
        e_rem = w_tot * e_neg
        preps.append(dict(
            r_t=r * e_incl, a_t=a * jnp.exp(cum - lw), b_t=b * e_neg, k_t=kd * e_neg,
            b_h=b * e_rem, k_h=kd * e_rem, v=v, w_tot=w_tot))
    return preps


def _scan_units(preps, reverse, s_idx, s_ref):
    L = CHUNK
    rr = lax.broadcasted_iota(jnp.int32, (GROUP_W, GROUP_W), 0)
    cc = lax.broadcasted_iota(jnp.int32, (GROUP_W, GROUP_W), 1)
    bd_mask = jnp.right_shift(rr, 6) == jnp.right_shift(cc, 6)
    tr = lax.broadcasted_iota(jnp.int32, (L, GROUP_W), 0)
    tc = jnp.bitwise_and(lax.broadcasted_iota(jnp.int32, (L, GROUP_W), 1), L - 1)
    eye = jnp.where(tc == tr, 1.0, 0.0)
    strict = {True: tc > tr, False: tc < tr}
    incl = {True: tc >= tr, False: tc <= tr}
    units = [(d, g) for d in range(len(preps)) for g in range(N_GROUPS)]
    sl = lambda g: slice(g * GROUP_W, (g + 1) * GROUP_W)
    ex = lambda x: _expand(x, bd_mask)

    lhs, a_ab, a_ak, a_rb, a_rk = {}, {}, {}, {}, {}
    for u in units:
        d, g = u
        p = preps[d]
        lhs[u] = jnp.concatenate([p["a_t"][:, sl(g)], p["r_t"][:, sl(g)]], axis=0)
        e_bk = jnp.concatenate([ex(p["b_t"][:, sl(g)]), ex(p["k_t"][:, sl(g)])], axis=0)
        a_all = _dot_nt(lhs[u], e_bk)
        a_ab[u] = jnp.where(strict[reverse[d]], a_all[0:L, 0:GROUP_W], 0.0)
        a_ak[u] = jnp.where(strict[reverse[d]], a_all[0:L, GROUP_W:], 0.0)
        a_rb[u] = jnp.where(incl[reverse[d]], a_all[L:, 0:GROUP_W], 0.0)
        a_rk[u] = jnp.where(incl[reverse[d]], a_all[L:, GROUP_W:], 0.0)
    av = {}
    for u in units:
        d, g = u
        av[u] = _dot(jnp.concatenate([a_ak[u], a_rk[u]], axis=0), ex(preps[d]["v"][:, sl(g)]))

    n_levels = 6
    npow = {u: _dot(a_ab[u], ex(a_ab[u])) for u in units}
    pinv = {u: eye + a_ab[u] for u in units}
    for lvl in range(1, n_levels):
        for u in units:
            rhs = ex(npow[u])
            if lvl < n_levels - 1:
                both = _dot(jnp.concatenate([npow[u], pinv[u]], axis=0), rhs)
                npow[u] = both[0:L]
                pinv[u] = pinv[u] + both[L:]
            else:
                pinv[u] = pinv[u] + _dot(pinv[u], rhs)

    s_old, sa, x_u, u_u, y_u = {}, {}, {}, {}, {}
    for u in units:
        d, g = u
        s_old[u] = s_ref[s_idx[d] + (g,)]
        sa[u] = _dot(lhs[u], s_old[u]) + av[u]
    for u in units:
        x_u[u] = sa[u][0:L]
        u_u[u] = _dot(pinv[u], ex(x_u[u]))
    for u in units:
        y_u[u] = sa[u][L:] + _dot(a_rb[u], ex(u_u[u]))
    w_col = [jnp.transpose(jnp.broadcast_to(p["w_tot"], (128, RWKV_WIDTH))) for p in preps]
    for u in units:
        d, g = u
        p = preps[d]
        upd = _dot_tn(jnp.concatenate([p["b_h"][:, sl(g)], p["k_h"][:, sl(g)]], axis=0),
                      jnp.concatenate([u_u[u], p["v"][:, sl(g)]], axis=0))
        decay = jnp.concatenate([w_col[d][sl(g), :]] * (GROUP_W // 128), axis=1)
        s_ref[s_idx[d] + (g,)] = s_old[u] * decay + jnp.where(bd_mask, upd, 0.0)
    return [jnp.concatenate([y_u[(d, g)] for g in range(N_GROUPS)], axis=1)
            for d in range(len(preps))]


def _scan_kernel(pf_ref, pb_ref, s0_ref, lw_ref, vec_ref, ho_ref, yf_ref, yb_ref, sfin_ref, s_scr):
    c = pl.program_id(1)

    @pl.when(c == 0)
    def _():
        s_scr[...] = s0_ref[...]

    n_streams = 2 * SCAN_BATCH
    reverse = [bool(i % 2) for i in range(n_streams)]
    s_idx = [(i // 2, i % 2) for i in range(n_streams)]
    blocks = []
    for bb in range(SCAN_BATCH):
        blocks += [(pf_ref[bb].astype(F32), False), (pb_ref[bb].astype(F32), True)]
    preps = _scan_prep(blocks, lw_ref, vec_ref, ho_ref[...])
    ys = _scan_units(preps, reverse, s_idx, s_scr)
    for bb in range(SCAN_BATCH):
        yf_ref[bb] = ys[2 * bb]
        yb_ref[bb] = ys[2 * bb + 1]

    @pl.when(c == pl.num_programs(1) - 1)
    def _():
        sfin_ref[...] = s_scr[...]


def _scan_call(p, s0, lora_w, vecs, head_ones):
    bsz, seq, _ = p.shape
    nc = seq // CHUNK
    state_block = (SCAN_BATCH, 2, N_GROUPS, GROUP_W, GROUP_W)
    state_map = lambda b, c: (b, 0, 0, 0, 0)
    return pl.pallas_call(
        _scan_kernel,
        grid=(bsz // SCAN_BATCH, nc),
        in_specs=[pl.BlockSpec((SCAN_BATCH, CHUNK, SCAN_COLS), lambda b, c: (b, c, 0)),
                  pl.BlockSpec((SCAN_BATCH, CHUNK, SCAN_COLS), lambda b, c: (b, nc - 1 - c, 0)),
                  pl.BlockSpec(state_block, state_map),
                  pl.BlockSpec((2, LORA_COLS, 2 * RWKV_WIDTH), lambda b, c: (0, 0, 0)),
                  pl.BlockSpec((2, 8, RWKV_WIDTH), lambda b, c: (0, 0, 0)),
                  pl.BlockSpec((GROUP_W, GROUP_W), lambda b, c: (0, 0))],
        out_specs=[pl.BlockSpec((SCAN_BATCH, CHUNK, RWKV_WIDTH), lambda b, c: (b, c, 0)),
                   pl.BlockSpec((SCAN_BATCH, CHUNK, RWKV_WIDTH), lambda b, c: (b, nc - 1 - c, 0)),
                   pl.BlockSpec(state_block, state_map)],
        out_shape=[jax.ShapeDtypeStruct((bsz, seq, RWKV_WIDTH), F32),
                   jax.ShapeDtypeStruct((bsz, seq, RWKV_WIDTH), F32),
                   jax.ShapeDtypeStruct((bsz, 2, N_GROUPS, GROUP_W, GROUP_W), F32)],
        scratch_shapes=[pltpu.VMEM(state_block, F32)],
        compiler_params=pltpu.CompilerParams(
            dimension_semantics=("parallel", "arbitrary"), vmem_limit_bytes=VMEM_LIMIT),
        name="wkv_scan",
    )(p, p, s0, lora_w, vecs, head_ones)


def _mix_kernel(x_ref, yf_ref, yb_ref, rkv_ref, sm_ref, cv_ref, mod_ref, rw_ref, vec_ref, ho_ref,
                cw_ref, cvec_ref, wout_ref, g_ref, o_ref, upad, *, tm):
    n_lines = tm // GRID_W
    ones_bd = ho_ref[...]
    y = yf_ref[0] + yb_ref[0]
    inv_n = 1.0 / HEAD_SIZE
    mu = _head_sum(y, ones_bd) * inv_n
    dy = y - mu
    var = _head_sum(dy * dy, ones_bd) * inv_n
    lnx_w = vec_ref[0:1, :]
    lnx_b = vec_ref[1:2, :]
    a0_f = vec_ref[2:3, :]
    a0_b = vec_ref[3:4, :]
    k_a = vec_ref[4:5, :]
    r_k = vec_ref[5:6, :]
    yn = dy * lax.rsqrt(var + EPS_GN) * lnx_w + lnx_b
    sm = sm_ref[0].astype(F32)
    lane = lax.broadcasted_iota(jnp.int32, sm.shape, 1)
    sm = jnp.where(lane >= LORA_COLS, jax.nn.sigmoid(sm), sm)
    lin = _dot(sm, rw_ref[...])
    iclr_f = jax.nn.sigmoid(lin[:, 0:RWKV_WIDTH] + a0_f)
    iclr_b = jax.nn.sigmoid(lin[:, RWKV_WIDTH:2 * RWKV_WIDTH] + a0_b)
    gate = lin[:, 2 * RWKV_WIDTH:]
    r = rkv_ref[0, :, 0:RWKV_WIDTH].astype(F32)
    k = rkv_ref[0, :, RWKV_WIDTH:2 * RWKV_WIDTH].astype(F32)
    v = rkv_ref[0, :, 2 * RWKV_WIDTH:3 * RWKV_WIDTH].astype(F32)
    k_bar = k * (1.0 + (0.5 * (iclr_f + iclr_b) - 1.0) * k_a)
    bonus = _head_sum(r * k_bar * r_k, ones_bd) * v
    rw_out = (yn + bonus) * gate

    cv = cv_ref[0].astype(F32)
    u = cv[:, 0:CONV_WIDTH] * jax.nn.sigmoid(cv[:, CONV_WIDTH:])
    zeros = jnp.zeros((16, CONV_WIDTH), F32)
    for ln in range(n_lines):
        u_ln = u[ln * GRID_W:(ln + 1) * GRID_W, :]
        for s in range(8):
            upad[s, ln, 0:16, :] = zeros
            upad[s, ln, CONV_ROWS - 16:CONV_ROWS, :] = zeros
            upad[s, ln, 16 - s:16 - s + GRID_W, :] = u_ln
    conv_b = cvec_ref[0:1, :]
    ln_w = cvec_ref[1:2, :]
    ln_b = cvec_ref[2:3, :]
    conv_lines = []
    for ln in range(n_lines):
        acc = jnp.zeros((GRID_W, CONV_WIDTH), F32)
        for j in range(CONV_KERNEL):
            off = 16 - CONV_PAD + j
            q8 = (off // 8) * 8
            acc = acc + cw_ref[j:j + 1, :] * upad[off % 8, ln, q8:q8 + GRID_W, :]
        conv_lines.append(acc)
    yc = jnp.concatenate(conv_lines, axis=0) + conv_b
    mu_c = jnp.mean(yc, axis=-1, keepdims=True)
    dc = yc - mu_c
    var_c = jnp.mean(dc * dc, axis=-1, keepdims=True)
    conv_out = _silu(dc * lax.rsqrt(var_c + EPS_LN) * ln_w + ln_b)

    mix_in = jnp.concatenate([rw_out.astype(BF16), conv_out.astype(BF16)], axis=1)
    mix = jnp.dot(mix_in, wout_ref[...], preferred_element_type=F32)
    gate_mix = mod_ref[0, 2:3, :]
    o_ref[0] = x_ref[0] + gate_mix * _rms(mix, g_ref[...])


def _mix_call(x, y_f, y_b, p, mods, rw_w, rvecs, head_ones, conv_w, cvecs, w_out, post_g):
    bsz, seq, _ = x.shape
    tm = 512
    tok = lambda width, blk: pl.BlockSpec((1, tm, width), lambda b, t: (b, t, blk))
    const2 = lambda shape: pl.BlockSpec(shape, lambda b, t: (0, 0))
    return pl.pallas_call(
        functools.partial(_mix_kernel, tm=tm),
        grid=(bsz, seq // tm),
        in_specs=[tok(D_MODEL, 0), tok(RWKV_WIDTH, 0), tok(RWKV_WIDTH, 0),
                  tok(3 * RWKV_WIDTH, 0),
                  tok(512, LORA_OFF // 512),
                  tok(2 * CONV_WIDTH, CONV_OFF // (2 * CONV_WIDTH)),
                  pl.BlockSpec((1, N_MOD, D_MODEL), lambda b, t: (b, 0, 0)),
                  const2((512, 3 * RWKV_WIDTH)), const2((8, RWKV_WIDTH)),
                  const2((GROUP_W, GROUP_W)), const2((32, CONV_WIDTH)),
                  const2((8, CONV_WIDTH)), const2((D_MODEL, D_MODEL)), const2((1, D_MODEL))],
        out_specs=tok(D_MODEL, 0),
        out_shape=jax.ShapeDtypeStruct((bsz, seq, D_MODEL), F32),
        scratch_shapes=[pltpu.VMEM((8, tm // GRID_W, CONV_ROWS, CONV_WIDTH), F32)],
        compiler_params=pltpu.CompilerParams(
            dimension_semantics=("parallel", "parallel"), vmem_limit_bytes=VMEM_LIMIT),
        name="mix_out",
    )(x, y_f, y_b, p, p, p, mods, rw_w, rvecs, head_ones, conv_w, cvecs, w_out, post_g)


def _mlp_kernel(x_ref, mod_ref, gpre_ref, gpost_ref, w1_ref, w2_ref, o_ref, acc_ref):
    x = x_ref[0]
    shift = mod_ref[0, 3:4, :]
    scale = mod_ref[0, 4:5, :]
    gate = mod_ref[0, 5:6, :]
    h = (_rms(x, gpre_ref[...]) * (1.0 + scale) + shift).astype(BF16)
    ff_blk = 1024
    for c in range(D_FF // ff_blk):
        hid = jnp.dot(h, w1_ref[:, c * ff_blk:(c + 1) * ff_blk], preferred_element_type=F32)
        hid = jnp.square(jnp.maximum(hid, 0.0)).astype(BF16)
        part = jnp.dot(hid, w2_ref[c * ff_blk:(c + 1) * ff_blk, :], preferred_element_type=F32)
        if c == 0:
            acc_ref[...] = part
        else:
            acc_ref[...] += part
    o_ref[0] = x + gate * _rms(acc_ref[...], gpost_ref[...])


def _mlp_call(x, mods, pre_g, post_g, w1, w2):
    bsz, seq, _ = x.shape
    tm = 512
    return pl.pallas_call(
        _mlp_kernel,
        grid=(bsz, seq // tm),
        in_specs=[pl.BlockSpec((1, tm, D_MODEL), lambda b, t: (b, t, 0)),
                  pl.BlockSpec((1, N_MOD, D_MODEL), lambda b, t: (b, 0, 0)),
                  pl.BlockSpec((1, D_MODEL), lambda b, t: (0, 0)),
                  pl.BlockSpec((1, D_MODEL), lambda b, t: (0, 0)),
                  pl.BlockSpec((D_MODEL, D_FF), lambda b, t: (0, 0)),
                  pl.BlockSpec((D_FF, D_MODEL), lambda b, t: (0, 0))],
        out_specs=pl.BlockSpec((1, tm, D_MODEL), lambda b, t: (b, t, 0)),
        out_shape=jax.ShapeDtypeStruct((bsz, seq, D_MODEL), F32),
        scratch_shapes=[pltpu.VMEM((tm, D_MODEL), F32)],
        compiler_params=pltpu.CompilerParams(
            dimension_semantics=("parallel", "parallel"), vmem_limit_bytes=VMEM_LIMIT),
        name="sqrelu_mlp",
    )(x, mods, pre_g, post_g, w1, w2)


def _pad_cols(a, lo, hi):
    pad = jnp.zeros(a.shape[:-1] + (GATE_PAD - GATE_RANK,), a.dtype)
    return jnp.concatenate([a[..., :lo], pad, a[..., lo:hi]], axis=-1)


def _layer(x, ctx, mods_x, mods_c, prm):
    (mix_pre_g, mix_post_g, mlp_pre_g, mlp_post_g, w_in, mu_prev, mu_next, decay_w0, decay_w2,
     iclr_a0, iclr_a2, k_k, k_a, r_k, gate_w2, lnx_w, lnx_b, conv_w, conv_b, conv_ln_w,
     conv_ln_b, w_out, mlp_w1, mlp_w2) = prm
    bsz = x.shape[0]
    shift_cols = GATE_OFF + GATE_RANK
    in_cols = w_in.shape[1]

    w_pad = _pad_cols(w_in, shift_cols, in_cols).astype(BF16)
    zc = jnp.zeros((2 * CONV_WIDTH,), F32)
    mp_pad = _pad_cols(jnp.concatenate([mu_prev, zc]), shift_cols, in_cols)[None, :]
    mn_pad = _pad_cols(jnp.concatenate([mu_next, zc]), shift_cols, in_cols)[None, :]
    zr = jnp.zeros((DECAY_RANK, RWKV_WIDTH), F32)
    lora_dir = []
    for d in range(2):
        dec_rows = [zr, zr]
        dec_rows[d] = decay_w2[d]
        icl_rows = [zr, zr]
        icl_rows[d] = iclr_a2[d]
        left = jnp.concatenate(dec_rows + [zr, zr], axis=0)
        right = jnp.concatenate([zr, zr] + icl_rows, axis=0)
        lora_dir.append(jnp.concatenate([left, right], axis=1))
    lora_w = jnp.stack(lora_dir).astype(BF16)
    z4 = jnp.zeros((4, RWKV_WIDTH), F32)
    scan_vecs = jnp.stack([jnp.concatenate([decay_w0[d][None], iclr_a0[d][None], k_k[None],
                                            k_a[None], z4], axis=0) for d in range(2)])
    hid = jnp.arange(RWKV_WIDTH) // HEAD_SIZE
    head_ones = (hid[:, None] == hid[None, :]).astype(BF16)
    zl = jnp.zeros((2 * DECAY_RANK, 3 * RWKV_WIDTH), F32)
    za = jnp.zeros((ICLR_RANK, RWKV_WIDTH), F32)
    zg = jnp.zeros((GATE_PAD, RWKV_WIDTH), F32)
    gate_pad = jnp.concatenate([gate_w2, jnp.zeros((GATE_PAD - GATE_RANK, RWKV_WIDTH), F32)], axis=0)
    rw_w = jnp.concatenate([
        zl,
        jnp.concatenate([iclr_a2[0], za, za], axis=1),
        jnp.concatenate([za, iclr_a2[1], za], axis=1),
        jnp.concatenate([zg, zg, gate_pad], axis=1)], axis=0).astype(BF16)
    rvecs = jnp.concatenate([lnx_w[None], lnx_b[None], iclr_a0[0][None], iclr_a0[1][None],
                             k_a[None], r_k.reshape(1, RWKV_WIDTH), jnp.zeros((2, RWKV_WIDTH), F32)],
                            axis=0)
    conv_w_pad = jnp.concatenate([conv_w, jnp.zeros((1, CONV_WIDTH), F32)], axis=0)
    cvecs = jnp.concatenate([conv_b[None], conv_ln_w[None], conv_ln_b[None],
                             jnp.zeros((5, CONV_WIDTH), F32)], axis=0)

    p_x = _proj_call(x, mods_x, mix_pre_g[None], w_pad, mp_pad, mn_pad, P_COLS // PROJ_BLOCK)
    p_c = _proj_call(ctx, mods_c, mix_pre_g[None], w_pad, mp_pad, mn_pad, N_SHIFT_BLOCKS)
    s_zero = jnp.zeros((bsz, 2, N_GROUPS, GROUP_W, GROUP_W), F32)
    _, _, s_ctx = _scan_call(p_c, s_zero, lora_w, scan_vecs, head_ones)
    y_f, y_b, _ = _scan_call(p_x, s_ctx, lora_w, scan_vecs, head_ones)
    x = _mix_call(x, y_f, y_b, p_x, mods_x, rw_w, rvecs, head_ones, conv_w_pad, cvecs,
                  w_out.astype(BF16), mix_post_g[None])
    x = _mlp_call(x, mods_x, mlp_pre_g[None], mlp_post_g[None], mlp_w1.astype(BF16),
                  mlp_w2.astype(BF16))
    return x


def kernel(x, c, ctx, c_ctx, ada_w, ada_b, mix_pre_g, mix_post_g, mlp_pre_g, mlp_post_g, w_in, mu_prev, mu_next, decay_w0, decay_w2, iclr_a0, iclr_a2, k_k, k_a, r_k, gate_w2, lnx_w, lnx_b, conv_w, conv_b, conv_ln_w, conv_ln_b, w_out, mlp_w1, mlp_w2):
    depth = ada_w.shape[0]
    assert depth == 1, "context-stream update between layers is not implemented"
    bsz = x.shape[0]
    rows = 8 * ((bsz + 1 + 7) // 8)
    cc = jnp.concatenate([c, c_ctx[None, :], jnp.zeros((rows - bsz - 1, D_MODEL), F32)], axis=0)
    per_layer = (mix_pre_g, mix_post_g, mlp_pre_g, mlp_post_g, w_in, mu_prev, mu_next, decay_w0,
                 decay_w2, iclr_a0, iclr_a2, k_k, k_a, r_k, gate_w2, lnx_w, lnx_b, conv_w, conv_b,
                 conv_ln_w, conv_ln_b, w_out, mlp_w1, mlp_w2)
    for l in range(depth):
        mods = _ada_call(cc, ada_w[l], ada_b[l][None, :]).reshape(rows, N_MOD, D_MODEL)
        mods_x = mods[:bsz]
        mods_c = jnp.broadcast_to(mods[bsz:bsz + 1], (bsz, N_MOD, D_MODEL))
        x = _layer(x, ctx, mods_x, mods_c, tuple(a[l] for a in per_layer))
    return x
```

```python
import functools

import jax
import jax.numpy as jnp
from jax import lax
from jax.experimental import pallas as pl
from jax.experimental.pallas import tpu as pltpu

F32 = jnp.float32
BF16 = jnp.bfloat16

D_MODEL = 1024
RWKV_WIDTH = 512
CONV_WIDTH = 512
HEAD_SIZE = 64
RWKV_HEADS = 8
DECAY_RANK = 64
ICLR_RANK = 64
GATE_RANK = 160
CONV_KERNEL = 31
CONV_PAD = CONV_KERNEL // 2
GRID_W = 64
CONV_ROWS = GRID_W + 24
D_FF = 4 * D_MODEL
N_MOD = 6
EPS_RMS = 1e-6
EPS_LN = 1e-5
EPS_GN = 64e-5
DECAY_SCALE = 0.6065306597126334
LOG2_E = 1.4426950408889634

LORA_OFF = 3 * RWKV_WIDTH
LORA_COLS = 256
GATE_OFF = LORA_OFF + LORA_COLS
GATE_PAD = 256
CONV_OFF = GATE_OFF + GATE_PAD
P_COLS = CONV_OFF + 2 * CONV_WIDTH
SCAN_COLS = GATE_OFF
PROJ_BLOCK = 1024
N_SHIFT_BLOCKS = CONV_OFF // PROJ_BLOCK

CHUNK = 64
GROUP_HEADS = 4
GROUP_W = GROUP_HEADS * HEAD_SIZE
N_GROUPS = RWKV_HEADS // GROUP_HEADS
SCAN_BATCH = 4

VMEM_LIMIT = 56 * 1024 * 1024


def _sigmoid(x):
    return 0.5 * jnp.tanh(0.5 * x) + 0.5


def _silu(x):
    return x * _sigmoid(x)


def _rms(x, g):
    return x * lax.rsqrt(jnp.mean(x * x, axis=-1, keepdims=True) + EPS_RMS) * g


def _dot(a, b):
    return jnp.dot(a.astype(BF16), b.astype(BF16), preferred_element_type=F32)


def _dot_nt(a, b):
    return lax.dot_general(a.astype(BF16), b.astype(BF16), (((1,), (1,)), ((), ())),
                           preferred_element_type=F32)


def _dot_tn(a, b):
    return lax.dot_general(a.astype(BF16), b.astype(BF16), (((0,), (0,)), ((), ())),
                           preferred_element_type=F32)


def _dot_split_rhs(a_bf16, x):
    hi = x.astype(BF16)
    lo = (x - hi.astype(F32)).astype(BF16)
    d = lambda p: jnp.dot(a_bf16, p, preferred_element_type=F32)
    return d(hi) + d(lo)


def _head_sum(z, ones_bd):
    rows = z.shape[0]
    hi = z.astype(BF16)
    lo = (z - hi.astype(F32)).astype(BF16)
    pieces = []
    for g in range(N_GROUPS):
        pieces += [hi[:, g * GROUP_W:(g + 1) * GROUP_W], lo[:, g * GROUP_W:(g + 1) * GROUP_W]]
    res = jnp.dot(jnp.concatenate(pieces, axis=0), ones_bd, preferred_element_type=F32)
    return jnp.concatenate(
        [res[(2 * g) * rows:(2 * g + 1) * rows] + res[(2 * g + 1) * rows:(2 * g + 2) * rows]
         for g in range(N_GROUPS)], axis=1)


def _ada_kernel(c_ref, w_ref, b_ref, o_ref):
    o_ref[...] = _dot(_silu(c_ref[...]), w_ref[...]) + b_ref[...]


def _ada_call(cc, ada_w, ada_b):
    rows = cc.shape[0]
    n_out = ada_w.shape[1]
    blk = 1536
    return pl.pallas_call(
        _ada_kernel,
        grid=(n_out // blk,),
        in_specs=[pl.BlockSpec((rows, D_MODEL), lambda n: (0, 0)),
                  pl.BlockSpec((D_MODEL, blk), lambda n: (0, n)),
                  pl.BlockSpec((1, blk), lambda n: (0, n))],
        out_specs=pl.BlockSpec((rows, blk), lambda n: (0, n)),
        out_shape=jax.ShapeDtypeStruct((rows, n_out), F32),
        compiler_params=pltpu.CompilerParams(vmem_limit_bytes=VMEM_LIMIT),
        name="ada_mod",
    )(cc, ada_w, ada_b)


def _proj_kernel(x_ref, mod_ref, g_ref, w_ref, mp_ref, mn_ref, o_ref, h_scr, z_scr, *, seq):
    n = pl.program_id(1)
    row_chunk = min(seq, 512)

    @pl.when(n == 0)
    def _():
        shift = mod_ref[0, 0:1, :]
        scale = mod_ref[0, 1:2, :]
        g = g_ref[...]
        for s in range(0, seq, row_chunk):
            xs = x_ref[0, s:s + row_chunk, :]
            h_scr[s:s + row_chunk, :] = (_rms(xs, g) * (1.0 + scale) + shift).astype(BF16)

    @pl.when(n < N_SHIFT_BLOCKS)
    def _():
        z_scr[0:8, :] = jnp.zeros((8, PROJ_BLOCK), F32)
        z_scr[8 + seq:16 + seq, :] = jnp.zeros((8, PROJ_BLOCK), F32)
        mp = mp_ref[...]
        mn = mn_ref[...]

        def shift_rows(s):
            cur = z_scr[8 + s:8 + s + row_chunk, :]
            prv = z_scr[7 + s:7 + s + row_chunk, :]
            nxt = z_scr[9 + s:9 + s + row_chunk, :]
            o_ref[0, s:s + row_chunk, :] = (cur + mp * (prv - cur) + mn * (nxt - cur)).astype(BF16)

        for s in range(0, seq, row_chunk):
            z_scr[8 + s:8 + s + row_chunk, :] = jnp.dot(
                h_scr[s:s + row_chunk, :], w_ref[...], preferred_element_type=F32)
            if s > 0:
                shift_rows(s - row_chunk)
        shift_rows(seq - row_chunk)

    @pl.when(n >= N_SHIFT_BLOCKS)
    def _():
        for s in range(0, seq, row_chunk):
            o_ref[0, s:s + row_chunk, :] = jnp.dot(
                h_scr[s:s + row_chunk, :], w_ref[...], preferred_element_type=F32).astype(BF16)


def _proj_call(x, mods, g, w_pad, mp_pad, mn_pad, n_blocks):
    bsz, seq, _ = x.shape
    n_cols = n_blocks * PROJ_BLOCK
    return pl.pallas_call(
        functools.partial(_proj_kernel, seq=seq),
        grid=(bsz, n_blocks),
        in_specs=[pl.BlockSpec((1, seq, D_MODEL), lambda b, n: (b, 0, 0)),
                  pl.BlockSpec((1, N_MOD, D_MODEL), lambda b, n: (b, 0, 0)),
                  pl.BlockSpec((1, D_MODEL), lambda b, n: (0, 0)),
                  pl.BlockSpec((D_MODEL, PROJ_BLOCK), lambda b, n: (0, n)),
                  pl.BlockSpec((1, PROJ_BLOCK), lambda b, n: (0, n)),
                  pl.BlockSpec((1, PROJ_BLOCK), lambda b, n: (0, n))],
        out_specs=pl.BlockSpec((1, seq, PROJ_BLOCK), lambda b, n: (b, 0, n)),
        out_shape=jax.ShapeDtypeStruct((bsz, seq, n_cols), BF16),
        scratch_shapes=[pltpu.VMEM((seq, D_MODEL), BF16),
                        pltpu.VMEM((seq + 16, PROJ_BLOCK), F32)],
        compiler_params=pltpu.CompilerParams(
            dimension_semantics=("parallel", "arbitrary"), vmem_limit_bytes=VMEM_LIMIT),
        name="in_proj",
    )(x, mods, g, w_pad, mp_pad, mn_pad)


def _expand(x, bd_mask):
    return jnp.where(bd_mask, jnp.concatenate([x] * GROUP_HEADS, axis=0), 0.0)


def _scan_prep(blocks, lw_ref, vec_ref, ones_bd):
    L = CHUNK
    n = len(blocks)
    lane = lax.broadcasted_iota(jnp.int32, (L, LORA_COLS), 1)
    loras = []
    for pb, _ in blocks:
        lora = pb[:, LORA_OFF:LORA_OFF + LORA_COLS]
        loras.append(jnp.where(lane < 2 * DECAY_RANK, jnp.tanh(lora), lora))
    lin = [None] * n
    for d in range(2):
        idx = [i for i in range(n) if blocks[i][1] == bool(d)]
        res = _dot(jnp.concatenate([loras[i] for i in idx], axis=0), lw_ref[d])
        for j, i in enumerate(idx):
            lin[i] = res[j * L:(j + 1) * L]

    kk0s, pieces = [], []
    for i, (pb, reverse) in enumerate(blocks):
        kk0 = pb[:, RWKV_WIDTH:2 * RWKV_WIDTH] * vec_ref[int(reverse), 2:3, :]
        kk0s.append(kk0)
        sq = (kk0 * kk0).astype(BF16)
        pieces += [sq[:, g * GROUP_W:(g + 1) * GROUP_W] for g in range(N_GROUPS)]
    sums = jnp.dot(jnp.concatenate(pieces, axis=0), ones_bd, preferred_element_type=F32)

    row = lax.broadcasted_iota(jnp.int32, (L, L), 0)
    col = lax.broadcasted_iota(jnp.int32, (L, L), 1)
    preps = []
    for i, (pb, reverse) in enumerate(blocks):
        vecs = vec_ref[int(reverse)]
        w0 = vecs[0:1, :]
        a0 = vecs[1:2, :]
        k_a = vecs[3:4, :]
        r = pb[:, 0:RWKV_WIDTH]
        k = pb[:, RWKV_WIDTH:2 * RWKV_WIDTH]
        v = pb[:, 2 * RWKV_WIDTH:3 * RWKV_WIDTH]
        half = 0.5 * LOG2_E * DECAY_SCALE
        lw = -half * jnp.tanh(0.5 * (lin[i][:, 0:RWKV_WIDTH] + w0)) - half
        neg_iclr = -0.5 * jnp.tanh(0.5 * (lin[i][:, RWKV_WIDTH:] + a0)) - 0.5
        base = i * N_GROUPS * L
        ss = jnp.concatenate(
            [sums[base + g * L:base + (g + 1) * L] for g in range(N_GROUPS)], axis=1)
        kk = kk0s[i] * lax.rsqrt(jnp.maximum(ss, 1e-24))
        kd = k * (1.0 - (neg_iclr + 1.0) * k_a)
        na = kk
        nb = kk * neg_iclr
        tri = jnp.where((col >= row) if reverse else (col <= row), 1.0, 0.0).astype(BF16)
        cum = _dot_split_rhs(tri, lw)
        tot = cum[0:1, :] if reverse else cum[L - 1:L, :]
        e_incl = jnp.exp2(cum)
        e_neg = 1.0 / e_incl
        w_tot = jnp.exp2(tot)
        e_rem = w_tot * e_neg
        preps.append(dict(
            r_t=r * e_incl, a_t=na * jnp.exp2(cum - lw), b_t=nb * e_neg, k_t=kd * e_neg,
            b_h=nb * e_rem, k_h=kd * e_rem, v=v, w_tot=w_tot))
    return preps


def _scan_units(preps, reverse, s_idx, s_ref):
    L = CHUNK
    rr = lax.broadcasted_iota(jnp.int32, (GROUP_W, GROUP_W), 0)
    cc = lax.broadcasted_iota(jnp.int32, (GROUP_W, GROUP_W), 1)
    bd_mask = jnp.right_shift(rr, 6) == jnp.right_shift(cc, 6)
    tr = lax.broadcasted_iota(jnp.int32, (L, GROUP_W), 0)
    tc = jnp.bitwise_and(lax.broadcasted_iota(jnp.int32, (L, GROUP_W), 1), L - 1)
    eye = jnp.where(tc == tr, 1.0, 0.0)
    strict = {True: tc > tr, False: tc < tr}
    incl = {True: tc >= tr, False: tc <= tr}
    units = [(d, g) for d in range(len(preps)) for g in range(N_GROUPS)]
    sl = lambda g: slice(g * GROUP_W, (g + 1) * GROUP_W)
    ex = lambda x: _expand(x, bd_mask)

    lhs, a_ab, a_ak, a_rb, a_rk = {}, {}, {}, {}, {}
    for u in units:
        d, g = u
        p = preps[d]
        lhs[u] = jnp.concatenate([p["a_t"][:, sl(g)], p["r_t"][:, sl(g)]], axis=0)
        e_bk = jnp.concatenate([ex(p["b_t"][:, sl(g)]), ex(p["k_t"][:, sl(g)])], axis=0)
        a_all = _dot_nt(lhs[u], e_bk)
        a_ab[u] = jnp.where(strict[reverse[d]], a_all[0:L, 0:GROUP_W], 0.0)
        a_ak[u] = jnp.where(strict[reverse[d]], a_all[0:L, GROUP_W:], 0.0)
        a_rb[u] = jnp.where(incl[reverse[d]], a_all[L:, 0:GROUP_W], 0.0)
        a_rk[u] = jnp.where(incl[reverse[d]], a_all[L:, GROUP_W:], 0.0)
    av = {}
    for u in units:
        d, g = u
        av[u] = _dot(jnp.concatenate([a_ak[u], a_rk[u]], axis=0), ex(preps[d]["v"][:, sl(g)]))

    n_levels = 6
    npow = {u: _dot(a_ab[u], ex(a_ab[u])) for u in units}
    pinv = {u: eye + a_ab[u] for u in units}
    for lvl in range(1, n_levels):
        for u in units:
            rhs = ex(npow[u])
            if lvl < n_levels - 1:
                both = _dot(jnp.concatenate([npow[u], pinv[u]], axis=0), rhs)
                npow[u] = both[0:L]
                pinv[u] = pinv[u] + both[L:]
            else:
                pinv[u] = pinv[u] + _dot(pinv[u], rhs)

    s_old, sa, x_u, u_u, y_u = {}, {}, {}, {}, {}
    for u in units:
        d, g = u
        s_old[u] = s_ref[s_idx[d] + (g,)]
        sa[u] = _dot(lhs[u], s_old[u]) + av[u]
    for u in units:
        x_u[u] = sa[u][0:L]
        u_u[u] = _dot(pinv[u], ex(x_u[u]))
    for u in units:
        y_u[u] = sa[u][L:] + _dot(a_rb[u], ex(u_u[u]))
    w_col = [jnp.transpose(jnp.broadcast_to(p["w_tot"], (128, RWKV_WIDTH))) for p in preps]
    for u in units:
        d, g = u
        p = preps[d]
        upd = _dot_tn(jnp.concatenate([p["b_h"][:, sl(g)], p["k_h"][:, sl(g)]], axis=0),
                      jnp.concatenate([u_u[u], p["v"][:, sl(g)]], axis=0))
        decay = jnp.concatenate([w_col[d][sl(g), :]] * (GROUP_W // 128), axis=1)
        s_ref[s_idx[d] + (g,)] = s_old[u] * decay + jnp.where(bd_mask, upd, 0.0)
    return [jnp.concatenate([y_u[(d, g)] for g in range(N_GROUPS)], axis=1)
            for d in range(len(preps))]


def _scan_kernel(pf_ref, pb_ref, s0_ref, lw_ref, vec_ref, ho_ref, yf_ref, yb_ref, sfin_ref, s_scr):
    c = pl.program_id(1)

    @pl.when(c == 0)
    def _():
        s_scr[...] = s0_ref[...]

    n_streams = 2 * SCAN_BATCH
    reverse = [bool(i % 2) for i in range(n_streams)]
    s_idx = [(i // 2, i % 2) for i in range(n_streams)]
    blocks = []
    for bb in range(SCAN_BATCH):
        blocks += [(pf_ref[bb].astype(F32), False), (pb_ref[bb].astype(F32), True)]
    preps = _scan_prep(blocks, lw_ref, vec_ref, ho_ref[...])
    ys = _scan_units(preps, reverse, s_idx, s_scr)
    for bb in range(SCAN_BATCH):
        yf_ref[bb] = ys[2 * bb]
        yb_ref[bb] = ys[2 * bb + 1]

    @pl.when(c == pl.num_programs(1) - 1)
    def _():
        sfin_ref[...] = s_scr[...]


def _scan_call(p, s0, lora_w, vecs, head_ones):
    bsz, seq, _ = p.shape
    nc = seq // CHUNK
    state_block = (SCAN_BATCH, 2, N_GROUPS, GROUP_W, GROUP_W)
    state_map = lambda b, c: (b, 0, 0, 0, 0)
    return pl.pallas_call(
        _scan_kernel,
        grid=(bsz // SCAN_BATCH, nc),
        in_specs=[pl.BlockSpec((SCAN_BATCH, CHUNK, SCAN_COLS), lambda b, c: (b, c, 0)),
                  pl.BlockSpec((SCAN_BATCH, CHUNK, SCAN_COLS), lambda b, c: (b, nc - 1 - c, 0)),
                  pl.BlockSpec(state_block, state_map),
                  pl.BlockSpec((2, LORA_COLS, 2 * RWKV_WIDTH), lambda b, c: (0, 0, 0)),
                  pl.BlockSpec((2, 8, RWKV_WIDTH), lambda b, c: (0, 0, 0)),
                  pl.BlockSpec((GROUP_W, GROUP_W), lambda b, c: (0, 0))],
        out_specs=[pl.BlockSpec((SCAN_BATCH, CHUNK, RWKV_WIDTH), lambda b, c: (b, c, 0)),
                   pl.BlockSpec((SCAN_BATCH, CHUNK, RWKV_WIDTH), lambda b, c: (b, nc - 1 - c, 0)),
                   pl.BlockSpec(state_block, state_map)],
        out_shape=[jax.ShapeDtypeStruct((bsz, seq, RWKV_WIDTH), F32),
                   jax.ShapeDtypeStruct((bsz, seq, RWKV_WIDTH), F32),
                   jax.ShapeDtypeStruct((bsz, 2, N_GROUPS, GROUP_W, GROUP_W), F32)],
        scratch_shapes=[pltpu.VMEM(state_block, F32)],
        compiler_params=pltpu.CompilerParams(
            dimension_semantics=("parallel", "arbitrary"), vmem_limit_bytes=VMEM_LIMIT),
        name="wkv_scan",
    )(p, p, s0, lora_w, vecs, head_ones)


def _mix_kernel(x_ref, yf_ref, yb_ref, rkv_ref, sm_ref, cv_ref, mod_ref, rw_ref, vec_ref, ho_ref,
                cw_ref, cvec_ref, wout_ref, g_ref, o_ref, upad, *, tm):
    n_lines = tm // GRID_W
    ones_bd = ho_ref[...]
    y = yf_ref[0] + yb_ref[0]
    inv_n = 1.0 / HEAD_SIZE
    mu = _head_sum(y, ones_bd) * inv_n
    dy = y - mu
    var = _head_sum(dy * dy, ones_bd) * inv_n
    lnx_w = vec_ref[0:1, :]
    lnx_b = vec_ref[1:2, :]
    a0_f = vec_ref[2:3, :]
    a0_b = vec_ref[3:4, :]
    k_a = vec_ref[4:5, :]
    r_k = vec_ref[5:6, :]
    yn = dy * lax.rsqrt(var + EPS_GN) * lnx_w + lnx_b
    sm = sm_ref[0].astype(F32)
    lane = lax.broadcasted_iota(jnp.int32, sm.shape, 1)
    sm = jnp.where(lane >= LORA_COLS, _sigmoid(sm), sm)
    lin = _dot(sm, rw_ref[...])
    iclr_f = _sigmoid(lin[:, 0:RWKV_WIDTH] + a0_f)
    iclr_b = _sigmoid(lin[:, RWKV_WIDTH:2 * RWKV_WIDTH] + a0_b)
    gate = lin[:, 2 * RWKV_WIDTH:]
    r = rkv_ref[0, :, 0:RWKV_WIDTH].astype(F32)
    k = rkv_ref[0, :, RWKV_WIDTH:2 * RWKV_WIDTH].astype(F32)
    v = rkv_ref[0, :, 2 * RWKV_WIDTH:3 * RWKV_WIDTH].astype(F32)
    k_bar = k * (1.0 + (0.5 * (iclr_f + iclr_b) - 1.0) * k_a)
    bonus = _head_sum(r * k_bar * r_k, ones_bd) * v
    rw_out = (yn + bonus) * gate

    cv = cv_ref[0].astype(F32)
    u = cv[:, 0:CONV_WIDTH] * _sigmoid(cv[:, CONV_WIDTH:])
    zeros = jnp.zeros((16, CONV_WIDTH), F32)
    for ln in range(n_lines):
        u_ln = u[ln * GRID_W:(ln + 1) * GRID_W, :]
        for s in range(8):
            upad[s, ln, 0:16, :] = zeros
            upad[s, ln, CONV_ROWS - 16:CONV_ROWS, :] = zeros
            upad[s, ln, 16 - s:16 - s + GRID_W, :] = u_ln
    conv_b = cvec_ref[0:1, :]
    ln_w = cvec_ref[1:2, :]
    ln_b = cvec_ref[2:3, :]
    conv_lines = []
    for ln in range(n_lines):
        acc = jnp.zeros((GRID_W, CONV_WIDTH), F32)
        for j in range(CONV_KERNEL):
            off = 16 - CONV_PAD + j
            q8 = (off // 8) * 8
            acc = acc + cw_ref[j:j + 1, :] * upad[off % 8, ln, q8:q8 + GRID_W, :]
        conv_lines.append(acc)
    yc = jnp.concatenate(conv_lines, axis=0) + conv_b
    mu_c = jnp.mean(yc, axis=-1, keepdims=True)
    dc = yc - mu_c
    var_c = jnp.mean(dc * dc, axis=-1, keepdims=True)
    conv_out = _silu(dc * lax.rsqrt(var_c + EPS_LN) * ln_w + ln_b)

    mix_in = jnp.concatenate([rw_out.astype(BF16), conv_out.astype(BF16)], axis=1)
    mix = jnp.dot(mix_in, wout_ref[...], preferred_element_type=F32)
    gate_mix = mod_ref[0, 2:3, :]
    o_ref[0] = x_ref[0] + gate_mix * _rms(mix, g_ref[...])


def _mix_call(x, y_f, y_b, p, mods, rw_w, rvecs, head_ones, conv_w, cvecs, w_out, post_g):
    bsz, seq, _ = x.shape
    tm = 512
    tok = lambda width, blk: pl.BlockSpec((1, tm, width), lambda b, t: (b, t, blk))
    const2 = lambda shape: pl.BlockSpec(shape, lambda b, t: (0, 0))
    return pl.pallas_call(
        functools.partial(_mix_kernel, tm=tm),
        grid=(bsz, seq // tm),
        in_specs=[tok(D_MODEL, 0), tok(RWKV_WIDTH, 0), tok(RWKV_WIDTH, 0),
                  tok(3 * RWKV_WIDTH, 0),
                  tok(512, LORA_OFF // 512),
                  tok(2 * CONV_WIDTH, CONV_OFF // (2 * CONV_WIDTH)),
                  pl.BlockSpec((1, N_MOD, D_MODEL), lambda b, t: (b, 0, 0)),
                  const2((512, 3 * RWKV_WIDTH)), const2((8, RWKV_WIDTH)),
                  const2((GROUP_W, GROUP_W)), const2((32, CONV_WIDTH)),
                  const2((8, CONV_WIDTH)), const2((D_MODEL, D_MODEL)), const2((1, D_MODEL))],
        out_specs=tok(D_MODEL, 0),
        out_shape=jax.ShapeDtypeStruct((bsz, seq, D_MODEL), F32),
        scratch_shapes=[pltpu.VMEM((8, tm // GRID_W, CONV_ROWS, CONV_WIDTH), F32)],
        compiler_params=pltpu.CompilerParams(
            dimension_semantics=("parallel", "parallel"), vmem_limit_bytes=VMEM_LIMIT),
        name="mix_out",
    )(x, y_f, y_b, p, p, p, mods, rw_w, rvecs, head_ones, conv_w, cvecs, w_out, post_g)


def _mlp_kernel(x_ref, mod_ref, gpre_ref, gpost_ref, w1_ref, w2_ref, o_ref, acc_ref):
    x = x_ref[0]
    shift = mod_ref[0, 3:4, :]
    scale = mod_ref[0, 4:5, :]
    gate = mod_ref[0, 5:6, :]
    h = (_rms(x, gpre_ref[...]) * (1.0 + scale) + shift).astype(BF16)
    ff_blk = 1024
    for c in range(D_FF // ff_blk):
        hid = jnp.dot(h, w1_ref[:, c * ff_blk:(c + 1) * ff_blk], preferred_element_type=F32)
        hid = jnp.square(jnp.maximum(hid, 0.0)).astype(BF16)
        part = jnp.dot(hid, w2_ref[c * ff_blk:(c + 1) * ff_blk, :], preferred_element_type=F32)
        if c == 0:
            acc_ref[...] = part
        else:
            acc_ref[...] += part
    o_ref[0] = x + gate * _rms(acc_ref[...], gpost_ref[...])


def _mlp_call(x, mods, pre_g, post_g, w1, w2):
    bsz, seq, _ = x.shape
    tm = 512
    return pl.pallas_call(
        _mlp_kernel,
        grid=(bsz, seq // tm),
        in_specs=[pl.BlockSpec((1, tm, D_MODEL), lambda b, t: (b, t, 0)),
                  pl.BlockSpec((1, N_MOD, D_MODEL), lambda b, t: (b, 0, 0)),
                  pl.BlockSpec((1, D_MODEL), lambda b, t: (0, 0)),
                  pl.BlockSpec((1, D_MODEL), lambda b, t: (0, 0)),
                  pl.BlockSpec((D_MODEL, D_FF), lambda b, t: (0, 0)),
                  pl.BlockSpec((D_FF, D_MODEL), lambda b, t: (0, 0))],
        out_specs=pl.BlockSpec((1, tm, D_MODEL), lambda b, t: (b, t, 0)),
        out_shape=jax.ShapeDtypeStruct((bsz, seq, D_MODEL), F32),
        scratch_shapes=[pltpu.VMEM((tm, D_MODEL), F32)],
        compiler_params=pltpu.CompilerParams(
            dimension_semantics=("parallel", "parallel"), vmem_limit_bytes=VMEM_LIMIT),
        name="sqrelu_mlp",
    )(x, mods, pre_g, post_g, w1, w2)


def _pad_cols(a, lo, hi):
    pad = jnp.zeros(a.shape[:-1] + (GATE_PAD - GATE_RANK,), a.dtype)
    return jnp.concatenate([a[..., :lo], pad, a[..., lo:hi]], axis=-1)


def _layer(x, ctx, mods_x, mods_c, prm):
    (mix_pre_g, mix_post_g, mlp_pre_g, mlp_post_g, w_in, mu_prev, mu_next, decay_w0, decay_w2,
     iclr_a0, iclr_a2, k_k, k_a, r_k, gate_w2, lnx_w, lnx_b, conv_w, conv_b, conv_ln_w,
     conv_ln_b, w_out, mlp_w1, mlp_w2) = prm
    bsz = x.shape[0]
    shift_cols = GATE_OFF + GATE_RANK
    in_cols = w_in.shape[1]

    w_pad = _pad_cols(w_in, shift_cols, in_cols).astype(BF16)
    zc = jnp.zeros((2 * CONV_WIDTH,), F32)
    mp_pad = _pad_cols(jnp.concatenate([mu_prev, zc]), shift_cols, in_cols)[None, :]
    mn_pad = _pad_cols(jnp.concatenate([mu_next, zc]), shift_cols, in_cols)[None, :]
    zr = jnp.zeros((DECAY_RANK, RWKV_WIDTH), F32)
    lora_dir = []
    for d in range(2):
        dec_rows = [zr, zr]
        dec_rows[d] = decay_w2[d]
        icl_rows = [zr, zr]
        icl_rows[d] = iclr_a2[d]
        left = jnp.concatenate(dec_rows + [zr, zr], axis=0)
        right = jnp.concatenate([zr, zr] + icl_rows, axis=0)
        lora_dir.append(jnp.concatenate([left, right], axis=1))
    lora_w = jnp.stack(lora_dir).astype(BF16)
    z4 = jnp.zeros((4, RWKV_WIDTH), F32)
    scan_vecs = jnp.stack([jnp.concatenate([decay_w0[d][None], iclr_a0[d][None], k_k[None],
                                            k_a[None], z4], axis=0) for d in range(2)])
    hid = jnp.arange(RWKV_WIDTH) // HEAD_SIZE
    head_ones = (hid[:, None] == hid[None, :]).astype(BF16)
    zl = jnp.zeros((2 * DECAY_RANK, 3 * RWKV_WIDTH), F32)
    za = jnp.zeros((ICLR_RANK, RWKV_WIDTH), F32)
    zg = jnp.zeros((GATE_PAD, RWKV_WIDTH), F32)
    gate_pad = jnp.concatenate([gate_w2, jnp.zeros((GATE_PAD - GATE_RANK, RWKV_WIDTH), F32)], axis=0)
    rw_w = jnp.concatenate([
        zl,
        jnp.concatenate([iclr_a2[0], za, za], axis=1),
        jnp.concatenate([za, iclr_a2[1], za], axis=1),
        jnp.concatenate([zg, zg, gate_pad], axis=1)], axis=0).astype(BF16)
    rvecs = jnp.concatenate([lnx_w[None], lnx_b[None], iclr_a0[0][None], iclr_a0[1][None],
                             k_a[None], r_k.reshape(1, RWKV_WIDTH), jnp.zeros((2, RWKV_WIDTH), F32)],
                            axis=0)
    conv_w_pad = jnp.concatenate([conv_w, jnp.zeros((1, CONV_WIDTH), F32)], axis=0)
    cvecs = jnp.concatenate([conv_b[None], conv_ln_w[None], conv_ln_b[None],
                             jnp.zeros((5, CONV_WIDTH), F32)], axis=0)

    p_x = _proj_call(x, mods_x, mix_pre_g[None], w_pad, mp_pad, mn_pad, P_COLS // PROJ_BLOCK)
    p_c = _proj_call(ctx, mods_c, mix_pre_g[None], w_pad, mp_pad, mn_pad, N_SHIFT_BLOCKS)
    s_zero = jnp.zeros((bsz, 2, N_GROUPS, GROUP_W, GROUP_W), F32)
    _, _, s_ctx = _scan_call(p_c, s_zero, lora_w, scan_vecs, head_ones)
    y_f, y_b, _ = _scan_call(p_x, s_ctx, lora_w, scan_vecs, head_ones)
    x = _mix_call(x, y_f, y_b, p_x, mods_x, rw_w, rvecs, head_ones, conv_w_pad, cvecs,
                  w_out.astype(BF16), mix_post_g[None])
    x = _mlp_call(x, mods_x, mlp_pre_g[None], mlp_post_g[None], mlp_w1.astype(BF16),
                  mlp_w2.astype(BF16))
    return x


def kernel(x, c, ctx, c_ctx, ada_w, ada_b, mix_pre_g, mix_post_g, mlp_pre_g, mlp_post_g, w_in, mu_prev, mu_next, decay_w0, decay_w2, iclr_a0, iclr_a2, k_k, k_a, r_k, gate_w2, lnx_w, lnx_b, conv_w, conv_b, conv_ln_w, conv_ln_b, w_out, mlp_w1, mlp_w2):
    depth = ada_w.shape[0]
    assert depth == 1, "context-stream update between layers is not implemented"
    bsz = x.shape[0]
    rows = 8 * ((bsz + 1 + 7) // 8)
    cc = jnp.concatenate([c, c_ctx[None, :], jnp.zeros((rows - bsz - 1, D_MODEL), F32)], axis=0)
    per_layer = (mix_pre_g, mix_post_g, mlp_pre_g, mlp_post_g, w_in, mu_prev, mu_next, decay_w0,
                 decay_w2, iclr_a0, iclr_a2, k_k, k_a, r_k, gate_w2, lnx_w, lnx_b, conv_w, conv_b,
                 conv_ln_w, conv_ln_b, w_out, mlp_w1, mlp_w2)
    for l in range(depth):
        mods = _ada_call(cc, ada_w[l], ada_b[l][None, :]).reshape(rows, N_MOD, D_MODEL)
        mods_x = mods[:bsz]
        mods_c = jnp.broadcast_to(mods[bsz:bsz + 1], (bsz, N_MOD, D_MODEL))
        x = _layer(x, ctx, mods_x, mods_c, tuple(a[l] for a in per_layer))
    return x
```

```python
import functools

import jax
import jax.numpy as jnp
from jax import lax
from jax.experimental import pallas as pl
from jax.experimental.pallas import tpu as pltpu

F32 = jnp.float32
BF16 = jnp.bfloat16

D_MODEL = 1024
RWKV_WIDTH = 512
CONV_WIDTH = 512
HEAD_SIZE = 64
RWKV_HEADS = 8
DECAY_RANK = 64
ICLR_RANK = 64
GATE_RANK = 160
CONV_KERNEL = 31
CONV_PAD = CONV_KERNEL // 2
GRID_W = 64
CONV_ROWS = GRID_W + 24
D_FF = 4 * D_MODEL
N_MOD = 6
EPS_RMS = 1e-6
EPS_LN = 1e-5
EPS_GN = 64e-5
DECAY_SCALE = 0.6065306597126334
LOG2_E = 1.4426950408889634

LORA_OFF = 3 * RWKV_WIDTH
LORA_COLS = 256
GATE_OFF = LORA_OFF + LORA_COLS
GATE_PAD = 256
CONV_OFF = GATE_OFF + GATE_PAD
P_COLS = CONV_OFF + 2 * CONV_WIDTH
SCAN_COLS = GATE_OFF
PROJ_BLOCK = 1024
N_SHIFT_BLOCKS = CONV_OFF // PROJ_BLOCK

CHUNK = 64
GROUP_HEADS = 4
GROUP_W = GROUP_HEADS * HEAD_SIZE
N_GROUPS = RWKV_HEADS // GROUP_HEADS
SCAN_BATCH = 4

LANES = 128
VMEM_LIMIT = 56 * 1024 * 1024


def _sigmoid(x):
    return 0.5 * jnp.tanh(0.5 * x) + 0.5


def _silu(x):
    return x * _sigmoid(x)


def _rms(x, g):
    return x * lax.rsqrt(jnp.mean(x * x, axis=-1, keepdims=True) + EPS_RMS) * g


def _dot(a, b):
    return jnp.dot(a.astype(BF16), b.astype(BF16), preferred_element_type=F32)


def _dot_nt(a, b):
    return lax.dot_general(a.astype(BF16), b.astype(BF16), (((1,), (1,)), ((), ())),
                           preferred_element_type=F32)


def _dot_tn(a, b):
    return lax.dot_general(a.astype(BF16), b.astype(BF16), (((0,), (0,)), ((), ())),
                           preferred_element_type=F32)


def _dot_split_rhs(a_bf16, x):
    hi = x.astype(BF16)
    lo = (x - hi.astype(F32)).astype(BF16)
    d = lambda p: jnp.dot(a_bf16, p, preferred_element_type=F32)
    return d(hi) + d(lo)


def _head_sum(z, ones_bd):
    rows = z.shape[0]
    hi = z.astype(BF16)
    lo = (z - hi.astype(F32)).astype(BF16)
    pieces = []
    for g in range(N_GROUPS):
        pieces += [hi[:, g * GROUP_W:(g + 1) * GROUP_W], lo[:, g * GROUP_W:(g + 1) * GROUP_W]]
    res = jnp.dot(jnp.concatenate(pieces, axis=0), ones_bd, preferred_element_type=F32)
    return jnp.concatenate(
        [res[(2 * g) * rows:(2 * g + 1) * rows] + res[(2 * g + 1) * rows:(2 * g + 2) * rows]
         for g in range(N_GROUPS)], axis=1)


def _ada_kernel(c_ref, w_ref, b_ref, o_ref):
    o_ref[...] = _dot(_silu(c_ref[...]), w_ref[...]) + b_ref[...]


def _ada_call(cc, ada_w, ada_b):
    rows = cc.shape[0]
    n_out = ada_w.shape[1]
    blk = 1536
    return pl.pallas_call(
        _ada_kernel,
        grid=(n_out // blk,),
        in_specs=[pl.BlockSpec((rows, D_MODEL), lambda n: (0, 0)),
                  pl.BlockSpec((D_MODEL, blk), lambda n: (0, n)),
                  pl.BlockSpec((1, blk), lambda n: (0, n))],
        out_specs=pl.BlockSpec((rows, blk), lambda n: (0, n)),
        out_shape=jax.ShapeDtypeStruct((rows, n_out), F32),
        compiler_params=pltpu.CompilerParams(vmem_limit_bytes=VMEM_LIMIT),
        name="ada_mod",
    )(cc, ada_w, ada_b)


def _proj_kernel(x_ref, mod_ref, g_ref, w_ref, mp_ref, mn_ref, o_ref, h_scr, z_scr, *, seq):
    n = pl.program_id(1)
    row_chunk = min(seq, 1024)

    @pl.when(n == 0)
    def _():
        shift = mod_ref[0, 0:1, :]
        scale = mod_ref[0, 1:2, :]
        g = g_ref[...]
        for s in range(0, seq, row_chunk):
            xs = x_ref[0, s:s + row_chunk, :]
            h_scr[s:s + row_chunk, :] = (_rms(xs, g) * (1.0 + scale) + shift).astype(BF16)

    @pl.when(n < N_SHIFT_BLOCKS)
    def _():
        z_scr[0:8, :] = jnp.zeros((8, PROJ_BLOCK), F32)
        z_scr[8 + seq:16 + seq, :] = jnp.zeros((8, PROJ_BLOCK), F32)
        mp = mp_ref[...]
        mn = mn_ref[...]

        def shift_rows(s):
            cur = z_scr[8 + s:8 + s + row_chunk, :]
            prv = z_scr[7 + s:7 + s + row_chunk, :]
            nxt = z_scr[9 + s:9 + s + row_chunk, :]
            o_ref[0, s:s + row_chunk, :] = (cur + mp * (prv - cur) + mn * (nxt - cur)).astype(BF16)

        for s in range(0, seq, row_chunk):
            z_scr[8 + s:8 + s + row_chunk, :] = jnp.dot(
                h_scr[s:s + row_chunk, :], w_ref[...], preferred_element_type=F32)
            if s > 0:
                shift_rows(s - row_chunk)
        shift_rows(seq - row_chunk)

    @pl.when(n >= N_SHIFT_BLOCKS)
    def _():
        for s in range(0, seq, row_chunk):
            o_ref[0, s:s + row_chunk, :] = jnp.dot(
                h_scr[s:s + row_chunk, :], w_ref[...], preferred_element_type=F32).astype(BF16)


def _proj_call(x, mods, g, w_pad, mp_pad, mn_pad, n_blocks):
    bsz, seq, _ = x.shape
    n_cols = n_blocks * PROJ_BLOCK
    return pl.pallas_call(
        functools.partial(_proj_kernel, seq=seq),
        grid=(bsz, n_blocks),
        in_specs=[pl.BlockSpec((1, seq, D_MODEL), lambda b, n: (b, 0, 0)),
                  pl.BlockSpec((1, N_MOD, D_MODEL), lambda b, n: (b, 0, 0)),
                  pl.BlockSpec((1, D_MODEL), lambda b, n: (0, 0)),
                  pl.BlockSpec((D_MODEL, PROJ_BLOCK), lambda b, n: (0, n)),
                  pl.BlockSpec((1, PROJ_BLOCK), lambda b, n: (0, n)),
                  pl.BlockSpec((1, PROJ_BLOCK), lambda b, n: (0, n))],
        out_specs=pl.BlockSpec((1, seq, PROJ_BLOCK), lambda b, n: (b, 0, n)),
        out_shape=jax.ShapeDtypeStruct((bsz, seq, n_cols), BF16),
        scratch_shapes=[pltpu.VMEM((seq, D_MODEL), BF16),
                        pltpu.VMEM((seq + 16, PROJ_BLOCK), F32)],
        compiler_params=pltpu.CompilerParams(
            dimension_semantics=("parallel", "arbitrary"), vmem_limit_bytes=VMEM_LIMIT),
        name="in_proj",
    )(x, mods, g, w_pad, mp_pad, mn_pad)


def _expand(x, bd_mask):
    return jnp.where(bd_mask, jnp.concatenate([x] * GROUP_HEADS, axis=0), 0.0)


def _scan_prep(blocks, lw_ref, vec_ref, ones_bd):
    L = CHUNK
    n = len(blocks)
    lane = lax.broadcasted_iota(jnp.int32, (L, LORA_COLS), 1)
    loras = []
    for pb, _ in blocks:
        lora = pb[:, LORA_OFF:LORA_OFF + LORA_COLS]
        loras.append(jnp.where(lane < 2 * DECAY_RANK, jnp.tanh(lora), lora))
    lin = [None] * n
    for d in range(2):
        idx = [i for i in range(n) if blocks[i][1] == bool(d)]
        res = _dot(jnp.concatenate([loras[i] for i in idx], axis=0), lw_ref[d])
        for j, i in enumerate(idx):
            lin[i] = res[j * L:(j + 1) * L]

    kk0s, pieces = [], []
    for i, (pb, reverse) in enumerate(blocks):
        kk0 = pb[:, RWKV_WIDTH:2 * RWKV_WIDTH] * vec_ref[int(reverse), 2:3, :]
        kk0s.append(kk0)
        sq = (kk0 * kk0).astype(BF16)
        pieces += [sq[:, g * GROUP_W:(g + 1) * GROUP_W] for g in range(N_GROUPS)]
    sums = jnp.dot(jnp.concatenate(pieces, axis=0), ones_bd, preferred_element_type=F32)

    row = lax.broadcasted_iota(jnp.int32, (L, L), 0)
    col = lax.broadcasted_iota(jnp.int32, (L, L), 1)
    preps = []
    for i, (pb, reverse) in enumerate(blocks):
        vecs = vec_ref[int(reverse)]
        w0 = vecs[0:1, :]
        a0 = vecs[1:2, :]
        k_a = vecs[3:4, :]
        r = pb[:, 0:RWKV_WIDTH]
        k = pb[:, RWKV_WIDTH:2 * RWKV_WIDTH]
        v = pb[:, 2 * RWKV_WIDTH:3 * RWKV_WIDTH]
        half = 0.5 * LOG2_E * DECAY_SCALE
        lw = -half * jnp.tanh(0.5 * (lin[i][:, 0:RWKV_WIDTH] + w0)) - half
        neg_iclr = -0.5 * jnp.tanh(0.5 * (lin[i][:, RWKV_WIDTH:] + a0)) - 0.5
        base = i * N_GROUPS * L
        ss = jnp.concatenate(
            [sums[base + g * L:base + (g + 1) * L] for g in range(N_GROUPS)], axis=1)
        kk = kk0s[i] * lax.rsqrt(jnp.maximum(ss, 1e-24))
        kd = k * (1.0 - (neg_iclr + 1.0) * k_a)
        na = kk
        nb = kk * neg_iclr
        tri = jnp.where((col >= row) if reverse else (col <= row), 1.0, 0.0).astype(BF16)
        cum = _dot_split_rhs(tri, lw)
        tot = cum[0:1, :] if reverse else cum[L - 1:L, :]
        e_incl = jnp.exp2(cum)
        e_neg = 1.0 / e_incl
        w_tot = jnp.exp2(tot)
        e_rem = w_tot * e_neg
        preps.append(dict(
            r_t=r * e_incl, a_t=na * jnp.exp2(cum - lw), b_t=nb * e_neg, k_t=kd * e_neg,
            b_h=nb * e_rem, k_h=kd * e_rem, v=v, w_tot=w_tot))
    return preps


def _scan_units(preps, reverse, s_idx, s_ref):
    L = CHUNK
    rr = lax.broadcasted_iota(jnp.int32, (GROUP_W, GROUP_W), 0)
    cc = lax.broadcasted_iota(jnp.int32, (GROUP_W, GROUP_W), 1)
    head_shift = HEAD_SIZE.bit_length() - 1
    bd_mask = jnp.right_shift(rr, head_shift) == jnp.right_shift(cc, head_shift)
    tr = lax.broadcasted_iota(jnp.int32, (L, GROUP_W), 0)
    tc = jnp.bitwise_and(lax.broadcasted_iota(jnp.int32, (L, GROUP_W), 1), L - 1)
    eye = jnp.where(tc == tr, 1.0, 0.0)
    strict = {True: tc > tr, False: tc < tr}
    incl = {True: tc >= tr, False: tc <= tr}
    units = [(d, g) for d in range(len(preps)) for g in range(N_GROUPS)]
    sl = lambda g: slice(g * GROUP_W, (g + 1) * GROUP_W)
    ex = lambda x: _expand(x, bd_mask)

    lhs, a_ab, a_ak, a_rb, a_rk = {}, {}, {}, {}, {}
    for u in units:
        d, g = u
        p = preps[d]
        lhs[u] = jnp.concatenate([p["a_t"][:, sl(g)], p["r_t"][:, sl(g)]], axis=0)
        e_bk = jnp.concatenate([ex(p["b_t"][:, sl(g)]), ex(p["k_t"][:, sl(g)])], axis=0)
        a_all = _dot_nt(lhs[u], e_bk)
        a_ab[u] = jnp.where(strict[reverse[d]], a_all[0:L, 0:GROUP_W], 0.0)
        a_ak[u] = jnp.where(strict[reverse[d]], a_all[0:L, GROUP_W:], 0.0)
        a_rb[u] = jnp.where(incl[reverse[d]], a_all[L:, 0:GROUP_W], 0.0)
        a_rk[u] = jnp.where(incl[reverse[d]], a_all[L:, GROUP_W:], 0.0)
    av = {}
    for u in units:
        d, g = u
        av[u] = _dot(jnp.concatenate([a_ak[u], a_rk[u]], axis=0), ex(preps[d]["v"][:, sl(g)]))

    n_levels = 6
    npow = {u: _dot(a_ab[u], ex(a_ab[u])) for u in units}
    pinv = {u: eye + a_ab[u] for u in units}
    for lvl in range(1, n_levels):
        for u in units:
            rhs = ex(npow[u])
            if lvl < n_levels - 1:
                both = _dot(jnp.concatenate([npow[u], pinv[u]], axis=0), rhs)
                npow[u] = both[0:L]
                pinv[u] = pinv[u] + both[L:]
            else:
                pinv[u] = pinv[u] + _dot(pinv[u], rhs)

    s_old, sa, x_u, u_u, y_u = {}, {}, {}, {}, {}
    for u in units:
        d, g = u
        s_old[u] = s_ref[s_idx[d] + (g,)]
        sa[u] = _dot(lhs[u], s_old[u]) + av[u]
    for u in units:
        x_u[u] = sa[u][0:L]
        u_u[u] = _dot(pinv[u], ex(x_u[u]))
    for u in units:
        y_u[u] = sa[u][L:] + _dot(a_rb[u], ex(u_u[u]))
    w_col = [jnp.transpose(jnp.broadcast_to(p["w_tot"], (LANES, RWKV_WIDTH))) for p in preps]
    for u in units:
        d, g = u
        p = preps[d]
        upd = _dot_tn(jnp.concatenate([p["b_h"][:, sl(g)], p["k_h"][:, sl(g)]], axis=0),
                      jnp.concatenate([u_u[u], p["v"][:, sl(g)]], axis=0))
        decay = jnp.concatenate([w_col[d][sl(g), :]] * (GROUP_W // LANES), axis=1)
        s_ref[s_idx[d] + (g,)] = s_old[u] * decay + jnp.where(bd_mask, upd, 0.0)
    return [jnp.concatenate([y_u[(d, g)] for g in range(N_GROUPS)], axis=1)
            for d in range(len(preps))]


def _scan_kernel(pf_ref, pb_ref, s0_ref, lw_ref, vec_ref, ho_ref, yf_ref, yb_ref, sfin_ref, s_scr):
    c = pl.program_id(1)

    @pl.when(c == 0)
    def _():
        s_scr[...] = s0_ref[...]

    n_streams = 2 * SCAN_BATCH
    reverse = [bool(i % 2) for i in range(n_streams)]
    s_idx = [(i // 2, i % 2) for i in range(n_streams)]
    blocks = []
    for bb in range(SCAN_BATCH):
        blocks += [(pf_ref[bb].astype(F32), False), (pb_ref[bb].astype(F32), True)]
    preps = _scan_prep(blocks, lw_ref, vec_ref, ho_ref[...])
    ys = _scan_units(preps, reverse, s_idx, s_scr)
    for bb in range(SCAN_BATCH):
        yf_ref[bb] = ys[2 * bb]
        yb_ref[bb] = ys[2 * bb + 1]

    @pl.when(c == pl.num_programs(1) - 1)
    def _():
        sfin_ref[...] = s_scr[...]


def _scan_call(p, s0, lora_w, vecs, head_ones):
    bsz, seq, _ = p.shape
    nc = seq // CHUNK
    state_block = (SCAN_BATCH, 2, N_GROUPS, GROUP_W, GROUP_W)
    state_map = lambda b, c: (b, 0, 0, 0, 0)
    return pl.pallas_call(
        _scan_kernel,
        grid=(bsz // SCAN_BATCH, nc),
        in_specs=[pl.BlockSpec((SCAN_BATCH, CHUNK, SCAN_COLS), lambda b, c: (b, c, 0)),
                  pl.BlockSpec((SCAN_BATCH, CHUNK, SCAN_COLS), lambda b, c: (b, nc - 1 - c, 0)),
                  pl.BlockSpec(state_block, state_map),
                  pl.BlockSpec((2, LORA_COLS, 2 * RWKV_WIDTH), lambda b, c: (0, 0, 0)),
                  pl.BlockSpec((2, 8, RWKV_WIDTH), lambda b, c: (0, 0, 0)),
                  pl.BlockSpec((GROUP_W, GROUP_W), lambda b, c: (0, 0))],
        out_specs=[pl.BlockSpec((SCAN_BATCH, CHUNK, RWKV_WIDTH), lambda b, c: (b, c, 0)),
                   pl.BlockSpec((SCAN_BATCH, CHUNK, RWKV_WIDTH), lambda b, c: (b, nc - 1 - c, 0)),
                   pl.BlockSpec(state_block, state_map)],
        out_shape=[jax.ShapeDtypeStruct((bsz, seq, RWKV_WIDTH), F32),
                   jax.ShapeDtypeStruct((bsz, seq, RWKV_WIDTH), F32),
                   jax.ShapeDtypeStruct((bsz, 2, N_GROUPS, GROUP_W, GROUP_W), F32)],
        scratch_shapes=[pltpu.VMEM(state_block, F32)],
        compiler_params=pltpu.CompilerParams(
            dimension_semantics=("parallel", "arbitrary"), vmem_limit_bytes=VMEM_LIMIT),
        name="wkv_scan",
    )(p, p, s0, lora_w, vecs, head_ones)


def _mix_kernel(x_ref, yf_ref, yb_ref, rkv_ref, sm_ref, cv_ref, mod_ref, rw_ref, vec_ref, ho_ref,
                cw_ref, cvec_ref, wout_ref, g_ref, o_ref, upad, *, tm):
    n_lines = tm // GRID_W
    ones_bd = ho_ref[...]
    y = yf_ref[0] + yb_ref[0]
    inv_n = 1.0 / HEAD_SIZE
    mu = _head_sum(y, ones_bd) * inv_n
    dy = y - mu
    var = _head_sum(dy * dy, ones_bd) * inv_n
    lnx_w = vec_ref[0:1, :]
    lnx_b = vec_ref[1:2, :]
    a0_f = vec_ref[2:3, :]
    a0_b = vec_ref[3:4, :]
    k_a = vec_ref[4:5, :]
    r_k = vec_ref[5:6, :]
    yn = dy * lax.rsqrt(var + EPS_GN) * lnx_w + lnx_b
    sm = sm_ref[0].astype(F32)
    lane = lax.broadcasted_iota(jnp.int32, sm.shape, 1)
    sm = jnp.where(lane >= LORA_COLS, _sigmoid(sm), sm)
    lin = _dot(sm, rw_ref[...])
    iclr_f = _sigmoid(lin[:, 0:RWKV_WIDTH] + a0_f)
    iclr_b = _sigmoid(lin[:, RWKV_WIDTH:2 * RWKV_WIDTH] + a0_b)
    gate = lin[:, 2 * RWKV_WIDTH:]
    r = rkv_ref[0, :, 0:RWKV_WIDTH].astype(F32)
    k = rkv_ref[0, :, RWKV_WIDTH:2 * RWKV_WIDTH].astype(F32)
    v = rkv_ref[0, :, 2 * RWKV_WIDTH:3 * RWKV_WIDTH].astype(F32)
    k_bar = k * (1.0 + (0.5 * (iclr_f + iclr_b) - 1.0) * k_a)
    bonus = _head_sum(r * k_bar * r_k, ones_bd) * v
    rw_out = (yn + bonus) * gate

    cv = cv_ref[0].astype(F32)
    u = cv[:, 0:CONV_WIDTH] * _sigmoid(cv[:, CONV_WIDTH:])
    zeros = jnp.zeros((16, CONV_WIDTH), F32)
    for ln in range(n_lines):
        u_ln = u[ln * GRID_W:(ln + 1) * GRID_W, :]
        for s in range(8):
            upad[s, ln, 0:16, :] = zeros
            upad[s, ln, CONV_ROWS - 16:CONV_ROWS, :] = zeros
            upad[s, ln, 16 - s:16 - s + GRID_W, :] = u_ln
    conv_b = cvec_ref[0:1, :]
    ln_w = cvec_ref[1:2, :]
    ln_b = cvec_ref[2:3, :]
    conv_lines = []
    for ln in range(n_lines):
        strips = []
        for lb in range(CONV_WIDTH // LANES):
            lanes = slice(lb * LANES, (lb + 1) * LANES)
            acc = jnp.zeros((GRID_W, LANES), F32)
            for j in range(CONV_KERNEL):
                off = 16 - CONV_PAD + j
                q8 = (off // 8) * 8
                acc = acc + cw_ref[j:j + 1, lanes] * upad[off % 8, ln, q8:q8 + GRID_W, lanes]
            strips.append(acc)
        conv_lines.append(jnp.concatenate(strips, axis=1))
    yc = jnp.concatenate(conv_lines, axis=0) + conv_b
    mu_c = jnp.mean(yc, axis=-1, keepdims=True)
    dc = yc - mu_c
    var_c = jnp.mean(dc * dc, axis=-1, keepdims=True)
    conv_out = _silu(dc * lax.rsqrt(var_c + EPS_LN) * ln_w + ln_b)

    mix_in = jnp.concatenate([rw_out.astype(BF16), conv_out.astype(BF16)], axis=1)
    mix = jnp.dot(mix_in, wout_ref[...], preferred_element_type=F32)
    gate_mix = mod_ref[0, 2:3, :]
    o_ref[0] = x_ref[0] + gate_mix * _rms(mix, g_ref[...])


def _mix_call(x, y_f, y_b, p, mods, rw_w, rvecs, head_ones, conv_w, cvecs, w_out, post_g):
    bsz, seq, _ = x.shape
    tm = 512
    tok = lambda width, blk: pl.BlockSpec((1, tm, width), lambda b, t: (b, t, blk))
    const2 = lambda shape: pl.BlockSpec(shape, lambda b, t: (0, 0))
    return pl.pallas_call(
        functools.partial(_mix_kernel, tm=tm),
        grid=(bsz, seq // tm),
        in_specs=[tok(D_MODEL, 0), tok(RWKV_WIDTH, 0), tok(RWKV_WIDTH, 0),
                  tok(3 * RWKV_WIDTH, 0),
                  tok(512, LORA_OFF // 512),
                  tok(2 * CONV_WIDTH, CONV_OFF // (2 * CONV_WIDTH)),
                  pl.BlockSpec((1, N_MOD, D_MODEL), lambda b, t: (b, 0, 0)),
                  const2((512, 3 * RWKV_WIDTH)), const2((8, RWKV_WIDTH)),
                  const2((GROUP_W, GROUP_W)), const2((32, CONV_WIDTH)),
                  const2((8, CONV_WIDTH)), const2((D_MODEL, D_MODEL)), const2((1, D_MODEL))],
        out_specs=tok(D_MODEL, 0),
        out_shape=jax.ShapeDtypeStruct((bsz, seq, D_MODEL), F32),
        scratch_shapes=[pltpu.VMEM((8, tm // GRID_W, CONV_ROWS, CONV_WIDTH), F32)],
        compiler_params=pltpu.CompilerParams(
            dimension_semantics=("parallel", "parallel"), vmem_limit_bytes=VMEM_LIMIT),
        name="mix_out",
    )(x, y_f, y_b, p, p, p, mods, rw_w, rvecs, head_ones, conv_w, cvecs, w_out, post_g)


def _mlp_kernel(x_ref, mod_ref, gpre_ref, gpost_ref, w1_ref, w2_ref, o_ref, acc_ref):
    x = x_ref[0]
    shift = mod_ref[0, 3:4, :]
    scale = mod_ref[0, 4:5, :]
    gate = mod_ref[0, 5:6, :]
    h = (_rms(x, gpre_ref[...]) * (1.0 + scale) + shift).astype(BF16)
    ff_blk = 1024
    for c in range(D_FF // ff_blk):
        hid = jnp.dot(h, w1_ref[:, c * ff_blk:(c + 1) * ff_blk], preferred_element_type=F32)
        hid = jnp.square(jnp.maximum(hid, 0.0)).astype(BF16)
        part = jnp.dot(hid, w2_ref[c * ff_blk:(c + 1) * ff_blk, :], preferred_element_type=F32)
        if c == 0:
            acc_ref[...] = part
        else:
            acc_ref[...] += part
    o_ref[0] = x + gate * _rms(acc_ref[...], gpost_ref[...])


def _mlp_call(x, mods, pre_g, post_g, w1, w2):
    bsz, seq, _ = x.shape
    tm = 512
    return pl.pallas_call(
        _mlp_kernel,
        grid=(bsz, seq // tm),
        in_specs=[pl.BlockSpec((1, tm, D_MODEL), lambda b, t: (b, t, 0)),
                  pl.BlockSpec((1, N_MOD, D_MODEL), lambda b, t: (b, 0, 0)),
                  pl.BlockSpec((1, D_MODEL), lambda b, t: (0, 0)),
                  pl.BlockSpec((1, D_MODEL), lambda b, t: (0, 0)),
                  pl.BlockSpec((D_MODEL, D_FF), lambda b, t: (0, 0)),
                  pl.BlockSpec((D_FF, D_MODEL), lambda b, t: (0, 0))],
        out_specs=pl.BlockSpec((1, tm, D_MODEL), lambda b, t: (b, t, 0)),
        out_shape=jax.ShapeDtypeStruct((bsz, seq, D_MODEL), F32),
        scratch_shapes=[pltpu.VMEM((tm, D_MODEL), F32)],
        compiler_params=pltpu.CompilerParams(
            dimension_semantics=("parallel", "parallel"), vmem_limit_bytes=VMEM_LIMIT),
        name="sqrelu_mlp",
    )(x, mods, pre_g, post_g, w1, w2)


def _pad_cols(a, lo, hi):
    pad = jnp.zeros(a.shape[:-1] + (GATE_PAD - GATE_RANK,), a.dtype)
    return jnp.concatenate([a[..., :lo], pad, a[..., lo:hi]], axis=-1)


def _layer(x, ctx, mods_x, mods_c, prm):
    (mix_pre_g, mix_post_g, mlp_pre_g, mlp_post_g, w_in, mu_prev, mu_next, decay_w0, decay_w2,
     iclr_a0, iclr_a2, k_k, k_a, r_k, gate_w2, lnx_w, lnx_b, conv_w, conv_b, conv_ln_w,
     conv_ln_b, w_out, mlp_w1, mlp_w2) = prm
    bsz = x.shape[0]
    shift_cols = GATE_OFF + GATE_RANK
    in_cols = w_in.shape[1]

    w_pad = _pad_cols(w_in, shift_cols, in_cols).astype(BF16)
    zc = jnp.zeros((2 * CONV_WIDTH,), F32)
    mp_pad = _pad_cols(jnp.concatenate([mu_prev, zc]), shift_cols, in_cols)[None, :]
    mn_pad = _pad_cols(jnp.concatenate([mu_next, zc]), shift_cols, in_cols)[None, :]
    zr = jnp.zeros((DECAY_RANK, RWKV_WIDTH), F32)
    lora_dir = []
    for d in range(2):
        dec_rows = [zr, zr]
        dec_rows[d] = decay_w2[d]
        icl_rows = [zr, zr]
        icl_rows[d] = iclr_a2[d]
        left = jnp.concatenate(dec_rows + [zr, zr], axis=0)
        right = jnp.concatenate([zr, zr] + icl_rows, axis=0)
        lora_dir.append(jnp.concatenate([left, right], axis=1))
    lora_w = jnp.stack(lora_dir).astype(BF16)
    z4 = jnp.zeros((4, RWKV_WIDTH), F32)
    scan_vecs = jnp.stack([jnp.concatenate([decay_w0[d][None], iclr_a0[d][None], k_k[None],
                                            k_a[None], z4], axis=0) for d in range(2)])
    hid = jnp.arange(RWKV_WIDTH) // HEAD_SIZE
    head_ones = (hid[:, None] == hid[None, :]).astype(BF16)
    zl = jnp.zeros((2 * DECAY_RANK, 3 * RWKV_WIDTH), F32)
    za = jnp.zeros((ICLR_RANK, RWKV_WIDTH), F32)
    zg = jnp.zeros((GATE_PAD, RWKV_WIDTH), F32)
    gate_pad = jnp.concatenate([gate_w2, jnp.zeros((GATE_PAD - GATE_RANK, RWKV_WIDTH), F32)], axis=0)
    rw_w = jnp.concatenate([
        zl,
        jnp.concatenate([iclr_a2[0], za, za], axis=1),
        jnp.concatenate([za, iclr_a2[1], za], axis=1),
        jnp.concatenate([zg, zg, gate_pad], axis=1)], axis=0).astype(BF16)
    rvecs = jnp.concatenate([lnx_w[None], lnx_b[None], iclr_a0[0][None], iclr_a0[1][None],
                             k_a[None], r_k.reshape(1, RWKV_WIDTH), jnp.zeros((2, RWKV_WIDTH), F32)],
                            axis=0)
    conv_w_pad = jnp.concatenate([conv_w, jnp.zeros((1, CONV_WIDTH), F32)], axis=0)
    cvecs = jnp.concatenate([conv_b[None], conv_ln_w[None], conv_ln_b[None],
                             jnp.zeros((5, CONV_WIDTH), F32)], axis=0)

    p_x = _proj_call(x, mods_x, mix_pre_g[None], w_pad, mp_pad, mn_pad, P_COLS // PROJ_BLOCK)
    p_c = _proj_call(ctx, mods_c, mix_pre_g[None], w_pad, mp_pad, mn_pad, N_SHIFT_BLOCKS)
    s_zero = jnp.zeros((bsz, 2, N_GROUPS, GROUP_W, GROUP_W), F32)
    _, _, s_ctx = _scan_call(p_c, s_zero, lora_w, scan_vecs, head_ones)
    y_f, y_b, _ = _scan_call(p_x, s_ctx, lora_w, scan_vecs, head_ones)
    x = _mix_call(x, y_f, y_b, p_x, mods_x, rw_w, rvecs, head_ones, conv_w_pad, cvecs,
                  w_out.astype(BF16), mix_post_g[None])
    x = _mlp_call(x, mods_x, mlp_pre_g[None], mlp_post_g[None], mlp_w1.astype(BF16),
                  mlp_w2.astype(BF16))
    return x


def kernel(x, c, ctx, c_ctx, ada_w, ada_b, mix_pre_g, mix_post_g, mlp_pre_g, mlp_post_g, w_in, mu_prev, mu_next, decay_w0, decay_w2, iclr_a0, iclr_a2, k_k, k_a, r_k, gate_w2, lnx_w, lnx_b, conv_w, conv_b, conv_ln_w, conv_ln_b, w_out, mlp_w1, mlp_w2):
    depth = ada_w.shape[0]
    assert depth == 1, "context-stream update between layers is not implemented"
    bsz = x.shape[0]
    rows = 8 * ((bsz + 1 + 7) // 8)
    cc = jnp.concatenate([c, c_ctx[None, :], jnp.zeros((rows - bsz - 1, D_MODEL), F32)], axis=0)
    per_layer = (mix_pre_g, mix_post_g, mlp_pre_g, mlp_post_g, w_in, mu_prev, mu_next, decay_w0,
                 decay_w2, iclr_a0, iclr_a2, k_k, k_a, r_k, gate_w2, lnx_w, lnx_b, conv_w, conv_b,
                 conv_ln_w, conv_ln_b, w_out, mlp_w1, mlp_w2)
    for l in range(depth):
        mods = _ada_call(cc, ada_w[l], ada_b[l][None, :]).reshape(rows, N_MOD, D_MODEL)
        mods_x = mods[:bsz]
        mods_c = jnp.broadcast_to(mods[bsz:bsz + 1], (bsz, N_MOD, D_MODEL))
        x = _layer(x, ctx, mods_x, mods_c, tuple(a[l] for a in per_layer))
    return x
```

```python
import functools

import jax
import jax.numpy as jnp
from jax import lax
from jax.experimental import pallas as pl
from jax.experimental.pallas import tpu as pltpu

F32 = jnp.float32
BF16 = jnp.bfloat16

D_MODEL = 1024
RWKV_WIDTH = 512
CONV_WIDTH = 512
HEAD_SIZE = 64
RWKV_HEADS = 8
DECAY_RANK = 64
ICLR_RANK = 64
GATE_RANK = 160
CONV_KERNEL = 31
CONV_PAD = CONV_KERNEL // 2
GRID_W = 64
CONV_ROWS = GRID_W + 24
D_FF = 4 * D_MODEL
N_MOD = 6
EPS_RMS = 1e-6
EPS_LN = 1e-5
EPS_GN = 64e-5
DECAY_SCALE = 0.6065306597126334
LOG2_E = 1.4426950408889634

LORA_OFF = 3 * RWKV_WIDTH
LORA_COLS = 256
GATE_OFF = LORA_OFF + LORA_COLS
GATE_PAD = 256
CONV_OFF = GATE_OFF + GATE_PAD
P_COLS = CONV_OFF + 2 * CONV_WIDTH
SCAN_COLS = GATE_OFF
PROJ_BLOCK = 1024
N_SHIFT_BLOCKS = CONV_OFF // PROJ_BLOCK

CHUNK = 64
GROUP_HEADS = 4
GROUP_W = GROUP_HEADS * HEAD_SIZE
N_GROUPS = RWKV_HEADS // GROUP_HEADS
SCAN_BATCH = 8

LANES = 128
VMEM_LIMIT = 56 * 1024 * 1024


def _sigmoid(x):
    return 0.5 * jnp.tanh(0.5 * x) + 0.5


def _silu(x):
    return x * _sigmoid(x)


def _rms(x, g):
    return x * lax.rsqrt(jnp.mean(x * x, axis=-1, keepdims=True) + EPS_RMS) * g


def _dot(a, b):
    return jnp.dot(a.astype(BF16), b.astype(BF16), preferred_element_type=F32)


def _dot_nt(a, b):
    return lax.dot_general(a.astype(BF16), b.astype(BF16), (((1,), (1,)), ((), ())),
                           preferred_element_type=F32)


def _dot_tn(a, b):
    return lax.dot_general(a.astype(BF16), b.astype(BF16), (((0,), (0,)), ((), ())),
                           preferred_element_type=F32)


def _dot_split_rhs(a_bf16, x):
    hi = x.astype(BF16)
    lo = (x - hi.astype(F32)).astype(BF16)
    d = lambda p: jnp.dot(a_bf16, p, preferred_element_type=F32)
    return d(hi) + d(lo)


def _head_sum(z, ones_bd):
    rows = z.shape[0]
    hi = z.astype(BF16)
    lo = (z - hi.astype(F32)).astype(BF16)
    pieces = []
    for g in range(N_GROUPS):
        pieces += [hi[:, g * GROUP_W:(g + 1) * GROUP_W], lo[:, g * GROUP_W:(g + 1) * GROUP_W]]
    res = jnp.dot(jnp.concatenate(pieces, axis=0), ones_bd, preferred_element_type=F32)
    return jnp.concatenate(
        [res[(2 * g) * rows:(2 * g + 1) * rows] + res[(2 * g + 1) * rows:(2 * g + 2) * rows]
         for g in range(N_GROUPS)], axis=1)


def _ada_kernel(c_ref, w_ref, b_ref, o_ref):
    o_ref[...] = _dot(_silu(c_ref[...]), w_ref[...]) + b_ref[...]


def _ada_call(cc, ada_w, ada_b):
    rows = cc.shape[0]
    n_out = ada_w.shape[1]
    blk = 1536
    return pl.pallas_call(
        _ada_kernel,
        grid=(n_out // blk,),
        in_specs=[pl.BlockSpec((rows, D_MODEL), lambda n: (0, 0)),
                  pl.BlockSpec((D_MODEL, blk), lambda n: (0, n)),
                  pl.BlockSpec((1, blk), lambda n: (0, n))],
        out_specs=pl.BlockSpec((rows, blk), lambda n: (0, n)),
        out_shape=jax.ShapeDtypeStruct((rows, n_out), F32),
        compiler_params=pltpu.CompilerParams(vmem_limit_bytes=VMEM_LIMIT),
        name="ada_mod",
    )(cc, ada_w, ada_b)


def _proj_kernel(x_ref, mod_ref, g_ref, w_ref, mp_ref, mn_ref, o_ref, h_scr, z_scr, *, seq):
    n = pl.program_id(1)
    row_chunk = min(seq, 1024)

    @pl.when(n == 0)
    def _():
        shift = mod_ref[0, 0:1, :]
        scale = mod_ref[0, 1:2, :]
        g = g_ref[...]
        for s in range(0, seq, row_chunk):
            xs = x_ref[0, s:s + row_chunk, :]
            h_scr[s:s + row_chunk, :] = (_rms(xs, g) * (1.0 + scale) + shift).astype(BF16)

    @pl.when(n < N_SHIFT_BLOCKS)
    def _():
        z_scr[0:8, :] = jnp.zeros((8, PROJ_BLOCK), F32)
        z_scr[8 + seq:16 + seq, :] = jnp.zeros((8, PROJ_BLOCK), F32)
        mp = mp_ref[...]
        mn = mn_ref[...]

        def shift_rows(s):
            cur = z_scr[8 + s:8 + s + row_chunk, :]
            prv = z_scr[7 + s:7 + s + row_chunk, :]
            nxt = z_scr[9 + s:9 + s + row_chunk, :]
            o_ref[0, s:s + row_chunk, :] = (cur + mp * (prv - cur) + mn * (nxt - cur)).astype(BF16)

        for s in range(0, seq, row_chunk):
            z_scr[8 + s:8 + s + row_chunk, :] = jnp.dot(
                h_scr[s:s + row_chunk, :], w_ref[...], preferred_element_type=F32)
            if s > 0:
                shift_rows(s - row_chunk)
        shift_rows(seq - row_chunk)

    @pl.when(n >= N_SHIFT_BLOCKS)
    def _():
        for s in range(0, seq, row_chunk):
            o_ref[0, s:s + row_chunk, :] = jnp.dot(
                h_scr[s:s + row_chunk, :], w_ref[...], preferred_element_type=F32).astype(BF16)


def _proj_call(x, mods, g, w_pad, mp_pad, mn_pad, n_blocks):
    bsz, seq, _ = x.shape
    n_cols = n_blocks * PROJ_BLOCK
    return pl.pallas_call(
        functools.partial(_proj_kernel, seq=seq),
        grid=(bsz, n_blocks),
        in_specs=[pl.BlockSpec((1, seq, D_MODEL), lambda b, n: (b, 0, 0)),
                  pl.BlockSpec((1, N_MOD, D_MODEL), lambda b, n: (b, 0, 0)),
                  pl.BlockSpec((1, D_MODEL), lambda b, n: (0, 0)),
                  pl.BlockSpec((D_MODEL, PROJ_BLOCK), lambda b, n: (0, n)),
                  pl.BlockSpec((1, PROJ_BLOCK), lambda b, n: (0, n)),
                  pl.BlockSpec((1, PROJ_BLOCK), lambda b, n: (0, n))],
        out_specs=pl.BlockSpec((1, seq, PROJ_BLOCK), lambda b, n: (b, 0, n)),
        out_shape=jax.ShapeDtypeStruct((bsz, seq, n_cols), BF16),
        scratch_shapes=[pltpu.VMEM((seq, D_MODEL), BF16),
                        pltpu.VMEM((seq + 16, PROJ_BLOCK), F32)],
        compiler_params=pltpu.CompilerParams(
            dimension_semantics=("parallel", "arbitrary"), vmem_limit_bytes=VMEM_LIMIT),
        name="in_proj",
    )(x, mods, g, w_pad, mp_pad, mn_pad)


def _expand(x, bd_mask):
    return jnp.where(bd_mask, jnp.concatenate([x] * GROUP_HEADS, axis=0), 0.0)


def _scan_prep(blocks, lw_ref, vec_ref, ones_bd):
    L = CHUNK
    n = len(blocks)
    lane = lax.broadcasted_iota(jnp.int32, (L, LORA_COLS), 1)
    loras = []
    for pb, _ in blocks:
        lora = pb[:, LORA_OFF:LORA_OFF + LORA_COLS]
        loras.append(jnp.where(lane < 2 * DECAY_RANK, jnp.tanh(lora), lora))
    lin = [None] * n
    for d in range(2):
        idx = [i for i in range(n) if blocks[i][1] == bool(d)]
        res = _dot(jnp.concatenate([loras[i] for i in idx], axis=0), lw_ref[d])
        for j, i in enumerate(idx):
            lin[i] = res[j * L:(j + 1) * L]

    kk0s, pieces = [], []
    for i, (pb, reverse) in enumerate(blocks):
        kk0 = pb[:, RWKV_WIDTH:2 * RWKV_WIDTH] * vec_ref[int(reverse), 2:3, :]
        kk0s.append(kk0)
        sq = (kk0 * kk0).astype(BF16)
        pieces += [sq[:, g * GROUP_W:(g + 1) * GROUP_W] for g in range(N_GROUPS)]
    sums = jnp.dot(jnp.concatenate(pieces, axis=0), ones_bd, preferred_element_type=F32)

    row = lax.broadcasted_iota(jnp.int32, (L, L), 0)
    col = lax.broadcasted_iota(jnp.int32, (L, L), 1)
    preps = []
    for i, (pb, reverse) in enumerate(blocks):
        vecs = vec_ref[int(reverse)]
        w0 = vecs[0:1, :]
        a0 = vecs[1:2, :]
        k_a = vecs[3:4, :]
        r = pb[:, 0:RWKV_WIDTH]
        k = pb[:, RWKV_WIDTH:2 * RWKV_WIDTH]
        v = pb[:, 2 * RWKV_WIDTH:3 * RWKV_WIDTH]
        half = 0.5 * LOG2_E * DECAY_SCALE
        lw = -half * jnp.tanh(0.5 * (lin[i][:, 0:RWKV_WIDTH] + w0)) - half
        neg_iclr = -0.5 * jnp.tanh(0.5 * (lin[i][:, RWKV_WIDTH:] + a0)) - 0.5
        base = i * N_GROUPS * L
        ss = jnp.concatenate(
            [sums[base + g * L:base + (g + 1) * L] for g in range(N_GROUPS)], axis=1)
        kk = kk0s[i] * lax.rsqrt(jnp.maximum(ss, 1e-24))
        kd = k * (1.0 - (neg_iclr + 1.0) * k_a)
        na = kk
        nb = kk * neg_iclr
        tri = jnp.where((col >= row) if reverse else (col <= row), 1.0, 0.0).astype(BF16)
        cum = _dot_split_rhs(tri, lw)
        tot = cum[0:1, :] if reverse else cum[L - 1:L, :]
        e_incl = jnp.exp2(cum)
        e_neg = 1.0 / e_incl
        w_tot = jnp.exp2(tot)
        e_rem = w_tot * e_neg
        preps.append(dict(
            r_t=r * e_incl, a_t=na * jnp.exp2(cum - lw), b_t=nb * e_neg, k_t=kd * e_neg,
            b_h=nb * e_rem, k_h=kd * e_rem, v=v, w_tot=w_tot))
    return preps


def _scan_units(preps, reverse, s_idx, s_ref):
    L = CHUNK
    rr = lax.broadcasted_iota(jnp.int32, (GROUP_W, GROUP_W), 0)
    cc = lax.broadcasted_iota(jnp.int32, (GROUP_W, GROUP_W), 1)
    head_shift = HEAD_SIZE.bit_length() - 1
    bd_mask = jnp.right_shift(rr, head_shift) == jnp.right_shift(cc, head_shift)
    tr = lax.broadcasted_iota(jnp.int32, (L, GROUP_W), 0)
    tc = jnp.bitwise_and(lax.broadcasted_iota(jnp.int32, (L, GROUP_W), 1), L - 1)
    eye = jnp.where(tc == tr, 1.0, 0.0)
    strict = {True: tc > tr, False: tc < tr}
    incl = {True: tc >= tr, False: tc <= tr}
    units = [(d, g) for d in range(len(preps)) for g in range(N_GROUPS)]
    sl = lambda g: slice(g * GROUP_W, (g + 1) * GROUP_W)
    ex = lambda x: _expand(x, bd_mask)

    lhs, a_ab, a_ak, a_rb, a_rk = {}, {}, {}, {}, {}
    for u in units:
        d, g = u
        p = preps[d]
        lhs[u] = jnp.concatenate([p["a_t"][:, sl(g)], p["r_t"][:, sl(g)]], axis=0)
        e_bk = jnp.concatenate([ex(p["b_t"][:, sl(g)]), ex(p["k_t"][:, sl(g)])], axis=0)
        a_all = _dot_nt(lhs[u], e_bk)
        a_ab[u] = jnp.where(strict[reverse[d]], a_all[0:L, 0:GROUP_W], 0.0)
        a_ak[u] = jnp.where(strict[reverse[d]], a_all[0:L, GROUP_W:], 0.0)
        a_rb[u] = jnp.where(incl[reverse[d]], a_all[L:, 0:GROUP_W], 0.0)
        a_rk[u] = jnp.where(incl[reverse[d]], a_all[L:, GROUP_W:], 0.0)
    av = {}
    for u in units:
        d, g = u
        av[u] = _dot(jnp.concatenate([a_ak[u], a_rk[u]], axis=0), ex(preps[d]["v"][:, sl(g)]))

    n_levels = 6
    npow = {u: _dot(a_ab[u], ex(a_ab[u])) for u in units}
    pinv = {u: eye + a_ab[u] for u in units}
    for lvl in range(1, n_levels):
        for u in units:
            rhs = ex(npow[u])
            if lvl < n_levels - 1:
                both = _dot(jnp.concatenate([npow[u], pinv[u]], axis=0), rhs)
                npow[u] = both[0:L]
                pinv[u] = pinv[u] + both[L:]
            else:
                pinv[u] = pinv[u] + _dot(pinv[u], rhs)

    s_old, sa, x_u, u_u, y_u = {}, {}, {}, {}, {}
    for u in units:
        d, g = u
        s_old[u] = s_ref[s_idx[d] + (g,)]
        sa[u] = _dot(lhs[u], s_old[u]) + av[u]
    for u in units:
        x_u[u] = sa[u][0:L]
        u_u[u] = _dot(pinv[u], ex(x_u[u]))
    for u in units:
        y_u[u] = sa[u][L:] + _dot(a_rb[u], ex(u_u[u]))
    w_col = [jnp.transpose(jnp.broadcast_to(p["w_tot"], (LANES, RWKV_WIDTH))) for p in preps]
    for u in units:
        d, g = u
        p = preps[d]
        upd = _dot_tn(jnp.concatenate([p["b_h"][:, sl(g)], p["k_h"][:, sl(g)]], axis=0),
                      jnp.concatenate([u_u[u], p["v"][:, sl(g)]], axis=0))
        decay = jnp.concatenate([w_col[d][sl(g), :]] * (GROUP_W // LANES), axis=1)
        s_ref[s_idx[d] + (g,)] = s_old[u] * decay + jnp.where(bd_mask, upd, 0.0)
    return [jnp.concatenate([y_u[(d, g)] for g in range(N_GROUPS)], axis=1)
            for d in range(len(preps))]


def _scan_kernel(pf_ref, pb_ref, s0_ref, lw_ref, vec_ref, ho_ref, yf_ref, yb_ref, s_ref):
    c = pl.program_id(1)

    @pl.when(c == 0)
    def _():
        s_ref[...] = s0_ref[...]

    n_streams = 2 * SCAN_BATCH
    reverse = [bool(i % 2) for i in range(n_streams)]
    s_idx = [(i // 2, i % 2) for i in range(n_streams)]
    blocks = []
    for bb in range(SCAN_BATCH):
        blocks += [(pf_ref[bb].astype(F32), False), (pb_ref[bb].astype(F32), True)]
    preps = _scan_prep(blocks, lw_ref, vec_ref, ho_ref[...])
    ys = _scan_units(preps, reverse, s_idx, s_ref)
    for bb in range(SCAN_BATCH):
        yf_ref[bb] = ys[2 * bb]
        yb_ref[bb] = ys[2 * bb + 1]


def _scan_call(p, s0, lora_w, vecs, head_ones):
    bsz, seq, _ = p.shape
    nc = seq // CHUNK
    state_block = (SCAN_BATCH, 2, N_GROUPS, GROUP_W, GROUP_W)
    state_map = lambda b, c: (b, 0, 0, 0, 0)
    return pl.pallas_call(
        _scan_kernel,
        grid=(bsz // SCAN_BATCH, nc),
        in_specs=[pl.BlockSpec((SCAN_BATCH, CHUNK, SCAN_COLS), lambda b, c: (b, c, 0)),
                  pl.BlockSpec((SCAN_BATCH, CHUNK, SCAN_COLS), lambda b, c: (b, nc - 1 - c, 0)),
                  pl.BlockSpec(state_block, state_map, pipeline_mode=pl.Buffered(1)),
                  pl.BlockSpec((2, LORA_COLS, 2 * RWKV_WIDTH), lambda b, c: (0, 0, 0)),
                  pl.BlockSpec((2, 8, RWKV_WIDTH), lambda b, c: (0, 0, 0)),
                  pl.BlockSpec((GROUP_W, GROUP_W), lambda b, c: (0, 0))],
        out_specs=[pl.BlockSpec((SCAN_BATCH, CHUNK, RWKV_WIDTH), lambda b, c: (b, c, 0)),
                   pl.BlockSpec((SCAN_BATCH, CHUNK, RWKV_WIDTH), lambda b, c: (b, nc - 1 - c, 0)),
                   pl.BlockSpec(state_block, state_map)],
        out_shape=[jax.ShapeDtypeStruct((bsz, seq, RWKV_WIDTH), F32),
                   jax.ShapeDtypeStruct((bsz, seq, RWKV_WIDTH), F32),
                   jax.ShapeDtypeStruct((bsz, 2, N_GROUPS, GROUP_W, GROUP_W), F32)],
        compiler_params=pltpu.CompilerParams(
            dimension_semantics=("parallel", "arbitrary"), vmem_limit_bytes=VMEM_LIMIT),
        name="wkv_scan",
    )(p, p, s0, lora_w, vecs, head_ones)


def _mix_kernel(x_ref, yf_ref, yb_ref, rkv_ref, sm_ref, cv_ref, mod_ref, rw_ref, vec_ref, ho_ref,
                cw_ref, cvec_ref, wout_ref, g_ref, o_ref, upad, *, tm):
    n_lines = tm // GRID_W
    ones_bd = ho_ref[...]
    y = yf_ref[0] + yb_ref[0]
    inv_n = 1.0 / HEAD_SIZE
    mu = _head_sum(y, ones_bd) * inv_n
    dy = y - mu
    var = _head_sum(dy * dy, ones_bd) * inv_n
    lnx_w = vec_ref[0:1, :]
    lnx_b = vec_ref[1:2, :]
    a0_f = vec_ref[2:3, :]
    a0_b = vec_ref[3:4, :]
    k_a = vec_ref[4:5, :]
    r_k = vec_ref[5:6, :]
    yn = dy * lax.rsqrt(var + EPS_GN) * lnx_w + lnx_b
    sm = sm_ref[0].astype(F32)
    lane = lax.broadcasted_iota(jnp.int32, sm.shape, 1)
    sm = jnp.where(lane >= LORA_COLS, _sigmoid(sm), sm)
    lin = _dot(sm, rw_ref[...])
    iclr_f = _sigmoid(lin[:, 0:RWKV_WIDTH] + a0_f)
    iclr_b = _sigmoid(lin[:, RWKV_WIDTH:2 * RWKV_WIDTH] + a0_b)
    gate = lin[:, 2 * RWKV_WIDTH:]
    r = rkv_ref[0, :, 0:RWKV_WIDTH].astype(F32)
    k = rkv_ref[0, :, RWKV_WIDTH:2 * RWKV_WIDTH].astype(F32)
    v = rkv_ref[0, :, 2 * RWKV_WIDTH:3 * RWKV_WIDTH].astype(F32)
    k_bar = k * (1.0 + (0.5 * (iclr_f + iclr_b) - 1.0) * k_a)
    bonus = _head_sum(r * k_bar * r_k, ones_bd) * v
    rw_out = (yn + bonus) * gate

    cv = cv_ref[0].astype(F32)
    u = cv[:, 0:CONV_WIDTH] * _sigmoid(cv[:, CONV_WIDTH:])
    zeros = jnp.zeros((16, CONV_WIDTH), F32)
    for ln in range(n_lines):
        u_ln = u[ln * GRID_W:(ln + 1) * GRID_W, :]
        for s in range(8):
            upad[s, ln, 0:16, :] = zeros
            upad[s, ln, CONV_ROWS - 16:CONV_ROWS, :] = zeros
            upad[s, ln, 16 - s:16 - s + GRID_W, :] = u_ln
    conv_b = cvec_ref[0:1, :]
    ln_w = cvec_ref[1:2, :]
    ln_b = cvec_ref[2:3, :]
    conv_lines = []
    for ln in range(n_lines):
        strips = []
        for lb in range(CONV_WIDTH // LANES):
            lanes = slice(lb * LANES, (lb + 1) * LANES)
            acc = jnp.zeros((GRID_W, LANES), F32)
            for j in range(CONV_KERNEL):
                off = 16 - CONV_PAD + j
                q8 = (off // 8) * 8
                acc = acc + cw_ref[j:j + 1, lanes] * upad[off % 8, ln, q8:q8 + GRID_W, lanes]
            strips.append(acc)
        conv_lines.append(jnp.concatenate(strips, axis=1))
    yc = jnp.concatenate(conv_lines, axis=0) + conv_b
    mu_c = jnp.mean(yc, axis=-1, keepdims=True)
    dc = yc - mu_c
    var_c = jnp.mean(dc * dc, axis=-1, keepdims=True)
    conv_out = _silu(dc * lax.rsqrt(var_c + EPS_LN) * ln_w + ln_b)

    mix_in = jnp.concatenate([rw_out.astype(BF16), conv_out.astype(BF16)], axis=1)
    mix = jnp.dot(mix_in, wout_ref[...], preferred_element_type=F32)
    gate_mix = mod_ref[0, 2:3, :]
    o_ref[0] = x_ref[0] + gate_mix * _rms(mix, g_ref[...])


def _mix_call(x, y_f, y_b, p, mods, rw_w, rvecs, head_ones, conv_w, cvecs, w_out, post_g):
    bsz, seq, _ = x.shape
    tm = 512
    tok = lambda width, blk: pl.BlockSpec((1, tm, width), lambda b, t: (b, t, blk))
    const2 = lambda shape: pl.BlockSpec(shape, lambda b, t: (0, 0))
    return pl.pallas_call(
        functools.partial(_mix_kernel, tm=tm),
        grid=(bsz, seq // tm),
        in_specs=[tok(D_MODEL, 0), tok(RWKV_WIDTH, 0), tok(RWKV_WIDTH, 0),
                  tok(3 * RWKV_WIDTH, 0),
                  tok(512, LORA_OFF // 512),
                  tok(2 * CONV_WIDTH, CONV_OFF // (2 * CONV_WIDTH)),
                  pl.BlockSpec((1, N_MOD, D_MODEL), lambda b, t: (b, 0, 0)),
                  const2((512, 3 * RWKV_WIDTH)), const2((8, RWKV_WIDTH)),
                  const2((GROUP_W, GROUP_W)), const2((32, CONV_WIDTH)),
                  const2((8, CONV_WIDTH)), const2((D_MODEL, D_MODEL)), const2((1, D_MODEL))],
        out_specs=tok(D_MODEL, 0),
        out_shape=jax.ShapeDtypeStruct((bsz, seq, D_MODEL), F32),
        scratch_shapes=[pltpu.VMEM((8, tm // GRID_W, CONV_ROWS, CONV_WIDTH), F32)],
        compiler_params=pltpu.CompilerParams(
            dimension_semantics=("parallel", "parallel"), vmem_limit_bytes=VMEM_LIMIT),
        name="mix_out",
    )(x, y_f, y_b, p, p, p, mods, rw_w, rvecs, head_ones, conv_w, cvecs, w_out, post_g)


def _mlp_kernel(x_ref, mod_ref, gpre_ref, gpost_ref, w1_ref, w2_ref, o_ref, acc_ref):
    x = x_ref[0]
    shift = mod_ref[0, 3:4, :]
    scale = mod_ref[0, 4:5, :]
    gate = mod_ref[0, 5:6, :]
    h = (_rms(x, gpre_ref[...]) * (1.0 + scale) + shift).astype(BF16)
    ff_blk = 1024
    for c in range(D_FF // ff_blk):
        hid = jnp.dot(h, w1_ref[:, c * ff_blk:(c + 1) * ff_blk], preferred_element_type=F32)
        hid = jnp.square(jnp.maximum(hid, 0.0)).astype(BF16)
        part = jnp.dot(hid, w2_ref[c * ff_blk:(c + 1) * ff_blk, :], preferred_element_type=F32)
        if c == 0:
            acc_ref[...] = part
        else:
            acc_ref[...] += part
    o_ref[0] = x + gate * _rms(acc_ref[...], gpost_ref[...])


def _mlp_call(x, mods, pre_g, post_g, w1, w2):
    bsz, seq, _ = x.shape
    tm = 512
    return pl.pallas_call(
        _mlp_kernel,
        grid=(bsz, seq // tm),
        in_specs=[pl.BlockSpec((1, tm, D_MODEL), lambda b, t: (b, t, 0)),
                  pl.BlockSpec((1, N_MOD, D_MODEL), lambda b, t: (b, 0, 0)),
                  pl.BlockSpec((1, D_MODEL), lambda b, t: (0, 0)),
                  pl.BlockSpec((1, D_MODEL), lambda b, t: (0, 0)),
                  pl.BlockSpec((D_MODEL, D_FF), lambda b, t: (0, 0)),
                  pl.BlockSpec((D_FF, D_MODEL), lambda b, t: (0, 0))],
        out_specs=pl.BlockSpec((1, tm, D_MODEL), lambda b, t: (b, t, 0)),
        out_shape=jax.ShapeDtypeStruct((bsz, seq, D_MODEL), F32),
        scratch_shapes=[pltpu.VMEM((tm, D_MODEL), F32)],
        compiler_params=pltpu.CompilerParams(
            dimension_semantics=("parallel", "parallel"), vmem_limit_bytes=VMEM_LIMIT),
        name="sqrelu_mlp",
    )(x, mods, pre_g, post_g, w1, w2)


def _pad_cols(a, lo, hi):
    pad = jnp.zeros(a.shape[:-1] + (GATE_PAD - GATE_RANK,), a.dtype)
    return jnp.concatenate([a[..., :lo], pad, a[..., lo:hi]], axis=-1)


def _layer(x, ctx, mods_x, mods_c, prm):
    (mix_pre_g, mix_post_g, mlp_pre_g, mlp_post_g, w_in, mu_prev, mu_next, decay_w0, decay_w2,
     iclr_a0, iclr_a2, k_k, k_a, r_k, gate_w2, lnx_w, lnx_b, conv_w, conv_b, conv_ln_w,
     conv_ln_b, w_out, mlp_w1, mlp_w2) = prm
    bsz = x.shape[0]
    shift_cols = GATE_OFF + GATE_RANK
    in_cols = w_in.shape[1]

    w_pad = _pad_cols(w_in, shift_cols, in_cols).astype(BF16)
    zc = jnp.zeros((2 * CONV_WIDTH,), F32)
    mp_pad = _pad_cols(jnp.concatenate([mu_prev, zc]), shift_cols, in_cols)[None, :]
    mn_pad = _pad_cols(jnp.concatenate([mu_next, zc]), shift_cols, in_cols)[None, :]
    zr = jnp.zeros((DECAY_RANK, RWKV_WIDTH), F32)
    lora_dir = []
    for d in range(2):
        dec_rows = [zr, zr]
        dec_rows[d] = decay_w2[d]
        icl_rows = [zr, zr]
        icl_rows[d] = iclr_a2[d]
        left = jnp.concatenate(dec_rows + [zr, zr], axis=0)
        right = jnp.concatenate([zr, zr] + icl_rows, axis=0)
        lora_dir.append(jnp.concatenate([left, right], axis=1))
    lora_w = jnp.stack(lora_dir).astype(BF16)
    z4 = jnp.zeros((4, RWKV_WIDTH), F32)
    scan_vecs = jnp.stack([jnp.concatenate([decay_w0[d][None], iclr_a0[d][None], k_k[None],
                                            k_a[None], z4], axis=0) for d in range(2)])
    hid = jnp.arange(RWKV_WIDTH) // HEAD_SIZE
    head_ones = (hid[:, None] == hid[None, :]).astype(BF16)
    zl = jnp.zeros((2 * DECAY_RANK, 3 * RWKV_WIDTH), F32)
    za = jnp.zeros((ICLR_RANK, RWKV_WIDTH), F32)
    zg = jnp.zeros((GATE_PAD, RWKV_WIDTH), F32)
    gate_pad = jnp.concatenate([gate_w2, jnp.zeros((GATE_PAD - GATE_RANK, RWKV_WIDTH), F32)], axis=0)
    rw_w = jnp.concatenate([
        zl,
        jnp.concatenate([iclr_a2[0], za, za], axis=1),
        jnp.concatenate([za, iclr_a2[1], za], axis=1),
        jnp.concatenate([zg, zg, gate_pad], axis=1)], axis=0).astype(BF16)
    rvecs = jnp.concatenate([lnx_w[None], lnx_b[None], iclr_a0[0][None], iclr_a0[1][None],
                             k_a[None], r_k.reshape(1, RWKV_WIDTH), jnp.zeros((2, RWKV_WIDTH), F32)],
                            axis=0)
    conv_w_pad = jnp.concatenate([conv_w, jnp.zeros((1, CONV_WIDTH), F32)], axis=0)
    cvecs = jnp.concatenate([conv_b[None], conv_ln_w[None], conv_ln_b[None],
                             jnp.zeros((5, CONV_WIDTH), F32)], axis=0)

    p_x = _proj_call(x, mods_x, mix_pre_g[None], w_pad, mp_pad, mn_pad, P_COLS // PROJ_BLOCK)
    p_c = _proj_call(ctx, mods_c, mix_pre_g[None], w_pad, mp_pad, mn_pad, N_SHIFT_BLOCKS)
    s_zero = jnp.zeros((bsz, 2, N_GROUPS, GROUP_W, GROUP_W), F32)
    _, _, s_ctx = _scan_call(p_c, s_zero, lora_w, scan_vecs, head_ones)
    y_f, y_b, _ = _scan_call(p_x, s_ctx, lora_w, scan_vecs, head_ones)
    x = _mix_call(x, y_f, y_b, p_x, mods_x, rw_w, rvecs, head_ones, conv_w_pad, cvecs,
                  w_out.astype(BF16), mix_post_g[None])
    x = _mlp_call(x, mods_x, mlp_pre_g[None], mlp_post_g[None], mlp_w1.astype(BF16),
                  mlp_w2.astype(BF16))
    return x


def kernel(x, c, ctx, c_ctx, ada_w, ada_b, mix_pre_g, mix_post_g, mlp_pre_g, mlp_post_g, w_in, mu_prev, mu_next, decay_w0, decay_w2, iclr_a0, iclr_a2, k_k, k_a, r_k, gate_w2, lnx_w, lnx_b, conv_w, conv_b, conv_ln_w, conv_ln_b, w_out, mlp_w1, mlp_w2):
    depth = ada_w.shape[0]
    assert depth == 1, "context-stream update between layers is not implemented"
    bsz = x.shape[0]
    rows = 8 * ((bsz + 1 + 7) // 8)
    cc = jnp.concatenate([c, c_ctx[None, :], jnp.zeros((rows - bsz - 1, D_MODEL), F32)], axis=0)
    per_layer = (mix_pre_g, mix_post_g, mlp_pre_g, mlp_post_g, w_in, mu_prev, mu_next, decay_w0,
                 decay_w2, iclr_a0, iclr_a2, k_k, k_a, r_k, gate_w2, lnx_w, lnx_b, conv_w, conv_b,
                 conv_ln_w, conv_ln_b, w_out, mlp_w1, mlp_w2)
    for l in range(depth):
        mods = _ada_call(cc, ada_w[l], ada_b[l][None, :]).reshape(rows, N_MOD, D_MODEL)
        mods_x = mods[:bsz]
        mods_c = jnp.broadcast_to(mods[bsz:bsz + 1], (bsz, N_MOD, D_MODEL))
        x = _layer(x, ctx, mods_x, mods_c, tuple(a[l] for a in per_layer))
    return x
```

```python
import functools

import jax
import jax.numpy as jnp
from jax import lax
from jax.experimental import pallas as pl
from jax.experimental.pallas import tpu as pltpu

F32 = jnp.float32
BF16 = jnp.bfloat16

D_MODEL = 1024
RWKV_WIDTH = 512
CONV_WIDTH = 512
HEAD_SIZE = 64
RWKV_HEADS = 8
DECAY_RANK = 64
ICLR_RANK = 64
GATE_RANK = 160
CONV_KERNEL = 31
CONV_PAD = CONV_KERNEL // 2
GRID_W = 64
CONV_ROWS = GRID_W + 24
D_FF = 4 * D_MODEL
N_MOD = 6
EPS_RMS = 1e-6
EPS_LN = 1e-5
EPS_GN = 64e-5
DECAY_SCALE = 0.6065306597126334
LOG2_E = 1.4426950408889634

LORA_OFF = 3 * RWKV_WIDTH
LORA_COLS = 256
GATE_OFF = LORA_OFF + LORA_COLS
GATE_PAD = 256
CONV_OFF = GATE_OFF + GATE_PAD
P_COLS = CONV_OFF + 2 * CONV_WIDTH
SCAN_COLS = GATE_OFF
PROJ_BLOCK = 1024
N_SHIFT_BLOCKS = CONV_OFF // PROJ_BLOCK

CHUNK = 64
INV_BLOCK = 8
GROUP_HEADS = 4
GROUP_W = GROUP_HEADS * HEAD_SIZE
N_GROUPS = RWKV_HEADS // GROUP_HEADS
SCAN_BATCH = 4

LANES = 128
VMEM_LIMIT = 56 * 1024 * 1024


def _sigmoid(x):
    return 0.5 * jnp.tanh(0.5 * x) + 0.5


def _silu(x):
    return x * _sigmoid(x)


def _rms(x, g):
    return x * lax.rsqrt(jnp.mean(x * x, axis=-1, keepdims=True) + EPS_RMS) * g


def _dot(a, b):
    return jnp.dot(a.astype(BF16), b.astype(BF16), preferred_element_type=F32)


def _dot_nt(a, b):
    return lax.dot_general(a.astype(BF16), b.astype(BF16), (((1,), (1,)), ((), ())),
                           preferred_element_type=F32)


def _dot_tn(a, b):
    return lax.dot_general(a.astype(BF16), b.astype(BF16), (((0,), (0,)), ((), ())),
                           preferred_element_type=F32)


def _dot_split_rhs(a_bf16, x):
    hi = x.astype(BF16)
    lo = (x - hi.astype(F32)).astype(BF16)
    d = lambda p: jnp.dot(a_bf16, p, preferred_element_type=F32)
    return d(hi) + d(lo)


def _head_sum(z, ones_bd):
    rows = z.shape[0]
    hi = z.astype(BF16)
    lo = (z - hi.astype(F32)).astype(BF16)
    pieces = []
    for g in range(N_GROUPS):
        pieces += [hi[:, g * GROUP_W:(g + 1) * GROUP_W], lo[:, g * GROUP_W:(g + 1) * GROUP_W]]
    res = jnp.dot(jnp.concatenate(pieces, axis=0), ones_bd, preferred_element_type=F32)
    return jnp.concatenate(
        [res[(2 * g) * rows:(2 * g + 1) * rows] + res[(2 * g + 1) * rows:(2 * g + 2) * rows]
         for g in range(N_GROUPS)], axis=1)


def _ada_kernel(c_ref, w_ref, b_ref, o_ref):
    o_ref[...] = _dot(_silu(c_ref[...]), w_ref[...]) + b_ref[...]


def _ada_call(cc, ada_w, ada_b):
    rows = cc.shape[0]
    n_out = ada_w.shape[1]
    blk = 1536
    return pl.pallas_call(
        _ada_kernel,
        grid=(n_out // blk,),
        in_specs=[pl.BlockSpec((rows, D_MODEL), lambda n: (0, 0)),
                  pl.BlockSpec((D_MODEL, blk), lambda n: (0, n)),
                  pl.BlockSpec((1, blk), lambda n: (0, n))],
        out_specs=pl.BlockSpec((rows, blk), lambda n: (0, n)),
        out_shape=jax.ShapeDtypeStruct((rows, n_out), F32),
        compiler_params=pltpu.CompilerParams(vmem_limit_bytes=VMEM_LIMIT),
        name="ada_mod",
    )(cc, ada_w, ada_b)


def _proj_kernel(x_ref, mod_ref, g_ref, w_ref, mp_ref, mn_ref, o_ref, h_scr, z_scr, *, seq):
    n = pl.program_id(1)
    row_chunk = min(seq, 1024)

    @pl.when(n == 0)
    def _():
        shift = mod_ref[0, 0:1, :]
        scale = mod_ref[0, 1:2, :]
        g = g_ref[...]
        for s in range(0, seq, row_chunk):
            xs = x_ref[0, s:s + row_chunk, :]
            h_scr[s:s + row_chunk, :] = (_rms(xs, g) * (1.0 + scale) + shift).astype(BF16)

    @pl.when(n < N_SHIFT_BLOCKS)
    def _():
        z_scr[0:8, :] = jnp.zeros((8, PROJ_BLOCK), F32)
        z_scr[8 + seq:16 + seq, :] = jnp.zeros((8, PROJ_BLOCK), F32)
        mp = mp_ref[...]
        mn = mn_ref[...]

        def shift_rows(s):
            cur = z_scr[8 + s:8 + s + row_chunk, :]
            prv = z_scr[7 + s:7 + s + row_chunk, :]
            nxt = z_scr[9 + s:9 + s + row_chunk, :]
            o_ref[0, s:s + row_chunk, :] = (cur + mp * (prv - cur) + mn * (nxt - cur)).astype(BF16)

        for s in range(0, seq, row_chunk):
            z_scr[8 + s:8 + s + row_chunk, :] = jnp.dot(
                h_scr[s:s + row_chunk, :], w_ref[...], preferred_element_type=F32)
            if s > 0:
                shift_rows(s - row_chunk)
        shift_rows(seq - row_chunk)

    @pl.when(n >= N_SHIFT_BLOCKS)
    def _():
        for s in range(0, seq, row_chunk):
            o_ref[0, s:s + row_chunk, :] = jnp.dot(
                h_scr[s:s + row_chunk, :], w_ref[...], preferred_element_type=F32).astype(BF16)


def _proj_call(x, mods, g, w_pad, mp_pad, mn_pad, n_blocks):
    bsz, seq, _ = x.shape
    n_cols = n_blocks * PROJ_BLOCK
    return pl.pallas_call(
        functools.partial(_proj_kernel, seq=seq),
        grid=(bsz, n_blocks),
        in_specs=[pl.BlockSpec((1, seq, D_MODEL), lambda b, n: (b, 0, 0)),
                  pl.BlockSpec((1, N_MOD, D_MODEL), lambda b, n: (b, 0, 0)),
                  pl.BlockSpec((1, D_MODEL), lambda b, n: (0, 0)),
                  pl.BlockSpec((D_MODEL, PROJ_BLOCK), lambda b, n: (0, n)),
                  pl.BlockSpec((1, PROJ_BLOCK), lambda b, n: (0, n)),
                  pl.BlockSpec((1, PROJ_BLOCK), lambda b, n: (0, n))],
        out_specs=pl.BlockSpec((1, seq, PROJ_BLOCK), lambda b, n: (b, 0, n)),
        out_shape=jax.ShapeDtypeStruct((bsz, seq, n_cols), BF16),
        scratch_shapes=[pltpu.VMEM((seq, D_MODEL), BF16),
                        pltpu.VMEM((seq + 16, PROJ_BLOCK), F32)],
        compiler_params=pltpu.CompilerParams(
            dimension_semantics=("parallel", "arbitrary"), vmem_limit_bytes=VMEM_LIMIT),
        name="in_proj",
    )(x, mods, g, w_pad, mp_pad, mn_pad)


def _expand(x, bd_mask):
    return jnp.where(bd_mask, jnp.concatenate([x] * GROUP_HEADS, axis=0), 0.0)


def _scan_prep(blocks, lw_ref, vec_ref, ones_bd):
    L = CHUNK
    n = len(blocks)
    lane = lax.broadcasted_iota(jnp.int32, (L, LORA_COLS), 1)
    loras = []
    for pb, _ in blocks:
        lora = pb[:, LORA_OFF:LORA_OFF + LORA_COLS]
        loras.append(jnp.where(lane < 2 * DECAY_RANK, jnp.tanh(lora), lora))
    lin = [None] * n
    for d in range(2):
        idx = [i for i in range(n) if blocks[i][1] == bool(d)]
        res = _dot(jnp.concatenate([loras[i] for i in idx], axis=0), lw_ref[d])
        for j, i in enumerate(idx):
            lin[i] = res[j * L:(j + 1) * L]

    kk0s, pieces = [], []
    for i, (pb, reverse) in enumerate(blocks):
        kk0 = pb[:, RWKV_WIDTH:2 * RWKV_WIDTH] * vec_ref[int(reverse), 2:3, :]
        kk0s.append(kk0)
        sq = (kk0 * kk0).astype(BF16)
        pieces += [sq[:, g * GROUP_W:(g + 1) * GROUP_W] for g in range(N_GROUPS)]
    sums = jnp.dot(jnp.concatenate(pieces, axis=0), ones_bd, preferred_element_type=F32)

    row = lax.broadcasted_iota(jnp.int32, (L, L), 0)
    col = lax.broadcasted_iota(jnp.int32, (L, L), 1)
    preps = []
    for i, (pb, reverse) in enumerate(blocks):
        vecs = vec_ref[int(reverse)]
        w0 = vecs[0:1, :]
        a0 = vecs[1:2, :]
        k_a = vecs[3:4, :]
        r = pb[:, 0:RWKV_WIDTH]
        k = pb[:, RWKV_WIDTH:2 * RWKV_WIDTH]
        v = pb[:, 2 * RWKV_WIDTH:3 * RWKV_WIDTH]
        half = 0.5 * LOG2_E * DECAY_SCALE
        lw = -half * jnp.tanh(0.5 * (lin[i][:, 0:RWKV_WIDTH] + w0)) - half
        neg_iclr = -0.5 * jnp.tanh(0.5 * (lin[i][:, RWKV_WIDTH:] + a0)) - 0.5
        base = i * N_GROUPS * L
        ss = jnp.concatenate(
            [sums[base + g * L:base + (g + 1) * L] for g in range(N_GROUPS)], axis=1)
        kk = kk0s[i] * lax.rsqrt(jnp.maximum(ss, 1e-24))
        kd = k * (1.0 - (neg_iclr + 1.0) * k_a)
        na = kk
        nb = kk * neg_iclr
        tri = jnp.where((col >= row) if reverse else (col <= row), 1.0, 0.0).astype(BF16)
        cum = _dot_split_rhs(tri, lw)
        tot = cum[0:1, :] if reverse else cum[L - 1:L, :]
        e_incl = jnp.exp2(cum)
        e_neg = 1.0 / e_incl
        w_tot = jnp.exp2(tot)
        e_rem = w_tot * e_neg
        preps.append(dict(
            r_t=r * e_incl, a_t=na * jnp.exp2(cum - lw), b_t=nb * e_neg, k_t=kd * e_neg,
            b_h=nb * e_rem, k_h=kd * e_rem, v=v, w_tot=w_tot))
    return preps


def _scan_units(preps, reverse, s_idx, s_ref):
    L = CHUNK
    rr = lax.broadcasted_iota(jnp.int32, (GROUP_W, GROUP_W), 0)
    cc = lax.broadcasted_iota(jnp.int32, (GROUP_W, GROUP_W), 1)
    head_shift = HEAD_SIZE.bit_length() - 1
    bd_mask = jnp.right_shift(rr, head_shift) == jnp.right_shift(cc, head_shift)
    tr = lax.broadcasted_iota(jnp.int32, (L, GROUP_W), 0)
    tc = jnp.bitwise_and(lax.broadcasted_iota(jnp.int32, (L, GROUP_W), 1), L - 1)
    eye = jnp.where(tc == tr, 1.0, 0.0)
    strict = {True: tc > tr, False: tc < tr}
    incl = {True: tc >= tr, False: tc <= tr}
    units = [(d, g) for d in range(len(preps)) for g in range(N_GROUPS)]
    sl = lambda g: slice(g * GROUP_W, (g + 1) * GROUP_W)
    ex = lambda x: _expand(x, bd_mask)

    lhs, a_ab, a_ak, a_rb, a_rk = {}, {}, {}, {}, {}
    for u in units:
        d, g = u
        p = preps[d]
        lhs[u] = jnp.concatenate([p["a_t"][:, sl(g)], p["r_t"][:, sl(g)]], axis=0)
        e_bk = jnp.concatenate([ex(p["b_t"][:, sl(g)]), ex(p["k_t"][:, sl(g)])], axis=0)
        a_all = _dot_nt(lhs[u], e_bk)
        a_ab[u] = jnp.where(strict[reverse[d]], a_all[0:L, 0:GROUP_W], 0.0)
        a_ak[u] = jnp.where(strict[reverse[d]], a_all[0:L, GROUP_W:], 0.0)
        a_rb[u] = jnp.where(incl[reverse[d]], a_all[L:, 0:GROUP_W], 0.0)
        a_rk[u] = jnp.where(incl[reverse[d]], a_all[L:, GROUP_W:], 0.0)
    av = {}
    for u in units:
        d, g = u
        av[u] = _dot(jnp.concatenate([a_ak[u], a_rk[u]], axis=0), ex(preps[d]["v"][:, sl(g)]))

    def nilpotent_inverse(n, index):
        prod = {u: eye + n[u] for u in units}
        npow = {u: _dot(n[u], ex(n[u])) for u in units}
        size = 2
        while size < index:
            for u in units:
                rhs = ex(npow[u])
                if 2 * size < index:
                    both = _dot(jnp.concatenate([npow[u], prod[u]], axis=0), rhs)
                    npow[u] = both[0:L]
                    prod[u] = prod[u] + both[L:]
                else:
                    prod[u] = prod[u] + _dot(prod[u], rhs)
            size *= 2
        return prod

    sh = INV_BLOCK.bit_length() - 1
    in_diag = jnp.right_shift(tr, sh) == jnp.right_shift(tc, sh)
    d_inv = nilpotent_inverse({u: jnp.where(in_diag, a_ab[u], 0.0) for u in units}, INV_BLOCK)
    e_mat = {u: _dot(d_inv[u], ex(jnp.where(in_diag, 0.0, a_ab[u]))) for u in units}
    e_inv = nilpotent_inverse(e_mat, L // INV_BLOCK)
    pinv = {u: _dot(e_inv[u], ex(d_inv[u])) for u in units}

    s_old, sa, x_u, u_u, y_u = {}, {}, {}, {}, {}
    for u in units:
        d, g = u
        s_old[u] = s_ref[s_idx[d] + (g,)]
        sa[u] = _dot(lhs[u], s_old[u]) + av[u]
    for u in units:
        x_u[u] = sa[u][0:L]
        u_u[u] = _dot(pinv[u], ex(x_u[u]))
    for u in units:
        y_u[u] = sa[u][L:] + _dot(a_rb[u], ex(u_u[u]))
    w_col = [jnp.transpose(jnp.broadcast_to(p["w_tot"], (LANES, RWKV_WIDTH))) for p in preps]
    for u in units:
        d, g = u
        p = preps[d]
        upd = _dot_tn(jnp.concatenate([p["b_h"][:, sl(g)], p["k_h"][:, sl(g)]], axis=0),
                      jnp.concatenate([u_u[u], p["v"][:, sl(g)]], axis=0))
        decay = jnp.concatenate([w_col[d][sl(g), :]] * (GROUP_W // LANES), axis=1)
        s_ref[s_idx[d] + (g,)] = s_old[u] * decay + jnp.where(bd_mask, upd, 0.0)
    return [jnp.concatenate([y_u[(d, g)] for g in range(N_GROUPS)], axis=1)
            for d in range(len(preps))]


def _scan_kernel(pf_ref, pb_ref, s0_ref, lw_ref, vec_ref, ho_ref, yf_ref, yb_ref, sfin_ref, s_scr):
    c = pl.program_id(1)

    @pl.when(c == 0)
    def _():
        s_scr[...] = s0_ref[...]

    n_streams = 2 * SCAN_BATCH
    reverse = [bool(i % 2) for i in range(n_streams)]
    s_idx = [(i // 2, i % 2) for i in range(n_streams)]
    blocks = []
    for bb in range(SCAN_BATCH):
        blocks += [(pf_ref[bb].astype(F32), False), (pb_ref[bb].astype(F32), True)]
    preps = _scan_prep(blocks, lw_ref, vec_ref, ho_ref[...])
    ys = _scan_units(preps, reverse, s_idx, s_scr)
    for bb in range(SCAN_BATCH):
        yf_ref[bb] = ys[2 * bb]
        yb_ref[bb] = ys[2 * bb + 1]

    @pl.when(c == pl.num_programs(1) - 1)
    def _():
        sfin_ref[...] = s_scr[...]


def _scan_call(p, s0, lora_w, vecs, head_ones):
    bsz, seq, _ = p.shape
    nc = seq // CHUNK
    state_block = (SCAN_BATCH, 2, N_GROUPS, GROUP_W, GROUP_W)
    state_map = lambda b, c: (b, 0, 0, 0, 0)
    return pl.pallas_call(
        _scan_kernel,
        grid=(bsz // SCAN_BATCH, nc),
        in_specs=[pl.BlockSpec((SCAN_BATCH, CHUNK, SCAN_COLS), lambda b, c: (b, c, 0)),
                  pl.BlockSpec((SCAN_BATCH, CHUNK, SCAN_COLS), lambda b, c: (b, nc - 1 - c, 0)),
                  pl.BlockSpec(state_block, state_map),
                  pl.BlockSpec((2, LORA_COLS, 2 * RWKV_WIDTH), lambda b, c: (0, 0, 0)),
                  pl.BlockSpec((2, 8, RWKV_WIDTH), lambda b, c: (0, 0, 0)),
                  pl.BlockSpec((GROUP_W, GROUP_W), lambda b, c: (0, 0))],
        out_specs=[pl.BlockSpec((SCAN_BATCH, CHUNK, RWKV_WIDTH), lambda b, c: (b, c, 0)),
                   pl.BlockSpec((SCAN_BATCH, CHUNK, RWKV_WIDTH), lambda b, c: (b, nc - 1 - c, 0)),
                   pl.BlockSpec(state_block, state_map)],
        out_shape=[jax.ShapeDtypeStruct((bsz, seq, RWKV_WIDTH), F32),
                   jax.ShapeDtypeStruct((bsz, seq, RWKV_WIDTH), F32),
                   jax.ShapeDtypeStruct((bsz, 2, N_GROUPS, GROUP_W, GROUP_W), F32)],
        scratch_shapes=[pltpu.VMEM(state_block, F32)],
        compiler_params=pltpu.CompilerParams(
            dimension_semantics=("parallel", "arbitrary"), vmem_limit_bytes=VMEM_LIMIT),
        name="wkv_scan",
    )(p, p, s0, lora_w, vecs, head_ones)


def _mix_kernel(x_ref, yf_ref, yb_ref, rkv_ref, sm_ref, cv_ref, mod_ref, rw_ref, vec_ref, ho_ref,
                cw_ref, cvec_ref, wout_ref, g_ref, o_ref, upad, *, tm):
    n_lines = tm // GRID_W
    ones_bd = ho_ref[...]
    y = yf_ref[0] + yb_ref[0]
    inv_n = 1.0 / HEAD_SIZE
    mu = _head_sum(y, ones_bd) * inv_n
    dy = y - mu
    var = _head_sum(dy * dy, ones_bd) * inv_n
    lnx_w = vec_ref[0:1, :]
    lnx_b = vec_ref[1:2, :]
    a0_f = vec_ref[2:3, :]
    a0_b = vec_ref[3:4, :]
    k_a = vec_ref[4:5, :]
    r_k = vec_ref[5:6, :]
    yn = dy * lax.rsqrt(var + EPS_GN) * lnx_w + lnx_b
    sm = sm_ref[0].astype(F32)
    lane = lax.broadcasted_iota(jnp.int32, sm.shape, 1)
    sm = jnp.where(lane >= LORA_COLS, _sigmoid(sm), sm)
    lin = _dot(sm, rw_ref[...])
    iclr_f = _sigmoid(lin[:, 0:RWKV_WIDTH] + a0_f)
    iclr_b = _sigmoid(lin[:, RWKV_WIDTH:2 * RWKV_WIDTH] + a0_b)
    gate = lin[:, 2 * RWKV_WIDTH:]
    r = rkv_ref[0, :, 0:RWKV_WIDTH].astype(F32)
    k = rkv_ref[0, :, RWKV_WIDTH:2 * RWKV_WIDTH].astype(F32)
    v = rkv_ref[0, :, 2 * RWKV_WIDTH:3 * RWKV_WIDTH].astype(F32)
    k_bar = k * (1.0 + (0.5 * (iclr_f + iclr_b) - 1.0) * k_a)
    bonus = _head_sum(r * k_bar * r_k, ones_bd) * v
    rw_out = (yn + bonus) * gate

    cv = cv_ref[0].astype(F32)
    u = cv[:, 0:CONV_WIDTH] * _sigmoid(cv[:, CONV_WIDTH:])
    zeros = jnp.zeros((16, CONV_WIDTH), F32)
    for ln in range(n_lines):
        u_ln = u[ln * GRID_W:(ln + 1) * GRID_W, :]
        for s in range(8):
            upad[s, ln, 0:16, :] = zeros
            upad[s, ln, CONV_ROWS - 16:CONV_ROWS, :] = zeros
            upad[s, ln, 16 - s:16 - s + GRID_W, :] = u_ln
    conv_b = cvec_ref[0:1, :]
    ln_w = cvec_ref[1:2, :]
    ln_b = cvec_ref[2:3, :]
    conv_lines = []
    for ln in range(n_lines):
        strips = []
        for lb in range(CONV_WIDTH // LANES):
            lanes = slice(lb * LANES, (lb + 1) * LANES)
            acc = jnp.zeros((GRID_W, LANES), F32)
            for j in range(CONV_KERNEL):
                off = 16 - CONV_PAD + j
                q8 = (off // 8) * 8
                acc = acc + cw_ref[j:j + 1, lanes] * upad[off % 8, ln, q8:q8 + GRID_W, lanes]
            strips.append(acc)
        conv_lines.append(jnp.concatenate(strips, axis=1))
    yc = jnp.concatenate(conv_lines, axis=0) + conv_b
    mu_c = jnp.mean(yc, axis=-1, keepdims=True)
    dc = yc - mu_c
    var_c = jnp.mean(dc * dc, axis=-1, keepdims=True)
    conv_out = _silu(dc * lax.rsqrt(var_c + EPS_LN) * ln_w + ln_b)

    mix_in = jnp.concatenate([rw_out.astype(BF16), conv_out.astype(BF16)], axis=1)
    mix = jnp.dot(mix_in, wout_ref[...], preferred_element_type=F32)
    gate_mix = mod_ref[0, 2:3, :]
    o_ref[0] = x_ref[0] + gate_mix * _rms(mix, g_ref[...])


def _mix_call(x, y_f, y_b, p, mods, rw_w, rvecs, head_ones, conv_w, cvecs, w_out, post_g):
    bsz, seq, _ = x.shape
    tm = 512
    tok = lambda width, blk: pl.BlockSpec((1, tm, width), lambda b, t: (b, t, blk))
    const2 = lambda shape: pl.BlockSpec(shape, lambda b, t: (0, 0))
    return pl.pallas_call(
        functools.partial(_mix_kernel, tm=tm),
        grid=(bsz, seq // tm),
        in_specs=[tok(D_MODEL, 0), tok(RWKV_WIDTH, 0), tok(RWKV_WIDTH, 0),
                  tok(3 * RWKV_WIDTH, 0),
                  tok(512, LORA_OFF // 512),
                  tok(2 * CONV_WIDTH, CONV_OFF // (2 * CONV_WIDTH)),
                  pl.BlockSpec((1, N_MOD, D_MODEL), lambda b, t: (b, 0, 0)),
                  const2((512, 3 * RWKV_WIDTH)), const2((8, RWKV_WIDTH)),
                  const2((GROUP_W, GROUP_W)), const2((32, CONV_WIDTH)),
                  const2((8, CONV_WIDTH)), const2((D_MODEL, D_MODEL)), const2((1, D_MODEL))],
        out_specs=tok(D_MODEL, 0),
        out_shape=jax.ShapeDtypeStruct((bsz, seq, D_MODEL), F32),
        scratch_shapes=[pltpu.VMEM((8, tm // GRID_W, CONV_ROWS, CONV_WIDTH), F32)],
        compiler_params=pltpu.CompilerParams(
            dimension_semantics=("parallel", "parallel"), vmem_limit_bytes=VMEM_LIMIT),
        name="mix_out",
    )(x, y_f, y_b, p, p, p, mods, rw_w, rvecs, head_ones, conv_w, cvecs, w_out, post_g)


def _mlp_kernel(x_ref, mod_ref, gpre_ref, gpost_ref, w1_ref, w2_ref, o_ref, acc_ref):
    x = x_ref[0]
    shift = mod_ref[0, 3:4, :]
    scale = mod_ref[0, 4:5, :]
    gate = mod_ref[0, 5:6, :]
    h = (_rms(x, gpre_ref[...]) * (1.0 + scale) + shift).astype(BF16)
    ff_blk = 1024
    for c in range(D_FF // ff_blk):
        hid = jnp.dot(h, w1_ref[:, c * ff_blk:(c + 1) * ff_blk], preferred_element_type=F32)
        hid = jnp.square(jnp.maximum(hid, 0.0)).astype(BF16)
        part = jnp.dot(hid, w2_ref[c * ff_blk:(c + 1) * ff_blk, :], preferred_element_type=F32)
        if c == 0:
            acc_ref[...] = part
        else:
            acc_ref[...] += part
    o_ref[0] = x + gate * _rms(acc_ref[...], gpost_ref[...])


def _mlp_call(x, mods, pre_g, post_g, w1, w2):
    bsz, seq, _ = x.shape
    tm = 512
    return pl.pallas_call(
        _mlp_kernel,
        grid=(bsz, seq // tm),
        in_specs=[pl.BlockSpec((1, tm, D_MODEL), lambda b, t: (b, t, 0)),
                  pl.BlockSpec((1, N_MOD, D_MODEL), lambda b, t: (b, 0, 0)),
                  pl.BlockSpec((1, D_MODEL), lambda b, t: (0, 0)),
                  pl.BlockSpec((1, D_MODEL), lambda b, t: (0, 0)),
                  pl.BlockSpec((D_MODEL, D_FF), lambda b, t: (0, 0)),
                  pl.BlockSpec((D_FF, D_MODEL), lambda b, t: (0, 0))],
        out_specs=pl.BlockSpec((1, tm, D_MODEL), lambda b, t: (b, t, 0)),
        out_shape=jax.ShapeDtypeStruct((bsz, seq, D_MODEL), F32),
        scratch_shapes=[pltpu.VMEM((tm, D_MODEL), F32)],
        compiler_params=pltpu.CompilerParams(
            dimension_semantics=("parallel", "parallel"), vmem_limit_bytes=VMEM_LIMIT),
        name="sqrelu_mlp",
    )(x, mods, pre_g, post_g, w1, w2)


def _pad_cols(a, lo, hi):
    pad = jnp.zeros(a.shape[:-1] + (GATE_PAD - GATE_RANK,), a.dtype)
    return jnp.concatenate([a[..., :lo], pad, a[..., lo:hi]], axis=-1)


def _layer(x, ctx, mods_x, mods_c, prm):
    (mix_pre_g, mix_post_g, mlp_pre_g, mlp_post_g, w_in, mu_prev, mu_next, decay_w0, decay_w2,
     iclr_a0, iclr_a2, k_k, k_a, r_k, gate_w2, lnx_w, lnx_b, conv_w, conv_b, conv_ln_w,
     conv_ln_b, w_out, mlp_w1, mlp_w2) = prm
    bsz = x.shape[0]
    shift_cols = GATE_OFF + GATE_RANK
    in_cols = w_in.shape[1]

    w_pad = _pad_cols(w_in, shift_cols, in_cols).astype(BF16)
    zc = jnp.zeros((2 * CONV_WIDTH,), F32)
    mp_pad = _pad_cols(jnp.concatenate([mu_prev, zc]), shift_cols, in_cols)[None, :]
    mn_pad = _pad_cols(jnp.concatenate([mu_next, zc]), shift_cols, in_cols)[None, :]
    zr = jnp.zeros((DECAY_RANK, RWKV_WIDTH), F32)
    lora_dir = []
    for d in range(2):
        dec_rows = [zr, zr]
        dec_rows[d] = decay_w2[d]
        icl_rows = [zr, zr]
        icl_rows[d] = iclr_a2[d]
        left = jnp.concatenate(dec_rows + [zr, zr], axis=0)
        right = jnp.concatenate([zr, zr] + icl_rows, axis=0)
        lora_dir.append(jnp.concatenate([left, right], axis=1))
    lora_w = jnp.stack(lora_dir).astype(BF16)
    z4 = jnp.zeros((4, RWKV_WIDTH), F32)
    scan_vecs = jnp.stack([jnp.concatenate([decay_w0[d][None], iclr_a0[d][None], k_k[None],
                                            k_a[None], z4], axis=0) for d in range(2)])
    hid = jnp.arange(RWKV_WIDTH) // HEAD_SIZE
    head_ones = (hid[:, None] == hid[None, :]).astype(BF16)
    zl = jnp.zeros((2 * DECAY_RANK, 3 * RWKV_WIDTH), F32)
    za = jnp.zeros((ICLR_RANK, RWKV_WIDTH), F32)
    zg = jnp.zeros((GATE_PAD, RWKV_WIDTH), F32)
    gate_pad = jnp.concatenate([gate_w2, jnp.zeros((GATE_PAD - GATE_RANK, RWKV_WIDTH), F32)], axis=0)
    rw_w = jnp.concatenate([
        zl,
        jnp.concatenate([iclr_a2[0], za, za], axis=1),
        jnp.concatenate([za, iclr_a2[1], za], axis=1),
        jnp.concatenate([zg, zg, gate_pad], axis=1)], axis=0).astype(BF16)
    rvecs = jnp.concatenate([lnx_w[None], lnx_b[None], iclr_a0[0][None], iclr_a0[1][None],
                             k_a[None], r_k.reshape(1, RWKV_WIDTH), jnp.zeros((2, RWKV_WIDTH), F32)],
                            axis=0)
    conv_w_pad = jnp.concatenate([conv_w, jnp.zeros((1, CONV_WIDTH), F32)], axis=0)
    cvecs = jnp.concatenate([conv_b[None], conv_ln_w[None], conv_ln_b[None],
                             jnp.zeros((5, CONV_WIDTH), F32)], axis=0)

    p_x = _proj_call(x, mods_x, mix_pre_g[None], w_pad, mp_pad, mn_pad, P_COLS // PROJ_BLOCK)
    p_c = _proj_call(ctx, mods_c, mix_pre_g[None], w_pad, mp_pad, mn_pad, N_SHIFT_BLOCKS)
    s_zero = jnp.zeros((bsz, 2, N_GROUPS, GROUP_W, GROUP_W), F32)
    _, _, s_ctx = _scan_call(p_c, s_zero, lora_w, scan_vecs, head_ones)
    y_f, y_b, _ = _scan_call(p_x, s_ctx, lora_w, scan_vecs, head_ones)
    x = _mix_call(x, y_f, y_b, p_x, mods_x, rw_w, rvecs, head_ones, conv_w_pad, cvecs,
                  w_out.astype(BF16), mix_post_g[None])
    x = _mlp_call(x, mods_x, mlp_pre_g[None], mlp_post_g[None], mlp_w1.astype(BF16),
                  mlp_w2.astype(BF16))
    return x


def kernel(x, c, ctx, c_ctx, ada_w, ada_b, mix_pre_g, mix_post_g, mlp_pre_g, mlp_post_g, w_in, mu_prev, mu_next, decay_w0, decay_w2, iclr_a0, iclr_a2, k_k, k_a, r_k, gate_w2, lnx_w, lnx_b, conv_w, conv_b, conv_ln_w, conv_ln_b, w_out, mlp_w1, mlp_w2):
    depth = ada_w.shape[0]
    assert depth == 1, "context-stream update between layers is not implemented"
    bsz = x.shape[0]
    rows = 8 * ((bsz + 1 + 7) // 8)
    cc = jnp.concatenate([c, c_ctx[None, :], jnp.zeros((rows - bsz - 1, D_MODEL), F32)], axis=0)
    per_layer = (mix_pre_g, mix_post_g, mlp_pre_g, mlp_post_g, w_in, mu_prev, mu_next, decay_w0,
                 decay_w2, iclr_a0, iclr_a2, k_k, k_a, r_k, gate_w2, lnx_w, lnx_b, conv_w, conv_b,
                 conv_ln_w, conv_ln_b, w_out, mlp_w1, mlp_w2)
    for l in range(depth):
        mods = _ada_call(cc, ada_w[l], ada_b[l][None, :]).reshape(rows, N_MOD, D_MODEL)
        mods_x = mods[:bsz]
        mods_c = jnp.broadcast_to(mods[bsz:bsz + 1], (bsz, N_MOD, D_MODEL))
        x = _layer(x, ctx, mods_x, mods_c, tuple(a[l] for a in per_layer))
    return x
```

```python
import functools

import jax
import jax.numpy as jnp
from jax import lax
from jax.experimental import pallas as pl
from jax.experimental.pallas import tpu as pltpu

F32 = jnp.float32
BF16 = jnp.bfloat16

D_MODEL = 1024
RWKV_WIDTH = 512
CONV_WIDTH = 512
HEAD_SIZE = 64
RWKV_HEADS = 8
DECAY_RANK = 64
ICLR_RANK = 64
GATE_RANK = 160
CONV_KERNEL = 31
CONV_PAD = CONV_KERNEL // 2
GRID_W = 64
SUBLANES = 8
CONV_LEAD = 2 * SUBLANES
CONV_ROWS = (CONV_LEAD + CONV_PAD) // SUBLANES * SUBLANES + GRID_W
D_FF = 4 * D_MODEL
N_MOD = 6
EPS_RMS = 1e-6
EPS_LN = 1e-5
EPS_GN = 64e-5
DECAY_SCALE = 0.6065306597126334
LOG2_E = 1.4426950408889634

LORA_OFF = 3 * RWKV_WIDTH
LORA_COLS = 256
GATE_OFF = LORA_OFF + LORA_COLS
GATE_PAD = 256
CONV_OFF = GATE_OFF + GATE_PAD
P_COLS = CONV_OFF + 2 * CONV_WIDTH
SCAN_COLS = GATE_OFF
PROJ_BLOCK = 1024
N_SHIFT_BLOCKS = CONV_OFF // PROJ_BLOCK

CHUNK = 64
INV_BLOCK = 8
GROUP_HEADS = 4
GROUP_W = GROUP_HEADS * HEAD_SIZE
N_GROUPS = RWKV_HEADS // GROUP_HEADS
SCAN_BATCH = 4

LANES = 128
VMEM_LIMIT = 56 * 1024 * 1024


def _sigmoid(x):
    return 0.5 * jnp.tanh(0.5 * x) + 0.5


def _silu(x):
    return x * _sigmoid(x)


def _rms(x, g):
    return x * lax.rsqrt(jnp.mean(x * x, axis=-1, keepdims=True) + EPS_RMS) * g


def _dot(a, b):
    return jnp.dot(a.astype(BF16), b.astype(BF16), preferred_element_type=F32)


def _dot_nt(a, b):
    return lax.dot_general(a.astype(BF16), b.astype(BF16), (((1,), (1,)), ((), ())),
                           preferred_element_type=F32)


def _dot_tn(a, b):
    return lax.dot_general(a.astype(BF16), b.astype(BF16), (((0,), (0,)), ((), ())),
                           preferred_element_type=F32)


def _dot_split_rhs(a_bf16, x):
    hi = x.astype(BF16)
    lo = (x - hi.astype(F32)).astype(BF16)
    d = lambda p: jnp.dot(a_bf16, p, preferred_element_type=F32)
    return d(hi) + d(lo)


def _head_sum(z, ones_bd):
    rows = z.shape[0]
    hi = z.astype(BF16)
    lo = (z - hi.astype(F32)).astype(BF16)
    pieces = []
    for g in range(N_GROUPS):
        pieces += [hi[:, g * GROUP_W:(g + 1) * GROUP_W], lo[:, g * GROUP_W:(g + 1) * GROUP_W]]
    res = jnp.dot(jnp.concatenate(pieces, axis=0), ones_bd, preferred_element_type=F32)
    return jnp.concatenate(
        [res[(2 * g) * rows:(2 * g + 1) * rows] + res[(2 * g + 1) * rows:(2 * g + 2) * rows]
         for g in range(N_GROUPS)], axis=1)


def _ada_kernel(c_ref, w_ref, b_ref, o_ref):
    o_ref[...] = _dot(_silu(c_ref[...]), w_ref[...]) + b_ref[...]


def _ada_call(cc, ada_w, ada_b):
    rows = cc.shape[0]
    n_out = ada_w.shape[1]
    blk = 1536
    return pl.pallas_call(
        _ada_kernel,
        grid=(n_out // blk,),
        in_specs=[pl.BlockSpec((rows, D_MODEL), lambda n: (0, 0)),
                  pl.BlockSpec((D_MODEL, blk), lambda n: (0, n)),
                  pl.BlockSpec((1, blk), lambda n: (0, n))],
        out_specs=pl.BlockSpec((rows, blk), lambda n: (0, n)),
        out_shape=jax.ShapeDtypeStruct((rows, n_out), F32),
        compiler_params=pltpu.CompilerParams(vmem_limit_bytes=VMEM_LIMIT),
        name="ada_mod",
    )(cc, ada_w, ada_b)


def _proj_kernel(x_ref, mod_ref, g_ref, w_ref, mp_ref, mn_ref, o_ref, h_scr, z_scr, *, seq):
    n = pl.program_id(1)
    row_chunk = min(seq, 1024)

    @pl.when(n == 0)
    def _():
        shift = mod_ref[0, 0:1, :]
        scale = mod_ref[0, 1:2, :]
        g = g_ref[...]
        for s in range(0, seq, row_chunk):
            xs = x_ref[0, s:s + row_chunk, :]
            h_scr[s:s + row_chunk, :] = (_rms(xs, g) * (1.0 + scale) + shift).astype(BF16)

    @pl.when(n < N_SHIFT_BLOCKS)
    def _():
        lead = SUBLANES
        z_scr[0:lead, :] = jnp.zeros((lead, PROJ_BLOCK), F32)
        z_scr[lead + seq:2 * lead + seq, :] = jnp.zeros((lead, PROJ_BLOCK), F32)
        mp = mp_ref[...]
        mn = mn_ref[...]

        def shift_rows(s):
            cur = z_scr[lead + s:lead + s + row_chunk, :]
            prv = z_scr[lead - 1 + s:lead - 1 + s + row_chunk, :]
            nxt = z_scr[lead + 1 + s:lead + 1 + s + row_chunk, :]
            o_ref[0, s:s + row_chunk, :] = (cur + mp * (prv - cur) + mn * (nxt - cur)).astype(BF16)

        for s in range(0, seq, row_chunk):
            z_scr[lead + s:lead + s + row_chunk, :] = jnp.dot(
                h_scr[s:s + row_chunk, :], w_ref[...], preferred_element_type=F32)
            if s > 0:
                shift_rows(s - row_chunk)
        shift_rows(seq - row_chunk)

    @pl.when(n >= N_SHIFT_BLOCKS)
    def _():
        for s in range(0, seq, row_chunk):
            o_ref[0, s:s + row_chunk, :] = jnp.dot(
                h_scr[s:s + row_chunk, :], w_ref[...], preferred_element_type=F32).astype(BF16)


def _proj_call(x, mods, g, w_pad, mp_pad, mn_pad, n_blocks):
    bsz, seq, _ = x.shape
    n_cols = n_blocks * PROJ_BLOCK
    return pl.pallas_call(
        functools.partial(_proj_kernel, seq=seq),
        grid=(bsz, n_blocks),
        in_specs=[pl.BlockSpec((1, seq, D_MODEL), lambda b, n: (b, 0, 0)),
                  pl.BlockSpec((1, N_MOD, D_MODEL), lambda b, n: (b, 0, 0)),
                  pl.BlockSpec((1, D_MODEL), lambda b, n: (0, 0)),
                  pl.BlockSpec((D_MODEL, PROJ_BLOCK), lambda b, n: (0, n)),
                  pl.BlockSpec((1, PROJ_BLOCK), lambda b, n: (0, n)),
                  pl.BlockSpec((1, PROJ_BLOCK), lambda b, n: (0, n))],
        out_specs=pl.BlockSpec((1, seq, PROJ_BLOCK), lambda b, n: (b, 0, n)),
        out_shape=jax.ShapeDtypeStruct((bsz, seq, n_cols), BF16),
        scratch_shapes=[pltpu.VMEM((seq, D_MODEL), BF16),
                        pltpu.VMEM((seq + 2 * SUBLANES, PROJ_BLOCK), F32)],
        compiler_params=pltpu.CompilerParams(
            dimension_semantics=("parallel", "arbitrary"), vmem_limit_bytes=VMEM_LIMIT),
        name="in_proj",
    )(x, mods, g, w_pad, mp_pad, mn_pad)


def _expand(x, bd_mask):
    return jnp.where(bd_mask, jnp.concatenate([x] * GROUP_HEADS, axis=0), 0.0)


def _scan_prep(blocks, lw_ref, vec_ref, ones_bd):
    L = CHUNK
    n = len(blocks)
    lane = lax.broadcasted_iota(jnp.int32, (L, LORA_COLS), 1)
    loras = []
    for pb, _ in blocks:
        lora = pb[:, LORA_OFF:LORA_OFF + LORA_COLS]
        loras.append(jnp.where(lane < 2 * DECAY_RANK, jnp.tanh(lora), lora))
    lin = [None] * n
    for d in range(2):
        idx = [i for i in range(n) if blocks[i][1] == bool(d)]
        res = _dot(jnp.concatenate([loras[i] for i in idx], axis=0), lw_ref[d])
        for j, i in enumerate(idx):
            lin[i] = res[j * L:(j + 1) * L]

    kk0s, pieces = [], []
    for i, (pb, reverse) in enumerate(blocks):
        kk0 = pb[:, RWKV_WIDTH:2 * RWKV_WIDTH] * vec_ref[int(reverse), 2:3, :]
        kk0s.append(kk0)
        sq = (kk0 * kk0).astype(BF16)
        pieces += [sq[:, g * GROUP_W:(g + 1) * GROUP_W] for g in range(N_GROUPS)]
    sums = jnp.dot(jnp.concatenate(pieces, axis=0), ones_bd, preferred_element_type=F32)

    row = lax.broadcasted_iota(jnp.int32, (L, L), 0)
    col = lax.broadcasted_iota(jnp.int32, (L, L), 1)
    preps = []
    for i, (pb, reverse) in enumerate(blocks):
        vecs = vec_ref[int(reverse)]
        w0 = vecs[0:1, :]
        a0 = vecs[1:2, :]
        k_a = vecs[3:4, :]
        r = pb[:, 0:RWKV_WIDTH]
        k = pb[:, RWKV_WIDTH:2 * RWKV_WIDTH]
        v = pb[:, 2 * RWKV_WIDTH:3 * RWKV_WIDTH]
        half = 0.5 * LOG2_E * DECAY_SCALE
        lw = -half * jnp.tanh(0.5 * (lin[i][:, 0:RWKV_WIDTH] + w0)) - half
        neg_iclr = -0.5 * jnp.tanh(0.5 * (lin[i][:, RWKV_WIDTH:] + a0)) - 0.5
        base = i * N_GROUPS * L
        ss = jnp.concatenate(
            [sums[base + g * L:base + (g + 1) * L] for g in range(N_GROUPS)], axis=1)
        kk = kk0s[i] * lax.rsqrt(jnp.maximum(ss, 1e-24))
        kd = k * (1.0 - (neg_iclr + 1.0) * k_a)
        na = kk
        nb = kk * neg_iclr
        tri = jnp.where((col >= row) if reverse else (col <= row), 1.0, 0.0).astype(BF16)
        cum = _dot_split_rhs(tri, lw)
        tot = cum[0:1, :] if reverse else cum[L - 1:L, :]
        e_incl = jnp.exp2(cum)
        e_neg = 1.0 / e_incl
        w_tot = jnp.exp2(tot)
        e_rem = w_tot * e_neg
        preps.append(dict(
            r_t=r * e_incl, a_t=na * jnp.exp2(cum - lw), b_t=nb * e_neg, k_t=kd * e_neg,
            b_h=nb * e_rem, k_h=kd * e_rem, v=v, w_tot=w_tot))
    return preps


def _scan_units(preps, reverse, s_idx, s_ref):
    L = CHUNK
    rr = lax.broadcasted_iota(jnp.int32, (GROUP_W, GROUP_W), 0)
    cc = lax.broadcasted_iota(jnp.int32, (GROUP_W, GROUP_W), 1)
    head_shift = HEAD_SIZE.bit_length() - 1
    bd_mask = jnp.right_shift(rr, head_shift) == jnp.right_shift(cc, head_shift)
    tr = lax.broadcasted_iota(jnp.int32, (L, GROUP_W), 0)
    tc = jnp.bitwise_and(lax.broadcasted_iota(jnp.int32, (L, GROUP_W), 1), L - 1)
    eye = jnp.where(tc == tr, 1.0, 0.0)
    strict = {True: tc > tr, False: tc < tr}
    incl = {True: tc >= tr, False: tc <= tr}
    units = [(d, g) for d in range(len(preps)) for g in range(N_GROUPS)]
    sl = lambda g: slice(g * GROUP_W, (g + 1) * GROUP_W)
    ex = lambda x: _expand(x, bd_mask)

    lhs, a_ab, a_ak, a_rb, a_rk = {}, {}, {}, {}, {}
    for u in units:
        d, g = u
        p = preps[d]
        lhs[u] = jnp.concatenate([p["a_t"][:, sl(g)], p["r_t"][:, sl(g)]], axis=0)
        e_bk = jnp.concatenate([ex(p["b_t"][:, sl(g)]), ex(p["k_t"][:, sl(g)])], axis=0)
        a_all = _dot_nt(lhs[u], e_bk)
        a_ab[u] = jnp.where(strict[reverse[d]], a_all[0:L, 0:GROUP_W], 0.0)
        a_ak[u] = jnp.where(strict[reverse[d]], a_all[0:L, GROUP_W:], 0.0)
        a_rb[u] = jnp.where(incl[reverse[d]], a_all[L:, 0:GROUP_W], 0.0)
        a_rk[u] = jnp.where(incl[reverse[d]], a_all[L:, GROUP_W:], 0.0)
    av = {}
    for u in units:
        d, g = u
        av[u] = _dot(jnp.concatenate([a_ak[u], a_rk[u]], axis=0), ex(preps[d]["v"][:, sl(g)]))

    def nilpotent_inverse(n, index):
        prod = {u: eye + n[u] for u in units}
        npow = {u: _dot(n[u], ex(n[u])) for u in units}
        size = 2
        while size < index:
            for u in units:
                rhs = ex(npow[u])
                if 2 * size < index:
                    both = _dot(jnp.concatenate([npow[u], prod[u]], axis=0), rhs)
                    npow[u] = both[0:L]
                    prod[u] = prod[u] + both[L:]
                else:
                    prod[u] = prod[u] + _dot(prod[u], rhs)
            size *= 2
        return prod

    sh = INV_BLOCK.bit_length() - 1
    in_diag = jnp.right_shift(tr, sh) == jnp.right_shift(tc, sh)
    d_inv = nilpotent_inverse({u: jnp.where(in_diag, a_ab[u], 0.0) for u in units}, INV_BLOCK)
    e_mat = {u: _dot(d_inv[u], ex(jnp.where(in_diag, 0.0, a_ab[u]))) for u in units}
    e_inv = nilpotent_inverse(e_mat, L // INV_BLOCK)
    pinv = {u: _dot(e_inv[u], ex(d_inv[u])) for u in units}

    s_old, sa, x_u, u_u, y_u = {}, {}, {}, {}, {}
    for u in units:
        d, g = u
        s_old[u] = s_ref[s_idx[d] + (g,)]
        sa[u] = _dot(lhs[u], s_old[u]) + av[u]
    for u in units:
        x_u[u] = sa[u][0:L]
        u_u[u] = _dot(pinv[u], ex(x_u[u]))
    for u in units:
        y_u[u] = sa[u][L:] + _dot(a_rb[u], ex(u_u[u]))
    w_col = [jnp.transpose(jnp.broadcast_to(p["w_tot"], (LANES, RWKV_WIDTH))) for p in preps]
    for u in units:
        d, g = u
        p = preps[d]
        upd = _dot_tn(jnp.concatenate([p["b_h"][:, sl(g)], p["k_h"][:, sl(g)]], axis=0),
                      jnp.concatenate([u_u[u], p["v"][:, sl(g)]], axis=0))
        decay = jnp.concatenate([w_col[d][sl(g), :]] * (GROUP_W // LANES), axis=1)
        s_ref[s_idx[d] + (g,)] = s_old[u] * decay + jnp.where(bd_mask, upd, 0.0)
    return [jnp.concatenate([y_u[(d, g)] for g in range(N_GROUPS)], axis=1)
            for d in range(len(preps))]


def _scan_kernel(pf_ref, pb_ref, s0_ref, lw_ref, vec_ref, ho_ref, yf_ref, yb_ref, sfin_ref, s_scr):
    c = pl.program_id(1)

    @pl.when(c == 0)
    def _():
        s_scr[...] = s0_ref[...]

    n_streams = 2 * SCAN_BATCH
    reverse = [bool(i % 2) for i in range(n_streams)]
    s_idx = [(i // 2, i % 2) for i in range(n_streams)]
    blocks = []
    for bb in range(SCAN_BATCH):
        blocks += [(pf_ref[bb].astype(F32), False), (pb_ref[bb].astype(F32), True)]
    preps = _scan_prep(blocks, lw_ref, vec_ref, ho_ref[...])
    ys = _scan_units(preps, reverse, s_idx, s_scr)
    for bb in range(SCAN_BATCH):
        yf_ref[bb] = ys[2 * bb]
        yb_ref[bb] = ys[2 * bb + 1]

    @pl.when(c == pl.num_programs(1) - 1)
    def _():
        sfin_ref[...] = s_scr[...]


def _scan_call(p, s0, lora_w, vecs, head_ones):
    bsz, seq, _ = p.shape
    nc = seq // CHUNK
    state_block = (SCAN_BATCH, 2, N_GROUPS, GROUP_W, GROUP_W)
    state_map = lambda b, c: (b, 0, 0, 0, 0)
    return pl.pallas_call(
        _scan_kernel,
        grid=(bsz // SCAN_BATCH, nc),
        in_specs=[pl.BlockSpec((SCAN_BATCH, CHUNK, SCAN_COLS), lambda b, c: (b, c, 0)),
                  pl.BlockSpec((SCAN_BATCH, CHUNK, SCAN_COLS), lambda b, c: (b, nc - 1 - c, 0)),
                  pl.BlockSpec(state_block, state_map),
                  pl.BlockSpec((2, LORA_COLS, 2 * RWKV_WIDTH), lambda b, c: (0, 0, 0)),
                  pl.BlockSpec((2, 8, RWKV_WIDTH), lambda b, c: (0, 0, 0)),
                  pl.BlockSpec((GROUP_W, GROUP_W), lambda b, c: (0, 0))],
        out_specs=[pl.BlockSpec((SCAN_BATCH, CHUNK, RWKV_WIDTH), lambda b, c: (b, c, 0)),
                   pl.BlockSpec((SCAN_BATCH, CHUNK, RWKV_WIDTH), lambda b, c: (b, nc - 1 - c, 0)),
                   pl.BlockSpec(state_block, state_map)],
        out_shape=[jax.ShapeDtypeStruct((bsz, seq, RWKV_WIDTH), F32),
                   jax.ShapeDtypeStruct((bsz, seq, RWKV_WIDTH), F32),
                   jax.ShapeDtypeStruct((bsz, 2, N_GROUPS, GROUP_W, GROUP_W), F32)],
        scratch_shapes=[pltpu.VMEM(state_block, F32)],
        compiler_params=pltpu.CompilerParams(
            dimension_semantics=("parallel", "arbitrary"), vmem_limit_bytes=VMEM_LIMIT),
        name="wkv_scan",
    )(p, p, s0, lora_w, vecs, head_ones)


def _mix_kernel(x_ref, yf_ref, yb_ref, rkv_ref, sm_ref, cv_ref, mod_ref, rw_ref, vec_ref, ho_ref,
                cw_ref, cvec_ref, wout_ref, g_ref, o_ref, upad, *, tm):
    n_lines = tm // GRID_W
    ones_bd = ho_ref[...]
    y = yf_ref[0] + yb_ref[0]
    inv_n = 1.0 / HEAD_SIZE
    mu = _head_sum(y, ones_bd) * inv_n
    dy = y - mu
    var = _head_sum(dy * dy, ones_bd) * inv_n
    lnx_w = vec_ref[0:1, :]
    lnx_b = vec_ref[1:2, :]
    a0_f = vec_ref[2:3, :]
    a0_b = vec_ref[3:4, :]
    k_a = vec_ref[4:5, :]
    r_k = vec_ref[5:6, :]
    yn = dy * lax.rsqrt(var + EPS_GN) * lnx_w + lnx_b
    sm = sm_ref[0].astype(F32)
    lane = lax.broadcasted_iota(jnp.int32, sm.shape, 1)
    sm = jnp.where(lane >= LORA_COLS, _sigmoid(sm), sm)
    lin = _dot(sm, rw_ref[...])
    iclr_f = _sigmoid(lin[:, 0:RWKV_WIDTH] + a0_f)
    iclr_b = _sigmoid(lin[:, RWKV_WIDTH:2 * RWKV_WIDTH] + a0_b)
    gate = lin[:, 2 * RWKV_WIDTH:]
    r = rkv_ref[0, :, 0:RWKV_WIDTH].astype(F32)
    k = rkv_ref[0, :, RWKV_WIDTH:2 * RWKV_WIDTH].astype(F32)
    v = rkv_ref[0, :, 2 * RWKV_WIDTH:3 * RWKV_WIDTH].astype(F32)
    k_bar = k * (1.0 + (0.5 * (iclr_f + iclr_b) - 1.0) * k_a)
    bonus = _head_sum(r * k_bar * r_k, ones_bd) * v
    rw_out = (yn + bonus) * gate

    cv = cv_ref[0].astype(F32)
    u = cv[:, 0:CONV_WIDTH] * _sigmoid(cv[:, CONV_WIDTH:])
    zeros = jnp.zeros((CONV_LEAD, CONV_WIDTH), F32)
    for ln in range(n_lines):
        u_ln = u[ln * GRID_W:(ln + 1) * GRID_W, :]
        for s in range(SUBLANES):
            upad[s, ln, 0:CONV_LEAD, :] = zeros
            upad[s, ln, CONV_ROWS - CONV_LEAD:CONV_ROWS, :] = zeros
            upad[s, ln, CONV_LEAD - s:CONV_LEAD - s + GRID_W, :] = u_ln
    conv_b = cvec_ref[0:1, :]
    ln_w = cvec_ref[1:2, :]
    ln_b = cvec_ref[2:3, :]
    conv_lines = []
    for ln in range(n_lines):
        strips = []
        for lb in range(CONV_WIDTH // LANES):
            lanes = slice(lb * LANES, (lb + 1) * LANES)
            acc = jnp.zeros((GRID_W, LANES), F32)
            for j in range(CONV_KERNEL):
                off = CONV_LEAD - CONV_PAD + j
                q8 = off // SUBLANES * SUBLANES
                acc = acc + cw_ref[j:j + 1, lanes] * upad[off % SUBLANES, ln, q8:q8 + GRID_W, lanes]
            strips.append(acc)
        conv_lines.append(jnp.concatenate(strips, axis=1))
    yc = jnp.concatenate(conv_lines, axis=0) + conv_b
    mu_c = jnp.mean(yc, axis=-1, keepdims=True)
    dc = yc - mu_c
    var_c = jnp.mean(dc * dc, axis=-1, keepdims=True)
    conv_out = _silu(dc * lax.rsqrt(var_c + EPS_LN) * ln_w + ln_b)

    mix_in = jnp.concatenate([rw_out.astype(BF16), conv_out.astype(BF16)], axis=1)
    mix = jnp.dot(mix_in, wout_ref[...], preferred_element_type=F32)
    gate_mix = mod_ref[0, 2:3, :]
    o_ref[0] = x_ref[0] + gate_mix * _rms(mix, g_ref[...])


def _mix_call(x, y_f, y_b, p, mods, rw_w, rvecs, head_ones, conv_w, cvecs, w_out, post_g):
    bsz, seq, _ = x.shape
    tm = 512
    tok = lambda width, blk: pl.BlockSpec((1, tm, width), lambda b, t: (b, t, blk))
    const2 = lambda shape: pl.BlockSpec(shape, lambda b, t: (0, 0))
    return pl.pallas_call(
        functools.partial(_mix_kernel, tm=tm),
        grid=(bsz, seq // tm),
        in_specs=[tok(D_MODEL, 0), tok(RWKV_WIDTH, 0), tok(RWKV_WIDTH, 0),
                  tok(3 * RWKV_WIDTH, 0),
                  tok(512, LORA_OFF // 512),
                  tok(2 * CONV_WIDTH, CONV_OFF // (2 * CONV_WIDTH)),
                  pl.BlockSpec((1, N_MOD, D_MODEL), lambda b, t: (b, 0, 0)),
                  const2((512, 3 * RWKV_WIDTH)), const2((8, RWKV_WIDTH)),
                  const2((GROUP_W, GROUP_W)), const2((32, CONV_WIDTH)),
                  const2((8, CONV_WIDTH)), const2((D_MODEL, D_MODEL)), const2((1, D_MODEL))],
        out_specs=tok(D_MODEL, 0),
        out_shape=jax.ShapeDtypeStruct((bsz, seq, D_MODEL), F32),
        scratch_shapes=[pltpu.VMEM((SUBLANES, tm // GRID_W, CONV_ROWS, CONV_WIDTH), F32)],
        compiler_params=pltpu.CompilerParams(
            dimension_semantics=("parallel", "parallel"), vmem_limit_bytes=VMEM_LIMIT),
        name="mix_out",
    )(x, y_f, y_b, p, p, p, mods, rw_w, rvecs, head_ones, conv_w, cvecs, w_out, post_g)


def _mlp_kernel(x_ref, mod_ref, gpre_ref, gpost_ref, w1_ref, w2_ref, o_ref, acc_ref):
    x = x_ref[0]
    shift = mod_ref[0, 3:4, :]
    scale = mod_ref[0, 4:5, :]
    gate = mod_ref[0, 5:6, :]
    h = (_rms(x, gpre_ref[...]) * (1.0 + scale) + shift).astype(BF16)
    ff_blk = 1024
    for c in range(D_FF // ff_blk):
        hid = jnp.dot(h, w1_ref[:, c * ff_blk:(c + 1) * ff_blk], preferred_element_type=F32)
        hid = jnp.square(jnp.maximum(hid, 0.0)).astype(BF16)
        part = jnp.dot(hid, w2_ref[c * ff_blk:(c + 1) * ff_blk, :], preferred_element_type=F32)
        if c == 0:
            acc_ref[...] = part
        else:
            acc_ref[...] += part
    o_ref[0] = x + gate * _rms(acc_ref[...], gpost_ref[...])


def _mlp_call(x, mods, pre_g, post_g, w1, w2):
    bsz, seq, _ = x.shape
    tm = 512
    return pl.pallas_call(
        _mlp_kernel,
        grid=(bsz, seq // tm),
        in_specs=[pl.BlockSpec((1, tm, D_MODEL), lambda b, t: (b, t, 0)),
                  pl.BlockSpec((1, N_MOD, D_MODEL), lambda b, t: (b, 0, 0)),
                  pl.BlockSpec((1, D_MODEL), lambda b, t: (0, 0)),
                  pl.BlockSpec((1, D_MODEL), lambda b, t: (0, 0)),
                  pl.BlockSpec((D_MODEL, D_FF), lambda b, t: (0, 0)),
                  pl.BlockSpec((D_FF, D_MODEL), lambda b, t: (0, 0))],
        out_specs=pl.BlockSpec((1, tm, D_MODEL), lambda b, t: (b, t, 0)),
        out_shape=jax.ShapeDtypeStruct((bsz, seq, D_MODEL), F32),
        scratch_shapes=[pltpu.VMEM((tm, D_MODEL), F32)],
        compiler_params=pltpu.CompilerParams(
            dimension_semantics=("parallel", "parallel"), vmem_limit_bytes=VMEM_LIMIT),
        name="sqrelu_mlp",
    )(x, mods, pre_g, post_g, w1, w2)


def _pad_cols(a, lo, hi):
    pad = jnp.zeros(a.shape[:-1] + (GATE_PAD - GATE_RANK,), a.dtype)
    return jnp.concatenate([a[..., :lo], pad, a[..., lo:hi]], axis=-1)


def _layer(x, ctx, mods_x, mods_c, prm):
    (mix_pre_g, mix_post_g, mlp_pre_g, mlp_post_g, w_in, mu_prev, mu_next, decay_w0, decay_w2,
     iclr_a0, iclr_a2, k_k, k_a, r_k, gate_w2, lnx_w, lnx_b, conv_w, conv_b, conv_ln_w,
     conv_ln_b, w_out, mlp_w1, mlp_w2) = prm
    bsz = x.shape[0]
    shift_cols = GATE_OFF + GATE_RANK
    in_cols = w_in.shape[1]

    w_pad = _pad_cols(w_in, shift_cols, in_cols).astype(BF16)
    zc = jnp.zeros((2 * CONV_WIDTH,), F32)
    mp_pad = _pad_cols(jnp.concatenate([mu_prev, zc]), shift_cols, in_cols)[None, :]
    mn_pad = _pad_cols(jnp.concatenate([mu_next, zc]), shift_cols, in_cols)[None, :]
    zr = jnp.zeros((DECAY_RANK, RWKV_WIDTH), F32)
    lora_dir = []
    for d in range(2):
        dec_rows = [zr, zr]
        dec_rows[d] = decay_w2[d]
        icl_rows = [zr, zr]
        icl_rows[d] = iclr_a2[d]
        left = jnp.concatenate(dec_rows + [zr, zr], axis=0)
        right = jnp.concatenate([zr, zr] + icl_rows, axis=0)
        lora_dir.append(jnp.concatenate([left, right], axis=1))
    lora_w = jnp.stack(lora_dir).astype(BF16)
    z4 = jnp.zeros((4, RWKV_WIDTH), F32)
    scan_vecs = jnp.stack([jnp.concatenate([decay_w0[d][None], iclr_a0[d][None], k_k[None],
                                            k_a[None], z4], axis=0) for d in range(2)])
    hid = jnp.arange(RWKV_WIDTH) // HEAD_SIZE
    head_ones = (hid[:, None] == hid[None, :]).astype(BF16)
    zl = jnp.zeros((2 * DECAY_RANK, 3 * RWKV_WIDTH), F32)
    za = jnp.zeros((ICLR_RANK, RWKV_WIDTH), F32)
    zg = jnp.zeros((GATE_PAD, RWKV_WIDTH), F32)
    gate_pad = jnp.concatenate([gate_w2, jnp.zeros((GATE_PAD - GATE_RANK, RWKV_WIDTH), F32)], axis=0)
    rw_w = jnp.concatenate([
        zl,
        jnp.concatenate([iclr_a2[0], za, za], axis=1),
        jnp.concatenate([za, iclr_a2[1], za], axis=1),
        jnp.concatenate([zg, zg, gate_pad], axis=1)], axis=0).astype(BF16)
    rvecs = jnp.concatenate([lnx_w[None], lnx_b[None], iclr_a0[0][None], iclr_a0[1][None],
                             k_a[None], r_k.reshape(1, RWKV_WIDTH), jnp.zeros((2, RWKV_WIDTH), F32)],
                            axis=0)
    conv_w_pad = jnp.concatenate([conv_w, jnp.zeros((1, CONV_WIDTH), F32)], axis=0)
    cvecs = jnp.concatenate([conv_b[None], conv_ln_w[None], conv_ln_b[None],
                             jnp.zeros((5, CONV_WIDTH), F32)], axis=0)

    p_x = _proj_call(x, mods_x, mix_pre_g[None], w_pad, mp_pad, mn_pad, P_COLS // PROJ_BLOCK)
    p_c = _proj_call(ctx, mods_c, mix_pre_g[None], w_pad, mp_pad, mn_pad, N_SHIFT_BLOCKS)
    s_zero = jnp.zeros((bsz, 2, N_GROUPS, GROUP_W, GROUP_W), F32)
    _, _, s_ctx = _scan_call(p_c, s_zero, lora_w, scan_vecs, head_ones)
    y_f, y_b, _ = _scan_call(p_x, s_ctx, lora_w, scan_vecs, head_ones)
    x = _mix_call(x, y_f, y_b, p_x, mods_x, rw_w, rvecs, head_ones, conv_w_pad, cvecs,
                  w_out.astype(BF16), mix_post_g[None])
    x = _mlp_call(x, mods_x, mlp_pre_g[None], mlp_post_g[None], mlp_w1.astype(BF16),
                  mlp_w2.astype(BF16))
    return x


def kernel(x, c, ctx, c_ctx, ada_w, ada_b, mix_pre_g, mix_post_g, mlp_pre_g, mlp_post_g, w_in, mu_prev, mu_next, decay_w0, decay_w2, iclr_a0, iclr_a2, k_k, k_a, r_k, gate_w2, lnx_w, lnx_b, conv_w, conv_b, conv_ln_w, conv_ln_b, w_out, mlp_w1, mlp_w2):
    depth = ada_w.shape[0]
    assert depth == 1, "context-stream update between layers is not implemented"
    bsz = x.shape[0]
    rows = 8 * ((bsz + 1 + 7) // 8)
    cc = jnp.concatenate([c, c_ctx[None, :], jnp.zeros((rows - bsz - 1, D_MODEL), F32)], axis=0)
    per_layer = (mix_pre_g, mix_post_g, mlp_pre_g, mlp_post_g, w_in, mu_prev, mu_next, decay_w0,
                 decay_w2, iclr_a0, iclr_a2, k_k, k_a, r_k, gate_w2, lnx_w, lnx_b, conv_w, conv_b,
                 conv_ln_w, conv_ln_b, w_out, mlp_w1, mlp_w2)
    for l in range(depth):
        mods = _ada_call(cc, ada_w[l], ada_b[l][None, :]).reshape(rows, N_MOD, D_MODEL)
        mods_x = mods[:bsz]
        mods_c = jnp.broadcast_to(mods[bsz:bsz + 1], (bsz, N_MOD, D_MODEL))
        x = _layer(x, ctx, mods_x, mods_c, tuple(a[l] for a in per_layer))
    return x
```

```python
import functools

import jax
import jax.numpy as jnp
from jax import lax
from jax.experimental import pallas as pl
from jax.experimental.pallas import tpu as pltpu

F32 = jnp.float32
BF16 = jnp.bfloat16

D_MODEL = 1024
RWKV_WIDTH = 512
CONV_WIDTH = 512
HEAD_SIZE = 64
RWKV_HEADS = 8
DECAY_RANK = 64
ICLR_RANK = 64
GATE_RANK = 160
CONV_KERNEL = 31
CONV_PAD = CONV_KERNEL // 2
GRID_W = 64
SUBLANES = 8
CONV_LEAD = 2 * SUBLANES
CONV_ROWS = (CONV_LEAD + CONV_PAD) // SUBLANES * SUBLANES + GRID_W
D_FF = 4 * D_MODEL
N_MOD = 6
EPS_RMS = 1e-6
EPS_LN = 1e-5
EPS_GN = 64e-5
DECAY_SCALE = 0.6065306597126334
LOG2_E = 1.4426950408889634

LORA_OFF = 3 * RWKV_WIDTH
LORA_COLS = 256
GATE_OFF = LORA_OFF + LORA_COLS
GATE_PAD = 256
CONV_OFF = GATE_OFF + GATE_PAD
P_COLS = CONV_OFF + 2 * CONV_WIDTH
SCAN_COLS = GATE_OFF
PROJ_BLOCK = 1024
N_SHIFT_BLOCKS = CONV_OFF // PROJ_BLOCK

CHUNK = 64
INV_BLOCK = 8
GROUP_HEADS = 4
GROUP_W = GROUP_HEADS * HEAD_SIZE
N_GROUPS = RWKV_HEADS // GROUP_HEADS
SCAN_BATCH = 4

LANES = 128
VMEM_LIMIT = 56 * 1024 * 1024


def _sigmoid(x):
    return 0.5 * jnp.tanh(0.5 * x) + 0.5


def _silu(x):
    return x * _sigmoid(x)


def _rms(x, g):
    return x * lax.rsqrt(jnp.mean(x * x, axis=-1, keepdims=True) + EPS_RMS) * g


def _dot(a, b):
    return jnp.dot(a.astype(BF16), b.astype(BF16), preferred_element_type=F32)


def _dot_nt(a, b):
    return lax.dot_general(a.astype(BF16), b.astype(BF16), (((1,), (1,)), ((), ())),
                           preferred_element_type=F32)


def _dot_tn(a, b):
    return lax.dot_general(a.astype(BF16), b.astype(BF16), (((0,), (0,)), ((), ())),
                           preferred_element_type=F32)


def _dot_split_rhs(a_bf16, x):
    hi = x.astype(BF16)
    lo = (x - hi.astype(F32)).astype(BF16)
    d = lambda p: jnp.dot(a_bf16, p, preferred_element_type=F32)
    return d(hi) + d(lo)


def _head_sum(z, ones_bd):
    rows = z.shape[0]
    hi = z.astype(BF16)
    lo = (z - hi.astype(F32)).astype(BF16)
    pieces = []
    for g in range(N_GROUPS):
        pieces += [hi[:, g * GROUP_W:(g + 1) * GROUP_W], lo[:, g * GROUP_W:(g + 1) * GROUP_W]]
    res = jnp.dot(jnp.concatenate(pieces, axis=0), ones_bd, preferred_element_type=F32)
    return jnp.concatenate(
        [res[(2 * g) * rows:(2 * g + 1) * rows] + res[(2 * g + 1) * rows:(2 * g + 2) * rows]
         for g in range(N_GROUPS)], axis=1)


def _ada_kernel(c_ref, w_ref, b_ref, o_ref):
    o_ref[...] = _dot(_silu(c_ref[...]), w_ref[...]) + b_ref[...]


def _ada_call(cc, ada_w, ada_b):
    rows = cc.shape[0]
    n_out = ada_w.shape[1]
    blk = 1536
    return pl.pallas_call(
        _ada_kernel,
        grid=(n_out // blk,),
        in_specs=[pl.BlockSpec((rows, D_MODEL), lambda n: (0, 0)),
                  pl.BlockSpec((D_MODEL, blk), lambda n: (0, n)),
                  pl.BlockSpec((1, blk), lambda n: (0, n))],
        out_specs=pl.BlockSpec((rows, blk), lambda n: (0, n)),
        out_shape=jax.ShapeDtypeStruct((rows, n_out), F32),
        compiler_params=pltpu.CompilerParams(vmem_limit_bytes=VMEM_LIMIT),
        name="ada_mod",
    )(cc, ada_w, ada_b)


def _proj_kernel(x_ref, mod_ref, g_ref, w_ref, mp_ref, mn_ref, o_ref, h_scr, z_scr, *, seq, seg):
    n = pl.program_id(1)
    row_chunk = min(seq, 1024)

    @pl.when(n == 0)
    def _():
        shift = mod_ref[0, 0:1, :]
        scale = mod_ref[0, 1:2, :]
        g = g_ref[...]
        for s in range(0, seq, row_chunk):
            xs = x_ref[0, s:s + row_chunk, :]
            h_scr[s:s + row_chunk, :] = (_rms(xs, g) * (1.0 + scale) + shift).astype(BF16)

    @pl.when(n < N_SHIFT_BLOCKS)
    def _():
        lead = SUBLANES
        z_scr[0:lead, :] = jnp.zeros((lead, PROJ_BLOCK), F32)
        z_scr[lead + seq:2 * lead + seq, :] = jnp.zeros((lead, PROJ_BLOCK), F32)
        mp = mp_ref[...]
        mn = mn_ref[...]

        def shift_rows(s):
            cur = z_scr[lead + s:lead + s + row_chunk, :]
            prv = z_scr[lead - 1 + s:lead - 1 + s + row_chunk, :]
            nxt = z_scr[lead + 1 + s:lead + 1 + s + row_chunk, :]
            if seg < seq:
                pos = jnp.bitwise_and(
                    s + lax.broadcasted_iota(jnp.int32, (row_chunk, PROJ_BLOCK), 0), seg - 1)
                prv = jnp.where(pos == 0, 0.0, prv)
                nxt = jnp.where(pos == seg - 1, 0.0, nxt)
            o_ref[0, s:s + row_chunk, :] = (cur + mp * (prv - cur) + mn * (nxt - cur)).astype(BF16)

        for s in range(0, seq, row_chunk):
            z_scr[lead + s:lead + s + row_chunk, :] = jnp.dot(
                h_scr[s:s + row_chunk, :], w_ref[...], preferred_element_type=F32)
            if s > 0:
                shift_rows(s - row_chunk)
        shift_rows(seq - row_chunk)

    @pl.when(n >= N_SHIFT_BLOCKS)
    def _():
        for s in range(0, seq, row_chunk):
            o_ref[0, s:s + row_chunk, :] = jnp.dot(
                h_scr[s:s + row_chunk, :], w_ref[...], preferred_element_type=F32).astype(BF16)


def _proj_call(x, mods, g, w_pad, mp_pad, mn_pad, n_blocks, seg):
    bsz, seq, _ = x.shape
    assert seq % seg == 0 and seg & (seg - 1) == 0
    n_cols = n_blocks * PROJ_BLOCK
    return pl.pallas_call(
        functools.partial(_proj_kernel, seq=seq, seg=seg),
        grid=(bsz, n_blocks),
        in_specs=[pl.BlockSpec((1, seq, D_MODEL), lambda b, n: (b, 0, 0)),
                  pl.BlockSpec((1, N_MOD, D_MODEL), lambda b, n: (b, 0, 0)),
                  pl.BlockSpec((1, D_MODEL), lambda b, n: (0, 0)),
                  pl.BlockSpec((D_MODEL, PROJ_BLOCK), lambda b, n: (0, n)),
                  pl.BlockSpec((1, PROJ_BLOCK), lambda b, n: (0, n)),
                  pl.BlockSpec((1, PROJ_BLOCK), lambda b, n: (0, n))],
        out_specs=pl.BlockSpec((1, seq, PROJ_BLOCK), lambda b, n: (b, 0, n)),
        out_shape=jax.ShapeDtypeStruct((bsz, seq, n_cols), BF16),
        scratch_shapes=[pltpu.VMEM((seq, D_MODEL), BF16),
                        pltpu.VMEM((seq + 2 * SUBLANES, PROJ_BLOCK), F32)],
        compiler_params=pltpu.CompilerParams(
            dimension_semantics=("parallel", "arbitrary"), vmem_limit_bytes=VMEM_LIMIT),
        name="in_proj",
    )(x, mods, g, w_pad, mp_pad, mn_pad)


def _expand(x, bd_mask):
    return jnp.where(bd_mask, jnp.concatenate([x] * GROUP_HEADS, axis=0), 0.0)


def _scan_prep(blocks, lw_ref, vec_ref, ones_bd):
    L = CHUNK
    n = len(blocks)
    lane = lax.broadcasted_iota(jnp.int32, (L, LORA_COLS), 1)
    loras = []
    for pb, _ in blocks:
        lora = pb[:, LORA_OFF:LORA_OFF + LORA_COLS]
        loras.append(jnp.where(lane < 2 * DECAY_RANK, jnp.tanh(lora), lora))
    lin = [None] * n
    for d in range(2):
        idx = [i for i in range(n) if blocks[i][1] == bool(d)]
        res = _dot(jnp.concatenate([loras[i] for i in idx], axis=0), lw_ref[d])
        for j, i in enumerate(idx):
            lin[i] = res[j * L:(j + 1) * L]

    kk0s, pieces = [], []
    for i, (pb, reverse) in enumerate(blocks):
        kk0 = pb[:, RWKV_WIDTH:2 * RWKV_WIDTH] * vec_ref[int(reverse), 2:3, :]
        kk0s.append(kk0)
        sq = (kk0 * kk0).astype(BF16)
        pieces += [sq[:, g * GROUP_W:(g + 1) * GROUP_W] for g in range(N_GROUPS)]
    sums = jnp.dot(jnp.concatenate(pieces, axis=0), ones_bd, preferred_element_type=F32)

    row = lax.broadcasted_iota(jnp.int32, (L, L), 0)
    col = lax.broadcasted_iota(jnp.int32, (L, L), 1)
    preps = []
    for i, (pb, reverse) in enumerate(blocks):
        vecs = vec_ref[int(reverse)]
        w0 = vecs[0:1, :]
        a0 = vecs[1:2, :]
        k_a = vecs[3:4, :]
        r = pb[:, 0:RWKV_WIDTH]
        k = pb[:, RWKV_WIDTH:2 * RWKV_WIDTH]
        v = pb[:, 2 * RWKV_WIDTH:3 * RWKV_WIDTH]
        half = 0.5 * LOG2_E * DECAY_SCALE
        lw = -half * jnp.tanh(0.5 * (lin[i][:, 0:RWKV_WIDTH] + w0)) - half
        neg_iclr = -0.5 * jnp.tanh(0.5 * (lin[i][:, RWKV_WIDTH:] + a0)) - 0.5
        base = i * N_GROUPS * L
        ss = jnp.concatenate(
            [sums[base + g * L:base + (g + 1) * L] for g in range(N_GROUPS)], axis=1)
        kk = kk0s[i] * lax.rsqrt(jnp.maximum(ss, 1e-24))
        kd = k * (1.0 - (neg_iclr + 1.0) * k_a)
        na = kk
        nb = kk * neg_iclr
        tri = jnp.where((col >= row) if reverse else (col <= row), 1.0, 0.0).astype(BF16)
        cum = _dot_split_rhs(tri, lw)
        tot = cum[0:1, :] if reverse else cum[L - 1:L, :]
        e_incl = jnp.exp2(cum)
        e_neg = 1.0 / e_incl
        w_tot = jnp.exp2(tot)
        e_rem = w_tot * e_neg
        preps.append(dict(
            r_t=r * e_incl, a_t=na * jnp.exp2(cum - lw), b_t=nb * e_neg, k_t=kd * e_neg,
            b_h=nb * e_rem, k_h=kd * e_rem, v=v, w_tot=w_tot))
    return preps


def _scan_units(preps, reverse, s_idx, s_ref):
    L = CHUNK
    rr = lax.broadcasted_iota(jnp.int32, (GROUP_W, GROUP_W), 0)
    cc = lax.broadcasted_iota(jnp.int32, (GROUP_W, GROUP_W), 1)
    head_shift = HEAD_SIZE.bit_length() - 1
    bd_mask = jnp.right_shift(rr, head_shift) == jnp.right_shift(cc, head_shift)
    tr = lax.broadcasted_iota(jnp.int32, (L, GROUP_W), 0)
    tc = jnp.bitwise_and(lax.broadcasted_iota(jnp.int32, (L, GROUP_W), 1), L - 1)
    eye = jnp.where(tc == tr, 1.0, 0.0)
    strict = {True: tc > tr, False: tc < tr}
    incl = {True: tc >= tr, False: tc <= tr}
    units = [(d, g) for d in range(len(preps)) for g in range(N_GROUPS)]
    sl = lambda g: slice(g * GROUP_W, (g + 1) * GROUP_W)
    ex = lambda x: _expand(x, bd_mask)

    lhs, a_ab, a_ak, a_rb, a_rk = {}, {}, {}, {}, {}
    for u in units:
        d, g = u
        p = preps[d]
        lhs[u] = jnp.concatenate([p["a_t"][:, sl(g)], p["r_t"][:, sl(g)]], axis=0)
        e_bk = jnp.concatenate([ex(p["b_t"][:, sl(g)]), ex(p["k_t"][:, sl(g)])], axis=0)
        a_all = _dot_nt(lhs[u], e_bk)
        a_ab[u] = jnp.where(strict[reverse[d]], a_all[0:L, 0:GROUP_W], 0.0)
        a_ak[u] = jnp.where(strict[reverse[d]], a_all[0:L, GROUP_W:], 0.0)
        a_rb[u] = jnp.where(incl[reverse[d]], a_all[L:, 0:GROUP_W], 0.0)
        a_rk[u] = jnp.where(incl[reverse[d]], a_all[L:, GROUP_W:], 0.0)
    av = {}
    for u in units:
        d, g = u
        av[u] = _dot(jnp.concatenate([a_ak[u], a_rk[u]], axis=0), ex(preps[d]["v"][:, sl(g)]))

    def nilpotent_inverse(n, index):
        prod = {u: eye + n[u] for u in units}
        npow = {u: _dot(n[u], ex(n[u])) for u in units}
        size = 2
        while size < index:
            for u in units:
                rhs = ex(npow[u])
                if 2 * size < index:
                    both = _dot(jnp.concatenate([npow[u], prod[u]], axis=0), rhs)
                    npow[u] = both[0:L]
                    prod[u] = prod[u] + both[L:]
                else:
                    prod[u] = prod[u] + _dot(prod[u], rhs)
            size *= 2
        return prod

    sh = INV_BLOCK.bit_length() - 1
    in_diag = jnp.right_shift(tr, sh) == jnp.right_shift(tc, sh)
    d_inv = nilpotent_inverse({u: jnp.where(in_diag, a_ab[u], 0.0) for u in units}, INV_BLOCK)
    e_mat = {u: _dot(d_inv[u], ex(jnp.where(in_diag, 0.0, a_ab[u]))) for u in units}
    e_inv = nilpotent_inverse(e_mat, L // INV_BLOCK)
    pinv = {u: _dot(e_inv[u], ex(d_inv[u])) for u in units}

    s_old, sa, x_u, u_u, y_u = {}, {}, {}, {}, {}
    for u in units:
        d, g = u
        s_old[u] = s_ref[s_idx[d] + (g,)]
        sa[u] = _dot(lhs[u], s_old[u]) + av[u]
    for u in units:
        x_u[u] = sa[u][0:L]
        u_u[u] = _dot(pinv[u], ex(x_u[u]))
    for u in units:
        y_u[u] = sa[u][L:] + _dot(a_rb[u], ex(u_u[u]))
    w_col = [jnp.transpose(jnp.broadcast_to(p["w_tot"], (LANES, RWKV_WIDTH))) for p in preps]
    for u in units:
        d, g = u
        p = preps[d]
        upd = _dot_tn(jnp.concatenate([p["b_h"][:, sl(g)], p["k_h"][:, sl(g)]], axis=0),
                      jnp.concatenate([u_u[u], p["v"][:, sl(g)]], axis=0))
        decay = jnp.concatenate([w_col[d][sl(g), :]] * (GROUP_W // LANES), axis=1)
        s_ref[s_idx[d] + (g,)] = s_old[u] * decay + jnp.where(bd_mask, upd, 0.0)
    return [jnp.concatenate([y_u[(d, g)] for g in range(N_GROUPS)], axis=1)
            for d in range(len(preps))]


def _scan_kernel(pf_ref, pb_ref, s0_ref, lw_ref, vec_ref, ho_ref, yf_ref, yb_ref, sfin_ref, s_scr):
    c = pl.program_id(1)

    @pl.when(c == 0)
    def _():
        s_scr[...] = s0_ref[...]

    n_streams = 2 * SCAN_BATCH
    reverse = [bool(i % 2) for i in range(n_streams)]
    s_idx = [(i // 2, i % 2) for i in range(n_streams)]
    blocks = []
    for bb in range(SCAN_BATCH):
        blocks += [(pf_ref[bb].astype(F32), False), (pb_ref[bb].astype(F32), True)]
    preps = _scan_prep(blocks, lw_ref, vec_ref, ho_ref[...])
    ys = _scan_units(preps, reverse, s_idx, s_scr)
    for bb in range(SCAN_BATCH):
        yf_ref[bb] = ys[2 * bb]
        yb_ref[bb] = ys[2 * bb + 1]

    @pl.when(c == pl.num_programs(1) - 1)
    def _():
        sfin_ref[...] = s_scr[...]


def _scan_call(p, s0, lora_w, vecs, head_ones):
    bsz, seq, _ = p.shape
    nc = seq // CHUNK
    state_block = (SCAN_BATCH, 2, N_GROUPS, GROUP_W, GROUP_W)
    state_map = lambda b, c: (b, 0, 0, 0, 0)
    return pl.pallas_call(
        _scan_kernel,
        grid=(bsz // SCAN_BATCH, nc),
        in_specs=[pl.BlockSpec((SCAN_BATCH, CHUNK, SCAN_COLS), lambda b, c: (b, c, 0)),
                  pl.BlockSpec((SCAN_BATCH, CHUNK, SCAN_COLS), lambda b, c: (b, nc - 1 - c, 0)),
                  pl.BlockSpec(state_block, state_map),
                  pl.BlockSpec((2, LORA_COLS, 2 * RWKV_WIDTH), lambda b, c: (0, 0, 0)),
                  pl.BlockSpec((2, 8, RWKV_WIDTH), lambda b, c: (0, 0, 0)),
                  pl.BlockSpec((GROUP_W, GROUP_W), lambda b, c: (0, 0))],
        out_specs=[pl.BlockSpec((SCAN_BATCH, CHUNK, RWKV_WIDTH), lambda b, c: (b, c, 0)),
                   pl.BlockSpec((SCAN_BATCH, CHUNK, RWKV_WIDTH), lambda b, c: (b, nc - 1 - c, 0)),
                   pl.BlockSpec(state_block, state_map)],
        out_shape=[jax.ShapeDtypeStruct((bsz, seq, RWKV_WIDTH), F32),
                   jax.ShapeDtypeStruct((bsz, seq, RWKV_WIDTH), F32),
                   jax.ShapeDtypeStruct((bsz, 2, N_GROUPS, GROUP_W, GROUP_W), F32)],
        scratch_shapes=[pltpu.VMEM(state_block, F32)],
        compiler_params=pltpu.CompilerParams(
            dimension_semantics=("parallel", "arbitrary"), vmem_limit_bytes=VMEM_LIMIT),
        name="wkv_scan",
    )(p, p, s0, lora_w, vecs, head_ones)


def _mix_kernel(x_ref, yf_ref, yb_ref, rkv_ref, sm_ref, cv_ref, mod_ref, rw_ref, vec_ref, ho_ref,
                cw_ref, cvec_ref, wout_ref, g_ref, o_ref, upad, *, tm):
    n_lines = tm // GRID_W
    ones_bd = ho_ref[...]
    y = yf_ref[0] + yb_ref[0]
    inv_n = 1.0 / HEAD_SIZE
    mu = _head_sum(y, ones_bd) * inv_n
    dy = y - mu
    var = _head_sum(dy * dy, ones_bd) * inv_n
    lnx_w = vec_ref[0:1, :]
    lnx_b = vec_ref[1:2, :]
    a0_f = vec_ref[2:3, :]
    a0_b = vec_ref[3:4, :]
    k_a = vec_ref[4:5, :]
    r_k = vec_ref[5:6, :]
    yn = dy * lax.rsqrt(var + EPS_GN) * lnx_w + lnx_b
    sm = sm_ref[0].astype(F32)
    lane = lax.broadcasted_iota(jnp.int32, sm.shape, 1)
    sm = jnp.where(lane >= LORA_COLS, _sigmoid(sm), sm)
    lin = _dot(sm, rw_ref[...])
    iclr_f = _sigmoid(lin[:, 0:RWKV_WIDTH] + a0_f)
    iclr_b = _sigmoid(lin[:, RWKV_WIDTH:2 * RWKV_WIDTH] + a0_b)
    gate = lin[:, 2 * RWKV_WIDTH:]
    r = rkv_ref[0, :, 0:RWKV_WIDTH].astype(F32)
    k = rkv_ref[0, :, RWKV_WIDTH:2 * RWKV_WIDTH].astype(F32)
    v = rkv_ref[0, :, 2 * RWKV_WIDTH:3 * RWKV_WIDTH].astype(F32)
    k_bar = k * (1.0 + (0.5 * (iclr_f + iclr_b) - 1.0) * k_a)
    bonus = _head_sum(r * k_bar * r_k, ones_bd) * v
    rw_out = (yn + bonus) * gate

    cv = cv_ref[0].astype(F32)
    u = cv[:, 0:CONV_WIDTH] * _sigmoid(cv[:, CONV_WIDTH:])
    zeros = jnp.zeros((CONV_LEAD, CONV_WIDTH), F32)
    for ln in range(n_lines):
        u_ln = u[ln * GRID_W:(ln + 1) * GRID_W, :]
        for s in range(SUBLANES):
            upad[s, ln, 0:CONV_LEAD, :] = zeros
            upad[s, ln, CONV_ROWS - CONV_LEAD:CONV_ROWS, :] = zeros
            upad[s, ln, CONV_LEAD - s:CONV_LEAD - s + GRID_W, :] = u_ln
    conv_b = cvec_ref[0:1, :]
    ln_w = cvec_ref[1:2, :]
    ln_b = cvec_ref[2:3, :]
    conv_lines = []
    for ln in range(n_lines):
        strips = []
        for lb in range(CONV_WIDTH // LANES):
            lanes = slice(lb * LANES, (lb + 1) * LANES)
            acc = jnp.zeros((GRID_W, LANES), F32)
            for j in range(CONV_KERNEL):
                off = CONV_LEAD - CONV_PAD + j
                q8 = off // SUBLANES * SUBLANES
                acc = acc + cw_ref[j:j + 1, lanes] * upad[off % SUBLANES, ln, q8:q8 + GRID_W, lanes]
            strips.append(acc)
        conv_lines.append(jnp.concatenate(strips, axis=1))
    yc = jnp.concatenate(conv_lines, axis=0) + conv_b
    mu_c = jnp.mean(yc, axis=-1, keepdims=True)
    dc = yc - mu_c
    var_c = jnp.mean(dc * dc, axis=-1, keepdims=True)
    conv_out = _silu(dc * lax.rsqrt(var_c + EPS_LN) * ln_w + ln_b)

    mix_in = jnp.concatenate([rw_out.astype(BF16), conv_out.astype(BF16)], axis=1)
    mix = jnp.dot(mix_in, wout_ref[...], preferred_element_type=F32)
    gate_mix = mod_ref[0, 2:3, :]
    o_ref[0] = x_ref[0] + gate_mix * _rms(mix, g_ref[...])


def _mix_call(x, y_f, y_b, p, mods, rw_w, rvecs, head_ones, conv_w, cvecs, w_out, post_g):
    bsz, seq, _ = x.shape
    tm = 512
    tok = lambda width, blk: pl.BlockSpec((1, tm, width), lambda b, t: (b, t, blk))
    const2 = lambda shape: pl.BlockSpec(shape, lambda b, t: (0, 0))
    return pl.pallas_call(
        functools.partial(_mix_kernel, tm=tm),
        grid=(bsz, seq // tm),
        in_specs=[tok(D_MODEL, 0), tok(RWKV_WIDTH, 0), tok(RWKV_WIDTH, 0),
                  tok(3 * RWKV_WIDTH, 0),
                  tok(512, LORA_OFF // 512),
                  tok(2 * CONV_WIDTH, CONV_OFF // (2 * CONV_WIDTH)),
                  pl.BlockSpec((1, N_MOD, D_MODEL), lambda b, t: (b, 0, 0)),
                  const2((512, 3 * RWKV_WIDTH)), const2((8, RWKV_WIDTH)),
                  const2((GROUP_W, GROUP_W)), const2((32, CONV_WIDTH)),
                  const2((8, CONV_WIDTH)), const2((D_MODEL, D_MODEL)), const2((1, D_MODEL))],
        out_specs=tok(D_MODEL, 0),
        out_shape=jax.ShapeDtypeStruct((bsz, seq, D_MODEL), F32),
        scratch_shapes=[pltpu.VMEM((SUBLANES, tm // GRID_W, CONV_ROWS, CONV_WIDTH), F32)],
        compiler_params=pltpu.CompilerParams(
            dimension_semantics=("parallel", "parallel"), vmem_limit_bytes=VMEM_LIMIT),
        name="mix_out",
    )(x, y_f, y_b, p, p, p, mods, rw_w, rvecs, head_ones, conv_w, cvecs, w_out, post_g)


def _mlp_kernel(x_ref, mod_ref, gpre_ref, gpost_ref, w1_ref, w2_ref, o_ref, acc_ref):
    x = x_ref[0]
    shift = mod_ref[0, 3:4, :]
    scale = mod_ref[0, 4:5, :]
    gate = mod_ref[0, 5:6, :]
    h = (_rms(x, gpre_ref[...]) * (1.0 + scale) + shift).astype(BF16)
    ff_blk = 1024
    for c in range(D_FF // ff_blk):
        hid = jnp.dot(h, w1_ref[:, c * ff_blk:(c + 1) * ff_blk], preferred_element_type=F32)
        hid = jnp.square(jnp.maximum(hid, 0.0)).astype(BF16)
        part = jnp.dot(hid, w2_ref[c * ff_blk:(c + 1) * ff_blk, :], preferred_element_type=F32)
        if c == 0:
            acc_ref[...] = part
        else:
            acc_ref[...] += part
    o_ref[0] = x + gate * _rms(acc_ref[...], gpost_ref[...])


def _mlp_call(x, mods, pre_g, post_g, w1, w2):
    bsz, seq, _ = x.shape
    tm = 512
    return pl.pallas_call(
        _mlp_kernel,
        grid=(bsz, seq // tm),
        in_specs=[pl.BlockSpec((1, tm, D_MODEL), lambda b, t: (b, t, 0)),
                  pl.BlockSpec((1, N_MOD, D_MODEL), lambda b, t: (b, 0, 0)),
                  pl.BlockSpec((1, D_MODEL), lambda b, t: (0, 0)),
                  pl.BlockSpec((1, D_MODEL), lambda b, t: (0, 0)),
                  pl.BlockSpec((D_MODEL, D_FF), lambda b, t: (0, 0)),
                  pl.BlockSpec((D_FF, D_MODEL), lambda b, t: (0, 0))],
        out_specs=pl.BlockSpec((1, tm, D_MODEL), lambda b, t: (b, t, 0)),
        out_shape=jax.ShapeDtypeStruct((bsz, seq, D_MODEL), F32),
        scratch_shapes=[pltpu.VMEM((tm, D_MODEL), F32)],
        compiler_params=pltpu.CompilerParams(
            dimension_semantics=("parallel", "parallel"), vmem_limit_bytes=VMEM_LIMIT),
        name="sqrelu_mlp",
    )(x, mods, pre_g, post_g, w1, w2)


def _pad_cols(a, lo, hi):
    pad = jnp.zeros(a.shape[:-1] + (GATE_PAD - GATE_RANK,), a.dtype)
    return jnp.concatenate([a[..., :lo], pad, a[..., lo:hi]], axis=-1)


def _layer(x, ctx, mods_x, mods_c, prm):
    (mix_pre_g, mix_post_g, mlp_pre_g, mlp_post_g, w_in, mu_prev, mu_next, decay_w0, decay_w2,
     iclr_a0, iclr_a2, k_k, k_a, r_k, gate_w2, lnx_w, lnx_b, conv_w, conv_b, conv_ln_w,
     conv_ln_b, w_out, mlp_w1, mlp_w2) = prm
    bsz = x.shape[0]
    shift_cols = GATE_OFF + GATE_RANK
    in_cols = w_in.shape[1]

    w_pad = _pad_cols(w_in, shift_cols, in_cols).astype(BF16)
    zc = jnp.zeros((2 * CONV_WIDTH,), F32)
    mp_pad = _pad_cols(jnp.concatenate([mu_prev, zc]), shift_cols, in_cols)[None, :]
    mn_pad = _pad_cols(jnp.concatenate([mu_next, zc]), shift_cols, in_cols)[None, :]
    zr = jnp.zeros((DECAY_RANK, RWKV_WIDTH), F32)
    lora_dir = []
    for d in range(2):
        dec_rows = [zr, zr]
        dec_rows[d] = decay_w2[d]
        icl_rows = [zr, zr]
        icl_rows[d] = iclr_a2[d]
        left = jnp.concatenate(dec_rows + [zr, zr], axis=0)
        right = jnp.concatenate([zr, zr] + icl_rows, axis=0)
        lora_dir.append(jnp.concatenate([left, right], axis=1))
    lora_w = jnp.stack(lora_dir).astype(BF16)
    z4 = jnp.zeros((4, RWKV_WIDTH), F32)
    scan_vecs = jnp.stack([jnp.concatenate([decay_w0[d][None], iclr_a0[d][None], k_k[None],
                                            k_a[None], z4], axis=0) for d in range(2)])
    hid = jnp.arange(RWKV_WIDTH) // HEAD_SIZE
    head_ones = (hid[:, None] == hid[None, :]).astype(BF16)
    zl = jnp.zeros((2 * DECAY_RANK, 3 * RWKV_WIDTH), F32)
    za = jnp.zeros((ICLR_RANK, RWKV_WIDTH), F32)
    zg = jnp.zeros((GATE_PAD, RWKV_WIDTH), F32)
    gate_pad = jnp.concatenate([gate_w2, jnp.zeros((GATE_PAD - GATE_RANK, RWKV_WIDTH), F32)], axis=0)
    rw_w = jnp.concatenate([
        zl,
        jnp.concatenate([iclr_a2[0], za, za], axis=1),
        jnp.concatenate([za, iclr_a2[1], za], axis=1),
        jnp.concatenate([zg, zg, gate_pad], axis=1)], axis=0).astype(BF16)
    rvecs = jnp.concatenate([lnx_w[None], lnx_b[None], iclr_a0[0][None], iclr_a0[1][None],
                             k_a[None], r_k.reshape(1, RWKV_WIDTH), jnp.zeros((2, RWKV_WIDTH), F32)],
                            axis=0)
    conv_w_pad = jnp.concatenate([conv_w, jnp.zeros((1, CONV_WIDTH), F32)], axis=0)
    cvecs = jnp.concatenate([conv_b[None], conv_ln_w[None], conv_ln_b[None],
                             jnp.zeros((5, CONV_WIDTH), F32)], axis=0)

    seq, ctx_len = x.shape[1], ctx.shape[1]
    p_x = _proj_call(x, mods_x, mix_pre_g[None], w_pad, mp_pad, mn_pad, P_COLS // PROJ_BLOCK, seq)
    pack = seq // ctx_len
    p_c = _proj_call(ctx.reshape(bsz // pack, seq, D_MODEL), mods_c[:bsz // pack], mix_pre_g[None],
                     w_pad, mp_pad, mn_pad, N_SHIFT_BLOCKS, ctx_len).reshape(bsz, ctx_len, -1)
    s_zero = jnp.zeros((bsz, 2, N_GROUPS, GROUP_W, GROUP_W), F32)
    _, _, s_ctx = _scan_call(p_c, s_zero, lora_w, scan_vecs, head_ones)
    y_f, y_b, _ = _scan_call(p_x, s_ctx, lora_w, scan_vecs, head_ones)
    x = _mix_call(x, y_f, y_b, p_x, mods_x, rw_w, rvecs, head_ones, conv_w_pad, cvecs,
                  w_out.astype(BF16), mix_post_g[None])
    x = _mlp_call(x, mods_x, mlp_pre_g[None], mlp_post_g[None], mlp_w1.astype(BF16),
                  mlp_w2.astype(BF16))
    return x


def kernel(x, c, ctx, c_ctx, ada_w, ada_b, mix_pre_g, mix_post_g, mlp_pre_g, mlp_post_g, w_in, mu_prev, mu_next, decay_w0, decay_w2, iclr_a0, iclr_a2, k_k, k_a, r_k, gate_w2, lnx_w, lnx_b, conv_w, conv_b, conv_ln_w, conv_ln_b, w_out, mlp_w1, mlp_w2):
    depth = ada_w.shape[0]
    assert depth == 1, "context-stream update between layers is not implemented"
    bsz = x.shape[0]
    rows = 8 * ((bsz + 1 + 7) // 8)
    cc = jnp.concatenate([c, c_ctx[None, :], jnp.zeros((rows - bsz - 1, D_MODEL), F32)], axis=0)
    per_layer = (mix_pre_g, mix_post_g, mlp_pre_g, mlp_post_g, w_in, mu_prev, mu_next, decay_w0,
                 decay_w2, iclr_a0, iclr_a2, k_k, k_a, r_k, gate_w2, lnx_w, lnx_b, conv_w, conv_b,
                 conv_ln_w, conv_ln_b, w_out, mlp_w1, mlp_w2)
    for l in range(depth):
        mods = _ada_call(cc, ada_w[l], ada_b[l][None, :]).reshape(rows, N_MOD, D_MODEL)
        mods_x = mods[:bsz]
        mods_c = jnp.broadcast_to(mods[bsz:bsz + 1], (bsz, N_MOD, D_MODEL))
        x = _layer(x, ctx, mods_x, mods_c, tuple(a[l] for a in per_layer))
    return x
```

```python
import functools

import jax
import jax.numpy as jnp
from jax import lax
from jax.experimental import pallas as pl
from jax.experimental.pallas import tpu as pltpu

F32 = jnp.float32
BF16 = jnp.bfloat16

D_MODEL = 1024
RWKV_WIDTH = 512
CONV_WIDTH = 512
HEAD_SIZE = 64
RWKV_HEADS = 8
DECAY_RANK = 64
ICLR_RANK = 64
GATE_RANK = 160
CONV_KERNEL = 31
CONV_PAD = CONV_KERNEL // 2
GRID_W = 64
SUBLANES = 8
CONV_LEAD = 2 * SUBLANES
CONV_ROWS = (CONV_LEAD + CONV_PAD) // SUBLANES * SUBLANES + GRID_W
D_FF = 4 * D_MODEL
N_MOD = 6
EPS_RMS = 1e-6
EPS_LN = 1e-5
EPS_GN = 64e-5
DECAY_SCALE = 0.6065306597126334
LOG2_E = 1.4426950408889634

LORA_OFF = 3 * RWKV_WIDTH
LORA_COLS = 256
GATE_OFF = LORA_OFF + LORA_COLS
GATE_PAD = 256
CONV_OFF = GATE_OFF + GATE_PAD
P_COLS = CONV_OFF + 2 * CONV_WIDTH
SCAN_COLS = GATE_OFF
PROJ_BLOCK = 1024
N_SHIFT_BLOCKS = CONV_OFF // PROJ_BLOCK

CHUNK = 64
INV_BLOCK = 8
GROUP_HEADS = 4
GROUP_W = GROUP_HEADS * HEAD_SIZE
N_GROUPS = RWKV_HEADS // GROUP_HEADS
SCAN_BATCH = 4

LANES = 128
VMEM_LIMIT = 56 * 1024 * 1024


def _sigmoid(x):
    return 0.5 * jnp.tanh(0.5 * x) + 0.5


def _silu(x):
    return x * _sigmoid(x)


def _rms(x, g):
    return x * lax.rsqrt(jnp.mean(x * x, axis=-1, keepdims=True) + EPS_RMS) * g


def _dot(a, b):
    return jnp.dot(a.astype(BF16), b.astype(BF16), preferred_element_type=F32)


def _dot_nt(a, b):
    return lax.dot_general(a.astype(BF16), b.astype(BF16), (((1,), (1,)), ((), ())),
                           preferred_element_type=F32)


def _dot_tn(a, b):
    return lax.dot_general(a.astype(BF16), b.astype(BF16), (((0,), (0,)), ((), ())),
                           preferred_element_type=F32)


def _dot_split_rhs(a_bf16, x):
    hi = x.astype(BF16)
    lo = (x - hi.astype(F32)).astype(BF16)
    d = lambda p: jnp.dot(a_bf16, p, preferred_element_type=F32)
    return d(hi) + d(lo)


def _head_sum(z, ones_bd):
    rows = z.shape[0]
    hi = z.astype(BF16)
    lo = (z - hi.astype(F32)).astype(BF16)
    pieces = []
    for g in range(N_GROUPS):
        pieces += [hi[:, g * GROUP_W:(g + 1) * GROUP_W], lo[:, g * GROUP_W:(g + 1) * GROUP_W]]
    res = jnp.dot(jnp.concatenate(pieces, axis=0), ones_bd, preferred_element_type=F32)
    return jnp.concatenate(
        [res[(2 * g) * rows:(2 * g + 1) * rows] + res[(2 * g + 1) * rows:(2 * g + 2) * rows]
         for g in range(N_GROUPS)], axis=1)


def _ada_kernel(c_ref, w_ref, b_ref, o_ref):
    o_ref[...] = _dot(_silu(c_ref[...]), w_ref[...]) + b_ref[...]


def _ada_call(cc, ada_w, ada_b):
    rows = cc.shape[0]
    n_out = ada_w.shape[1]
    blk = 1536
    return pl.pallas_call(
        _ada_kernel,
        grid=(n_out // blk,),
        in_specs=[pl.BlockSpec((rows, D_MODEL), lambda n: (0, 0)),
                  pl.BlockSpec((D_MODEL, blk), lambda n: (0, n)),
                  pl.BlockSpec((1, blk), lambda n: (0, n))],
        out_specs=pl.BlockSpec((rows, blk), lambda n: (0, n)),
        out_shape=jax.ShapeDtypeStruct((rows, n_out), F32),
        compiler_params=pltpu.CompilerParams(vmem_limit_bytes=VMEM_LIMIT),
        name="ada_mod",
    )(cc, ada_w, ada_b)


def _proj_kernel(x_ref, mod_ref, g_ref, w_ref, mp_ref, mn_ref, o_ref, h_scr, z_scr, *, seq, seg,
                 col_of_step):
    n = pl.program_id(1)
    col = col_of_step(n)
    row_chunk = min(seq, 1024)

    @pl.when(n == 0)
    def _():
        shift = mod_ref[0, 0:1, :]
        scale = mod_ref[0, 1:2, :]
        g = g_ref[...]
        for s in range(0, seq, row_chunk):
            xs = x_ref[0, s:s + row_chunk, :]
            h_scr[s:s + row_chunk, :] = (_rms(xs, g) * (1.0 + scale) + shift).astype(BF16)

    @pl.when(col < N_SHIFT_BLOCKS)
    def _():
        lead = SUBLANES
        z_scr[0:lead, :] = jnp.zeros((lead, PROJ_BLOCK), F32)
        z_scr[lead + seq:2 * lead + seq, :] = jnp.zeros((lead, PROJ_BLOCK), F32)
        mp = mp_ref[...]
        mn = mn_ref[...]

        def shift_rows(s):
            cur = z_scr[lead + s:lead + s + row_chunk, :]
            prv = z_scr[lead - 1 + s:lead - 1 + s + row_chunk, :]
            nxt = z_scr[lead + 1 + s:lead + 1 + s + row_chunk, :]
            if seg < seq:
                pos = jnp.bitwise_and(
                    s + lax.broadcasted_iota(jnp.int32, (row_chunk, PROJ_BLOCK), 0), seg - 1)
                prv = jnp.where(pos == 0, 0.0, prv)
                nxt = jnp.where(pos == seg - 1, 0.0, nxt)
            o_ref[0, s:s + row_chunk, :] = (cur + mp * (prv - cur) + mn * (nxt - cur)).astype(BF16)

        for s in range(0, seq, row_chunk):
            z_scr[lead + s:lead + s + row_chunk, :] = jnp.dot(
                h_scr[s:s + row_chunk, :], w_ref[...], preferred_element_type=F32)
            if s > 0:
                shift_rows(s - row_chunk)
        shift_rows(seq - row_chunk)

    @pl.when(col >= N_SHIFT_BLOCKS)
    def _():
        for s in range(0, seq, row_chunk):
            o_ref[0, s:s + row_chunk, :] = jnp.dot(
                h_scr[s:s + row_chunk, :], w_ref[...], preferred_element_type=F32).astype(BF16)


def _proj_call(x, mods, g, w_pad, mp_pad, mn_pad, n_blocks, seg):
    bsz, seq, _ = x.shape
    assert seq % seg == 0 and seg & (seg - 1) == 0
    n_cols = n_blocks * PROJ_BLOCK
    first = N_SHIFT_BLOCKS % n_blocks
    col_of_step = lambda n: lax.rem(n + first, n_blocks)
    return pl.pallas_call(
        functools.partial(_proj_kernel, seq=seq, seg=seg, col_of_step=col_of_step),
        grid=(bsz, n_blocks),
        in_specs=[pl.BlockSpec((1, seq, D_MODEL), lambda b, n: (b, 0, 0)),
                  pl.BlockSpec((1, N_MOD, D_MODEL), lambda b, n: (b, 0, 0)),
                  pl.BlockSpec((1, D_MODEL), lambda b, n: (0, 0)),
                  pl.BlockSpec((D_MODEL, PROJ_BLOCK), lambda b, n: (0, col_of_step(n))),
                  pl.BlockSpec((1, PROJ_BLOCK), lambda b, n: (0, col_of_step(n))),
                  pl.BlockSpec((1, PROJ_BLOCK), lambda b, n: (0, col_of_step(n)))],
        out_specs=pl.BlockSpec((1, seq, PROJ_BLOCK), lambda b, n: (b, 0, col_of_step(n))),
        out_shape=jax.ShapeDtypeStruct((bsz, seq, n_cols), BF16),
        scratch_shapes=[pltpu.VMEM((seq, D_MODEL), BF16),
                        pltpu.VMEM((seq + 2 * SUBLANES, PROJ_BLOCK), F32)],
        compiler_params=pltpu.CompilerParams(
            dimension_semantics=("parallel", "arbitrary"), vmem_limit_bytes=VMEM_LIMIT),
        name="in_proj",
    )(x, mods, g, w_pad, mp_pad, mn_pad)


def _expand(x, bd_mask):
    return jnp.where(bd_mask, jnp.concatenate([x] * GROUP_HEADS, axis=0), 0.0)


def _scan_prep(blocks, lw_ref, vec_ref, ones_bd):
    L = CHUNK
    n = len(blocks)
    lane = lax.broadcasted_iota(jnp.int32, (L, LORA_COLS), 1)
    loras = []
    for pb, _ in blocks:
        lora = pb[:, LORA_OFF:LORA_OFF + LORA_COLS]
        loras.append(jnp.where(lane < 2 * DECAY_RANK, jnp.tanh(lora), lora))
    lin = [None] * n
    for d in range(2):
        idx = [i for i in range(n) if blocks[i][1] == bool(d)]
        res = _dot(jnp.concatenate([loras[i] for i in idx], axis=0), lw_ref[d])
        for j, i in enumerate(idx):
            lin[i] = res[j * L:(j + 1) * L]

    kk0s, pieces = [], []
    for i, (pb, reverse) in enumerate(blocks):
        kk0 = pb[:, RWKV_WIDTH:2 * RWKV_WIDTH] * vec_ref[int(reverse), 2:3, :]
        kk0s.append(kk0)
        sq = (kk0 * kk0).astype(BF16)
        pieces += [sq[:, g * GROUP_W:(g + 1) * GROUP_W] for g in range(N_GROUPS)]
    sums = jnp.dot(jnp.concatenate(pieces, axis=0), ones_bd, preferred_element_type=F32)

    row = lax.broadcasted_iota(jnp.int32, (L, L), 0)
    col = lax.broadcasted_iota(jnp.int32, (L, L), 1)
    preps = []
    for i, (pb, reverse) in enumerate(blocks):
        vecs = vec_ref[int(reverse)]
        w0 = vecs[0:1, :]
        a0 = vecs[1:2, :]
        k_a = vecs[3:4, :]
        r = pb[:, 0:RWKV_WIDTH]
        k = pb[:, RWKV_WIDTH:2 * RWKV_WIDTH]
        v = pb[:, 2 * RWKV_WIDTH:3 * RWKV_WIDTH]
        half = 0.5 * LOG2_E * DECAY_SCALE
        lw = -half * jnp.tanh(0.5 * (lin[i][:, 0:RWKV_WIDTH] + w0)) - half
        neg_iclr = -0.5 * jnp.tanh(0.5 * (lin[i][:, RWKV_WIDTH:] + a0)) - 0.5
        base = i * N_GROUPS * L
        ss = jnp.concatenate(
            [sums[base + g * L:base + (g + 1) * L] for g in range(N_GROUPS)], axis=1)
        kk = kk0s[i] * lax.rsqrt(jnp.maximum(ss, 1e-24))
        kd = k * (1.0 - (neg_iclr + 1.0) * k_a)
        na = kk
        nb = kk * neg_iclr
        tri = jnp.where((col >= row) if reverse else (col <= row), 1.0, 0.0).astype(BF16)
        cum = _dot_split_rhs(tri, lw)
        tot = cum[0:1, :] if reverse else cum[L - 1:L, :]
        e_incl = jnp.exp2(cum)
        e_neg = 1.0 / e_incl
        w_tot = jnp.exp2(tot)
        e_rem = w_tot * e_neg
        preps.append(dict(
            r_t=r * e_incl, a_t=na * jnp.exp2(cum - lw), b_t=nb * e_neg, k_t=kd * e_neg,
            b_h=nb * e_rem, k_h=kd * e_rem, v=v, w_tot=w_tot))
    return preps


def _scan_units(preps, reverse, s_idx, s_ref):
    L = CHUNK
    rr = lax.broadcasted_iota(jnp.int32, (GROUP_W, GROUP_W), 0)
    cc = lax.broadcasted_iota(jnp.int32, (GROUP_W, GROUP_W), 1)
    head_shift = HEAD_SIZE.bit_length() - 1
    bd_mask = jnp.right_shift(rr, head_shift) == jnp.right_shift(cc, head_shift)
    tr = lax.broadcasted_iota(jnp.int32, (L, GROUP_W), 0)
    tc = jnp.bitwise_and(lax.broadcasted_iota(jnp.int32, (L, GROUP_W), 1), L - 1)
    eye = jnp.where(tc == tr, 1.0, 0.0)
    strict = {True: tc > tr, False: tc < tr}
    incl = {True: tc >= tr, False: tc <= tr}
    units = [(d, g) for d in range(len(preps)) for g in range(N_GROUPS)]
    sl = lambda g: slice(g * GROUP_W, (g + 1) * GROUP_W)
    ex = lambda x: _expand(x, bd_mask)

    lhs, a_ab, a_ak, a_rb, a_rk = {}, {}, {}, {}, {}
    for u in units:
        d, g = u
        p = preps[d]
        lhs[u] = jnp.concatenate([p["a_t"][:, sl(g)], p["r_t"][:, sl(g)]], axis=0)
        e_bk = jnp.concatenate([ex(p["b_t"][:, sl(g)]), ex(p["k_t"][:, sl(g)])], axis=0)
        a_all = _dot_nt(lhs[u], e_bk)
        a_ab[u] = jnp.where(strict[reverse[d]], a_all[0:L, 0:GROUP_W], 0.0)
        a_ak[u] = jnp.where(strict[reverse[d]], a_all[0:L, GROUP_W:], 0.0)
        a_rb[u] = jnp.where(incl[reverse[d]], a_all[L:, 0:GROUP_W], 0.0)
        a_rk[u] = jnp.where(incl[reverse[d]], a_all[L:, GROUP_W:], 0.0)
    av = {}
    for u in units:
        d, g = u
        av[u] = _dot(jnp.concatenate([a_ak[u], a_rk[u]], axis=0), ex(preps[d]["v"][:, sl(g)]))

    def nilpotent_inverse(n, index):
        prod = {u: eye + n[u] for u in units}
        npow = {u: _dot(n[u], ex(n[u])) for u in units}
        size = 2
        while size < index:
            for u in units:
                rhs = ex(npow[u])
                if 2 * size < index:
                    both = _dot(jnp.concatenate([npow[u], prod[u]], axis=0), rhs)
                    npow[u] = both[0:L]
                    prod[u] = prod[u] + both[L:]
                else:
                    prod[u] = prod[u] + _dot(prod[u], rhs)
            size *= 2
        return prod

    sh = INV_BLOCK.bit_length() - 1
    in_diag = jnp.right_shift(tr, sh) == jnp.right_shift(tc, sh)
    d_inv = nilpotent_inverse({u: jnp.where(in_diag, a_ab[u], 0.0) for u in units}, INV_BLOCK)
    e_mat = {u: _dot(d_inv[u], ex(jnp.where(in_diag, 0.0, a_ab[u]))) for u in units}
    e_inv = nilpotent_inverse(e_mat, L // INV_BLOCK)
    pinv = {u: _dot(e_inv[u], ex(d_inv[u])) for u in units}

    s_old, sa, x_u, u_u, y_u = {}, {}, {}, {}, {}
    for u in units:
        d, g = u
        s_old[u] = s_ref[s_idx[d] + (g,)]
        sa[u] = _dot(lhs[u], s_old[u]) + av[u]
    for u in units:
        x_u[u] = sa[u][0:L]
        u_u[u] = _dot(pinv[u], ex(x_u[u]))
    for u in units:
        y_u[u] = sa[u][L:] + _dot(a_rb[u], ex(u_u[u]))
    w_col = [jnp.transpose(jnp.broadcast_to(p["w_tot"], (LANES, RWKV_WIDTH))) for p in preps]
    for u in units:
        d, g = u
        p = preps[d]
        upd = _dot_tn(jnp.concatenate([p["b_h"][:, sl(g)], p["k_h"][:, sl(g)]], axis=0),
                      jnp.concatenate([u_u[u], p["v"][:, sl(g)]], axis=0))
        decay = jnp.concatenate([w_col[d][sl(g), :]] * (GROUP_W // LANES), axis=1)
        s_ref[s_idx[d] + (g,)] = s_old[u] * decay + jnp.where(bd_mask, upd, 0.0)
    return [jnp.concatenate([y_u[(d, g)] for g in range(N_GROUPS)], axis=1)
            for d in range(len(preps))]


def _scan_kernel(pf_ref, pb_ref, s0_ref, lw_ref, vec_ref, ho_ref, yf_ref, yb_ref, sfin_ref, s_scr):
    c = pl.program_id(1)

    @pl.when(c == 0)
    def _():
        s_scr[...] = s0_ref[...]

    n_streams = 2 * SCAN_BATCH
    reverse = [bool(i % 2) for i in range(n_streams)]
    s_idx = [(i // 2, i % 2) for i in range(n_streams)]
    blocks = []
    for bb in range(SCAN_BATCH):
        blocks += [(pf_ref[bb].astype(F32), False), (pb_ref[bb].astype(F32), True)]
    preps = _scan_prep(blocks, lw_ref, vec_ref, ho_ref[...])
    ys = _scan_units(preps, reverse, s_idx, s_scr)
    for bb in range(SCAN_BATCH):
        yf_ref[bb] = ys[2 * bb]
        yb_ref[bb] = ys[2 * bb + 1]

    @pl.when(c == pl.num_programs(1) - 1)
    def _():
        sfin_ref[...] = s_scr[...]


def _scan_call(p, s0, lora_w, vecs, head_ones):
    bsz, seq, _ = p.shape
    nc = seq // CHUNK
    state_block = (SCAN_BATCH, 2, N_GROUPS, GROUP_W, GROUP_W)
    state_map = lambda b, c: (b, 0, 0, 0, 0)
    return pl.pallas_call(
        _scan_kernel,
        grid=(bsz // SCAN_BATCH, nc),
        in_specs=[pl.BlockSpec((SCAN_BATCH, CHUNK, SCAN_COLS), lambda b, c: (b, c, 0)),
                  pl.BlockSpec((SCAN_BATCH, CHUNK, SCAN_COLS), lambda b, c: (b, nc - 1 - c, 0)),
                  pl.BlockSpec(state_block, state_map),
                  pl.BlockSpec((2, LORA_COLS, 2 * RWKV_WIDTH), lambda b, c: (0, 0, 0)),
                  pl.BlockSpec((2, 8, RWKV_WIDTH), lambda b, c: (0, 0, 0)),
                  pl.BlockSpec((GROUP_W, GROUP_W), lambda b, c: (0, 0))],
        out_specs=[pl.BlockSpec((SCAN_BATCH, CHUNK, RWKV_WIDTH), lambda b, c: (b, c, 0)),
                   pl.BlockSpec((SCAN_BATCH, CHUNK, RWKV_WIDTH), lambda b, c: (b, nc - 1 - c, 0)),
                   pl.BlockSpec(state_block, state_map)],
        out_shape=[jax.ShapeDtypeStruct((bsz, seq, RWKV_WIDTH), F32),
                   jax.ShapeDtypeStruct((bsz, seq, RWKV_WIDTH), F32),
                   jax.ShapeDtypeStruct((bsz, 2, N_GROUPS, GROUP_W, GROUP_W), F32)],
        scratch_shapes=[pltpu.VMEM(state_block, F32)],
        compiler_params=pltpu.CompilerParams(
            dimension_semantics=("parallel", "arbitrary"), vmem_limit_bytes=VMEM_LIMIT),
        name="wkv_scan",
    )(p, p, s0, lora_w, vecs, head_ones)


def _mix_kernel(x_ref, yf_ref, yb_ref, rkv_ref, sm_ref, cv_ref, mod_ref, rw_ref, vec_ref, ho_ref,
                cw_ref, cvec_ref, wout_ref, g_ref, o_ref, upad, *, tm):
    n_lines = tm // GRID_W
    ones_bd = ho_ref[...]
    y = yf_ref[0] + yb_ref[0]
    inv_n = 1.0 / HEAD_SIZE
    mu = _head_sum(y, ones_bd) * inv_n
    dy = y - mu
    var = _head_sum(dy * dy, ones_bd) * inv_n
    lnx_w = vec_ref[0:1, :]
    lnx_b = vec_ref[1:2, :]
    a0_f = vec_ref[2:3, :]
    a0_b = vec_ref[3:4, :]
    k_a = vec_ref[4:5, :]
    r_k = vec_ref[5:6, :]
    yn = dy * lax.rsqrt(var + EPS_GN) * lnx_w + lnx_b
    sm = sm_ref[0].astype(F32)
    lane = lax.broadcasted_iota(jnp.int32, sm.shape, 1)
    sm = jnp.where(lane >= LORA_COLS, _sigmoid(sm), sm)
    lin = _dot(sm, rw_ref[...])
    iclr_f = _sigmoid(lin[:, 0:RWKV_WIDTH] + a0_f)
    iclr_b = _sigmoid(lin[:, RWKV_WIDTH:2 * RWKV_WIDTH] + a0_b)
    gate = lin[:, 2 * RWKV_WIDTH:]
    r = rkv_ref[0, :, 0:RWKV_WIDTH].astype(F32)
    k = rkv_ref[0, :, RWKV_WIDTH:2 * RWKV_WIDTH].astype(F32)
    v = rkv_ref[0, :, 2 * RWKV_WIDTH:3 * RWKV_WIDTH].astype(F32)
    k_bar = k * (1.0 + (0.5 * (iclr_f + iclr_b) - 1.0) * k_a)
    bonus = _head_sum(r * k_bar * r_k, ones_bd) * v
    rw_out = (yn + bonus) * gate

    cv = cv_ref[0].astype(F32)
    u = cv[:, 0:CONV_WIDTH] * _sigmoid(cv[:, CONV_WIDTH:])
    zeros = jnp.zeros((CONV_LEAD, CONV_WIDTH), F32)
    for ln in range(n_lines):
        u_ln = u[ln * GRID_W:(ln + 1) * GRID_W, :]
        for s in range(SUBLANES):
            upad[s, ln, 0:CONV_LEAD, :] = zeros
            upad[s, ln, CONV_ROWS - CONV_LEAD:CONV_ROWS, :] = zeros
            upad[s, ln, CONV_LEAD - s:CONV_LEAD - s + GRID_W, :] = u_ln
    conv_b = cvec_ref[0:1, :]
    ln_w = cvec_ref[1:2, :]
    ln_b = cvec_ref[2:3, :]
    conv_lines = []
    for ln in range(n_lines):
        strips = []
        for lb in range(CONV_WIDTH // LANES):
            lanes = slice(lb * LANES, (lb + 1) * LANES)
            acc = jnp.zeros((GRID_W, LANES), F32)
            for j in range(CONV_KERNEL):
                off = CONV_LEAD - CONV_PAD + j
                q8 = off // SUBLANES * SUBLANES
                acc = acc + cw_ref[j:j + 1, lanes] * upad[off % SUBLANES, ln, q8:q8 + GRID_W, lanes]
            strips.append(acc)
        conv_lines.append(jnp.concatenate(strips, axis=1))
    yc = jnp.concatenate(conv_lines, axis=0) + conv_b
    mu_c = jnp.mean(yc, axis=-1, keepdims=True)
    dc = yc - mu_c
    var_c = jnp.mean(dc * dc, axis=-1, keepdims=True)
    conv_out = _silu(dc * lax.rsqrt(var_c + EPS_LN) * ln_w + ln_b)

    mix_in = jnp.concatenate([rw_out.astype(BF16), conv_out.astype(BF16)], axis=1)
    mix = jnp.dot(mix_in, wout_ref[...], preferred_element_type=F32)
    gate_mix = mod_ref[0, 2:3, :]
    o_ref[0] = x_ref[0] + gate_mix * _rms(mix, g_ref[...])


def _mix_call(x, y_f, y_b, p, mods, rw_w, rvecs, head_ones, conv_w, cvecs, w_out, post_g):
    bsz, seq, _ = x.shape
    tm = 512
    tok = lambda width, blk: pl.BlockSpec((1, tm, width), lambda b, t: (b, t, blk))
    const2 = lambda shape: pl.BlockSpec(shape, lambda b, t: (0, 0))
    return pl.pallas_call(
        functools.partial(_mix_kernel, tm=tm),
        grid=(bsz, seq // tm),
        in_specs=[tok(D_MODEL, 0), tok(RWKV_WIDTH, 0), tok(RWKV_WIDTH, 0),
                  tok(3 * RWKV_WIDTH, 0),
                  tok(512, LORA_OFF // 512),
                  tok(2 * CONV_WIDTH, CONV_OFF // (2 * CONV_WIDTH)),
                  pl.BlockSpec((1, N_MOD, D_MODEL), lambda b, t: (b, 0, 0)),
                  const2((512, 3 * RWKV_WIDTH)), const2((8, RWKV_WIDTH)),
                  const2((GROUP_W, GROUP_W)), const2((32, CONV_WIDTH)),
                  const2((8, CONV_WIDTH)), const2((D_MODEL, D_MODEL)), const2((1, D_MODEL))],
        out_specs=tok(D_MODEL, 0),
        out_shape=jax.ShapeDtypeStruct((bsz, seq, D_MODEL), F32),
        scratch_shapes=[pltpu.VMEM((SUBLANES, tm // GRID_W, CONV_ROWS, CONV_WIDTH), F32)],
        compiler_params=pltpu.CompilerParams(
            dimension_semantics=("parallel", "parallel"), vmem_limit_bytes=VMEM_LIMIT),
        name="mix_out",
    )(x, y_f, y_b, p, p, p, mods, rw_w, rvecs, head_ones, conv_w, cvecs, w_out, post_g)


def _mlp_kernel(x_ref, mod_ref, gpre_ref, gpost_ref, w1_ref, w2_ref, o_ref, acc_ref):
    x = x_ref[0]
    shift = mod_ref[0, 3:4, :]
    scale = mod_ref[0, 4:5, :]
    gate = mod_ref[0, 5:6, :]
    h = (_rms(x, gpre_ref[...]) * (1.0 + scale) + shift).astype(BF16)
    ff_blk = 1024
    for c in range(D_FF // ff_blk):
        hid = jnp.dot(h, w1_ref[:, c * ff_blk:(c + 1) * ff_blk], preferred_element_type=F32)
        hid = jnp.square(jnp.maximum(hid, 0.0)).astype(BF16)
        part = jnp.dot(hid, w2_ref[c * ff_blk:(c + 1) * ff_blk, :], preferred_element_type=F32)
        if c == 0:
            acc_ref[...] = part
        else:
            acc_ref[...] += part
    o_ref[0] = x + gate * _rms(acc_ref[...], gpost_ref[...])


def _mlp_call(x, mods, pre_g, post_g, w1, w2):
    bsz, seq, _ = x.shape
    tm = 512
    return pl.pallas_call(
        _mlp_kernel,
        grid=(bsz, seq // tm),
        in_specs=[pl.BlockSpec((1, tm, D_MODEL), lambda b, t: (b, t, 0)),
                  pl.BlockSpec((1, N_MOD, D_MODEL), lambda b, t: (b, 0, 0)),
                  pl.BlockSpec((1, D_MODEL), lambda b, t: (0, 0)),
                  pl.BlockSpec((1, D_MODEL), lambda b, t: (0, 0)),
                  pl.BlockSpec((D_MODEL, D_FF), lambda b, t: (0, 0)),
                  pl.BlockSpec((D_FF, D_MODEL), lambda b, t: (0, 0))],
        out_specs=pl.BlockSpec((1, tm, D_MODEL), lambda b, t: (b, t, 0)),
        out_shape=jax.ShapeDtypeStruct((bsz, seq, D_MODEL), F32),
        scratch_shapes=[pltpu.VMEM((tm, D_MODEL), F32)],
        compiler_params=pltpu.CompilerParams(
            dimension_semantics=("parallel", "parallel"), vmem_limit_bytes=VMEM_LIMIT),
        name="sqrelu_mlp",
    )(x, mods, pre_g, post_g, w1, w2)


def _pad_cols(a, lo, hi):
    pad = jnp.zeros(a.shape[:-1] + (GATE_PAD - GATE_RANK,), a.dtype)
    return jnp.concatenate([a[..., :lo], pad, a[..., lo:hi]], axis=-1)


def _layer(x, ctx, mods_x, mods_c, prm):
    (mix_pre_g, mix_post_g, mlp_pre_g, mlp_post_g, w_in, mu_prev, mu_next, decay_w0, decay_w2,
     iclr_a0, iclr_a2, k_k, k_a, r_k, gate_w2, lnx_w, lnx_b, conv_w, conv_b, conv_ln_w,
     conv_ln_b, w_out, mlp_w1, mlp_w2) = prm
    bsz = x.shape[0]
    shift_cols = GATE_OFF + GATE_RANK
    in_cols = w_in.shape[1]

    w_pad = _pad_cols(w_in, shift_cols, in_cols).astype(BF16)
    zc = jnp.zeros((2 * CONV_WIDTH,), F32)
    mp_pad = _pad_cols(jnp.concatenate([mu_prev, zc]), shift_cols, in_cols)[None, :]
    mn_pad = _pad_cols(jnp.concatenate([mu_next, zc]), shift_cols, in_cols)[None, :]
    zr = jnp.zeros((DECAY_RANK, RWKV_WIDTH), F32)
    lora_dir = []
    for d in range(2):
        dec_rows = [zr, zr]
        dec_rows[d] = decay_w2[d]
        icl_rows = [zr, zr]
        icl_rows[d] = iclr_a2[d]
        left = jnp.concatenate(dec_rows + [zr, zr], axis=0)
        right = jnp.concatenate([zr, zr] + icl_rows, axis=0)
        lora_dir.append(jnp.concatenate([left, right], axis=1))
    lora_w = jnp.stack(lora_dir).astype(BF16)
    z4 = jnp.zeros((4, RWKV_WIDTH), F32)
    scan_vecs = jnp.stack([jnp.concatenate([decay_w0[d][None], iclr_a0[d][None], k_k[None],
                                            k_a[None], z4], axis=0) for d in range(2)])
    hid = jnp.arange(RWKV_WIDTH) // HEAD_SIZE
    head_ones = (hid[:, None] == hid[None, :]).astype(BF16)
    zl = jnp.zeros((2 * DECAY_RANK, 3 * RWKV_WIDTH), F32)
    za = jnp.zeros((ICLR_RANK, RWKV_WIDTH), F32)
    zg = jnp.zeros((GATE_PAD, RWKV_WIDTH), F32)
    gate_pad = jnp.concatenate([gate_w2, jnp.zeros((GATE_PAD - GATE_RANK, RWKV_WIDTH), F32)], axis=0)
    rw_w = jnp.concatenate([
        zl,
        jnp.concatenate([iclr_a2[0], za, za], axis=1),
        jnp.concatenate([za, iclr_a2[1], za], axis=1),
        jnp.concatenate([zg, zg, gate_pad], axis=1)], axis=0).astype(BF16)
    rvecs = jnp.concatenate([lnx_w[None], lnx_b[None], iclr_a0[0][None], iclr_a0[1][None],
                             k_a[None], r_k.reshape(1, RWKV_WIDTH), jnp.zeros((2, RWKV_WIDTH), F32)],
                            axis=0)
    conv_w_pad = jnp.concatenate([conv_w, jnp.zeros((1, CONV_WIDTH), F32)], axis=0)
    cvecs = jnp.concatenate([conv_b[None], conv_ln_w[None], conv_ln_b[None],
                             jnp.zeros((5, CONV_WIDTH), F32)], axis=0)

    seq, ctx_len = x.shape[1], ctx.shape[1]
    p_x = _proj_call(x, mods_x, mix_pre_g[None], w_pad, mp_pad, mn_pad, P_COLS // PROJ_BLOCK, seq)
    pack = seq // ctx_len
    p_c = _proj_call(ctx.reshape(bsz // pack, seq, D_MODEL), mods_c[:bsz // pack], mix_pre_g[None],
                     w_pad, mp_pad, mn_pad, N_SHIFT_BLOCKS, ctx_len).reshape(bsz, ctx_len, -1)
    s_zero = jnp.zeros((bsz, 2, N_GROUPS, GROUP_W, GROUP_W), F32)
    _, _, s_ctx = _scan_call(p_c, s_zero, lora_w, scan_vecs, head_ones)
    y_f, y_b, _ = _scan_call(p_x, s_ctx, lora_w, scan_vecs, head_ones)
    x = _mix_call(x, y_f, y_b, p_x, mods_x, rw_w, rvecs, head_ones, conv_w_pad, cvecs,
                  w_out.astype(BF16), mix_post_g[None])
    x = _mlp_call(x, mods_x, mlp_pre_g[None], mlp_post_g[None], mlp_w1.astype(BF16),
                  mlp_w2.astype(BF16))
    return x


def kernel(x, c, ctx, c_ctx, ada_w, ada_b, mix_pre_g, mix_post_g, mlp_pre_g, mlp_post_g, w_in, mu_prev, mu_next, decay_w0, decay_w2, iclr_a0, iclr_a2, k_k, k_a, r_k, gate_w2, lnx_w, lnx_b, conv_w, conv_b, conv_ln_w, conv_ln_b, w_out, mlp_w1, mlp_w2):
    depth = ada_w.shape[0]
    assert depth == 1, "context-stream update between layers is not implemented"
    bsz = x.shape[0]
    rows = 8 * ((bsz + 1 + 7) // 8)
    cc = jnp.concatenate([c, c_ctx[None, :], jnp.zeros((rows - bsz - 1, D_MODEL), F32)], axis=0)
    per_layer = (mix_pre_g, mix_post_g, mlp_pre_g, mlp_post_g, w_in, mu_prev, mu_next, decay_w0,
                 decay_w2, iclr_a0, iclr_a2, k_k, k_a, r_k, gate_w2, lnx_w, lnx_b, conv_w, conv_b,
                 conv_ln_w, conv_ln_b, w_out, mlp_w1, mlp_w2)
    for l in range(depth):
        mods = _ada_call(cc, ada_w[l], ada_b[l][None, :]).reshape(rows, N_MOD, D_MODEL)
        mods_x = mods[:bsz]
        mods_c = jnp.broadcast_to(mods[bsz:bsz + 1], (bsz, N_MOD, D_MODEL))
        x = _layer(x, ctx, mods_x, mods_c, tuple(a[l] for a in per_layer))
    return x
```

```python
import functools

import jax
import jax.numpy as jnp
from jax import lax
from jax.experimental import pallas as pl
from jax.experimental.pallas import tpu as pltpu

F32 = jnp.float32
BF16 = jnp.bfloat16

D_MODEL = 1024
RWKV_WIDTH = 512
CONV_WIDTH = 512
HEAD_SIZE = 64
RWKV_HEADS = 8
DECAY_RANK = 64
ICLR_RANK = 64
GATE_RANK = 160
CONV_KERNEL = 31
CONV_PAD = CONV_KERNEL // 2
GRID_W = 64
SUBLANES = 8
CONV_LEAD = 2 * SUBLANES
CONV_ROWS = (CONV_LEAD + CONV_PAD) // SUBLANES * SUBLANES + GRID_W
D_FF = 4 * D_MODEL
N_MOD = 6
EPS_RMS = 1e-6
EPS_LN = 1e-5
EPS_GN = 64e-5
DECAY_SCALE = 0.6065306597126334
LOG2_E = 1.4426950408889634

LORA_OFF = 3 * RWKV_WIDTH
LORA_COLS = 256
GATE_OFF = LORA_OFF + LORA_COLS
GATE_PAD = 256
CONV_OFF = GATE_OFF + GATE_PAD
P_COLS = CONV_OFF + 2 * CONV_WIDTH
SCAN_COLS = GATE_OFF
PROJ_BLOCK = 1024
N_SHIFT_BLOCKS = CONV_OFF // PROJ_BLOCK

CHUNK = 64
INV_BLOCK = 8
GROUP_HEADS = 4
GROUP_W = GROUP_HEADS * HEAD_SIZE
N_GROUPS = RWKV_HEADS // GROUP_HEADS
SCAN_BATCH = 4
SCAN_CHUNKS = 2

LANES = 128
VMEM_LIMIT = 56 * 1024 * 1024


def _sigmoid(x):
    return 0.5 * jnp.tanh(0.5 * x) + 0.5


def _silu(x):
    return x * _sigmoid(x)


def _rms(x, g):
    return x * lax.rsqrt(jnp.mean(x * x, axis=-1, keepdims=True) + EPS_RMS) * g


def _dot(a, b):
    return jnp.dot(a.astype(BF16), b.astype(BF16), preferred_element_type=F32)


def _dot_nt(a, b):
    return lax.dot_general(a.astype(BF16), b.astype(BF16), (((1,), (1,)), ((), ())),
                           preferred_element_type=F32)


def _dot_tn(a, b):
    return lax.dot_general(a.astype(BF16), b.astype(BF16), (((0,), (0,)), ((), ())),
                           preferred_element_type=F32)


def _dot_split_rhs(a_bf16, x):
    hi = x.astype(BF16)
    lo = (x - hi.astype(F32)).astype(BF16)
    d = lambda p: jnp.dot(a_bf16, p, preferred_element_type=F32)
    return d(hi) + d(lo)


def _head_sum(z, ones_bd):
    rows = z.shape[0]
    hi = z.astype(BF16)
    lo = (z - hi.astype(F32)).astype(BF16)
    pieces = []
    for g in range(N_GROUPS):
        pieces += [hi[:, g * GROUP_W:(g + 1) * GROUP_W], lo[:, g * GROUP_W:(g + 1) * GROUP_W]]
    res = jnp.dot(jnp.concatenate(pieces, axis=0), ones_bd, preferred_element_type=F32)
    return jnp.concatenate(
        [res[(2 * g) * rows:(2 * g + 1) * rows] + res[(2 * g + 1) * rows:(2 * g + 2) * rows]
         for g in range(N_GROUPS)], axis=1)


def _ada_kernel(c_ref, w_ref, b_ref, o_ref):
    o_ref[...] = _dot(_silu(c_ref[...]), w_ref[...]) + b_ref[...]


def _ada_call(cc, ada_w, ada_b):
    rows = cc.shape[0]
    n_out = ada_w.shape[1]
    blk = 1536
    return pl.pallas_call(
        _ada_kernel,
        grid=(n_out // blk,),
        in_specs=[pl.BlockSpec((rows, D_MODEL), lambda n: (0, 0)),
                  pl.BlockSpec((D_MODEL, blk), lambda n: (0, n)),
                  pl.BlockSpec((1, blk), lambda n: (0, n))],
        out_specs=pl.BlockSpec((rows, blk), lambda n: (0, n)),
        out_shape=jax.ShapeDtypeStruct((rows, n_out), F32),
        compiler_params=pltpu.CompilerParams(vmem_limit_bytes=VMEM_LIMIT),
        name="ada_mod",
    )(cc, ada_w, ada_b)


def _proj_kernel(x_ref, mod_ref, g_ref, w_ref, mp_ref, mn_ref, o_ref, h_scr, z_scr, *, seq, seg,
                 col_of_step):
    n = pl.program_id(1)
    col = col_of_step(n)
    row_chunk = min(seq, 1024)

    @pl.when(n == 0)
    def _():
        shift = mod_ref[0, 0:1, :]
        scale = mod_ref[0, 1:2, :]
        g = g_ref[...]
        for s in range(0, seq, row_chunk):
            xs = x_ref[0, s:s + row_chunk, :]
            h_scr[s:s + row_chunk, :] = (_rms(xs, g) * (1.0 + scale) + shift).astype(BF16)

    @pl.when(col < N_SHIFT_BLOCKS)
    def _():
        lead = SUBLANES
        z_scr[0:lead, :] = jnp.zeros((lead, PROJ_BLOCK), F32)
        z_scr[lead + seq:2 * lead + seq, :] = jnp.zeros((lead, PROJ_BLOCK), F32)
        mp = mp_ref[...]
        mn = mn_ref[...]

        def shift_rows(s):
            cur = z_scr[lead + s:lead + s + row_chunk, :]
            prv = z_scr[lead - 1 + s:lead - 1 + s + row_chunk, :]
            nxt = z_scr[lead + 1 + s:lead + 1 + s + row_chunk, :]
            if seg < seq:
                pos = jnp.bitwise_and(
                    s + lax.broadcasted_iota(jnp.int32, (row_chunk, PROJ_BLOCK), 0), seg - 1)
                prv = jnp.where(pos == 0, 0.0, prv)
                nxt = jnp.where(pos == seg - 1, 0.0, nxt)
            o_ref[0, s:s + row_chunk, :] = (cur + mp * (prv - cur) + mn * (nxt - cur)).astype(BF16)

        for s in range(0, seq, row_chunk):
            z_scr[lead + s:lead + s + row_chunk, :] = jnp.dot(
                h_scr[s:s + row_chunk, :], w_ref[...], preferred_element_type=F32)
            if s > 0:
                shift_rows(s - row_chunk)
        shift_rows(seq - row_chunk)

    @pl.when(col >= N_SHIFT_BLOCKS)
    def _():
        for s in range(0, seq, row_chunk):
            o_ref[0, s:s + row_chunk, :] = jnp.dot(
                h_scr[s:s + row_chunk, :], w_ref[...], preferred_element_type=F32).astype(BF16)


def _proj_call(x, mods, g, w_pad, mp_pad, mn_pad, n_blocks, seg):
    bsz, seq, _ = x.shape
    assert seq % seg == 0 and seg & (seg - 1) == 0
    n_cols = n_blocks * PROJ_BLOCK
    first = N_SHIFT_BLOCKS % n_blocks
    col_of_step = lambda n: lax.rem(n + first, n_blocks)
    return pl.pallas_call(
        functools.partial(_proj_kernel, seq=seq, seg=seg, col_of_step=col_of_step),
        grid=(bsz, n_blocks),
        in_specs=[pl.BlockSpec((1, seq, D_MODEL), lambda b, n: (b, 0, 0)),
                  pl.BlockSpec((1, N_MOD, D_MODEL), lambda b, n: (b, 0, 0)),
                  pl.BlockSpec((1, D_MODEL), lambda b, n: (0, 0)),
                  pl.BlockSpec((D_MODEL, PROJ_BLOCK), lambda b, n: (0, col_of_step(n))),
                  pl.BlockSpec((1, PROJ_BLOCK), lambda b, n: (0, col_of_step(n))),
                  pl.BlockSpec((1, PROJ_BLOCK), lambda b, n: (0, col_of_step(n)))],
        out_specs=pl.BlockSpec((1, seq, PROJ_BLOCK), lambda b, n: (b, 0, col_of_step(n))),
        out_shape=jax.ShapeDtypeStruct((bsz, seq, n_cols), BF16),
        scratch_shapes=[pltpu.VMEM((seq, D_MODEL), BF16),
                        pltpu.VMEM((seq + 2 * SUBLANES, PROJ_BLOCK), F32)],
        compiler_params=pltpu.CompilerParams(
            dimension_semantics=("parallel", "arbitrary"), vmem_limit_bytes=VMEM_LIMIT),
        name="in_proj",
    )(x, mods, g, w_pad, mp_pad, mn_pad)


def _expand(x, bd_mask):
    return jnp.where(bd_mask, jnp.concatenate([x] * GROUP_HEADS, axis=0), 0.0)


def _scan_prep(blocks, lw_ref, vec_ref, ones_bd):
    L = CHUNK
    n = len(blocks)
    lane = lax.broadcasted_iota(jnp.int32, (L, LORA_COLS), 1)
    loras = []
    for pb, _ in blocks:
        lora = pb[:, LORA_OFF:LORA_OFF + LORA_COLS]
        loras.append(jnp.where(lane < 2 * DECAY_RANK, jnp.tanh(lora), lora))
    lin = [None] * n
    for d in range(2):
        idx = [i for i in range(n) if blocks[i][1] == bool(d)]
        res = _dot(jnp.concatenate([loras[i] for i in idx], axis=0), lw_ref[d])
        for j, i in enumerate(idx):
            lin[i] = res[j * L:(j + 1) * L]

    kk0s, pieces = [], []
    for i, (pb, reverse) in enumerate(blocks):
        kk0 = pb[:, RWKV_WIDTH:2 * RWKV_WIDTH] * vec_ref[int(reverse), 2:3, :]
        kk0s.append(kk0)
        sq = (kk0 * kk0).astype(BF16)
        pieces += [sq[:, g * GROUP_W:(g + 1) * GROUP_W] for g in range(N_GROUPS)]
    sums = jnp.dot(jnp.concatenate(pieces, axis=0), ones_bd, preferred_element_type=F32)

    row = lax.broadcasted_iota(jnp.int32, (L, L), 0)
    col = lax.broadcasted_iota(jnp.int32, (L, L), 1)
    preps = []
    for i, (pb, reverse) in enumerate(blocks):
        vecs = vec_ref[int(reverse)]
        w0 = vecs[0:1, :]
        a0 = vecs[1:2, :]
        k_a = vecs[3:4, :]
        r = pb[:, 0:RWKV_WIDTH]
        k = pb[:, RWKV_WIDTH:2 * RWKV_WIDTH]
        v = pb[:, 2 * RWKV_WIDTH:3 * RWKV_WIDTH]
        half = 0.5 * LOG2_E * DECAY_SCALE
        lw = -half * jnp.tanh(0.5 * (lin[i][:, 0:RWKV_WIDTH] + w0)) - half
        neg_iclr = -0.5 * jnp.tanh(0.5 * (lin[i][:, RWKV_WIDTH:] + a0)) - 0.5
        base = i * N_GROUPS * L
        ss = jnp.concatenate(
            [sums[base + g * L:base + (g + 1) * L] for g in range(N_GROUPS)], axis=1)
        kk = kk0s[i] * lax.rsqrt(jnp.maximum(ss, 1e-24))
        kd = k * (1.0 - (neg_iclr + 1.0) * k_a)
        na = kk
        nb = kk * neg_iclr
        tri = jnp.where((col >= row) if reverse else (col <= row), 1.0, 0.0).astype(BF16)
        cum = _dot_split_rhs(tri, lw)
        tot = cum[0:1, :] if reverse else cum[L - 1:L, :]
        e_incl = jnp.exp2(cum)
        e_neg = 1.0 / e_incl
        w_tot = jnp.exp2(tot)
        e_rem = w_tot * e_neg
        preps.append(dict(
            r_t=r * e_incl, a_t=na * jnp.exp2(cum - lw), b_t=nb * e_neg, k_t=kd * e_neg,
            b_h=nb * e_rem, k_h=kd * e_rem, v=v, w_tot=w_tot))
    return preps


def _scan_units(preps, reverse, s_idx, s_ref):
    L = CHUNK
    rr = lax.broadcasted_iota(jnp.int32, (GROUP_W, GROUP_W), 0)
    cc = lax.broadcasted_iota(jnp.int32, (GROUP_W, GROUP_W), 1)
    head_shift = HEAD_SIZE.bit_length() - 1
    bd_mask = jnp.right_shift(rr, head_shift) == jnp.right_shift(cc, head_shift)
    tr = lax.broadcasted_iota(jnp.int32, (L, GROUP_W), 0)
    tc = jnp.bitwise_and(lax.broadcasted_iota(jnp.int32, (L, GROUP_W), 1), L - 1)
    eye = jnp.where(tc == tr, 1.0, 0.0)
    strict = {True: tc > tr, False: tc < tr}
    incl = {True: tc >= tr, False: tc <= tr}
    units = [(d, g) for d in range(len(preps)) for g in range(N_GROUPS)]
    sl = lambda g: slice(g * GROUP_W, (g + 1) * GROUP_W)
    ex = lambda x: _expand(x, bd_mask)

    lhs, a_ab, a_ak, a_rb, a_rk = {}, {}, {}, {}, {}
    for u in units:
        d, g = u
        p = preps[d]
        lhs[u] = jnp.concatenate([p["a_t"][:, sl(g)], p["r_t"][:, sl(g)]], axis=0)
        e_bk = jnp.concatenate([ex(p["b_t"][:, sl(g)]), ex(p["k_t"][:, sl(g)])], axis=0)
        a_all = _dot_nt(lhs[u], e_bk)
        a_ab[u] = jnp.where(strict[reverse[d]], a_all[0:L, 0:GROUP_W], 0.0)
        a_ak[u] = jnp.where(strict[reverse[d]], a_all[0:L, GROUP_W:], 0.0)
        a_rb[u] = jnp.where(incl[reverse[d]], a_all[L:, 0:GROUP_W], 0.0)
        a_rk[u] = jnp.where(incl[reverse[d]], a_all[L:, GROUP_W:], 0.0)
    av = {}
    for u in units:
        d, g = u
        av[u] = _dot(jnp.concatenate([a_ak[u], a_rk[u]], axis=0), ex(preps[d]["v"][:, sl(g)]))

    def nilpotent_inverse(n, index):
        prod = {u: eye + n[u] for u in units}
        npow = {u: _dot(n[u], ex(n[u])) for u in units}
        size = 2
        while size < index:
            for u in units:
                rhs = ex(npow[u])
                if 2 * size < index:
                    both = _dot(jnp.concatenate([npow[u], prod[u]], axis=0), rhs)
                    npow[u] = both[0:L]
                    prod[u] = prod[u] + both[L:]
                else:
                    prod[u] = prod[u] + _dot(prod[u], rhs)
            size *= 2
        return prod

    sh = INV_BLOCK.bit_length() - 1
    in_diag = jnp.right_shift(tr, sh) == jnp.right_shift(tc, sh)
    d_inv = nilpotent_inverse({u: jnp.where(in_diag, a_ab[u], 0.0) for u in units}, INV_BLOCK)
    e_mat = {u: _dot(d_inv[u], ex(jnp.where(in_diag, 0.0, a_ab[u]))) for u in units}
    e_inv = nilpotent_inverse(e_mat, L // INV_BLOCK)
    pinv = {u: _dot(e_inv[u], ex(d_inv[u])) for u in units}

    s_old, sa, x_u, u_u, y_u = {}, {}, {}, {}, {}
    for u in units:
        d, g = u
        s_old[u] = s_ref[s_idx[d] + (g,)]
        sa[u] = _dot(lhs[u], s_old[u]) + av[u]
    for u in units:
        x_u[u] = sa[u][0:L]
        u_u[u] = _dot(pinv[u], ex(x_u[u]))
    for u in units:
        y_u[u] = sa[u][L:] + _dot(a_rb[u], ex(u_u[u]))
    w_col = [jnp.transpose(jnp.broadcast_to(p["w_tot"], (LANES, RWKV_WIDTH))) for p in preps]
    for u in units:
        d, g = u
        p = preps[d]
        upd = _dot_tn(jnp.concatenate([p["b_h"][:, sl(g)], p["k_h"][:, sl(g)]], axis=0),
                      jnp.concatenate([u_u[u], p["v"][:, sl(g)]], axis=0))
        decay = jnp.concatenate([w_col[d][sl(g), :]] * (GROUP_W // LANES), axis=1)
        s_ref[s_idx[d] + (g,)] = s_old[u] * decay + jnp.where(bd_mask, upd, 0.0)
    return [jnp.concatenate([y_u[(d, g)] for g in range(N_GROUPS)], axis=1)
            for d in range(len(preps))]


def _scan_kernel(pf_ref, pb_ref, s0_ref, lw_ref, vec_ref, ho_ref, yf_ref, yb_ref, sfin_ref, s_scr):
    c = pl.program_id(1)

    @pl.when(c == 0)
    def _():
        s_scr[...] = s0_ref[...]

    n_streams = 2 * SCAN_BATCH
    reverse = [bool(i % 2) for i in range(n_streams)]
    s_idx = [(i // 2, i % 2) for i in range(n_streams)]
    preps = []
    for k in range(SCAN_CHUNKS):
        rf = slice(k * CHUNK, (k + 1) * CHUNK)
        rb = slice((SCAN_CHUNKS - 1 - k) * CHUNK, (SCAN_CHUNKS - k) * CHUNK)
        blocks = []
        for bb in range(SCAN_BATCH):
            blocks += [(pf_ref[bb, rf, :].astype(F32), False), (pb_ref[bb, rb, :].astype(F32), True)]
        preps.append((rf, rb, _scan_prep(blocks, lw_ref, vec_ref, ho_ref[...])))
    for rf, rb, prep in preps:
        ys = _scan_units(prep, reverse, s_idx, s_scr)
        for bb in range(SCAN_BATCH):
            yf_ref[bb, rf, :] = ys[2 * bb]
            yb_ref[bb, rb, :] = ys[2 * bb + 1]

    @pl.when(c == pl.num_programs(1) - 1)
    def _():
        sfin_ref[...] = s_scr[...]


def _scan_call(p, s0, lora_w, vecs, head_ones):
    bsz, seq, _ = p.shape
    rows = SCAN_CHUNKS * CHUNK
    nc = seq // rows
    state_block = (SCAN_BATCH, 2, N_GROUPS, GROUP_W, GROUP_W)
    state_map = lambda b, c: (b, 0, 0, 0, 0)
    return pl.pallas_call(
        _scan_kernel,
        grid=(bsz // SCAN_BATCH, nc),
        in_specs=[pl.BlockSpec((SCAN_BATCH, rows, SCAN_COLS), lambda b, c: (b, c, 0)),
                  pl.BlockSpec((SCAN_BATCH, rows, SCAN_COLS), lambda b, c: (b, nc - 1 - c, 0)),
                  pl.BlockSpec(state_block, state_map),
                  pl.BlockSpec((2, LORA_COLS, 2 * RWKV_WIDTH), lambda b, c: (0, 0, 0)),
                  pl.BlockSpec((2, 8, RWKV_WIDTH), lambda b, c: (0, 0, 0)),
                  pl.BlockSpec((GROUP_W, GROUP_W), lambda b, c: (0, 0))],
        out_specs=[pl.BlockSpec((SCAN_BATCH, rows, RWKV_WIDTH), lambda b, c: (b, c, 0)),
                   pl.BlockSpec((SCAN_BATCH, rows, RWKV_WIDTH), lambda b, c: (b, nc - 1 - c, 0)),
                   pl.BlockSpec(state_block, state_map)],
        out_shape=[jax.ShapeDtypeStruct((bsz, seq, RWKV_WIDTH), F32),
                   jax.ShapeDtypeStruct((bsz, seq, RWKV_WIDTH), F32),
                   jax.ShapeDtypeStruct((bsz, 2, N_GROUPS, GROUP_W, GROUP_W), F32)],
        scratch_shapes=[pltpu.VMEM(state_block, F32)],
        compiler_params=pltpu.CompilerParams(
            dimension_semantics=("parallel", "arbitrary"), vmem_limit_bytes=VMEM_LIMIT),
        name="wkv_scan",
    )(p, p, s0, lora_w, vecs, head_ones)


def _mix_kernel(x_ref, yf_ref, yb_ref, rkv_ref, sm_ref, cv_ref, mod_ref, rw_ref, vec_ref, ho_ref,
                cw_ref, cvec_ref, wout_ref, g_ref, o_ref, upad, *, tm):
    n_lines = tm // GRID_W
    ones_bd = ho_ref[...]
    y = yf_ref[0] + yb_ref[0]
    inv_n = 1.0 / HEAD_SIZE
    mu = _head_sum(y, ones_bd) * inv_n
    dy = y - mu
    var = _head_sum(dy * dy, ones_bd) * inv_n
    lnx_w = vec_ref[0:1, :]
    lnx_b = vec_ref[1:2, :]
    a0_f = vec_ref[2:3, :]
    a0_b = vec_ref[3:4, :]
    k_a = vec_ref[4:5, :]
    r_k = vec_ref[5:6, :]
    yn = dy * lax.rsqrt(var + EPS_GN) * lnx_w + lnx_b
    sm = sm_ref[0].astype(F32)
    lane = lax.broadcasted_iota(jnp.int32, sm.shape, 1)
    sm = jnp.where(lane >= LORA_COLS, _sigmoid(sm), sm)
    lin = _dot(sm, rw_ref[...])
    iclr_f = _sigmoid(lin[:, 0:RWKV_WIDTH] + a0_f)
    iclr_b = _sigmoid(lin[:, RWKV_WIDTH:2 * RWKV_WIDTH] + a0_b)
    gate = lin[:, 2 * RWKV_WIDTH:]
    r = rkv_ref[0, :, 0:RWKV_WIDTH].astype(F32)
    k = rkv_ref[0, :, RWKV_WIDTH:2 * RWKV_WIDTH].astype(F32)
    v = rkv_ref[0, :, 2 * RWKV_WIDTH:3 * RWKV_WIDTH].astype(F32)
    k_bar = k * (1.0 + (0.5 * (iclr_f + iclr_b) - 1.0) * k_a)
    bonus = _head_sum(r * k_bar * r_k, ones_bd) * v
    rw_out = (yn + bonus) * gate

    cv = cv_ref[0].astype(F32)
    u = cv[:, 0:CONV_WIDTH] * _sigmoid(cv[:, CONV_WIDTH:])
    zeros = jnp.zeros((CONV_LEAD, CONV_WIDTH), F32)
    for ln in range(n_lines):
        u_ln = u[ln * GRID_W:(ln + 1) * GRID_W, :]
        for s in range(SUBLANES):
            upad[s, ln, 0:CONV_LEAD, :] = zeros
            upad[s, ln, CONV_ROWS - CONV_LEAD:CONV_ROWS, :] = zeros
            upad[s, ln, CONV_LEAD - s:CONV_LEAD - s + GRID_W, :] = u_ln
    conv_b = cvec_ref[0:1, :]
    ln_w = cvec_ref[1:2, :]
    ln_b = cvec_ref[2:3, :]
    conv_lines = []
    for ln in range(n_lines):
        strips = []
        for lb in range(CONV_WIDTH // LANES):
            lanes = slice(lb * LANES, (lb + 1) * LANES)
            acc = jnp.zeros((GRID_W, LANES), F32)
            for j in range(CONV_KERNEL):
                off = CONV_LEAD - CONV_PAD + j
                q8 = off // SUBLANES * SUBLANES
                acc = acc + cw_ref[j:j + 1, lanes] * upad[off % SUBLANES, ln, q8:q8 + GRID_W, lanes]
            strips.append(acc)
        conv_lines.append(jnp.concatenate(strips, axis=1))
    yc = jnp.concatenate(conv_lines, axis=0) + conv_b
    mu_c = jnp.mean(yc, axis=-1, keepdims=True)
    dc = yc - mu_c
    var_c = jnp.mean(dc * dc, axis=-1, keepdims=True)
    conv_out = _silu(dc * lax.rsqrt(var_c + EPS_LN) * ln_w + ln_b)

    mix_in = jnp.concatenate([rw_out.astype(BF16), conv_out.astype(BF16)], axis=1)
    mix = jnp.dot(mix_in, wout_ref[...], preferred_element_type=F32)
    gate_mix = mod_ref[0, 2:3, :]
    o_ref[0] = x_ref[0] + gate_mix * _rms(mix, g_ref[...])


def _mix_call(x, y_f, y_b, p, mods, rw_w, rvecs, head_ones, conv_w, cvecs, w_out, post_g):
    bsz, seq, _ = x.shape
    tm = 512
    tok = lambda width, blk: pl.BlockSpec((1, tm, width), lambda b, t: (b, t, blk))
    const2 = lambda shape: pl.BlockSpec(shape, lambda b, t: (0, 0))
    return pl.pallas_call(
        functools.partial(_mix_kernel, tm=tm),
        grid=(bsz, seq // tm),
        in_specs=[tok(D_MODEL, 0), tok(RWKV_WIDTH, 0), tok(RWKV_WIDTH, 0),
                  tok(3 * RWKV_WIDTH, 0),
                  tok(512, LORA_OFF // 512),
                  tok(2 * CONV_WIDTH, CONV_OFF // (2 * CONV_WIDTH)),
                  pl.BlockSpec((1, N_MOD, D_MODEL), lambda b, t: (b, 0, 0)),
                  const2((512, 3 * RWKV_WIDTH)), const2((8, RWKV_WIDTH)),
                  const2((GROUP_W, GROUP_W)), const2((32, CONV_WIDTH)),
                  const2((8, CONV_WIDTH)), const2((D_MODEL, D_MODEL)), const2((1, D_MODEL))],
        out_specs=tok(D_MODEL, 0),
        out_shape=jax.ShapeDtypeStruct((bsz, seq, D_MODEL), F32),
        scratch_shapes=[pltpu.VMEM((SUBLANES, tm // GRID_W, CONV_ROWS, CONV_WIDTH), F32)],
        compiler_params=pltpu.CompilerParams(
            dimension_semantics=("parallel", "parallel"), vmem_limit_bytes=VMEM_LIMIT),
        name="mix_out",
    )(x, y_f, y_b, p, p, p, mods, rw_w, rvecs, head_ones, conv_w, cvecs, w_out, post_g)


def _mlp_kernel(x_ref, mod_ref, gpre_ref, gpost_ref, w1_ref, w2_ref, o_ref, acc_ref):
    x = x_ref[0]
    shift = mod_ref[0, 3:4, :]
    scale = mod_ref[0, 4:5, :]
    gate = mod_ref[0, 5:6, :]
    h = (_rms(x, gpre_ref[...]) * (1.0 + scale) + shift).astype(BF16)
    ff_blk = 1024
    for c in range(D_FF // ff_blk):
        hid = jnp.dot(h, w1_ref[:, c * ff_blk:(c + 1) * ff_blk], preferred_element_type=F32)
        hid = jnp.square(jnp.maximum(hid, 0.0)).astype(BF16)
        part = jnp.dot(hid, w2_ref[c * ff_blk:(c + 1) * ff_blk, :], preferred_element_type=F32)
        if c == 0:
            acc_ref[...] = part
        else:
            acc_ref[...] += part
    o_ref[0] = x + gate * _rms(acc_ref[...], gpost_ref[...])


def _mlp_call(x, mods, pre_g, post_g, w1, w2):
    bsz, seq, _ = x.shape
    tm = 512
    return pl.pallas_call(
        _mlp_kernel,
        grid=(bsz, seq // tm),
        in_specs=[pl.BlockSpec((1, tm, D_MODEL), lambda b, t: (b, t, 0)),
                  pl.BlockSpec((1, N_MOD, D_MODEL), lambda b, t: (b, 0, 0)),
                  pl.BlockSpec((1, D_MODEL), lambda b, t: (0, 0)),
                  pl.BlockSpec((1, D_MODEL), lambda b, t: (0, 0)),
                  pl.BlockSpec((D_MODEL, D_FF), lambda b, t: (0, 0)),
                  pl.BlockSpec((D_FF, D_MODEL), lambda b, t: (0, 0))],
        out_specs=pl.BlockSpec((1, tm, D_MODEL), lambda b, t: (b, t, 0)),
        out_shape=jax.ShapeDtypeStruct((bsz, seq, D_MODEL), F32),
        scratch_shapes=[pltpu.VMEM((tm, D_MODEL), F32)],
        compiler_params=pltpu.CompilerParams(
            dimension_semantics=("parallel", "parallel"), vmem_limit_bytes=VMEM_LIMIT),
        name="sqrelu_mlp",
    )(x, mods, pre_g, post_g, w1, w2)


def _pad_cols(a, lo, hi):
    pad = jnp.zeros(a.shape[:-1] + (GATE_PAD - GATE_RANK,), a.dtype)
    return jnp.concatenate([a[..., :lo], pad, a[..., lo:hi]], axis=-1)


def _layer(x, ctx, mods_x, mods_c, prm):
    (mix_pre_g, mix_post_g, mlp_pre_g, mlp_post_g, w_in, mu_prev, mu_next, decay_w0, decay_w2,
     iclr_a0, iclr_a2, k_k, k_a, r_k, gate_w2, lnx_w, lnx_b, conv_w, conv_b, conv_ln_w,
     conv_ln_b, w_out, mlp_w1, mlp_w2) = prm
    bsz = x.shape[0]
    shift_cols = GATE_OFF + GATE_RANK
    in_cols = w_in.shape[1]

    w_pad = _pad_cols(w_in, shift_cols, in_cols).astype(BF16)
    zc = jnp.zeros((2 * CONV_WIDTH,), F32)
    mp_pad = _pad_cols(jnp.concatenate([mu_prev, zc]), shift_cols, in_cols)[None, :]
    mn_pad = _pad_cols(jnp.concatenate([mu_next, zc]), shift_cols, in_cols)[None, :]
    zr = jnp.zeros((DECAY_RANK, RWKV_WIDTH), F32)
    lora_dir = []
    for d in range(2):
        dec_rows = [zr, zr]
        dec_rows[d] = decay_w2[d]
        icl_rows = [zr, zr]
        icl_rows[d] = iclr_a2[d]
        left = jnp.concatenate(dec_rows + [zr, zr], axis=0)
        right = jnp.concatenate([zr, zr] + icl_rows, axis=0)
        lora_dir.append(jnp.concatenate([left, right], axis=1))
    lora_w = jnp.stack(lora_dir).astype(BF16)
    z4 = jnp.zeros((4, RWKV_WIDTH), F32)
    scan_vecs = jnp.stack([jnp.concatenate([decay_w0[d][None], iclr_a0[d][None], k_k[None],
                                            k_a[None], z4], axis=0) for d in range(2)])
    hid = jnp.arange(RWKV_WIDTH) // HEAD_SIZE
    head_ones = (hid[:, None] == hid[None, :]).astype(BF16)
    zl = jnp.zeros((2 * DECAY_RANK, 3 * RWKV_WIDTH), F32)
    za = jnp.zeros((ICLR_RANK, RWKV_WIDTH), F32)
    zg = jnp.zeros((GATE_PAD, RWKV_WIDTH), F32)
    gate_pad = jnp.concatenate([gate_w2, jnp.zeros((GATE_PAD - GATE_RANK, RWKV_WIDTH), F32)], axis=0)
    rw_w = jnp.concatenate([
        zl,
        jnp.concatenate([iclr_a2[0], za, za], axis=1),
        jnp.concatenate([za, iclr_a2[1], za], axis=1),
        jnp.concatenate([zg, zg, gate_pad], axis=1)], axis=0).astype(BF16)
    rvecs = jnp.concatenate([lnx_w[None], lnx_b[None], iclr_a0[0][None], iclr_a0[1][None],
                             k_a[None], r_k.reshape(1, RWKV_WIDTH), jnp.zeros((2, RWKV_WIDTH), F32)],
                            axis=0)
    conv_w_pad = jnp.concatenate([conv_w, jnp.zeros((1, CONV_WIDTH), F32)], axis=0)
    cvecs = jnp.concatenate([conv_b[None], conv_ln_w[None], conv_ln_b[None],
                             jnp.zeros((5, CONV_WIDTH), F32)], axis=0)

    seq, ctx_len = x.shape[1], ctx.shape[1]
    p_x = _proj_call(x, mods_x, mix_pre_g[None], w_pad, mp_pad, mn_pad, P_COLS // PROJ_BLOCK, seq)
    pack = seq // ctx_len
    p_c = _proj_call(ctx.reshape(bsz // pack, seq, D_MODEL), mods_c[:bsz // pack], mix_pre_g[None],
                     w_pad, mp_pad, mn_pad, N_SHIFT_BLOCKS, ctx_len).reshape(bsz, ctx_len, -1)
    s_zero = jnp.zeros((bsz, 2, N_GROUPS, GROUP_W, GROUP_W), F32)
    _, _, s_ctx = _scan_call(p_c, s_zero, lora_w, scan_vecs, head_ones)
    y_f, y_b, _ = _scan_call(p_x, s_ctx, lora_w, scan_vecs, head_ones)
    x = _mix_call(x, y_f, y_b, p_x, mods_x, rw_w, rvecs, head_ones, conv_w_pad, cvecs,
                  w_out.astype(BF16), mix_post_g[None])
    x = _mlp_call(x, mods_x, mlp_pre_g[None], mlp_post_g[None], mlp_w1.astype(BF16),
                  mlp_w2.astype(BF16))
    return x


def kernel(x, c, ctx, c_ctx, ada_w, ada_b, mix_pre_g, mix_post_g, mlp_pre_g, mlp_post_g, w_in, mu_prev, mu_next, decay_w0, decay_w2, iclr_a0, iclr_a2, k_k, k_a, r_k, gate_w2, lnx_w, lnx_b, conv_w, conv_b, conv_ln_w, conv_ln_b, w_out, mlp_w1, mlp_w2):
    depth = ada_w.shape[0]
    assert depth == 1, "context-stream update between layers is not implemented"
    bsz = x.shape[0]
    rows = 8 * ((bsz + 1 + 7) // 8)
    cc = jnp.concatenate([c, c_ctx[None, :], jnp.zeros((rows - bsz - 1, D_MODEL), F32)], axis=0)
    per_layer = (mix_pre_g, mix_post_g, mlp_pre_g, mlp_post_g, w_in, mu_prev, mu_next, decay_w0,
                 decay_w2, iclr_a0, iclr_a2, k_k, k_a, r_k, gate_w2, lnx_w, lnx_b, conv_w, conv_b,
                 conv_ln_w, conv_ln_b, w_out, mlp_w1, mlp_w2)
    for l in range(depth):
        mods = _ada_call(cc, ada_w[l], ada_b[l][None, :]).reshape(rows, N_MOD, D_MODEL)
        mods_x = mods[:bsz]
        mods_c = jnp.broadcast_to(mods[bsz:bsz + 1], (bsz, N_MOD, D_MODEL))
        x = _layer(x, ctx, mods_x, mods_c, tuple(a[l] for a in per_layer))
    return x
```

```python
import functools

import jax
import jax.numpy as jnp
from jax import lax
from jax.experimental import pallas as pl
from jax.experimental.pallas import tpu as pltpu

F32 = jnp.float32
BF16 = jnp.bfloat16

D_MODEL = 1024
RWKV_WIDTH = 512
CONV_WIDTH = 512
HEAD_SIZE = 64
RWKV_HEADS = 8
DECAY_RANK = 64
ICLR_RANK = 64
GATE_RANK = 160
CONV_KERNEL = 31
CONV_PAD = CONV_KERNEL // 2
GRID_W = 64
SUBLANES = 8
CONV_LEAD = 2 * SUBLANES
CONV_ROWS = (CONV_LEAD + CONV_PAD) // SUBLANES * SUBLANES + GRID_W
D_FF = 4 * D_MODEL
N_MOD = 6
EPS_RMS = 1e-6
EPS_LN = 1e-5
EPS_GN = 64e-5
DECAY_SCALE = 0.6065306597126334
LOG2_E = 1.4426950408889634

LORA_OFF = 3 * RWKV_WIDTH
LORA_COLS = 256
GATE_OFF = LORA_OFF + LORA_COLS
GATE_PAD = 256
CONV_OFF = GATE_OFF + GATE_PAD
P_COLS = CONV_OFF + 2 * CONV_WIDTH
SCAN_COLS = GATE_OFF
PROJ_BLOCK = 1024
N_SHIFT_BLOCKS = CONV_OFF // PROJ_BLOCK

CHUNK = 64
INV_BLOCK = 8
GROUP_HEADS = 4
GROUP_W = GROUP_HEADS * HEAD_SIZE
N_GROUPS = RWKV_HEADS // GROUP_HEADS
SCAN_BATCH = 4

LANES = 128
VMEM_LIMIT = 56 * 1024 * 1024


def _sigmoid(x):
    return 0.5 * jnp.tanh(0.5 * x) + 0.5


def _silu(x):
    return x * _sigmoid(x)


def _rms(x, g):
    return x * lax.rsqrt(jnp.mean(x * x, axis=-1, keepdims=True) + EPS_RMS) * g


def _dot(a, b):
    return jnp.dot(a.astype(BF16), b.astype(BF16), preferred_element_type=F32)


def _dot_nt(a, b):
    return lax.dot_general(a.astype(BF16), b.astype(BF16), (((1,), (1,)), ((), ())),
                           preferred_element_type=F32)


def _dot_tn(a, b):
    return lax.dot_general(a.astype(BF16), b.astype(BF16), (((0,), (0,)), ((), ())),
                           preferred_element_type=F32)


def _dot_split_rhs(a_bf16, x):
    hi = x.astype(BF16)
    lo = (x - hi.astype(F32)).astype(BF16)
    d = lambda p: jnp.dot(a_bf16, p, preferred_element_type=F32)
    return d(hi) + d(lo)


def _head_sum(z, ones_bd):
    rows = z.shape[0]
    hi = z.astype(BF16)
    lo = (z - hi.astype(F32)).astype(BF16)
    pieces = []
    for g in range(N_GROUPS):
        pieces += [hi[:, g * GROUP_W:(g + 1) * GROUP_W], lo[:, g * GROUP_W:(g + 1) * GROUP_W]]
    res = jnp.dot(jnp.concatenate(pieces, axis=0), ones_bd, preferred_element_type=F32)
    return jnp.concatenate(
        [res[(2 * g) * rows:(2 * g + 1) * rows] + res[(2 * g + 1) * rows:(2 * g + 2) * rows]
         for g in range(N_GROUPS)], axis=1)


def _ada_kernel(c_ref, w_ref, b_ref, o_ref):
    o_ref[...] = _dot(_silu(c_ref[...]), w_ref[...]) + b_ref[...]


def _ada_call(cc, ada_w, ada_b):
    rows = cc.shape[0]
    n_out = ada_w.shape[1]
    blk = 1536
    return pl.pallas_call(
        _ada_kernel,
        grid=(n_out // blk,),
        in_specs=[pl.BlockSpec((rows, D_MODEL), lambda n: (0, 0)),
                  pl.BlockSpec((D_MODEL, blk), lambda n: (0, n)),
                  pl.BlockSpec((1, blk), lambda n: (0, n))],
        out_specs=pl.BlockSpec((rows, blk), lambda n: (0, n)),
        out_shape=jax.ShapeDtypeStruct((rows, n_out), F32),
        compiler_params=pltpu.CompilerParams(vmem_limit_bytes=VMEM_LIMIT),
        name="ada_mod",
    )(cc, ada_w, ada_b)


def _proj_kernel(x_ref, mod_ref, g_ref, w_ref, mp_ref, mn_ref, o_ref, h_scr, z_scr, *, seq, seg,
                 col_of_step):
    n = pl.program_id(1)
    col = col_of_step(n)
    row_chunk = min(seq, 1024)

    @pl.when(n == 0)
    def _():
        shift = mod_ref[0, 0:1, :]
        scale = mod_ref[0, 1:2, :]
        g = g_ref[...]
        for s in range(0, seq, row_chunk):
            xs = x_ref[0, s:s + row_chunk, :]
            h_scr[s:s + row_chunk, :] = (_rms(xs, g) * (1.0 + scale) + shift).astype(BF16)

    @pl.when(col < N_SHIFT_BLOCKS)
    def _():
        lead = SUBLANES
        z_scr[0:lead, :] = jnp.zeros((lead, PROJ_BLOCK), F32)
        z_scr[lead + seq:2 * lead + seq, :] = jnp.zeros((lead, PROJ_BLOCK), F32)
        mp = mp_ref[...]
        mn = mn_ref[...]

        def shift_rows(s):
            cur = z_scr[lead + s:lead + s + row_chunk, :]
            prv = z_scr[lead - 1 + s:lead - 1 + s + row_chunk, :]
            nxt = z_scr[lead + 1 + s:lead + 1 + s + row_chunk, :]
            if seg < seq:
                pos = jnp.bitwise_and(
                    s + lax.broadcasted_iota(jnp.int32, (row_chunk, PROJ_BLOCK), 0), seg - 1)
                prv = jnp.where(pos == 0, 0.0, prv)
                nxt = jnp.where(pos == seg - 1, 0.0, nxt)
            o_ref[0, s:s + row_chunk, :] = (cur + mp * (prv - cur) + mn * (nxt - cur)).astype(BF16)

        for s in range(0, seq, row_chunk):
            z_scr[lead + s:lead + s + row_chunk, :] = jnp.dot(
                h_scr[s:s + row_chunk, :], w_ref[...], preferred_element_type=F32)
            if s > 0:
                shift_rows(s - row_chunk)
        shift_rows(seq - row_chunk)

    @pl.when(col >= N_SHIFT_BLOCKS)
    def _():
        for s in range(0, seq, row_chunk):
            o_ref[0, s:s + row_chunk, :] = jnp.dot(
                h_scr[s:s + row_chunk, :], w_ref[...], preferred_element_type=F32).astype(BF16)


def _proj_call(x, mods, g, w_pad, mp_pad, mn_pad, n_blocks, seg):
    bsz, seq, _ = x.shape
    assert seq % seg == 0 and seg & (seg - 1) == 0
    n_cols = n_blocks * PROJ_BLOCK
    first = N_SHIFT_BLOCKS % n_blocks
    col_of_step = lambda n: lax.rem(n + first, n_blocks)
    return pl.pallas_call(
        functools.partial(_proj_kernel, seq=seq, seg=seg, col_of_step=col_of_step),
        grid=(bsz, n_blocks),
        in_specs=[pl.BlockSpec((1, seq, D_MODEL), lambda b, n: (b, 0, 0)),
                  pl.BlockSpec((1, N_MOD, D_MODEL), lambda b, n: (b, 0, 0)),
                  pl.BlockSpec((1, D_MODEL), lambda b, n: (0, 0)),
                  pl.BlockSpec((D_MODEL, PROJ_BLOCK), lambda b, n: (0, col_of_step(n))),
                  pl.BlockSpec((1, PROJ_BLOCK), lambda b, n: (0, col_of_step(n))),
                  pl.BlockSpec((1, PROJ_BLOCK), lambda b, n: (0, col_of_step(n)))],
        out_specs=pl.BlockSpec((1, seq, PROJ_BLOCK), lambda b, n: (b, 0, col_of_step(n))),
        out_shape=jax.ShapeDtypeStruct((bsz, seq, n_cols), BF16),
        scratch_shapes=[pltpu.VMEM((seq, D_MODEL), BF16),
                        pltpu.VMEM((seq + 2 * SUBLANES, PROJ_BLOCK), F32)],
        compiler_params=pltpu.CompilerParams(
            dimension_semantics=("parallel", "arbitrary"), vmem_limit_bytes=VMEM_LIMIT),
        name="in_proj",
    )(x, mods, g, w_pad, mp_pad, mn_pad)


def _expand(x, bd_mask):
    return jnp.where(bd_mask, jnp.concatenate([x] * GROUP_HEADS, axis=0), 0.0)


def _scan_prep(blocks, lw_ref, vec_ref, ones_bd):
    L = CHUNK
    n = len(blocks)
    lane = lax.broadcasted_iota(jnp.int32, (L, LORA_COLS), 1)
    loras = []
    for pb, _ in blocks:
        lora = pb[:, LORA_OFF:LORA_OFF + LORA_COLS]
        loras.append(jnp.where(lane < 2 * DECAY_RANK, jnp.tanh(lora), lora))
    lin = [None] * n
    for d in range(2):
        idx = [i for i in range(n) if blocks[i][1] == bool(d)]
        res = _dot(jnp.concatenate([loras[i] for i in idx], axis=0), lw_ref[d])
        for j, i in enumerate(idx):
            lin[i] = res[j * L:(j + 1) * L]

    kk0s, pieces = [], []
    for i, (pb, reverse) in enumerate(blocks):
        kk0 = pb[:, RWKV_WIDTH:2 * RWKV_WIDTH] * vec_ref[int(reverse), 2:3, :]
        kk0s.append(kk0)
        sq = (kk0 * kk0).astype(BF16)
        pieces += [sq[:, g * GROUP_W:(g + 1) * GROUP_W] for g in range(N_GROUPS)]
    sums = jnp.dot(jnp.concatenate(pieces, axis=0), ones_bd, preferred_element_type=F32)

    row = lax.broadcasted_iota(jnp.int32, (L, L), 0)
    col = lax.broadcasted_iota(jnp.int32, (L, L), 1)
    preps = []
    for i, (pb, reverse) in enumerate(blocks):
        vecs = vec_ref[int(reverse)]
        w0 = vecs[0:1, :]
        a0 = vecs[1:2, :]
        k_a = vecs[3:4, :]
        r = pb[:, 0:RWKV_WIDTH]
        k = pb[:, RWKV_WIDTH:2 * RWKV_WIDTH]
        v = pb[:, 2 * RWKV_WIDTH:3 * RWKV_WIDTH]
        half = 0.5 * LOG2_E * DECAY_SCALE
        lw = -half * jnp.tanh(0.5 * (lin[i][:, 0:RWKV_WIDTH] + w0)) - half
        neg_iclr = -0.5 * jnp.tanh(0.5 * (lin[i][:, RWKV_WIDTH:] + a0)) - 0.5
        base = i * N_GROUPS * L
        ss = jnp.concatenate(
            [sums[base + g * L:base + (g + 1) * L] for g in range(N_GROUPS)], axis=1)
        kk = kk0s[i] * lax.rsqrt(jnp.maximum(ss, 1e-24))
        kd = k * (1.0 - (neg_iclr + 1.0) * k_a)
        na = kk
        nb = kk * neg_iclr
        tri = jnp.where((col >= row) if reverse else (col <= row), 1.0, 0.0).astype(BF16)
        cum = _dot_split_rhs(tri, lw)
        tot = cum[0:1, :] if reverse else cum[L - 1:L, :]
        e_incl = jnp.exp2(cum)
        e_neg = 1.0 / e_incl
        w_tot = jnp.exp2(tot)
        e_rem = w_tot * e_neg
        preps.append(dict(
            r_t=r * e_incl, a_t=na * jnp.exp2(cum - lw), b_t=nb * e_neg, k_t=kd * e_neg,
            b_h=nb * e_rem, k_h=kd * e_rem, v=v, w_tot=w_tot))
    return preps


def _scan_units(preps, reverse, s_idx, s_ref):
    L = CHUNK
    rr = lax.broadcasted_iota(jnp.int32, (GROUP_W, GROUP_W), 0)
    cc = lax.broadcasted_iota(jnp.int32, (GROUP_W, GROUP_W), 1)
    head_shift = HEAD_SIZE.bit_length() - 1
    bd_mask = jnp.right_shift(rr, head_shift) == jnp.right_shift(cc, head_shift)
    tr = lax.broadcasted_iota(jnp.int32, (L, GROUP_W), 0)
    tc = jnp.bitwise_and(lax.broadcasted_iota(jnp.int32, (L, GROUP_W), 1), L - 1)
    eye = jnp.where(tc == tr, 1.0, 0.0)
    strict = {True: tc > tr, False: tc < tr}
    incl = {True: tc >= tr, False: tc <= tr}
    units = [(d, g) for d in range(len(preps)) for g in range(N_GROUPS)]
    sl = lambda g: slice(g * GROUP_W, (g + 1) * GROUP_W)
    ex = lambda x: _expand(x, bd_mask)

    lhs, a_ab, a_ak, a_rb, a_rk = {}, {}, {}, {}, {}
    for u in units:
        d, g = u
        p = preps[d]
        lhs[u] = jnp.concatenate([p["a_t"][:, sl(g)], p["r_t"][:, sl(g)]], axis=0)
        e_bk = jnp.concatenate([ex(p["b_t"][:, sl(g)]), ex(p["k_t"][:, sl(g)])], axis=0)
        a_all = _dot_nt(lhs[u], e_bk)
        a_ab[u] = jnp.where(strict[reverse[d]], a_all[0:L, 0:GROUP_W], 0.0)
        a_ak[u] = jnp.where(strict[reverse[d]], a_all[0:L, GROUP_W:], 0.0)
        a_rb[u] = jnp.where(incl[reverse[d]], a_all[L:, 0:GROUP_W], 0.0)
        a_rk[u] = jnp.where(incl[reverse[d]], a_all[L:, GROUP_W:], 0.0)
    av = {}
    for u in units:
        d, g = u
        av[u] = _dot(jnp.concatenate([a_ak[u], a_rk[u]], axis=0), ex(preps[d]["v"][:, sl(g)]))

    def nilpotent_inverse(n, index):
        prod = {u: eye + n[u] for u in units}
        npow = {u: _dot(n[u], ex(n[u])) for u in units}
        size = 2
        while size < index:
            for u in units:
                rhs = ex(npow[u])
                if 2 * size < index:
                    both = _dot(jnp.concatenate([npow[u], prod[u]], axis=0), rhs)
                    npow[u] = both[0:L]
                    prod[u] = prod[u] + both[L:]
                else:
                    prod[u] = prod[u] + _dot(prod[u], rhs)
            size *= 2
        return prod

    sh = INV_BLOCK.bit_length() - 1
    in_diag = jnp.right_shift(tr, sh) == jnp.right_shift(tc, sh)
    d_inv = nilpotent_inverse({u: jnp.where(in_diag, a_ab[u], 0.0) for u in units}, INV_BLOCK)
    e_mat = {u: _dot(d_inv[u], ex(jnp.where(in_diag, 0.0, a_ab[u]))) for u in units}
    e_inv = nilpotent_inverse(e_mat, L // INV_BLOCK)
    pinv = {u: _dot(e_inv[u], ex(d_inv[u])) for u in units}

    s_old, sa, x_u, u_u, y_u = {}, {}, {}, {}, {}
    for u in units:
        d, g = u
        s_old[u] = s_ref[s_idx[d] + (g,)]
        sa[u] = _dot(lhs[u], s_old[u]) + av[u]
    for u in units:
        x_u[u] = sa[u][0:L]
        u_u[u] = _dot(pinv[u], ex(x_u[u]))
    for u in units:
        y_u[u] = sa[u][L:] + _dot(a_rb[u], ex(u_u[u]))
    w_col = [jnp.transpose(jnp.broadcast_to(p["w_tot"], (LANES, RWKV_WIDTH))) for p in preps]
    for u in units:
        d, g = u
        p = preps[d]
        upd = _dot_tn(jnp.concatenate([p["b_h"][:, sl(g)], p["k_h"][:, sl(g)]], axis=0),
                      jnp.concatenate([u_u[u], p["v"][:, sl(g)]], axis=0))
        decay = jnp.concatenate([w_col[d][sl(g), :]] * (GROUP_W // LANES), axis=1)
        s_ref[s_idx[d] + (g,)] = s_old[u] * decay + jnp.where(bd_mask, upd, 0.0)
    return [jnp.concatenate([y_u[(d, g)] for g in range(N_GROUPS)], axis=1)
            for d in range(len(preps))]


def _scan_kernel(*refs, has_s0):
    if has_s0:
        pf_ref, pb_ref, s0_ref, lw_ref, vec_ref, ho_ref, yf_ref, yb_ref, s_ref = refs
    else:
        pf_ref, pb_ref, lw_ref, vec_ref, ho_ref, yf_ref, yb_ref, s_ref = refs
    c = pl.program_id(1)

    @pl.when(c == 0)
    def _():
        s_ref[...] = s0_ref[...] if has_s0 else jnp.zeros_like(s_ref)

    n_streams = 2 * SCAN_BATCH
    reverse = [bool(i % 2) for i in range(n_streams)]
    s_idx = [(i // 2, i % 2) for i in range(n_streams)]
    blocks = []
    for bb in range(SCAN_BATCH):
        blocks += [(pf_ref[bb].astype(F32), False), (pb_ref[bb].astype(F32), True)]
    preps = _scan_prep(blocks, lw_ref, vec_ref, ho_ref[...])
    ys = _scan_units(preps, reverse, s_idx, s_ref)
    for bb in range(SCAN_BATCH):
        yf_ref[bb] = ys[2 * bb]
        yb_ref[bb] = ys[2 * bb + 1]


def _scan_call(p, s0, lora_w, vecs, head_ones):
    bsz, seq, _ = p.shape
    nc = seq // CHUNK
    state_block = (SCAN_BATCH, 2, N_GROUPS, GROUP_W, GROUP_W)
    state_map = lambda b, c: (b, 0, 0, 0, 0)
    state_in = [] if s0 is None else [pl.BlockSpec(state_block, state_map)]
    return pl.pallas_call(
        functools.partial(_scan_kernel, has_s0=s0 is not None),
        grid=(bsz // SCAN_BATCH, nc),
        in_specs=[pl.BlockSpec((SCAN_BATCH, CHUNK, SCAN_COLS), lambda b, c: (b, c, 0)),
                  pl.BlockSpec((SCAN_BATCH, CHUNK, SCAN_COLS), lambda b, c: (b, nc - 1 - c, 0)),
                  *state_in,
                  pl.BlockSpec((2, LORA_COLS, 2 * RWKV_WIDTH), lambda b, c: (0, 0, 0)),
                  pl.BlockSpec((2, 8, RWKV_WIDTH), lambda b, c: (0, 0, 0)),
                  pl.BlockSpec((GROUP_W, GROUP_W), lambda b, c: (0, 0))],
        out_specs=[pl.BlockSpec((SCAN_BATCH, CHUNK, RWKV_WIDTH), lambda b, c: (b, c, 0)),
                   pl.BlockSpec((SCAN_BATCH, CHUNK, RWKV_WIDTH), lambda b, c: (b, nc - 1 - c, 0)),
                   pl.BlockSpec(state_block, state_map)],
        out_shape=[jax.ShapeDtypeStruct((bsz, seq, RWKV_WIDTH), F32),
                   jax.ShapeDtypeStruct((bsz, seq, RWKV_WIDTH), F32),
                   jax.ShapeDtypeStruct((bsz, 2, N_GROUPS, GROUP_W, GROUP_W), F32)],
        compiler_params=pltpu.CompilerParams(
            dimension_semantics=("parallel", "arbitrary"), vmem_limit_bytes=VMEM_LIMIT),
        name="wkv_scan",
    )(p, p, *([] if s0 is None else [s0]), lora_w, vecs, head_ones)


def _mix_kernel(x_ref, yf_ref, yb_ref, rkv_ref, sm_ref, cv_ref, mod_ref, rw_ref, vec_ref, ho_ref,
                cw_ref, cvec_ref, wout_ref, g_ref, o_ref, upad, *, tm):
    n_lines = tm // GRID_W
    ones_bd = ho_ref[...]
    y = yf_ref[0] + yb_ref[0]
    inv_n = 1.0 / HEAD_SIZE
    mu = _head_sum(y, ones_bd) * inv_n
    dy = y - mu
    var = _head_sum(dy * dy, ones_bd) * inv_n
    lnx_w = vec_ref[0:1, :]
    lnx_b = vec_ref[1:2, :]
    a0_f = vec_ref[2:3, :]
    a0_b = vec_ref[3:4, :]
    k_a = vec_ref[4:5, :]
    r_k = vec_ref[5:6, :]
    yn = dy * lax.rsqrt(var + EPS_GN) * lnx_w + lnx_b
    sm = sm_ref[0].astype(F32)
    lane = lax.broadcasted_iota(jnp.int32, sm.shape, 1)
    sm = jnp.where(lane >= LORA_COLS, _sigmoid(sm), sm)
    lin = _dot(sm, rw_ref[...])
    iclr_f = _sigmoid(lin[:, 0:RWKV_WIDTH] + a0_f)
    iclr_b = _sigmoid(lin[:, RWKV_WIDTH:2 * RWKV_WIDTH] + a0_b)
    gate = lin[:, 2 * RWKV_WIDTH:]
    r = rkv_ref[0, :, 0:RWKV_WIDTH].astype(F32)
    k = rkv_ref[0, :, RWKV_WIDTH:2 * RWKV_WIDTH].astype(F32)
    v = rkv_ref[0, :, 2 * RWKV_WIDTH:3 * RWKV_WIDTH].astype(F32)
    k_bar = k * (1.0 + (0.5 * (iclr_f + iclr_b) - 1.0) * k_a)
    bonus = _head_sum(r * k_bar * r_k, ones_bd) * v
    rw_out = (yn + bonus) * gate

    cv = cv_ref[0].astype(F32)
    u = cv[:, 0:CONV_WIDTH] * _sigmoid(cv[:, CONV_WIDTH:])
    zeros = jnp.zeros((CONV_LEAD, CONV_WIDTH), F32)
    for ln in range(n_lines):
        u_ln = u[ln * GRID_W:(ln + 1) * GRID_W, :]
        for s in range(SUBLANES):
            upad[s, ln, 0:CONV_LEAD, :] = zeros
            upad[s, ln, CONV_ROWS - CONV_LEAD:CONV_ROWS, :] = zeros
            upad[s, ln, CONV_LEAD - s:CONV_LEAD - s + GRID_W, :] = u_ln
    conv_b = cvec_ref[0:1, :]
    ln_w = cvec_ref[1:2, :]
    ln_b = cvec_ref[2:3, :]
    conv_lines = []
    for ln in range(n_lines):
        strips = []
        for lb in range(CONV_WIDTH // LANES):
            lanes = slice(lb * LANES, (lb + 1) * LANES)
            acc = jnp.zeros((GRID_W, LANES), F32)
            for j in range(CONV_KERNEL):
                off = CONV_LEAD - CONV_PAD + j
                q8 = off // SUBLANES * SUBLANES
                acc = acc + cw_ref[j:j + 1, lanes] * upad[off % SUBLANES, ln, q8:q8 + GRID_W, lanes]
            strips.append(acc)
        conv_lines.append(jnp.concatenate(strips, axis=1))
    yc = jnp.concatenate(conv_lines, axis=0) + conv_b
    mu_c = jnp.mean(yc, axis=-1, keepdims=True)
    dc = yc - mu_c
    var_c = jnp.mean(dc * dc, axis=-1, keepdims=True)
    conv_out = _silu(dc * lax.rsqrt(var_c + EPS_LN) * ln_w + ln_b)

    mix_in = jnp.concatenate([rw_out.astype(BF16), conv_out.astype(BF16)], axis=1)
    mix = jnp.dot(mix_in, wout_ref[...], preferred_element_type=F32)
    gate_mix = mod_ref[0, 2:3, :]
    o_ref[0] = x_ref[0] + gate_mix * _rms(mix, g_ref[...])


def _mix_call(x, y_f, y_b, p, mods, rw_w, rvecs, head_ones, conv_w, cvecs, w_out, post_g):
    bsz, seq, _ = x.shape
    tm = 512
    tok = lambda width, blk: pl.BlockSpec((1, tm, width), lambda b, t: (b, t, blk))
    const2 = lambda shape: pl.BlockSpec(shape, lambda b, t: (0, 0))
    return pl.pallas_call(
        functools.partial(_mix_kernel, tm=tm),
        grid=(bsz, seq // tm),
        in_specs=[tok(D_MODEL, 0), tok(RWKV_WIDTH, 0), tok(RWKV_WIDTH, 0),
                  tok(3 * RWKV_WIDTH, 0),
                  tok(512, LORA_OFF // 512),
                  tok(2 * CONV_WIDTH, CONV_OFF // (2 * CONV_WIDTH)),
                  pl.BlockSpec((1, N_MOD, D_MODEL), lambda b, t: (b, 0, 0)),
                  const2((512, 3 * RWKV_WIDTH)), const2((8, RWKV_WIDTH)),
                  const2((GROUP_W, GROUP_W)), const2((32, CONV_WIDTH)),
                  const2((8, CONV_WIDTH)), const2((D_MODEL, D_MODEL)), const2((1, D_MODEL))],
        out_specs=tok(D_MODEL, 0),
        out_shape=jax.ShapeDtypeStruct((bsz, seq, D_MODEL), F32),
        scratch_shapes=[pltpu.VMEM((SUBLANES, tm // GRID_W, CONV_ROWS, CONV_WIDTH), F32)],
        compiler_params=pltpu.CompilerParams(
            dimension_semantics=("parallel", "parallel"), vmem_limit_bytes=VMEM_LIMIT),
        name="mix_out",
    )(x, y_f, y_b, p, p, p, mods, rw_w, rvecs, head_ones, conv_w, cvecs, w_out, post_g)


def _mlp_kernel(x_ref, mod_ref, gpre_ref, gpost_ref, w1_ref, w2_ref, o_ref, acc_ref):
    x = x_ref[0]
    shift = mod_ref[0, 3:4, :]
    scale = mod_ref[0, 4:5, :]
    gate = mod_ref[0, 5:6, :]
    h = (_rms(x, gpre_ref[...]) * (1.0 + scale) + shift).astype(BF16)
    ff_blk = 1024
    for c in range(D_FF // ff_blk):
        hid = jnp.dot(h, w1_ref[:, c * ff_blk:(c + 1) * ff_blk], preferred_element_type=F32)
        hid = jnp.square(jnp.maximum(hid, 0.0)).astype(BF16)
        part = jnp.dot(hid, w2_ref[c * ff_blk:(c + 1) * ff_blk, :], preferred_element_type=F32)
        if c == 0:
            acc_ref[...] = part
        else:
            acc_ref[...] += part
    o_ref[0] = x + gate * _rms(acc_ref[...], gpost_ref[...])


def _mlp_call(x, mods, pre_g, post_g, w1, w2):
    bsz, seq, _ = x.shape
    tm = 512
    return pl.pallas_call(
        _mlp_kernel,
        grid=(bsz, seq // tm),
        in_specs=[pl.BlockSpec((1, tm, D_MODEL), lambda b, t: (b, t, 0)),
                  pl.BlockSpec((1, N_MOD, D_MODEL), lambda b, t: (b, 0, 0)),
                  pl.BlockSpec((1, D_MODEL), lambda b, t: (0, 0)),
                  pl.BlockSpec((1, D_MODEL), lambda b, t: (0, 0)),
                  pl.BlockSpec((D_MODEL, D_FF), lambda b, t: (0, 0)),
                  pl.BlockSpec((D_FF, D_MODEL), lambda b, t: (0, 0))],
        out_specs=pl.BlockSpec((1, tm, D_MODEL), lambda b, t: (b, t, 0)),
        out_shape=jax.ShapeDtypeStruct((bsz, seq, D_MODEL), F32),
        scratch_shapes=[pltpu.VMEM((tm, D_MODEL), F32)],
        compiler_params=pltpu.CompilerParams(
            dimension_semantics=("parallel", "parallel"), vmem_limit_bytes=VMEM_LIMIT),
        name="sqrelu_mlp",
    )(x, mods, pre_g, post_g, w1, w2)


def _pad_cols(a, lo, hi):
    pad = jnp.zeros(a.shape[:-1] + (GATE_PAD - GATE_RANK,), a.dtype)
    return jnp.concatenate([a[..., :lo], pad, a[..., lo:hi]], axis=-1)


def _layer(x, ctx, mods_x, mods_c, prm):
    (mix_pre_g, mix_post_g, mlp_pre_g, mlp_post_g, w_in, mu_prev, mu_next, decay_w0, decay_w2,
     iclr_a0, iclr_a2, k_k, k_a, r_k, gate_w2, lnx_w, lnx_b, conv_w, conv_b, conv_ln_w,
     conv_ln_b, w_out, mlp_w1, mlp_w2) = prm
    bsz = x.shape[0]
    shift_cols = GATE_OFF + GATE_RANK
    in_cols = w_in.shape[1]

    w_pad = _pad_cols(w_in, shift_cols, in_cols).astype(BF16)
    zc = jnp.zeros((2 * CONV_WIDTH,), F32)
    mp_pad = _pad_cols(jnp.concatenate([mu_prev, zc]), shift_cols, in_cols)[None, :]
    mn_pad = _pad_cols(jnp.concatenate([mu_next, zc]), shift_cols, in_cols)[None, :]
    zr = jnp.zeros((DECAY_RANK, RWKV_WIDTH), F32)
    lora_dir = []
    for d in range(2):
        dec_rows = [zr, zr]
        dec_rows[d] = decay_w2[d]
        icl_rows = [zr, zr]
        icl_rows[d] = iclr_a2[d]
        left = jnp.concatenate(dec_rows + [zr, zr], axis=0)
        right = jnp.concatenate([zr, zr] + icl_rows, axis=0)
        lora_dir.append(jnp.concatenate([left, right], axis=1))
    lora_w = jnp.stack(lora_dir).astype(BF16)
    z4 = jnp.zeros((4, RWKV_WIDTH), F32)
    scan_vecs = jnp.stack([jnp.concatenate([decay_w0[d][None], iclr_a0[d][None], k_k[None],
                                            k_a[None], z4], axis=0) for d in range(2)])
    hid = jnp.arange(RWKV_WIDTH) // HEAD_SIZE
    head_ones = (hid[:, None] == hid[None, :]).astype(BF16)
    zl = jnp.zeros((2 * DECAY_RANK, 3 * RWKV_WIDTH), F32)
    za = jnp.zeros((ICLR_RANK, RWKV_WIDTH), F32)
    zg = jnp.zeros((GATE_PAD, RWKV_WIDTH), F32)
    gate_pad = jnp.concatenate([gate_w2, jnp.zeros((GATE_PAD - GATE_RANK, RWKV_WIDTH), F32)], axis=0)
    rw_w = jnp.concatenate([
        zl,
        jnp.concatenate([iclr_a2[0], za, za], axis=1),
        jnp.concatenate([za, iclr_a2[1], za], axis=1),
        jnp.concatenate([zg, zg, gate_pad], axis=1)], axis=0).astype(BF16)
    rvecs = jnp.concatenate([lnx_w[None], lnx_b[None], iclr_a0[0][None], iclr_a0[1][None],
                             k_a[None], r_k.reshape(1, RWKV_WIDTH), jnp.zeros((2, RWKV_WIDTH), F32)],
                            axis=0)
    conv_w_pad = jnp.concatenate([conv_w, jnp.zeros((1, CONV_WIDTH), F32)], axis=0)
    cvecs = jnp.concatenate([conv_b[None], conv_ln_w[None], conv_ln_b[None],
                             jnp.zeros((5, CONV_WIDTH), F32)], axis=0)

    seq, ctx_len = x.shape[1], ctx.shape[1]
    p_x = _proj_call(x, mods_x, mix_pre_g[None], w_pad, mp_pad, mn_pad, P_COLS // PROJ_BLOCK, seq)
    pack = seq // ctx_len
    p_c = _proj_call(ctx.reshape(bsz // pack, seq, D_MODEL), mods_c[:bsz // pack], mix_pre_g[None],
                     w_pad, mp_pad, mn_pad, N_SHIFT_BLOCKS, ctx_len).reshape(bsz, ctx_len, -1)
    _, _, s_ctx = _scan_call(p_c, None, lora_w, scan_vecs, head_ones)
    y_f, y_b, _ = _scan_call(p_x, s_ctx, lora_w, scan_vecs, head_ones)
    x = _mix_call(x, y_f, y_b, p_x, mods_x, rw_w, rvecs, head_ones, conv_w_pad, cvecs,
                  w_out.astype(BF16), mix_post_g[None])
    x = _mlp_call(x, mods_x, mlp_pre_g[None], mlp_post_g[None], mlp_w1.astype(BF16),
                  mlp_w2.astype(BF16))
    return x


def kernel(x, c, ctx, c_ctx, ada_w, ada_b, mix_pre_g, mix_post_g, mlp_pre_g, mlp_post_g, w_in, mu_prev, mu_next, decay_w0, decay_w2, iclr_a0, iclr_a2, k_k, k_a, r_k, gate_w2, lnx_w, lnx_b, conv_w, conv_b, conv_ln_w, conv_ln_b, w_out, mlp_w1, mlp_w2):
    depth = ada_w.shape[0]
    assert depth == 1, "context-stream update between layers is not implemented"
    bsz = x.shape[0]
    rows = 8 * ((bsz + 1 + 7) // 8)
    cc = jnp.concatenate([c, c_ctx[None, :], jnp.zeros((rows - bsz - 1, D_MODEL), F32)], axis=0)
    per_layer = (mix_pre_g, mix_post_g, mlp_pre_g, mlp_post_g, w_in, mu_prev, mu_next, decay_w0,
                 decay_w2, iclr_a0, iclr_a2, k_k, k_a, r_k, gate_w2, lnx_w, lnx_b, conv_w, conv_b,
                 conv_ln_w, conv_ln_b, w_out, mlp_w1, mlp_w2)
    for l in range(depth):
        mods = _ada_call(cc, ada_w[l], ada_b[l][None, :]).reshape(rows, N_MOD, D_MODEL)
        mods_x = mods[:bsz]
        mods_c = jnp.broadcast_to(mods[bsz:bsz + 1], (bsz, N_MOD, D_MODEL))
        x = _layer(x, ctx, mods_x, mods_c, tuple(a[l] for a in per_layer))
    return x
```

```python
import functools

import jax
import jax.numpy as jnp
from jax import lax
from jax.experimental import pallas as pl
from jax.experimental.pallas import tpu as pltpu

F32 = jnp.float32
BF16 = jnp.bfloat16

D_MODEL = 1024
RWKV_WIDTH = 512
CONV_WIDTH = 512
HEAD_SIZE = 64
RWKV_HEADS = 8
DECAY_RANK = 64
ICLR_RANK = 64
GATE_RANK = 160
CONV_KERNEL = 31
CONV_PAD = CONV_KERNEL // 2
GRID_W = 64
SUBLANES = 8
CONV_LEAD = 2 * SUBLANES
CONV_ROWS = (CONV_LEAD + CONV_PAD) // SUBLANES * SUBLANES + GRID_W
D_FF = 4 * D_MODEL
N_MOD = 6
EPS_RMS = 1e-6
EPS_LN = 1e-5
EPS_GN = 64e-5
DECAY_SCALE = 0.6065306597126334
LOG2_E = 1.4426950408889634

LORA_OFF = 3 * RWKV_WIDTH
LORA_COLS = 256
GATE_OFF = LORA_OFF + LORA_COLS
GATE_PAD = 256
CONV_OFF = GATE_OFF + GATE_PAD
P_COLS = CONV_OFF + 2 * CONV_WIDTH
SCAN_COLS = GATE_OFF
PROJ_BLOCK = 1024
N_SHIFT_BLOCKS = CONV_OFF // PROJ_BLOCK

CHUNK = 64
INV_BLOCK = 8
GROUP_HEADS = 4
GROUP_W = GROUP_HEADS * HEAD_SIZE
N_GROUPS = RWKV_HEADS // GROUP_HEADS
SCAN_BATCH = 4
SCAN_CHUNKS = 4

LANES = 128
VMEM_LIMIT = 56 * 1024 * 1024


def _sigmoid(x):
    return 0.5 * jnp.tanh(0.5 * x) + 0.5


def _silu(x):
    return x * _sigmoid(x)


def _rms(x, g):
    return x * lax.rsqrt(jnp.mean(x * x, axis=-1, keepdims=True) + EPS_RMS) * g


def _dot(a, b):
    return jnp.dot(a.astype(BF16), b.astype(BF16), preferred_element_type=F32)


def _dot_nt(a, b):
    return lax.dot_general(a.astype(BF16), b.astype(BF16), (((1,), (1,)), ((), ())),
                           preferred_element_type=F32)


def _dot_tn(a, b):
    return lax.dot_general(a.astype(BF16), b.astype(BF16), (((0,), (0,)), ((), ())),
                           preferred_element_type=F32)


def _dot_split_rhs(a_bf16, x):
    hi = x.astype(BF16)
    lo = (x - hi.astype(F32)).astype(BF16)
    d = lambda p: jnp.dot(a_bf16, p, preferred_element_type=F32)
    return d(hi) + d(lo)


def _head_sum(z, ones_bd):
    rows = z.shape[0]
    hi = z.astype(BF16)
    lo = (z - hi.astype(F32)).astype(BF16)
    pieces = []
    for g in range(N_GROUPS):
        pieces += [hi[:, g * GROUP_W:(g + 1) * GROUP_W], lo[:, g * GROUP_W:(g + 1) * GROUP_W]]
    res = jnp.dot(jnp.concatenate(pieces, axis=0), ones_bd, preferred_element_type=F32)
    return jnp.concatenate(
        [res[(2 * g) * rows:(2 * g + 1) * rows] + res[(2 * g + 1) * rows:(2 * g + 2) * rows]
         for g in range(N_GROUPS)], axis=1)


def _ada_kernel(c_ref, w_ref, b_ref, o_ref):
    o_ref[...] = _dot(_silu(c_ref[...]), w_ref[...]) + b_ref[...]


def _ada_call(cc, ada_w, ada_b):
    rows = cc.shape[0]
    n_out = ada_w.shape[1]
    blk = 1536
    return pl.pallas_call(
        _ada_kernel,
        grid=(n_out // blk,),
        in_specs=[pl.BlockSpec((rows, D_MODEL), lambda n: (0, 0)),
                  pl.BlockSpec((D_MODEL, blk), lambda n: (0, n)),
                  pl.BlockSpec((1, blk), lambda n: (0, n))],
        out_specs=pl.BlockSpec((rows, blk), lambda n: (0, n)),
        out_shape=jax.ShapeDtypeStruct((rows, n_out), F32),
        compiler_params=pltpu.CompilerParams(vmem_limit_bytes=VMEM_LIMIT),
        name="ada_mod",
    )(cc, ada_w, ada_b)


def _proj_kernel(x_ref, mod_ref, g_ref, w_ref, mp_ref, mn_ref, o_ref, h_scr, z_scr, *, seq, seg,
                 col_of_step):
    n = pl.program_id(1)
    col = col_of_step(n)
    row_chunk = min(seq, 1024)

    @pl.when(n == 0)
    def _():
        shift = mod_ref[0, 0:1, :]
        scale = mod_ref[0, 1:2, :]
        g = g_ref[...]
        for s in range(0, seq, row_chunk):
            xs = x_ref[0, s:s + row_chunk, :]
            h_scr[s:s + row_chunk, :] = (_rms(xs, g) * (1.0 + scale) + shift).astype(BF16)

    @pl.when(col < N_SHIFT_BLOCKS)
    def _():
        lead = SUBLANES
        z_scr[0:lead, :] = jnp.zeros((lead, PROJ_BLOCK), F32)
        z_scr[lead + seq:2 * lead + seq, :] = jnp.zeros((lead, PROJ_BLOCK), F32)
        mp = mp_ref[...]
        mn = mn_ref[...]

        def shift_rows(s):
            cur = z_scr[lead + s:lead + s + row_chunk, :]
            prv = z_scr[lead - 1 + s:lead - 1 + s + row_chunk, :]
            nxt = z_scr[lead + 1 + s:lead + 1 + s + row_chunk, :]
            if seg < seq:
                pos = jnp.bitwise_and(
                    s + lax.broadcasted_iota(jnp.int32, (row_chunk, PROJ_BLOCK), 0), seg - 1)
                prv = jnp.where(pos == 0, 0.0, prv)
                nxt = jnp.where(pos == seg - 1, 0.0, nxt)
            o_ref[0, s:s + row_chunk, :] = (cur + mp * (prv - cur) + mn * (nxt - cur)).astype(BF16)

        for s in range(0, seq, row_chunk):
            z_scr[lead + s:lead + s + row_chunk, :] = jnp.dot(
                h_scr[s:s + row_chunk, :], w_ref[...], preferred_element_type=F32)
            if s > 0:
                shift_rows(s - row_chunk)
        shift_rows(seq - row_chunk)

    @pl.when(col >= N_SHIFT_BLOCKS)
    def _():
        for s in range(0, seq, row_chunk):
            o_ref[0, s:s + row_chunk, :] = jnp.dot(
                h_scr[s:s + row_chunk, :], w_ref[...], preferred_element_type=F32).astype(BF16)


def _proj_call(x, mods, g, w_pad, mp_pad, mn_pad, n_blocks, seg):
    bsz, seq, _ = x.shape
    assert seq % seg == 0 and seg & (seg - 1) == 0
    n_cols = n_blocks * PROJ_BLOCK
    first = N_SHIFT_BLOCKS % n_blocks
    col_of_step = lambda n: lax.rem(n + first, n_blocks)
    return pl.pallas_call(
        functools.partial(_proj_kernel, seq=seq, seg=seg, col_of_step=col_of_step),
        grid=(bsz, n_blocks),
        in_specs=[pl.BlockSpec((1, seq, D_MODEL), lambda b, n: (b, 0, 0)),
                  pl.BlockSpec((1, N_MOD, D_MODEL), lambda b, n: (b, 0, 0)),
                  pl.BlockSpec((1, D_MODEL), lambda b, n: (0, 0)),
                  pl.BlockSpec((D_MODEL, PROJ_BLOCK), lambda b, n: (0, col_of_step(n))),
                  pl.BlockSpec((1, PROJ_BLOCK), lambda b, n: (0, col_of_step(n))),
                  pl.BlockSpec((1, PROJ_BLOCK), lambda b, n: (0, col_of_step(n)))],
        out_specs=pl.BlockSpec((1, seq, PROJ_BLOCK), lambda b, n: (b, 0, col_of_step(n))),
        out_shape=jax.ShapeDtypeStruct((bsz, seq, n_cols), BF16),
        scratch_shapes=[pltpu.VMEM((seq, D_MODEL), BF16),
                        pltpu.VMEM((seq + 2 * SUBLANES, PROJ_BLOCK), F32)],
        compiler_params=pltpu.CompilerParams(
            dimension_semantics=("parallel", "arbitrary"), vmem_limit_bytes=VMEM_LIMIT),
        name="in_proj",
    )(x, mods, g, w_pad, mp_pad, mn_pad)


def _expand(x, bd_mask):
    return jnp.where(bd_mask, jnp.concatenate([x] * GROUP_HEADS, axis=0), 0.0)


def _scan_prep(blocks, lw_ref, vec_ref, ones_bd):
    L = CHUNK
    n = len(blocks)
    lane = lax.broadcasted_iota(jnp.int32, (L, LORA_COLS), 1)
    loras = []
    for pb, _ in blocks:
        lora = pb[:, LORA_OFF:LORA_OFF + LORA_COLS]
        loras.append(jnp.where(lane < 2 * DECAY_RANK, jnp.tanh(lora), lora))
    lin = [None] * n
    for d in range(2):
        idx = [i for i in range(n) if blocks[i][1] == bool(d)]
        res = _dot(jnp.concatenate([loras[i] for i in idx], axis=0), lw_ref[d])
        for j, i in enumerate(idx):
            lin[i] = res[j * L:(j + 1) * L]

    kk0s, pieces = [], []
    for i, (pb, reverse) in enumerate(blocks):
        kk0 = pb[:, RWKV_WIDTH:2 * RWKV_WIDTH] * vec_ref[int(reverse), 2:3, :]
        kk0s.append(kk0)
        sq = (kk0 * kk0).astype(BF16)
        pieces += [sq[:, g * GROUP_W:(g + 1) * GROUP_W] for g in range(N_GROUPS)]
    sums = jnp.dot(jnp.concatenate(pieces, axis=0), ones_bd, preferred_element_type=F32)

    row = lax.broadcasted_iota(jnp.int32, (L, L), 0)
    col = lax.broadcasted_iota(jnp.int32, (L, L), 1)
    preps = []
    for i, (pb, reverse) in enumerate(blocks):
        vecs = vec_ref[int(reverse)]
        w0 = vecs[0:1, :]
        a0 = vecs[1:2, :]
        k_a = vecs[3:4, :]
        r = pb[:, 0:RWKV_WIDTH]
        k = pb[:, RWKV_WIDTH:2 * RWKV_WIDTH]
        v = pb[:, 2 * RWKV_WIDTH:3 * RWKV_WIDTH]
        half = 0.5 * LOG2_E * DECAY_SCALE
        lw = -half * jnp.tanh(0.5 * (lin[i][:, 0:RWKV_WIDTH] + w0)) - half
        neg_iclr = -0.5 * jnp.tanh(0.5 * (lin[i][:, RWKV_WIDTH:] + a0)) - 0.5
        base = i * N_GROUPS * L
        ss = jnp.concatenate(
            [sums[base + g * L:base + (g + 1) * L] for g in range(N_GROUPS)], axis=1)
        kk = kk0s[i] * lax.rsqrt(jnp.maximum(ss, 1e-24))
        kd = k * (1.0 - (neg_iclr + 1.0) * k_a)
        na = kk
        nb = kk * neg_iclr
        tri = jnp.where((col >= row) if reverse else (col <= row), 1.0, 0.0).astype(BF16)
        cum = _dot_split_rhs(tri, lw)
        tot = cum[0:1, :] if reverse else cum[L - 1:L, :]
        e_incl = jnp.exp2(cum)
        e_neg = 1.0 / e_incl
        w_tot = jnp.exp2(tot)
        e_rem = w_tot * e_neg
        preps.append(dict(
            r_t=r * e_incl, a_t=na * jnp.exp2(cum - lw), b_t=nb * e_neg, k_t=kd * e_neg,
            b_h=nb * e_rem, k_h=kd * e_rem, v=v, w_tot=w_tot))
    return preps


def _scan_units(preps, reverse, s_idx, s_ref):
    L = CHUNK
    rr = lax.broadcasted_iota(jnp.int32, (GROUP_W, GROUP_W), 0)
    cc = lax.broadcasted_iota(jnp.int32, (GROUP_W, GROUP_W), 1)
    head_shift = HEAD_SIZE.bit_length() - 1
    bd_mask = jnp.right_shift(rr, head_shift) == jnp.right_shift(cc, head_shift)
    tr = lax.broadcasted_iota(jnp.int32, (L, GROUP_W), 0)
    tc = jnp.bitwise_and(lax.broadcasted_iota(jnp.int32, (L, GROUP_W), 1), L - 1)
    eye = jnp.where(tc == tr, 1.0, 0.0)
    strict = {True: tc > tr, False: tc < tr}
    incl = {True: tc >= tr, False: tc <= tr}
    units = [(d, g) for d in range(len(preps)) for g in range(N_GROUPS)]
    sl = lambda g: slice(g * GROUP_W, (g + 1) * GROUP_W)
    ex = lambda x: _expand(x, bd_mask)

    lhs, a_ab, a_ak, a_rb, a_rk = {}, {}, {}, {}, {}
    for u in units:
        d, g = u
        p = preps[d]
        lhs[u] = jnp.concatenate([p["a_t"][:, sl(g)], p["r_t"][:, sl(g)]], axis=0)
        e_bk = jnp.concatenate([ex(p["b_t"][:, sl(g)]), ex(p["k_t"][:, sl(g)])], axis=0)
        a_all = _dot_nt(lhs[u], e_bk)
        a_ab[u] = jnp.where(strict[reverse[d]], a_all[0:L, 0:GROUP_W], 0.0)
        a_ak[u] = jnp.where(strict[reverse[d]], a_all[0:L, GROUP_W:], 0.0)
        a_rb[u] = jnp.where(incl[reverse[d]], a_all[L:, 0:GROUP_W], 0.0)
        a_rk[u] = jnp.where(incl[reverse[d]], a_all[L:, GROUP_W:], 0.0)
    av = {}
    for u in units:
        d, g = u
        av[u] = _dot(jnp.concatenate([a_ak[u], a_rk[u]], axis=0), ex(preps[d]["v"][:, sl(g)]))

    def nilpotent_inverse(n, index):
        prod = {u: eye + n[u] for u in units}
        npow = {u: _dot(n[u], ex(n[u])) for u in units}
        size = 2
        while size < index:
            for u in units:
                rhs = ex(npow[u])
                if 2 * size < index:
                    both = _dot(jnp.concatenate([npow[u], prod[u]], axis=0), rhs)
                    npow[u] = both[0:L]
                    prod[u] = prod[u] + both[L:]
                else:
                    prod[u] = prod[u] + _dot(prod[u], rhs)
            size *= 2
        return prod

    sh = INV_BLOCK.bit_length() - 1
    in_diag = jnp.right_shift(tr, sh) == jnp.right_shift(tc, sh)
    d_inv = nilpotent_inverse({u: jnp.where(in_diag, a_ab[u], 0.0) for u in units}, INV_BLOCK)
    e_mat = {u: _dot(d_inv[u], ex(jnp.where(in_diag, 0.0, a_ab[u]))) for u in units}
    e_inv = nilpotent_inverse(e_mat, L // INV_BLOCK)
    pinv = {u: _dot(e_inv[u], ex(d_inv[u])) for u in units}

    s_old, sa, x_u, u_u, y_u = {}, {}, {}, {}, {}
    for u in units:
        d, g = u
        s_old[u] = s_ref[s_idx[d] + (g,)]
        sa[u] = _dot(lhs[u], s_old[u]) + av[u]
    for u in units:
        x_u[u] = sa[u][0:L]
        u_u[u] = _dot(pinv[u], ex(x_u[u]))
    for u in units:
        y_u[u] = sa[u][L:] + _dot(a_rb[u], ex(u_u[u]))
    w_col = [jnp.transpose(jnp.broadcast_to(p["w_tot"], (LANES, RWKV_WIDTH))) for p in preps]
    for u in units:
        d, g = u
        p = preps[d]
        upd = _dot_tn(jnp.concatenate([p["b_h"][:, sl(g)], p["k_h"][:, sl(g)]], axis=0),
                      jnp.concatenate([u_u[u], p["v"][:, sl(g)]], axis=0))
        decay = jnp.concatenate([w_col[d][sl(g), :]] * (GROUP_W // LANES), axis=1)
        s_ref[s_idx[d] + (g,)] = s_old[u] * decay + jnp.where(bd_mask, upd, 0.0)
    return [jnp.concatenate([y_u[(d, g)] for g in range(N_GROUPS)], axis=1)
            for d in range(len(preps))]


def _scan_kernel(*refs, has_s0):
    if has_s0:
        pf_ref, pb_ref, s0_ref, lw_ref, vec_ref, ho_ref, yf_ref, yb_ref, s_ref = refs
    else:
        pf_ref, pb_ref, lw_ref, vec_ref, ho_ref, yf_ref, yb_ref, s_ref = refs
    c = pl.program_id(1)

    @pl.when(c == 0)
    def _():
        s_ref[...] = s0_ref[...] if has_s0 else jnp.zeros_like(s_ref)

    n_streams = 2 * SCAN_BATCH
    reverse = [bool(i % 2) for i in range(n_streams)]
    s_idx = [(i // 2, i % 2) for i in range(n_streams)]

    def chunk_body(i, carry):
        rf = pl.ds(pl.multiple_of(i * CHUNK, CHUNK), CHUNK)
        rb = pl.ds(pl.multiple_of((SCAN_CHUNKS - 1 - i) * CHUNK, CHUNK), CHUNK)
        blocks = []
        for bb in range(SCAN_BATCH):
            blocks += [(pf_ref[bb, rf, :].astype(F32), False), (pb_ref[bb, rb, :].astype(F32), True)]
        preps = _scan_prep(blocks, lw_ref, vec_ref, ho_ref[...])
        ys = _scan_units(preps, reverse, s_idx, s_ref)
        for bb in range(SCAN_BATCH):
            yf_ref[bb, rf, :] = ys[2 * bb]
            yb_ref[bb, rb, :] = ys[2 * bb + 1]
        return carry

    lax.fori_loop(0, SCAN_CHUNKS, chunk_body, 0)


def _scan_call(p, s0, lora_w, vecs, head_ones):
    bsz, seq, _ = p.shape
    rows = SCAN_CHUNKS * CHUNK
    nc = seq // rows
    state_block = (SCAN_BATCH, 2, N_GROUPS, GROUP_W, GROUP_W)
    state_map = lambda b, c: (b, 0, 0, 0, 0)
    state_in = [] if s0 is None else [pl.BlockSpec(state_block, state_map)]
    return pl.pallas_call(
        functools.partial(_scan_kernel, has_s0=s0 is not None),
        grid=(bsz // SCAN_BATCH, nc),
        in_specs=[pl.BlockSpec((SCAN_BATCH, rows, SCAN_COLS), lambda b, c: (b, c, 0)),
                  pl.BlockSpec((SCAN_BATCH, rows, SCAN_COLS), lambda b, c: (b, nc - 1 - c, 0)),
                  *state_in,
                  pl.BlockSpec((2, LORA_COLS, 2 * RWKV_WIDTH), lambda b, c: (0, 0, 0)),
                  pl.BlockSpec((2, 8, RWKV_WIDTH), lambda b, c: (0, 0, 0)),
                  pl.BlockSpec((GROUP_W, GROUP_W), lambda b, c: (0, 0))],
        out_specs=[pl.BlockSpec((SCAN_BATCH, rows, RWKV_WIDTH), lambda b, c: (b, c, 0)),
                   pl.BlockSpec((SCAN_BATCH, rows, RWKV_WIDTH), lambda b, c: (b, nc - 1 - c, 0)),
                   pl.BlockSpec(state_block, state_map)],
        out_shape=[jax.ShapeDtypeStruct((bsz, seq, RWKV_WIDTH), F32),
                   jax.ShapeDtypeStruct((bsz, seq, RWKV_WIDTH), F32),
                   jax.ShapeDtypeStruct((bsz, 2, N_GROUPS, GROUP_W, GROUP_W), F32)],
        compiler_params=pltpu.CompilerParams(
            dimension_semantics=("parallel", "arbitrary"), vmem_limit_bytes=VMEM_LIMIT),
        name="wkv_scan",
    )(p, p, *([] if s0 is None else [s0]), lora_w, vecs, head_ones)


def _mix_kernel(x_ref, yf_ref, yb_ref, rkv_ref, sm_ref, cv_ref, mod_ref, rw_ref, vec_ref, ho_ref,
                cw_ref, cvec_ref, wout_ref, g_ref, o_ref, upad, *, tm):
    n_lines = tm // GRID_W
    ones_bd = ho_ref[...]
    y = yf_ref[0] + yb_ref[0]
    inv_n = 1.0 / HEAD_SIZE
    mu = _head_sum(y, ones_bd) * inv_n
    dy = y - mu
    var = _head_sum(dy * dy, ones_bd) * inv_n
    lnx_w = vec_ref[0:1, :]
    lnx_b = vec_ref[1:2, :]
    a0_f = vec_ref[2:3, :]
    a0_b = vec_ref[3:4, :]
    k_a = vec_ref[4:5, :]
    r_k = vec_ref[5:6, :]
    yn = dy * lax.rsqrt(var + EPS_GN) * lnx_w + lnx_b
    sm = sm_ref[0].astype(F32)
    lane = lax.broadcasted_iota(jnp.int32, sm.shape, 1)
    sm = jnp.where(lane >= LORA_COLS, _sigmoid(sm), sm)
    lin = _dot(sm, rw_ref[...])
    iclr_f = _sigmoid(lin[:, 0:RWKV_WIDTH] + a0_f)
    iclr_b = _sigmoid(lin[:, RWKV_WIDTH:2 * RWKV_WIDTH] + a0_b)
    gate = lin[:, 2 * RWKV_WIDTH:]
    r = rkv_ref[0, :, 0:RWKV_WIDTH].astype(F32)
    k = rkv_ref[0, :, RWKV_WIDTH:2 * RWKV_WIDTH].astype(F32)
    v = rkv_ref[0, :, 2 * RWKV_WIDTH:3 * RWKV_WIDTH].astype(F32)
    k_bar = k * (1.0 + (0.5 * (iclr_f + iclr_b) - 1.0) * k_a)
    bonus = _head_sum(r * k_bar * r_k, ones_bd) * v
    rw_out = (yn + bonus) * gate

    cv = cv_ref[0].astype(F32)
    u = cv[:, 0:CONV_WIDTH] * _sigmoid(cv[:, CONV_WIDTH:])
    zeros = jnp.zeros((CONV_LEAD, CONV_WIDTH), F32)
    for ln in range(n_lines):
        u_ln = u[ln * GRID_W:(ln + 1) * GRID_W, :]
        for s in range(SUBLANES):
            upad[s, ln, 0:CONV_LEAD, :] = zeros
            upad[s, ln, CONV_ROWS - CONV_LEAD:CONV_ROWS, :] = zeros
            upad[s, ln, CONV_LEAD - s:CONV_LEAD - s + GRID_W, :] = u_ln
    conv_b = cvec_ref[0:1, :]
    ln_w = cvec_ref[1:2, :]
    ln_b = cvec_ref[2:3, :]
    conv_lines = []
    for ln in range(n_lines):
        strips = []
        for lb in range(CONV_WIDTH // LANES):
            lanes = slice(lb * LANES, (lb + 1) * LANES)
            acc = jnp.zeros((GRID_W, LANES), F32)
            for j in range(CONV_KERNEL):
                off = CONV_LEAD - CONV_PAD + j
                q8 = off // SUBLANES * SUBLANES
                acc = acc + cw_ref[j:j + 1, lanes] * upad[off % SUBLANES, ln, q8:q8 + GRID_W, lanes]
            strips.append(acc)
        conv_lines.append(jnp.concatenate(strips, axis=1))
    yc = jnp.concatenate(conv_lines, axis=0) + conv_b
    mu_c = jnp.mean(yc, axis=-1, keepdims=True)
    dc = yc - mu_c
    var_c = jnp.mean(dc * dc, axis=-1, keepdims=True)
    conv_out = _silu(dc * lax.rsqrt(var_c + EPS_LN) * ln_w + ln_b)

    mix_in = jnp.concatenate([rw_out.astype(BF16), conv_out.astype(BF16)], axis=1)
    mix = jnp.dot(mix_in, wout_ref[...], preferred_element_type=F32)
    gate_mix = mod_ref[0, 2:3, :]
    o_ref[0] = x_ref[0] + gate_mix * _rms(mix, g_ref[...])


def _mix_call(x, y_f, y_b, p, mods, rw_w, rvecs, head_ones, conv_w, cvecs, w_out, post_g):
    bsz, seq, _ = x.shape
    tm = 512
    tok = lambda width, blk: pl.BlockSpec((1, tm, width), lambda b, t: (b, t, blk))
    const2 = lambda shape: pl.BlockSpec(shape, lambda b, t: (0, 0))
    return pl.pallas_call(
        functools.partial(_mix_kernel, tm=tm),
        grid=(bsz, seq // tm),
        in_specs=[tok(D_MODEL, 0), tok(RWKV_WIDTH, 0), tok(RWKV_WIDTH, 0),
                  tok(3 * RWKV_WIDTH, 0),
                  tok(512, LORA_OFF // 512),
                  tok(2 * CONV_WIDTH, CONV_OFF // (2 * CONV_WIDTH)),
                  pl.BlockSpec((1, N_MOD, D_MODEL), lambda b, t: (b, 0, 0)),
                  const2((512, 3 * RWKV_WIDTH)), const2((8, RWKV_WIDTH)),
                  const2((GROUP_W, GROUP_W)), const2((32, CONV_WIDTH)),
                  const2((8, CONV_WIDTH)), const2((D_MODEL, D_MODEL)), const2((1, D_MODEL))],
        out_specs=tok(D_MODEL, 0),
        out_shape=jax.ShapeDtypeStruct((bsz, seq, D_MODEL), F32),
        scratch_shapes=[pltpu.VMEM((SUBLANES, tm // GRID_W, CONV_ROWS, CONV_WIDTH), F32)],
        compiler_params=pltpu.CompilerParams(
            dimension_semantics=("parallel", "parallel"), vmem_limit_bytes=VMEM_LIMIT),
        name="mix_out",
    )(x, y_f, y_b, p, p, p, mods, rw_w, rvecs, head_ones, conv_w, cvecs, w_out, post_g)


def _mlp_kernel(x_ref, mod_ref, gpre_ref, gpost_ref, w1_ref, w2_ref, o_ref, acc_ref):
    x = x_ref[0]
    shift = mod_ref[0, 3:4, :]
    scale = mod_ref[0, 4:5, :]
    gate = mod_ref[0, 5:6, :]
    h = (_rms(x, gpre_ref[...]) * (1.0 + scale) + shift).astype(BF16)
    ff_blk = 1024
    for c in range(D_FF // ff_blk):
        hid = jnp.dot(h, w1_ref[:, c * ff_blk:(c + 1) * ff_blk], preferred_element_type=F32)
        hid = jnp.square(jnp.maximum(hid, 0.0)).astype(BF16)
        part = jnp.dot(hid, w2_ref[c * ff_blk:(c + 1) * ff_blk, :], preferred_element_type=F32)
        if c == 0:
            acc_ref[...] = part
        else:
            acc_ref[...] += part
    o_ref[0] = x + gate * _rms(acc_ref[...], gpost_ref[...])


def _mlp_call(x, mods, pre_g, post_g, w1, w2):
    bsz, seq, _ = x.shape
    tm = 512
    return pl.pallas_call(
        _mlp_kernel,
        grid=(bsz, seq // tm),
        in_specs=[pl.BlockSpec((1, tm, D_MODEL), lambda b, t: (b, t, 0)),
                  pl.BlockSpec((1, N_MOD, D_MODEL), lambda b, t: (b, 0, 0)),
                  pl.BlockSpec((1, D_MODEL), lambda b, t: (0, 0)),
                  pl.BlockSpec((1, D_MODEL), lambda b, t: (0, 0)),
                  pl.BlockSpec((D_MODEL, D_FF), lambda b, t: (0, 0)),
                  pl.BlockSpec((D_FF, D_MODEL), lambda b, t: (0, 0))],
        out_specs=pl.BlockSpec((1, tm, D_MODEL), lambda b, t: (b, t, 0)),
        out_shape=jax.ShapeDtypeStruct((bsz, seq, D_MODEL), F32),
        scratch_shapes=[pltpu.VMEM((tm, D_MODEL), F32)],
        compiler_params=pltpu.CompilerParams(
            dimension_semantics=("parallel", "parallel"), vmem_limit_bytes=VMEM_LIMIT),
        name="sqrelu_mlp",
    )(x, mods, pre_g, post_g, w1, w2)


def _pad_cols(a, lo, hi):
    pad = jnp.zeros(a.shape[:-1] + (GATE_PAD - GATE_RANK,), a.dtype)
    return jnp.concatenate([a[..., :lo], pad, a[..., lo:hi]], axis=-1)


def _layer(x, ctx, mods_x, mods_c, prm):
    (mix_pre_g, mix_post_g, mlp_pre_g, mlp_post_g, w_in, mu_prev, mu_next, decay_w0, decay_w2,
     iclr_a0, iclr_a2, k_k, k_a, r_k, gate_w2, lnx_w, lnx_b, conv_w, conv_b, conv_ln_w,
     conv_ln_b, w_out, mlp_w1, mlp_w2) = prm
    bsz = x.shape[0]
    shift_cols = GATE_OFF + GATE_RANK
    in_cols = w_in.shape[1]

    w_pad = _pad_cols(w_in, shift_cols, in_cols).astype(BF16)
    zc = jnp.zeros((2 * CONV_WIDTH,), F32)
    mp_pad = _pad_cols(jnp.concatenate([mu_prev, zc]), shift_cols, in_cols)[None, :]
    mn_pad = _pad_cols(jnp.concatenate([mu_next, zc]), shift_cols, in_cols)[None, :]
    zr = jnp.zeros((DECAY_RANK, RWKV_WIDTH), F32)
    lora_dir = []
    for d in range(2):
        dec_rows = [zr, zr]
        dec_rows[d] = decay_w2[d]
        icl_rows = [zr, zr]
        icl_rows[d] = iclr_a2[d]
        left = jnp.concatenate(dec_rows + [zr, zr], axis=0)
        right = jnp.concatenate([zr, zr] + icl_rows, axis=0)
        lora_dir.append(jnp.concatenate([left, right], axis=1))
    lora_w = jnp.stack(lora_dir).astype(BF16)
    z4 = jnp.zeros((4, RWKV_WIDTH), F32)
    scan_vecs = jnp.stack([jnp.concatenate([decay_w0[d][None], iclr_a0[d][None], k_k[None],
                                            k_a[None], z4], axis=0) for d in range(2)])
    hid = jnp.arange(RWKV_WIDTH) // HEAD_SIZE
    head_ones = (hid[:, None] == hid[None, :]).astype(BF16)
    zl = jnp.zeros((2 * DECAY_RANK, 3 * RWKV_WIDTH), F32)
    za = jnp.zeros((ICLR_RANK, RWKV_WIDTH), F32)
    zg = jnp.zeros((GATE_PAD, RWKV_WIDTH), F32)
    gate_pad = jnp.concatenate([gate_w2, jnp.zeros((GATE_PAD - GATE_RANK, RWKV_WIDTH), F32)], axis=0)
    rw_w = jnp.concatenate([
        zl,
        jnp.concatenate([iclr_a2[0], za, za], axis=1),
        jnp.concatenate([za, iclr_a2[1], za], axis=1),
        jnp.concatenate([zg, zg, gate_pad], axis=1)], axis=0).astype(BF16)
    rvecs = jnp.concatenate([lnx_w[None], lnx_b[None], iclr_a0[0][None], iclr_a0[1][None],
                             k_a[None], r_k.reshape(1, RWKV_WIDTH), jnp.zeros((2, RWKV_WIDTH), F32)],
                            axis=0)
    conv_w_pad = jnp.concatenate([conv_w, jnp.zeros((1, CONV_WIDTH), F32)], axis=0)
    cvecs = jnp.concatenate([conv_b[None], conv_ln_w[None], conv_ln_b[None],
                             jnp.zeros((5, CONV_WIDTH), F32)], axis=0)

    seq, ctx_len = x.shape[1], ctx.shape[1]
    p_x = _proj_call(x, mods_x, mix_pre_g[None], w_pad, mp_pad, mn_pad, P_COLS // PROJ_BLOCK, seq)
    pack = seq // ctx_len
    p_c = _proj_call(ctx.reshape(bsz // pack, seq, D_MODEL), mods_c[:bsz // pack], mix_pre_g[None],
                     w_pad, mp_pad, mn_pad, N_SHIFT_BLOCKS, ctx_len).reshape(bsz, ctx_len, -1)
    _, _, s_ctx = _scan_call(p_c, None, lora_w, scan_vecs, head_ones)
    y_f, y_b, _ = _scan_call(p_x, s_ctx, lora_w, scan_vecs, head_ones)
    x = _mix_call(x, y_f, y_b, p_x, mods_x, rw_w, rvecs, head_ones, conv_w_pad, cvecs,
                  w_out.astype(BF16), mix_post_g[None])
    x = _mlp_call(x, mods_x, mlp_pre_g[None], mlp_post_g[None], mlp_w1.astype(BF16),
                  mlp_w2.astype(BF16))
    return x


def kernel(x, c, ctx, c_ctx, ada_w, ada_b, mix_pre_g, mix_post_g, mlp_pre_g, mlp_post_g, w_in, mu_prev, mu_next, decay_w0, decay_w2, iclr_a0, iclr_a2, k_k, k_a, r_k, gate_w2, lnx_w, lnx_b, conv_w, conv_b, conv_ln_w, conv_ln_b, w_out, mlp_w1, mlp_w2):
    depth = ada_w.shape[0]
    assert depth == 1, "context-stream update between layers is not implemented"
    bsz = x.shape[0]
    rows = 8 * ((bsz + 1 + 7) // 8)
    cc = jnp.concatenate([c, c_ctx[None, :], jnp.zeros((rows - bsz - 1, D_MODEL), F32)], axis=0)
    per_layer = (mix_pre_g, mix_post_g, mlp_pre_g, mlp_post_g, w_in, mu_prev, mu_next, decay_w0,
                 decay_w2, iclr_a0, iclr_a2, k_k, k_a, r_k, gate_w2, lnx_w, lnx_b, conv_w, conv_b,
                 conv_ln_w, conv_ln_b, w_out, mlp_w1, mlp_w2)
    for l in range(depth):
        mods = _ada_call(cc, ada_w[l], ada_b[l][None, :]).reshape(rows, N_MOD, D_MODEL)
        mods_x = mods[:bsz]
        mods_c = jnp.broadcast_to(mods[bsz:bsz + 1], (bsz, N_MOD, D_MODEL))
        x = _layer(x, ctx, mods_x, mods_c, tuple(a[l] for a in per_layer))
    return x
```

```python
import functools

import jax
import jax.numpy as jnp
from jax import lax
from jax.experimental import pallas as pl
from jax.experimental.pallas import tpu as pltpu

F32 = jnp.float32
BF16 = jnp.bfloat16

D_MODEL = 1024
RWKV_WIDTH = 512
CONV_WIDTH = 512
HEAD_SIZE = 64
RWKV_HEADS = 8
DECAY_RANK = 64
ICLR_RANK = 64
GATE_RANK = 160
CONV_KERNEL = 31
CONV_PAD = CONV_KERNEL // 2
GRID_W = 64
SUBLANES = 8
CONV_LEAD = 2 * SUBLANES
CONV_ROWS = (CONV_LEAD + CONV_PAD) // SUBLANES * SUBLANES + GRID_W
D_FF = 4 * D_MODEL
N_MOD = 6
EPS_RMS = 1e-6
EPS_LN = 1e-5
EPS_GN = 64e-5
DECAY_SCALE = 0.6065306597126334
LOG2_E = 1.4426950408889634

LORA_OFF = 3 * RWKV_WIDTH
LORA_COLS = 256
GATE_OFF = LORA_OFF + LORA_COLS
GATE_PAD = 256
CONV_OFF = GATE_OFF + GATE_PAD
P_COLS = CONV_OFF + 2 * CONV_WIDTH
SCAN_COLS = GATE_OFF
PROJ_BLOCK = 1024
N_SHIFT_BLOCKS = CONV_OFF // PROJ_BLOCK

CHUNK = 64
INV_BLOCK = 8
GROUP_HEADS = 4
GROUP_W = GROUP_HEADS * HEAD_SIZE
N_GROUPS = RWKV_HEADS // GROUP_HEADS
SCAN_BATCH = 4
SCAN_CHUNKS = 4

LANES = 128
VMEM_LIMIT = 56 * 1024 * 1024


def _sigmoid(x):
    return 0.5 * jnp.tanh(0.5 * x) + 0.5


def _silu(x):
    return x * _sigmoid(x)


def _rms(x, g):
    return x * lax.rsqrt(jnp.mean(x * x, axis=-1, keepdims=True) + EPS_RMS) * g


def _dot(a, b):
    return jnp.dot(a.astype(BF16), b.astype(BF16), preferred_element_type=F32)


def _dot_nt(a, b):
    return lax.dot_general(a.astype(BF16), b.astype(BF16), (((1,), (1,)), ((), ())),
                           preferred_element_type=F32)


def _dot_tn(a, b):
    return lax.dot_general(a.astype(BF16), b.astype(BF16), (((0,), (0,)), ((), ())),
                           preferred_element_type=F32)


def _dot_split_rhs(a_bf16, x):
    hi = x.astype(BF16)
    lo = (x - hi.astype(F32)).astype(BF16)
    d = lambda p: jnp.dot(a_bf16, p, preferred_element_type=F32)
    return d(hi) + d(lo)


def _head_sum(z, ones_bd):
    rows = z.shape[0]
    hi = z.astype(BF16)
    lo = (z - hi.astype(F32)).astype(BF16)
    pieces = []
    for g in range(N_GROUPS):
        pieces += [hi[:, g * GROUP_W:(g + 1) * GROUP_W], lo[:, g * GROUP_W:(g + 1) * GROUP_W]]
    res = jnp.dot(jnp.concatenate(pieces, axis=0), ones_bd, preferred_element_type=F32)
    return jnp.concatenate(
        [res[(2 * g) * rows:(2 * g + 1) * rows] + res[(2 * g + 1) * rows:(2 * g + 2) * rows]
         for g in range(N_GROUPS)], axis=1)


def _ada_kernel(c_ref, w_ref, b_ref, o_ref):
    o_ref[...] = _dot(_silu(c_ref[...]), w_ref[...]) + b_ref[...]


def _ada_call(cc, ada_w, ada_b):
    rows = cc.shape[0]
    n_out = ada_w.shape[1]
    blk = 1536
    return pl.pallas_call(
        _ada_kernel,
        grid=(n_out // blk,),
        in_specs=[pl.BlockSpec((rows, D_MODEL), lambda n: (0, 0)),
                  pl.BlockSpec((D_MODEL, blk), lambda n: (0, n)),
                  pl.BlockSpec((1, blk), lambda n: (0, n))],
        out_specs=pl.BlockSpec((rows, blk), lambda n: (0, n)),
        out_shape=jax.ShapeDtypeStruct((rows, n_out), F32),
        compiler_params=pltpu.CompilerParams(vmem_limit_bytes=VMEM_LIMIT),
        name="ada_mod",
    )(cc, ada_w, ada_b)


def _proj_kernel(x_ref, mod_ref, g_ref, w_ref, mp_ref, mn_ref, o_ref, h_scr, z_scr, *, seq, seg,
                 col_of_step):
    n = pl.program_id(1)
    col = col_of_step(n)
    row_chunk = min(seq, 1024)

    @pl.when(n == 0)
    def _():
        shift = mod_ref[0, 0:1, :]
        scale = mod_ref[0, 1:2, :]
        g = g_ref[...]
        for s in range(0, seq, row_chunk):
            xs = x_ref[0, s:s + row_chunk, :]
            h_scr[s:s + row_chunk, :] = (_rms(xs, g) * (1.0 + scale) + shift).astype(BF16)

    @pl.when(col < N_SHIFT_BLOCKS)
    def _():
        lead = SUBLANES
        z_scr[0:lead, :] = jnp.zeros((lead, PROJ_BLOCK), F32)
        z_scr[lead + seq:2 * lead + seq, :] = jnp.zeros((lead, PROJ_BLOCK), F32)
        mp = mp_ref[...]
        mn = mn_ref[...]

        def shift_rows(s):
            cur = z_scr[lead + s:lead + s + row_chunk, :]
            prv = z_scr[lead - 1 + s:lead - 1 + s + row_chunk, :]
            nxt = z_scr[lead + 1 + s:lead + 1 + s + row_chunk, :]
            if seg < seq:
                pos = jnp.bitwise_and(
                    s + lax.broadcasted_iota(jnp.int32, (row_chunk, PROJ_BLOCK), 0), seg - 1)
                prv = jnp.where(pos == 0, 0.0, prv)
                nxt = jnp.where(pos == seg - 1, 0.0, nxt)
            o_ref[0, s:s + row_chunk, :] = (cur + mp * (prv - cur) + mn * (nxt - cur)).astype(BF16)

        for s in range(0, seq, row_chunk):
            z_scr[lead + s:lead + s + row_chunk, :] = jnp.dot(
                h_scr[s:s + row_chunk, :], w_ref[col], preferred_element_type=F32)
            if s > 0:
                shift_rows(s - row_chunk)
        shift_rows(seq - row_chunk)

    @pl.when(col >= N_SHIFT_BLOCKS)
    def _():
        for s in range(0, seq, row_chunk):
            o_ref[0, s:s + row_chunk, :] = jnp.dot(
                h_scr[s:s + row_chunk, :], w_ref[col], preferred_element_type=F32).astype(BF16)


def _proj_call(x, mods, g, w_pad, mp_pad, mn_pad, n_blocks, seg):
    bsz, seq, _ = x.shape
    assert seq % seg == 0 and seg & (seg - 1) == 0
    n_cols = n_blocks * PROJ_BLOCK
    first = N_SHIFT_BLOCKS % n_blocks
    w_blocks = jnp.transpose(w_pad[:, :n_cols].reshape(D_MODEL, n_blocks, PROJ_BLOCK), (1, 0, 2))
    col_of_step = lambda n: lax.rem(n + first, n_blocks)
    return pl.pallas_call(
        functools.partial(_proj_kernel, seq=seq, seg=seg, col_of_step=col_of_step),
        grid=(bsz, n_blocks),
        in_specs=[pl.BlockSpec((1, seq, D_MODEL), lambda b, n: (b, 0, 0)),
                  pl.BlockSpec((1, N_MOD, D_MODEL), lambda b, n: (b, 0, 0)),
                  pl.BlockSpec((1, D_MODEL), lambda b, n: (0, 0)),
                  pl.BlockSpec((n_blocks, D_MODEL, PROJ_BLOCK), lambda b, n: (0, 0, 0)),
                  pl.BlockSpec((1, PROJ_BLOCK), lambda b, n: (0, col_of_step(n))),
                  pl.BlockSpec((1, PROJ_BLOCK), lambda b, n: (0, col_of_step(n)))],
        out_specs=pl.BlockSpec((1, seq, PROJ_BLOCK), lambda b, n: (b, 0, col_of_step(n))),
        out_shape=jax.ShapeDtypeStruct((bsz, seq, n_cols), BF16),
        scratch_shapes=[pltpu.VMEM((seq, D_MODEL), BF16),
                        pltpu.VMEM((seq + 2 * SUBLANES, PROJ_BLOCK), F32)],
        compiler_params=pltpu.CompilerParams(
            dimension_semantics=("parallel", "arbitrary"), vmem_limit_bytes=VMEM_LIMIT),
        name="in_proj",
    )(x, mods, g, w_blocks, mp_pad, mn_pad)


def _expand(x, bd_mask):
    return jnp.where(bd_mask, jnp.concatenate([x] * GROUP_HEADS, axis=0), 0.0)


def _scan_prep(blocks, lw_ref, vec_ref, ones_bd):
    L = CHUNK
    n = len(blocks)
    lane = lax.broadcasted_iota(jnp.int32, (L, LORA_COLS), 1)
    loras = []
    for pb, _ in blocks:
        lora = pb[:, LORA_OFF:LORA_OFF + LORA_COLS]
        loras.append(jnp.where(lane < 2 * DECAY_RANK, jnp.tanh(lora), lora))
    lin = [None] * n
    for d in range(2):
        idx = [i for i in range(n) if blocks[i][1] == bool(d)]
        res = _dot(jnp.concatenate([loras[i] for i in idx], axis=0), lw_ref[d])
        for j, i in enumerate(idx):
            lin[i] = res[j * L:(j + 1) * L]

    kk0s, pieces = [], []
    for i, (pb, reverse) in enumerate(blocks):
        kk0 = pb[:, RWKV_WIDTH:2 * RWKV_WIDTH] * vec_ref[int(reverse), 2:3, :]
        kk0s.append(kk0)
        sq = (kk0 * kk0).astype(BF16)
        pieces += [sq[:, g * GROUP_W:(g + 1) * GROUP_W] for g in range(N_GROUPS)]
    sums = jnp.dot(jnp.concatenate(pieces, axis=0), ones_bd, preferred_element_type=F32)

    row = lax.broadcasted_iota(jnp.int32, (L, L), 0)
    col = lax.broadcasted_iota(jnp.int32, (L, L), 1)
    preps = []
    for i, (pb, reverse) in enumerate(blocks):
        vecs = vec_ref[int(reverse)]
        w0 = vecs[0:1, :]
        a0 = vecs[1:2, :]
        k_a = vecs[3:4, :]
        r = pb[:, 0:RWKV_WIDTH]
        k = pb[:, RWKV_WIDTH:2 * RWKV_WIDTH]
        v = pb[:, 2 * RWKV_WIDTH:3 * RWKV_WIDTH]
        half = 0.5 * LOG2_E * DECAY_SCALE
        lw = -half * jnp.tanh(0.5 * (lin[i][:, 0:RWKV_WIDTH] + w0)) - half
        neg_iclr = -0.5 * jnp.tanh(0.5 * (lin[i][:, RWKV_WIDTH:] + a0)) - 0.5
        base = i * N_GROUPS * L
        ss = jnp.concatenate(
            [sums[base + g * L:base + (g + 1) * L] for g in range(N_GROUPS)], axis=1)
        kk = kk0s[i] * lax.rsqrt(jnp.maximum(ss, 1e-24))
        kd = k * (1.0 - (neg_iclr + 1.0) * k_a)
        na = kk
        nb = kk * neg_iclr
        tri = jnp.where((col >= row) if reverse else (col <= row), 1.0, 0.0).astype(BF16)
        cum = _dot_split_rhs(tri, lw)
        tot = cum[0:1, :] if reverse else cum[L - 1:L, :]
        e_incl = jnp.exp2(cum)
        e_neg = 1.0 / e_incl
        w_tot = jnp.exp2(tot)
        e_rem = w_tot * e_neg
        preps.append(dict(
            r_t=r * e_incl, a_t=na * jnp.exp2(cum - lw), b_t=nb * e_neg, k_t=kd * e_neg,
            b_h=nb * e_rem, k_h=kd * e_rem, v=v, w_tot=w_tot))
    return preps


def _scan_units(preps, reverse, s_idx, s_ref):
    L = CHUNK
    rr = lax.broadcasted_iota(jnp.int32, (GROUP_W, GROUP_W), 0)
    cc = lax.broadcasted_iota(jnp.int32, (GROUP_W, GROUP_W), 1)
    head_shift = HEAD_SIZE.bit_length() - 1
    bd_mask = jnp.right_shift(rr, head_shift) == jnp.right_shift(cc, head_shift)
    tr = lax.broadcasted_iota(jnp.int32, (L, GROUP_W), 0)
    tc = jnp.bitwise_and(lax.broadcasted_iota(jnp.int32, (L, GROUP_W), 1), L - 1)
    eye = jnp.where(tc == tr, 1.0, 0.0)
    strict = {True: tc > tr, False: tc < tr}
    incl = {True: tc >= tr, False: tc <= tr}
    units = [(d, g) for d in range(len(preps)) for g in range(N_GROUPS)]
    sl = lambda g: slice(g * GROUP_W, (g + 1) * GROUP_W)
    ex = lambda x: _expand(x, bd_mask)

    lhs, a_ab, a_ak, a_rb, a_rk = {}, {}, {}, {}, {}
    for u in units:
        d, g = u
        p = preps[d]
        lhs[u] = jnp.concatenate([p["a_t"][:, sl(g)], p["r_t"][:, sl(g)]], axis=0)
        e_bk = jnp.concatenate([ex(p["b_t"][:, sl(g)]), ex(p["k_t"][:, sl(g)])], axis=0)
        a_all = _dot_nt(lhs[u], e_bk)
        a_ab[u] = jnp.where(strict[reverse[d]], a_all[0:L, 0:GROUP_W], 0.0)
        a_ak[u] = jnp.where(strict[reverse[d]], a_all[0:L, GROUP_W:], 0.0)
        a_rb[u] = jnp.where(incl[reverse[d]], a_all[L:, 0:GROUP_W], 0.0)
        a_rk[u] = jnp.where(incl[reverse[d]], a_all[L:, GROUP_W:], 0.0)
    av = {}
    for u in units:
        d, g = u
        av[u] = _dot(jnp.concatenate([a_ak[u], a_rk[u]], axis=0), ex(preps[d]["v"][:, sl(g)]))

    def nilpotent_inverse(n, index):
        prod = {u: eye + n[u] for u in units}
        npow = {u: _dot(n[u], ex(n[u])) for u in units}
        size = 2
        while size < index:
            for u in units:
                rhs = ex(npow[u])
                if 2 * size < index:
                    both = _dot(jnp.concatenate([npow[u], prod[u]], axis=0), rhs)
                    npow[u] = both[0:L]
                    prod[u] = prod[u] + both[L:]
                else:
                    prod[u] = prod[u] + _dot(prod[u], rhs)
            size *= 2
        return prod

    sh = INV_BLOCK.bit_length() - 1
    in_diag = jnp.right_shift(tr, sh) == jnp.right_shift(tc, sh)
    d_inv = nilpotent_inverse({u: jnp.where(in_diag, a_ab[u], 0.0) for u in units}, INV_BLOCK)
    e_mat = {u: _dot(d_inv[u], ex(jnp.where(in_diag, 0.0, a_ab[u]))) for u in units}
    e_inv = nilpotent_inverse(e_mat, L // INV_BLOCK)
    pinv = {u: _dot(e_inv[u], ex(d_inv[u])) for u in units}

    s_old, sa, x_u, u_u, y_u = {}, {}, {}, {}, {}
    for u in units:
        d, g = u
        s_old[u] = s_ref[s_idx[d] + (g,)]
        sa[u] = _dot(lhs[u], s_old[u]) + av[u]
    for u in units:
        x_u[u] = sa[u][0:L]
        u_u[u] = _dot(pinv[u], ex(x_u[u]))
    for u in units:
        y_u[u] = sa[u][L:] + _dot(a_rb[u], ex(u_u[u]))
    w_col = [jnp.transpose(jnp.broadcast_to(p["w_tot"], (LANES, RWKV_WIDTH))) for p in preps]
    for u in units:
        d, g = u
        p = preps[d]
        upd = _dot_tn(jnp.concatenate([p["b_h"][:, sl(g)], p["k_h"][:, sl(g)]], axis=0),
                      jnp.concatenate([u_u[u], p["v"][:, sl(g)]], axis=0))
        decay = jnp.concatenate([w_col[d][sl(g), :]] * (GROUP_W // LANES), axis=1)
        s_ref[s_idx[d] + (g,)] = s_old[u] * decay + jnp.where(bd_mask, upd, 0.0)
    return [jnp.concatenate([y_u[(d, g)] for g in range(N_GROUPS)], axis=1)
            for d in range(len(preps))]


def _scan_kernel(*refs, has_s0):
    if has_s0:
        pf_ref, pb_ref, s0_ref, lw_ref, vec_ref, ho_ref, yf_ref, yb_ref, s_ref = refs
    else:
        pf_ref, pb_ref, lw_ref, vec_ref, ho_ref, yf_ref, yb_ref, s_ref = refs
    c = pl.program_id(1)

    @pl.when(c == 0)
    def _():
        s_ref[...] = s0_ref[...] if has_s0 else jnp.zeros_like(s_ref)

    n_streams = 2 * SCAN_BATCH
    reverse = [bool(i % 2) for i in range(n_streams)]
    s_idx = [(i // 2, i % 2) for i in range(n_streams)]

    def chunk_body(i, carry):
        rf = pl.ds(pl.multiple_of(i * CHUNK, CHUNK), CHUNK)
        rb = pl.ds(pl.multiple_of((SCAN_CHUNKS - 1 - i) * CHUNK, CHUNK), CHUNK)
        blocks = []
        for bb in range(SCAN_BATCH):
            blocks += [(pf_ref[bb, rf, :].astype(F32), False), (pb_ref[bb, rb, :].astype(F32), True)]
        preps = _scan_prep(blocks, lw_ref, vec_ref, ho_ref[...])
        ys = _scan_units(preps, reverse, s_idx, s_ref)
        for bb in range(SCAN_BATCH):
            yf_ref[bb, rf, :] = ys[2 * bb]
            yb_ref[bb, rb, :] = ys[2 * bb + 1]
        return carry

    lax.fori_loop(0, SCAN_CHUNKS, chunk_body, 0)


def _scan_call(p, s0, lora_w, vecs, head_ones):
    bsz, seq, _ = p.shape
    rows = SCAN_CHUNKS * CHUNK
    nc = seq // rows
    state_block = (SCAN_BATCH, 2, N_GROUPS, GROUP_W, GROUP_W)
    state_map = lambda b, c: (b, 0, 0, 0, 0)
    state_in = [] if s0 is None else [pl.BlockSpec(state_block, state_map)]
    return pl.pallas_call(
        functools.partial(_scan_kernel, has_s0=s0 is not None),
        grid=(bsz // SCAN_BATCH, nc),
        in_specs=[pl.BlockSpec((SCAN_BATCH, rows, SCAN_COLS), lambda b, c: (b, c, 0)),
                  pl.BlockSpec((SCAN_BATCH, rows, SCAN_COLS), lambda b, c: (b, nc - 1 - c, 0)),
                  *state_in,
                  pl.BlockSpec((2, LORA_COLS, 2 * RWKV_WIDTH), lambda b, c: (0, 0, 0)),
                  pl.BlockSpec((2, 8, RWKV_WIDTH), lambda b, c: (0, 0, 0)),
                  pl.BlockSpec((GROUP_W, GROUP_W), lambda b, c: (0, 0))],
        out_specs=[pl.BlockSpec((SCAN_BATCH, rows, RWKV_WIDTH), lambda b, c: (b, c, 0)),
                   pl.BlockSpec((SCAN_BATCH, rows, RWKV_WIDTH), lambda b, c: (b, nc - 1 - c, 0)),
                   pl.BlockSpec(state_block, state_map)],
        out_shape=[jax.ShapeDtypeStruct((bsz, seq, RWKV_WIDTH), F32),
                   jax.ShapeDtypeStruct((bsz, seq, RWKV_WIDTH), F32),
                   jax.ShapeDtypeStruct((bsz, 2, N_GROUPS, GROUP_W, GROUP_W), F32)],
        compiler_params=pltpu.CompilerParams(
            dimension_semantics=("parallel", "arbitrary"), vmem_limit_bytes=VMEM_LIMIT),
        name="wkv_scan",
    )(p, p, *([] if s0 is None else [s0]), lora_w, vecs, head_ones)


def _mix_kernel(x_ref, yf_ref, yb_ref, rkv_ref, sm_ref, cv_ref, mod_ref, rw_ref, vec_ref, ho_ref,
                cw_ref, cvec_ref, wout_ref, g_ref, o_ref, upad, *, tm):
    n_lines = tm // GRID_W
    ones_bd = ho_ref[...]
    y = yf_ref[0] + yb_ref[0]
    inv_n = 1.0 / HEAD_SIZE
    mu = _head_sum(y, ones_bd) * inv_n
    dy = y - mu
    var = _head_sum(dy * dy, ones_bd) * inv_n
    lnx_w = vec_ref[0:1, :]
    lnx_b = vec_ref[1:2, :]
    a0_f = vec_ref[2:3, :]
    a0_b = vec_ref[3:4, :]
    k_a = vec_ref[4:5, :]
    r_k = vec_ref[5:6, :]
    yn = dy * lax.rsqrt(var + EPS_GN) * lnx_w + lnx_b
    sm = sm_ref[0].astype(F32)
    lane = lax.broadcasted_iota(jnp.int32, sm.shape, 1)
    sm = jnp.where(lane >= LORA_COLS, _sigmoid(sm), sm)
    lin = _dot(sm, rw_ref[...])
    iclr_f = _sigmoid(lin[:, 0:RWKV_WIDTH] + a0_f)
    iclr_b = _sigmoid(lin[:, RWKV_WIDTH:2 * RWKV_WIDTH] + a0_b)
    gate = lin[:, 2 * RWKV_WIDTH:]
    r = rkv_ref[0, :, 0:RWKV_WIDTH].astype(F32)
    k = rkv_ref[0, :, RWKV_WIDTH:2 * RWKV_WIDTH].astype(F32)
    v = rkv_ref[0, :, 2 * RWKV_WIDTH:3 * RWKV_WIDTH].astype(F32)
    k_bar = k * (1.0 + (0.5 * (iclr_f + iclr_b) - 1.0) * k_a)
    bonus = _head_sum(r * k_bar * r_k, ones_bd) * v
    rw_out = (yn + bonus) * gate

    cv = cv_ref[0].astype(F32)
    u = cv[:, 0:CONV_WIDTH] * _sigmoid(cv[:, CONV_WIDTH:])
    zeros = jnp.zeros((CONV_LEAD, CONV_WIDTH), F32)
    for ln in range(n_lines):
        u_ln = u[ln * GRID_W:(ln + 1) * GRID_W, :]
        for s in range(SUBLANES):
            upad[s, ln, 0:CONV_LEAD, :] = zeros
            upad[s, ln, CONV_ROWS - CONV_LEAD:CONV_ROWS, :] = zeros
            upad[s, ln, CONV_LEAD - s:CONV_LEAD - s + GRID_W, :] = u_ln
    conv_b = cvec_ref[0:1, :]
    ln_w = cvec_ref[1:2, :]
    ln_b = cvec_ref[2:3, :]
    conv_lines = []
    for ln in range(n_lines):
        strips = []
        for lb in range(CONV_WIDTH // LANES):
            lanes = slice(lb * LANES, (lb + 1) * LANES)
            acc = jnp.zeros((GRID_W, LANES), F32)
            for j in range(CONV_KERNEL):
                off = CONV_LEAD - CONV_PAD + j
                q8 = off // SUBLANES * SUBLANES
                acc = acc + cw_ref[j:j + 1, lanes] * upad[off % SUBLANES, ln, q8:q8 + GRID_W, lanes]
            strips.append(acc)
        conv_lines.append(jnp.concatenate(strips, axis=1))
    yc = jnp.concatenate(conv_lines, axis=0) + conv_b
    mu_c = jnp.mean(yc, axis=-1, keepdims=True)
    dc = yc - mu_c
    var_c = jnp.mean(dc * dc, axis=-1, keepdims=True)
    conv_out = _silu(dc * lax.rsqrt(var_c + EPS_LN) * ln_w + ln_b)

    mix_in = jnp.concatenate([rw_out.astype(BF16), conv_out.astype(BF16)], axis=1)
    mix = jnp.dot(mix_in, wout_ref[...], preferred_element_type=F32)
    gate_mix = mod_ref[0, 2:3, :]
    o_ref[0] = x_ref[0] + gate_mix * _rms(mix, g_ref[...])


def _mix_call(x, y_f, y_b, p, mods, rw_w, rvecs, head_ones, conv_w, cvecs, w_out, post_g):
    bsz, seq, _ = x.shape
    tm = 512
    tok = lambda width, blk: pl.BlockSpec((1, tm, width), lambda b, t: (b, t, blk))
    const2 = lambda shape: pl.BlockSpec(shape, lambda b, t: (0, 0))
    return pl.pallas_call(
        functools.partial(_mix_kernel, tm=tm),
        grid=(bsz, seq // tm),
        in_specs=[tok(D_MODEL, 0), tok(RWKV_WIDTH, 0), tok(RWKV_WIDTH, 0),
                  tok(3 * RWKV_WIDTH, 0),
                  tok(512, LORA_OFF // 512),
                  tok(2 * CONV_WIDTH, CONV_OFF // (2 * CONV_WIDTH)),
                  pl.BlockSpec((1, N_MOD, D_MODEL), lambda b, t: (b, 0, 0)),
                  const2((512, 3 * RWKV_WIDTH)), const2((8, RWKV_WIDTH)),
                  const2((GROUP_W, GROUP_W)), const2((32, CONV_WIDTH)),
                  const2((8, CONV_WIDTH)), const2((D_MODEL, D_MODEL)), const2((1, D_MODEL))],
        out_specs=tok(D_MODEL, 0),
        out_shape=jax.ShapeDtypeStruct((bsz, seq, D_MODEL), F32),
        scratch_shapes=[pltpu.VMEM((SUBLANES, tm // GRID_W, CONV_ROWS, CONV_WIDTH), F32)],
        compiler_params=pltpu.CompilerParams(
            dimension_semantics=("parallel", "parallel"), vmem_limit_bytes=VMEM_LIMIT),
        name="mix_out",
    )(x, y_f, y_b, p, p, p, mods, rw_w, rvecs, head_ones, conv_w, cvecs, w_out, post_g)


def _mlp_kernel(x_ref, mod_ref, gpre_ref, gpost_ref, w1_ref, w2_ref, o_ref, acc_ref):
    x = x_ref[0]
    shift = mod_ref[0, 3:4, :]
    scale = mod_ref[0, 4:5, :]
    gate = mod_ref[0, 5:6, :]
    h = (_rms(x, gpre_ref[...]) * (1.0 + scale) + shift).astype(BF16)
    ff_blk = 1024
    for c in range(D_FF // ff_blk):
        hid = jnp.dot(h, w1_ref[:, c * ff_blk:(c + 1) * ff_blk], preferred_element_type=F32)
        hid = jnp.square(jnp.maximum(hid, 0.0)).astype(BF16)
        part = jnp.dot(hid, w2_ref[c * ff_blk:(c + 1) * ff_blk, :], preferred_element_type=F32)
        if c == 0:
            acc_ref[...] = part
        else:
            acc_ref[...] += part
    o_ref[0] = x + gate * _rms(acc_ref[...], gpost_ref[...])


def _mlp_call(x, mods, pre_g, post_g, w1, w2):
    bsz, seq, _ = x.shape
    tm = 512
    return pl.pallas_call(
        _mlp_kernel,
        grid=(bsz, seq // tm),
        in_specs=[pl.BlockSpec((1, tm, D_MODEL), lambda b, t: (b, t, 0)),
                  pl.BlockSpec((1, N_MOD, D_MODEL), lambda b, t: (b, 0, 0)),
                  pl.BlockSpec((1, D_MODEL), lambda b, t: (0, 0)),
                  pl.BlockSpec((1, D_MODEL), lambda b, t: (0, 0)),
                  pl.BlockSpec((D_MODEL, D_FF), lambda b, t: (0, 0)),
                  pl.BlockSpec((D_FF, D_MODEL), lambda b, t: (0, 0))],
        out_specs=pl.BlockSpec((1, tm, D_MODEL), lambda b, t: (b, t, 0)),
        out_shape=jax.ShapeDtypeStruct((bsz, seq, D_MODEL), F32),
        scratch_shapes=[pltpu.VMEM((tm, D_MODEL), F32)],
        compiler_params=pltpu.CompilerParams(
            dimension_semantics=("parallel", "parallel"), vmem_limit_bytes=VMEM_LIMIT),
        name="sqrelu_mlp",
    )(x, mods, pre_g, post_g, w1, w2)


def _pad_cols(a, lo, hi):
    pad = jnp.zeros(a.shape[:-1] + (GATE_PAD - GATE_RANK,), a.dtype)
    return jnp.concatenate([a[..., :lo], pad, a[..., lo:hi]], axis=-1)


def _layer(x, ctx, mods_x, mods_c, prm):
    (mix_pre_g, mix_post_g, mlp_pre_g, mlp_post_g, w_in, mu_prev, mu_next, decay_w0, decay_w2,
     iclr_a0, iclr_a2, k_k, k_a, r_k, gate_w2, lnx_w, lnx_b, conv_w, conv_b, conv_ln_w,
     conv_ln_b, w_out, mlp_w1, mlp_w2) = prm
    bsz = x.shape[0]
    shift_cols = GATE_OFF + GATE_RANK
    in_cols = w_in.shape[1]

    w_pad = _pad_cols(w_in, shift_cols, in_cols).astype(BF16)
    zc = jnp.zeros((2 * CONV_WIDTH,), F32)
    mp_pad = _pad_cols(jnp.concatenate([mu_prev, zc]), shift_cols, in_cols)[None, :]
    mn_pad = _pad_cols(jnp.concatenate([mu_next, zc]), shift_cols, in_cols)[None, :]
    zr = jnp.zeros((DECAY_RANK, RWKV_WIDTH), F32)
    lora_dir = []
    for d in range(2):
        dec_rows = [zr, zr]
        dec_rows[d] = decay_w2[d]
        icl_rows = [zr, zr]
        icl_rows[d] = iclr_a2[d]
        left = jnp.concatenate(dec_rows + [zr, zr], axis=0)
        right = jnp.concatenate([zr, zr] + icl_rows, axis=0)
        lora_dir.append(jnp.concatenate([left, right], axis=1))
    lora_w = jnp.stack(lora_dir).astype(BF16)
    z4 = jnp.zeros((4, RWKV_WIDTH), F32)
    scan_vecs = jnp.stack([jnp.concatenate([decay_w0[d][None], iclr_a0[d][None], k_k[None],
                                            k_a[None], z4], axis=0) for d in range(2)])
    hid = jnp.arange(RWKV_WIDTH) // HEAD_SIZE
    head_ones = (hid[:, None] == hid[None, :]).astype(BF16)
    zl = jnp.zeros((2 * DECAY_RANK, 3 * RWKV_WIDTH), F32)
    za = jnp.zeros((ICLR_RANK, RWKV_WIDTH), F32)
    zg = jnp.zeros((GATE_PAD, RWKV_WIDTH), F32)
    gate_pad = jnp.concatenate([gate_w2, jnp.zeros((GATE_PAD - GATE_RANK, RWKV_WIDTH), F32)], axis=0)
    rw_w = jnp.concatenate([
        zl,
        jnp.concatenate([iclr_a2[0], za, za], axis=1),
        jnp.concatenate([za, iclr_a2[1], za], axis=1),
        jnp.concatenate([zg, zg, gate_pad], axis=1)], axis=0).astype(BF16)
    rvecs = jnp.concatenate([lnx_w[None], lnx_b[None], iclr_a0[0][None], iclr_a0[1][None],
                             k_a[None], r_k.reshape(1, RWKV_WIDTH), jnp.zeros((2, RWKV_WIDTH), F32)],
                            axis=0)
    conv_w_pad = jnp.concatenate([conv_w, jnp.zeros((1, CONV_WIDTH), F32)], axis=0)
    cvecs = jnp.concatenate([conv_b[None], conv_ln_w[None], conv_ln_b[None],
                             jnp.zeros((5, CONV_WIDTH), F32)], axis=0)

    seq, ctx_len = x.shape[1], ctx.shape[1]
    p_x = _proj_call(x, mods_x, mix_pre_g[None], w_pad, mp_pad, mn_pad, P_COLS // PROJ_BLOCK, seq)
    pack = seq // ctx_len
    p_c = _proj_call(ctx.reshape(bsz // pack, seq, D_MODEL), mods_c[:bsz // pack], mix_pre_g[None],
                     w_pad, mp_pad, mn_pad, N_SHIFT_BLOCKS, ctx_len).reshape(bsz, ctx_len, -1)
    _, _, s_ctx = _scan_call(p_c, None, lora_w, scan_vecs, head_ones)
    y_f, y_b, _ = _scan_call(p_x, s_ctx, lora_w, scan_vecs, head_ones)
    x = _mix_call(x, y_f, y_b, p_x, mods_x, rw_w, rvecs, head_ones, conv_w_pad, cvecs,
                  w_out.astype(BF16), mix_post_g[None])
    x = _mlp_call(x, mods_x, mlp_pre_g[None], mlp_post_g[None], mlp_w1.astype(BF16),
                  mlp_w2.astype(BF16))
    return x


def kernel(x, c, ctx, c_ctx, ada_w, ada_b, mix_pre_g, mix_post_g, mlp_pre_g, mlp_post_g, w_in, mu_prev, mu_next, decay_w0, decay_w2, iclr_a0, iclr_a2, k_k, k_a, r_k, gate_w2, lnx_w, lnx_b, conv_w, conv_b, conv_ln_w, conv_ln_b, w_out, mlp_w1, mlp_w2):
    depth = ada_w.shape[0]
    assert depth == 1, "context-stream update between layers is not implemented"
    bsz = x.shape[0]
    rows = 8 * ((bsz + 1 + 7) // 8)
    cc = jnp.concatenate([c, c_ctx[None, :], jnp.zeros((rows - bsz - 1, D_MODEL), F32)], axis=0)
    per_layer = (mix_pre_g, mix_post_g, mlp_pre_g, mlp_post_g, w_in, mu_prev, mu_next, decay_w0,
                 decay_w2, iclr_a0, iclr_a2, k_k, k_a, r_k, gate_w2, lnx_w, lnx_b, conv_w, conv_b,
                 conv_ln_w, conv_ln_b, w_out, mlp_w1, mlp_w2)
    for l in range(depth):
        mods = _ada_call(cc, ada_w[l], ada_b[l][None, :]).reshape(rows, N_MOD, D_MODEL)
        mods_x = mods[:bsz]
        mods_c = jnp.broadcast_to(mods[bsz:bsz + 1], (bsz, N_MOD, D_MODEL))
        x = _layer(x, ctx, mods_x, mods_c, tuple(a[l] for a in per_layer))
    return x
```
